```python
import jax, jax.numpy as jnp
from jax import lax
import numpy as np

D_MODEL = 2048
BATCH = 8
SEQ = 2048
DEPTH = 2

ALPHA = (2 * DEPTH) ** 0.25
BETA = (8 * DEPTH) ** -0.25
LN_EPS = 1e-5

FOX_HEADS = 8
FOX_HEAD_DIM = D_MODEL // 16
FOX_WIDTH = FOX_HEADS * FOX_HEAD_DIM
QUERY_BLOCK = 128
CONV_CH = D_MODEL // 2
CONV_GROUPS = 8
CONV_WIDTH = 31
EVEN_IN = 3 * FOX_WIDTH + FOX_HEADS + 2 * CONV_CH
EVEN_MIX = FOX_WIDTH + CONV_CH

GLA_HEADS = 4
GLA_KEY_WIDTH = D_MODEL // 2
GLA_VAL_WIDTH = D_MODEL
GLA_HK = GLA_KEY_WIDTH // GLA_HEADS
GLA_HV = GLA_VAL_WIDTH // GLA_HEADS
GLA_LOW_RANK = 16
GLA_TAU = 16.0
GLA_CHUNK = 64
ODD_IN = 2 * GLA_KEY_WIDTH + 2 * GLA_VAL_WIDTH + GLA_LOW_RANK

N_EXPERTS = 16
N_GROUPS = 4
EXPERTS_PER_GROUP = N_EXPERTS // N_GROUPS
TOP_K = 2
D_EXPERT = D_MODEL * 11 // 16
EXPERT_BLOCK = 256

N_EVEN = (DEPTH + 1) // 2
N_ODD = DEPTH // 2

kernel_name = "fox_conformer_gla_sharedrouter_moe_deepnorm"


def layer_norm(x, g, b):
    xf = x.astype(jnp.float32)
    mu = jnp.mean(xf, axis=-1, keepdims=True)
    var = jnp.mean(jnp.square(xf - mu), axis=-1, keepdims=True)
    return ((xf - mu) * lax.rsqrt(var + LN_EPS) * g + b).astype(x.dtype)


def forgetting_attention(q, k, v, log_f):
    B, S, H, Dh = q.shape
    c = jnp.cumsum(log_f, axis=1).transpose(0, 2, 1)
    scale = Dh ** -0.5
    outs = []
    for i in range(S // QUERY_BLOCK):
        lo, hi = i * QUERY_BLOCK, (i + 1) * QUERY_BLOCK
        s = jnp.einsum('bqhd,bkhd->bhqk', q[:, lo:hi], k[:, :hi]).astype(jnp.float32) * scale
        s = s + c[:, :, lo:hi, None] - c[:, :, None, :hi]
        causal = (lo + jnp.arange(QUERY_BLOCK))[:, None] >= jnp.arange(hi)[None, :]
        p = jax.nn.softmax(jnp.where(causal, s, -jnp.inf), axis=-1)
        outs.append(jnp.einsum('bhqk,bkhd->bqhd', p.astype(v.dtype), v[:, :hi]))
    return jnp.concatenate(outs, axis=1)


def causal_depthwise_conv(u, w, b):
    y = lax.conv_general_dilated(u, w.astype(u.dtype), window_strides=(1,),
                                 padding=[(CONV_WIDTH - 1, 0)],
                                 dimension_numbers=('NWC', 'WIO', 'NWC'),
                                 feature_group_count=u.shape[-1])
    return y + b


def even_mixer(x, w_in, b_f, conv_w, conv_b, cn_g, cn_b, w_out):
    B, S, _ = x.shape
    h = x @ w_in
    q, k, v, f_logit, glu = jnp.split(
        h, [FOX_WIDTH, 2 * FOX_WIDTH, 3 * FOX_WIDTH, 3 * FOX_WIDTH + FOX_HEADS], axis=-1)
    heads = lambda t: t.reshape(B, S, FOX_HEADS, FOX_HEAD_DIM)
    log_f = jax.nn.log_sigmoid((f_logit + b_f).astype(jnp.float32))
    att = forgetting_attention(heads(q), heads(k), heads(v), log_f).reshape(B, S, FOX_WIDTH)
    a, gate = jnp.split(glu, 2, axis=-1)
    u = causal_depthwise_conv(a * jax.nn.sigmoid(gate), conv_w, conv_b)
    u = layer_norm(u.reshape(B, S, CONV_GROUPS, -1),
                   cn_g.reshape(CONV_GROUPS, -1), cn_b.reshape(CONV_GROUPS, -1))
    u = jax.nn.silu(u.reshape(B, S, CONV_CH))
    return jnp.concatenate([att, u], axis=-1) @ w_out


def gla_chunked(q, k, v, log_a):
    B, S, H, HK = q.shape
    HV = v.shape[-1]
    N, C = S // GLA_CHUNK, GLA_CHUNK
    chunks = lambda t: t.reshape(B, N, C, H, t.shape[-1]).transpose(1, 0, 3, 2, 4)
    q, k, v, log_a = chunks(q), chunks(k), chunks(v), chunks(log_a)
    b = jnp.cumsum(log_a, axis=-2)
    b_last = b[..., -1:, :]
    q_t = q * jnp.exp(b)
    k_t = k * jnp.exp(-b)
    k_end = k * jnp.exp(b_last - b)
    causal = jnp.tril(jnp.ones((C, C), dtype=bool))
    attn = jnp.where(causal, jnp.einsum('nbhtd,nbhsd->nbhts', q_t, k_t), 0.0)
    o_intra = jnp.einsum('nbhts,nbhsv->nbhtv', attn, v)

    def step(state, inp):
        q_n, k_n, v_n, dec_n = inp
        o = jnp.einsum('bhtd,bhdv->bhtv', q_n, state)
        state = state * dec_n[..., 0, :, None] + jnp.einsum('bhsd,bhsv->bhdv', k_n, v_n)
        return state, o

    state0 = jnp.zeros((B, H, HK, HV), jnp.float32)
    _, o_inter = lax.scan(step, state0, (q_t, k_end, v, jnp.exp(b_last)))
    o = o_intra + o_inter
    return o.transpose(1, 0, 3, 2, 4).reshape(B, S, H, HV)


def odd_mixer(x, w_in, w_a2, b_a, norm_g, w_out):
    B, S, _ = x.shape
    h = x @ w_in
    q, k, v, g, a_low = jnp.split(
        h, [GLA_KEY_WIDTH, 2 * GLA_KEY_WIDTH, 2 * GLA_KEY_WIDTH + GLA_VAL_WIDTH,
            2 * GLA_KEY_WIDTH + 2 * GLA_VAL_WIDTH], axis=-1)
    log_a = jax.nn.log_sigmoid((a_low @ w_a2 + b_a).astype(jnp.float32)) / GLA_TAU
    kh = lambda t: t.astype(jnp.float32).reshape(B, S, GLA_HEADS, GLA_HK)
    o = gla_chunked(kh(q) * GLA_HK ** -0.5, kh(k),
                    v.astype(jnp.float32).reshape(B, S, GLA_HEADS, GLA_HV), kh(log_a))
    o = o * lax.rsqrt(jnp.mean(jnp.square(o), axis=-1, keepdims=True) + LN_EPS)
    o = o.reshape(B, S, GLA_VAL_WIDTH) * norm_g * jax.nn.silu(g.astype(jnp.float32))
    return o.astype(x.dtype) @ w_out


def route(x2d, router_w, router_bias):
    N = x2d.shape[0]
    scores = jax.nn.sigmoid((x2d @ router_w).astype(jnp.float32))
    sel = (scores + router_bias.astype(jnp.float32)).reshape(N, N_GROUPS, EXPERTS_PER_GROUP)
    group_score = jnp.sum(lax.top_k(sel, 2)[0], axis=-1)
    grp = jnp.argmax(group_score, axis=-1)
    sel_g = jnp.take_along_axis(sel, grp[:, None, None], axis=1)[:, 0]
    _, local = lax.top_k(sel_g, TOP_K)
    idx = grp[:, None] * EXPERTS_PER_GROUP + local
    w = jnp.take_along_axis(scores, idx, axis=1)
    return idx, w / jnp.sum(w, axis=-1, keepdims=True)


def moe_ffn(x2d, idx, gates, w_gate, w_up, w_down):
    N, D = x2d.shape
    A = N * TOP_K
    P = ((A + N_EXPERTS * EXPERT_BLOCK + EXPERT_BLOCK - 1) // EXPERT_BLOCK) * EXPERT_BLOCK
    NB = P // EXPERT_BLOCK
    flat_e = idx.reshape(-1)
    flat_tok = jnp.repeat(jnp.arange(N), TOP_K)
    order = jnp.argsort(flat_e, stable=True)
    e_sorted = flat_e[order]
    tok_sorted = flat_tok[order]
    counts = jnp.bincount(flat_e, length=N_EXPERTS)
    starts = jnp.cumsum(counts) - counts
    padded = ((counts + EXPERT_BLOCK - 1) // EXPERT_BLOCK) * EXPERT_BLOCK
    pends = jnp.cumsum(padded)
    pstarts = pends - padded
    dest = pstarts[e_sorted] + (jnp.arange(A) - starts[e_sorted])
    slot_tok = jnp.full((P,), N, dtype=jnp.int32).at[dest].set(tok_sorted.astype(jnp.int32))
    x_pad = jnp.concatenate([x2d, jnp.zeros((1, D), x2d.dtype)], axis=0)
    xs = x_pad[slot_tok].reshape(NB, EXPERT_BLOCK, D)
    block_e = jnp.minimum(jnp.searchsorted(pends, jnp.arange(NB) * EXPERT_BLOCK, side='right'),
                          N_EXPERTS - 1)

    def expert_block(args):
        xb, e = args
        hdn = jax.nn.silu(xb @ w_gate[e]) * (xb @ w_up[e])
        return hdn @ w_down[e]

    ys = lax.map(expert_block, (xs, block_e)).reshape(P, D)
    y_sorted = ys[dest] * gates.reshape(-1)[order][:, None].astype(ys.dtype)
    return jax.ops.segment_sum(y_sorted, tok_sorted, num_segments=N)


def _normal(key, shape, scale):
    return jax.random.normal(key, shape, jnp.float32) * scale


def setup_inputs(seed: int = 0) -> dict:
    key = jax.random.key(seed)
    ks = jax.random.split(key, 24)
    D = D_MODEL
    x = _normal(ks[0], (BATCH, SEQ, D), 1.0)
    even_w_in = _normal(ks[1], (N_EVEN, D, EVEN_IN), D ** -0.5)
    even_w_in = even_w_in.at[..., 2 * FOX_WIDTH:3 * FOX_WIDTH].multiply(BETA)
    glu_a0 = 3 * FOX_WIDTH + FOX_HEADS
    even_w_in = even_w_in.at[..., glu_a0:glu_a0 + CONV_CH].multiply(BETA)
    even_b_f = jax.random.uniform(ks[2], (N_EVEN, FOX_HEADS), jnp.float32, 1.0, 4.0)
    even_conv_w = _normal(ks[3], (N_EVEN, CONV_WIDTH, 1, CONV_CH), CONV_WIDTH ** -0.5)
    even_conv_b = _normal(ks[4], (N_EVEN, CONV_CH), 0.02)
    even_conv_norm_g = 1.0 + _normal(ks[5], (N_EVEN, CONV_CH), 0.02)
    even_conv_norm_b = _normal(ks[6], (N_EVEN, CONV_CH), 0.02)
    even_w_out = _normal(ks[7], (N_EVEN, EVEN_MIX, D), EVEN_MIX ** -0.5 * BETA)
    odd_w_in = _normal(ks[8], (N_ODD, D, ODD_IN), D ** -0.5)
    odd_w_in = odd_w_in.at[..., 2 * GLA_KEY_WIDTH:2 * GLA_KEY_WIDTH + GLA_VAL_WIDTH].multiply(BETA)
    odd_w_a2 = _normal(ks[9], (N_ODD, GLA_LOW_RANK, GLA_KEY_WIDTH), GLA_LOW_RANK ** -0.5)
    odd_b_a = _normal(ks[10], (N_ODD, GLA_KEY_WIDTH), 0.1)
    odd_norm_g = 1.0 + _normal(ks[11], (N_ODD, GLA_VAL_WIDTH), 0.02)
    odd_w_out = _normal(ks[12], (N_ODD, GLA_VAL_WIDTH, D), GLA_VAL_WIDTH ** -0.5 * BETA)
    ln_mix_g = 1.0 + _normal(ks[13], (DEPTH, D), 0.02)
    ln_mix_b = _normal(ks[14], (DEPTH, D), 0.02)
    ln_ffn_g = 1.0 + _normal(ks[15], (DEPTH, D), 0.02)
    ln_ffn_b = _normal(ks[16], (DEPTH, D), 0.02)
    router_w = _normal(ks[17], (D, N_EXPERTS), D ** -0.5)
    router_bias = _normal(ks[18], (N_EXPERTS,), 0.01)
    expert_w_gate = _normal(ks[19], (DEPTH, N_EXPERTS, D, D_EXPERT), D ** -0.5)
    expert_w_up = _normal(ks[20], (DEPTH, N_EXPERTS, D, D_EXPERT), D ** -0.5 * BETA)
    expert_w_down = _normal(ks[21], (DEPTH, N_EXPERTS, D_EXPERT, D), D_EXPERT ** -0.5 * BETA)
    return {"x": x, "even_w_in": even_w_in, "even_b_f": even_b_f, "even_conv_w": even_conv_w,
            "even_conv_b": even_conv_b, "even_conv_norm_g": even_conv_norm_g,
            "even_conv_norm_b": even_conv_norm_b, "even_w_out": even_w_out,
            "odd_w_in": odd_w_in, "odd_w_a2": odd_w_a2, "odd_b_a": odd_b_a,
            "odd_norm_g": odd_norm_g, "odd_w_out": odd_w_out,
            "ln_mix_g": ln_mix_g, "ln_mix_b": ln_mix_b, "ln_ffn_g": ln_ffn_g, "ln_ffn_b": ln_ffn_b,
            "router_w": router_w, "router_bias": router_bias,
            "expert_w_gate": expert_w_gate, "expert_w_up": expert_w_up,
            "expert_w_down": expert_w_down}


def reference(x, even_w_in, even_b_f, even_conv_w, even_conv_b, even_conv_norm_g,
              even_conv_norm_b, even_w_out, odd_w_in, odd_w_a2, odd_b_a, odd_norm_g,
              odd_w_out, ln_mix_g, ln_mix_b, ln_ffn_g, ln_ffn_b, router_w, router_bias,
              expert_w_gate, expert_w_up, expert_w_down):
    B, S, D = x.shape
    for layer in range(DEPTH):
        i = layer // 2
        if layer % 2 == 0:
            mix = even_mixer(x, even_w_in[i], even_b_f[i], even_conv_w[i], even_conv_b[i],
                             even_conv_norm_g[i], even_conv_norm_b[i], even_w_out[i])
        else:
            mix = odd_mixer(x, odd_w_in[i], odd_w_a2[i], odd_b_a[i], odd_norm_g[i], odd_w_out[i])
        x = layer_norm(ALPHA * x + mix, ln_mix_g[layer], ln_mix_b[layer])
        x2d = x.reshape(B * S, D)
        idx, gates = route(x2d, router_w, router_bias)
        y = moe_ffn(x2d, idx, gates, expert_w_gate[layer], expert_w_up[layer], expert_w_down[layer])
        x = layer_norm(ALPHA * x + y.reshape(B, S, D), ln_ffn_g[layer], ln_ffn_b[layer])
    return x
```

```python
import functools

import jax
import jax.numpy as jnp
from jax import lax
from jax.experimental import pallas as pl
from jax.experimental.pallas import tpu as pltpu

F32 = jnp.float32
BF16 = jnp.bfloat16

DEPTH = 2
ALPHA = (2 * DEPTH) ** 0.25
LN_EPS = 1e-5
FOX_HEADS = 8
FOX_HEAD_DIM = 128
CONV_WIDTH = 31
CONV_GROUP = 128
GLA_HEADS = 4
GLA_LOW_RANK = 16
GLA_TAU = 16.0
GLA_CHUNK = 64
N_EXPERTS = 16
N_GROUPS = 4
EXPERTS_PER_GROUP = N_EXPERTS // N_GROUPS
TOP_K = 2

LANES = 128
V7X_VMEM_BYTES = 64 * 1024 * 1024
NEG_BIG = -1e30


def _vmem(mib):
    assert mib * 1024 * 1024 < V7X_VMEM_BYTES
    return mib * 1024 * 1024


def _sigmoid(z):
    return 1.0 / (1.0 + jnp.exp(-z))


def _log_sigmoid(z):
    return jnp.minimum(z, 0.0) - jnp.log1p(jnp.exp(-jnp.abs(z)))


def _dot(a, b):
    return jnp.dot(a, b, preferred_element_type=F32)


def _dot_nt(a, b):
    return lax.dot_general(a, b, (((1,), (1,)), ((), ())), preferred_element_type=F32)


def _dot_tn(a, b):
    return lax.dot_general(a, b, (((0,), (0,)), ((), ())), preferred_element_type=F32)


def _split_bf16(v):
    hi = v.astype(BF16)
    lo = (v - hi.astype(F32)).astype(BF16)
    return hi, lo


def _mm_kernel(x_ref, w_ref, o_ref):
    o_ref[...] = _dot(x_ref[...].astype(BF16), w_ref[...]).astype(o_ref.dtype)


def _matmul(x, w, out_dtype, tm, tn):
    m, k = x.shape
    n = w.shape[1]
    assert m % tm == 0 and n % tn == 0
    return pl.pallas_call(
        _mm_kernel,
        grid=(m // tm, n // tn),
        in_specs=[pl.BlockSpec((tm, k), lambda i, j: (i, 0)),
                  pl.BlockSpec((k, tn), lambda i, j: (0, j))],
        out_specs=pl.BlockSpec((tm, tn), lambda i, j: (i, j)),
        out_shape=jax.ShapeDtypeStruct((m, n), out_dtype),
        compiler_params=pltpu.CompilerParams(
            dimension_semantics=("parallel", "parallel"), vmem_limit_bytes=_vmem(52)),
        name="dense_proj",
    )(x, w)


def _fox_gate_kernel(x_ref, wf_ref, bf_ref, c_ref, carry_ref):
    j = pl.program_id(1)

    @pl.when(j == 0)
    def _():
        carry_ref[...] = jnp.zeros_like(carry_ref)

    z = _dot_nt(wf_ref[...], x_ref[...].astype(BF16)) + bf_ref[...]
    lf = _log_sigmoid(z)
    ts = lf.shape[1]
    lane = lax.broadcasted_iota(jnp.int32, lf.shape, 1)
    sh = 1
    while sh < ts:
        lf = lf + jnp.where(lane >= sh, pltpu.roll(lf, sh, axis=1), 0.0)
        sh *= 2
    c = lf + carry_ref[...]
    c_ref[0] = c
    carry_ref[...] = c[:, ts - 1:ts]


def _fox_gate(x2, wf_t, b_f, batch, seq, ts):
    heads = wf_t.shape[0]
    d = x2.shape[1]
    nj = seq // ts
    return pl.pallas_call(
        _fox_gate_kernel,
        grid=(batch, nj),
        in_specs=[pl.BlockSpec((ts, d), lambda b, j: (b * nj + j, 0)),
                  pl.BlockSpec((heads, d), lambda b, j: (0, 0)),
                  pl.BlockSpec((heads, 1), lambda b, j: (0, 0))],
        out_specs=pl.BlockSpec((1, heads, ts), lambda b, j: (b, 0, j)),
        out_shape=jax.ShapeDtypeStruct((batch, heads, seq), F32),
        scratch_shapes=[pltpu.VMEM((heads, 1), F32)],
        compiler_params=pltpu.CompilerParams(
            dimension_semantics=("parallel", "arbitrary"), vmem_limit_bytes=_vmem(40)),
        name="fox_gate",
    )(x2, wf_t, b_f)


def _fox_attn_kernel(q_ref, k_ref, v_ref, c_ref, o_ref, *, tile, scale):
    i = pl.program_id(2)
    q = q_ref[...]

    def scores(j):
        r0 = pl.multiple_of(j * tile, tile)
        kj = k_ref[pl.ds(r0, tile), :]
        vj = v_ref[pl.ds(r0, tile), :]
        s = _dot_nt(q, kj) * scale - c_ref[0, j]
        return s, vj

    def update(carry, s, vj):
        m, l, acc = carry
        m_new = jnp.maximum(m, jnp.max(s, axis=-1, keepdims=True))
        a = jnp.exp(m - m_new)
        p = jnp.exp(s - m_new)
        l = a * l + jnp.sum(p, axis=-1, keepdims=True)
        acc = a * acc + _dot(p.astype(BF16), vj)
        return m_new, l, acc

    def body(j, carry):
        s, vj = scores(j)
        return update(carry, s, vj)

    init = (jnp.full((tile, 1), NEG_BIG, F32), jnp.zeros((tile, 1), F32),
            jnp.zeros((tile, q.shape[1]), F32))
    carry = lax.fori_loop(0, i, body, init)
    s, vj = scores(i)
    row = lax.broadcasted_iota(jnp.int32, s.shape, 0)
    col = lax.broadcasted_iota(jnp.int32, s.shape, 1)
    s = jnp.where(row >= col, s, NEG_BIG)
    _, l, acc = update(carry, s, vj)
    o_ref[...] = (acc / l).astype(o_ref.dtype)


def _fox_attention(h, c, batch, seq, tile):
    n = h.shape[0]
    hd = FOX_HEAD_DIM
    nh = FOX_HEADS
    nq = seq // tile
    c4 = c.reshape(batch * nh, nq, 1, tile)
    kern = functools.partial(_fox_attn_kernel, tile=tile, scale=hd ** -0.5)
    return pl.pallas_call(
        kern,
        grid=(batch, nh, nq),
        in_specs=[pl.BlockSpec((tile, hd), lambda b, hh, i: (b * nq + i, hh)),
                  pl.BlockSpec((seq, hd), lambda b, hh, i: (b, nh + hh)),
                  pl.BlockSpec((seq, hd), lambda b, hh, i: (b, 2 * nh + hh)),
                  pl.BlockSpec((1, nq, 1, tile), lambda b, hh, i: (b * nh + hh, 0, 0, 0))],
        out_specs=pl.BlockSpec((tile, hd), lambda b, hh, i: (b * nq + i, hh)),
        out_shape=jax.ShapeDtypeStruct((n, nh * hd), BF16),
        compiler_params=pltpu.CompilerParams(
            dimension_semantics=("parallel", "parallel", "arbitrary"), vmem_limit_bytes=_vmem(32)),
        name="fox_attention",
    )(h, h, h, c4)


CONV_PAD = 32
CONV_ROWS = 256


def _conv_kernel(a_ref, g_ref, w_ref, cb_ref, ng_ref, nb_ref, o_ref, pad_ref):
    seq = a_ref.shape[0]
    pad_ref[pl.ds(0, CONV_PAD), :] = jnp.zeros((CONV_PAD, LANES), F32)
    pad_ref[pl.ds(CONV_PAD, seq), :] = a_ref[...].astype(F32) * _sigmoid(g_ref[...].astype(F32))
    off = CONV_PAD - (CONV_WIDTH - 1)
    for r in range(seq // CONV_ROWS):
        base = r * CONV_ROWS
        acc = jnp.zeros((CONV_ROWS, LANES), F32)
        for j in range(CONV_WIDTH):
            acc = acc + pad_ref[pl.ds(base + off + j, CONV_ROWS), :] * w_ref[pl.ds(j, 1), :]
        y = acc + cb_ref[...]
        mu = jnp.mean(y, axis=-1, keepdims=True)
        yc = y - mu
        var = jnp.mean(yc * yc, axis=-1, keepdims=True)
        yn = yc * lax.rsqrt(var + LN_EPS) * ng_ref[...] + nb_ref[...]
        o_ref[pl.ds(base, CONV_ROWS), :] = (yn * _sigmoid(yn)).astype(o_ref.dtype)


def _conv_module(h, a_col, g_col, conv_w, conv_b, cn_g, cn_b, batch, seq):
    n = h.shape[0]
    ch = conv_w.shape[1]
    ng = ch // CONV_GROUP
    assert seq % CONV_ROWS == 0
    vec = lambda: pl.BlockSpec((1, CONV_GROUP), lambda b, g: (0, g))
    return pl.pallas_call(
        _conv_kernel,
        grid=(batch, ng),
        in_specs=[pl.BlockSpec((seq, CONV_GROUP), lambda b, g: (b, a_col + g)),
                  pl.BlockSpec((seq, CONV_GROUP), lambda b, g: (b, g_col + g)),
                  pl.BlockSpec((CONV_WIDTH, CONV_GROUP), lambda b, g: (0, g)),
                  vec(), vec(), vec()],
        out_specs=pl.BlockSpec((seq, CONV_GROUP), lambda b, g: (b, g)),
        out_shape=jax.ShapeDtypeStruct((n, ch), BF16),
        scratch_shapes=[pltpu.VMEM((seq + CONV_PAD, LANES), F32)],
        compiler_params=pltpu.CompilerParams(
            dimension_semantics=("parallel", "parallel"), vmem_limit_bytes=_vmem(32)),
        name="conv_module",
    )(h, h, conv_w, conv_b, cn_g, cn_b)


def _layer_norm(y, g, b):
    mu = jnp.mean(y, axis=-1, keepdims=True)
    yc = y - mu
    var = jnp.mean(yc * yc, axis=-1, keepdims=True)
    return yc * lax.rsqrt(var + LN_EPS) * g + b


def _route_rows(xn, wr_hi, wr_lo, rbias):
    x_hi, x_lo = _split_bf16(xn)
    logits = _dot_nt(wr_hi, x_hi) + _dot_nt(wr_hi, x_lo) + _dot_nt(wr_lo, x_hi)
    scores = _sigmoid(logits)
    sel = scores + rbias
    s = [sel[e:e + 1, :] for e in range(N_EXPERTS)]
    r = [scores[e:e + 1, :] for e in range(N_EXPERTS)]
    pg = EXPERTS_PER_GROUP

    def top2_sum(vals):
        best = None
        for a in range(len(vals)):
            for b in range(a + 1, len(vals)):
                t = vals[a] + vals[b]
                best = t if best is None else jnp.maximum(best, t)
        return best

    gs = [top2_sum(s[g * pg:(g + 1) * pg]) for g in range(N_GROUPS)]
    best, grp = gs[0], jnp.zeros_like(gs[0], dtype=jnp.int32)
    for g in range(1, N_GROUPS):
        upd = gs[g] > best
        best = jnp.where(upd, gs[g], best)
        grp = jnp.where(upd, g, grp)

    def pick_group(vals, k):
        out = vals[(N_GROUPS - 1) * pg + k]
        for g in range(N_GROUPS - 2, -1, -1):
            out = jnp.where(grp == g, vals[g * pg + k], out)
        return out

    v = [pick_group(s, k) for k in range(pg)]
    w = [pick_group(r, k) for k in range(pg)]
    b1, i1, w1 = v[0], jnp.zeros_like(grp), w[0]
    for k in range(1, pg):
        upd = v[k] > b1
        b1 = jnp.where(upd, v[k], b1)
        i1 = jnp.where(upd, k, i1)
        w1 = jnp.where(upd, w[k], w1)
    b2 = jnp.full_like(b1, -jnp.inf)
    i2, w2 = jnp.zeros_like(grp), jnp.zeros_like(w1)
    for k in range(pg):
        upd = jnp.logical_and(i1 != k, v[k] > b2)
        b2 = jnp.where(upd, v[k], b2)
        i2 = jnp.where(upd, k, i2)
        w2 = jnp.where(upd, w[k], w2)
    tot = w1 + w2
    rows = [(grp * pg + i1).astype(F32), (grp * pg + i2).astype(F32), w1 / tot, w2 / tot]
    rows += [jnp.zeros_like(w1)] * (8 - len(rows))
    return jnp.concatenate(rows, axis=0)


def _outproj_kernel(a1_ref, a2_ref, w_ref, x_ref, g_ref, b_ref, wrh_ref, wrl_ref, rb_ref,
                    xn_ref, xb_ref, rt_ref):
    k1 = a1_ref.shape[1]
    k2 = a2_ref.shape[1]
    mix = _dot(a1_ref[...], w_ref[pl.ds(0, k1), :]) + _dot(a2_ref[...], w_ref[pl.ds(k1, k2), :])
    xn = _layer_norm(ALPHA * x_ref[...] + mix, g_ref[...], b_ref[...])
    xn_ref[...] = xn
    xb_ref[...] = xn.astype(BF16)
    rt_ref[...] = _route_rows(xn, wrh_ref[...], wrl_ref[...], rb_ref[...])


def _outproj_ln_route(a1, a1_col, a2, a2_col, kw, w, xres, ln_g, ln_b, wr_hi, wr_lo, rbias, tm):
    n, d = xres.shape
    full = lambda shape: pl.BlockSpec(shape, lambda i: (0, 0))
    return pl.pallas_call(
        _outproj_kernel,
        grid=(n // tm,),
        in_specs=[pl.BlockSpec((tm, kw), lambda i: (i, a1_col)),
                  pl.BlockSpec((tm, kw), lambda i: (i, a2_col)),
                  full(w.shape),
                  pl.BlockSpec((tm, d), lambda i: (i, 0)),
                  full((1, d)), full((1, d)),
                  full(wr_hi.shape), full(wr_lo.shape), full(rbias.shape)],
        out_specs=[pl.BlockSpec((tm, d), lambda i: (i, 0)),
                   pl.BlockSpec((tm, d), lambda i: (i, 0)),
                   pl.BlockSpec((8, tm), lambda i: (0, i))],
        out_shape=[jax.ShapeDtypeStruct((n, d), F32),
                   jax.ShapeDtypeStruct((n, d), BF16),
                   jax.ShapeDtypeStruct((8, n), F32)],
        compiler_params=pltpu.CompilerParams(
            dimension_semantics=("parallel",), vmem_limit_bytes=_vmem(52)),
        name="outproj_ln_route",
    )(a1, a2, w, xres, ln_g, ln_b, wr_hi, wr_lo, rbias)


def _expert_kernel(be_ref, na_ref, xs_ref, gt_ref, wgu_ref, wd_ref, o_ref):
    del be_ref

    @pl.when(pl.program_id(0) < na_ref[0])
    def _():
        de = wd_ref.shape[0]
        gu = _dot(xs_ref[...], wgu_ref[...])
        g = gu[:, :de]
        u = gu[:, de:]
        gate = jnp.concatenate([gt_ref[...]] * (de // LANES), axis=1)
        hdn = (g * _sigmoid(g)) * u * gate
        o_ref[...] = _dot(hdn.astype(BF16), wd_ref[...]).astype(o_ref.dtype)


def _expert_ffn(xs, gate128, block_expert, n_active, wgu, wd, tm):
    p, d = xs.shape
    nb = p // tm
    de = wd.shape[1]
    row = lambda i, be, na: (jnp.minimum(i, na[0] - 1), 0)
    return pl.pallas_call(
        _expert_kernel,
        grid_spec=pltpu.PrefetchScalarGridSpec(
            num_scalar_prefetch=2,
            grid=(nb,),
            in_specs=[pl.BlockSpec((tm, d), row),
                      pl.BlockSpec((tm, LANES), row),
                      pl.BlockSpec((None, d, 2 * de), lambda i, be, na: (be[i], 0, 0)),
                      pl.BlockSpec((None, de, d), lambda i, be, na: (be[i], 0, 0))],
            out_specs=pl.BlockSpec((tm, d), lambda i, be, na: (jnp.where(i < na[0], i, nb), 0))),
        out_shape=jax.ShapeDtypeStruct((p + tm, d), BF16),
        compiler_params=pltpu.CompilerParams(
            dimension_semantics=("arbitrary",), vmem_limit_bytes=_vmem(58)),
        name="expert_ffn",
    )(block_expert, n_active, xs, gate128, wgu, wd)


def _combine_kernel(x_ref, y0_ref, y1_ref, g_ref, b_ref, xn_ref, xb_ref):
    y = ALPHA * x_ref[...] + (y0_ref[...].astype(F32) + y1_ref[...].astype(F32))
    xn = _layer_norm(y, g_ref[...], b_ref[...])
    xn_ref[...] = xn
    xb_ref[...] = xn.astype(BF16)


def _combine_ln(x, y0, y1, ln_g, ln_b, tm):
    n, d = x.shape
    row = lambda: pl.BlockSpec((tm, d), lambda i: (i, 0))
    vec = lambda: pl.BlockSpec((1, d), lambda i: (0, 0))
    return pl.pallas_call(
        _combine_kernel,
        grid=(n // tm,),
        in_specs=[row(), row(), row(), vec(), vec()],
        out_specs=[row(), row()],
        out_shape=[jax.ShapeDtypeStruct((n, d), F32), jax.ShapeDtypeStruct((n, d), BF16)],
        compiler_params=pltpu.CompilerParams(
            dimension_semantics=("parallel",), vmem_limit_bytes=_vmem(48)),
        name="combine_ln",
    )(x, y0, y1, ln_g, ln_b)


def _gla_gate_kernel(x_ref, wl_ref, w2_ref, ba_ref, o_ref):
    low = _dot(x_ref[...], wl_ref[...]).astype(BF16)
    z = _dot(low, w2_ref[...]) + ba_ref[...]
    o_ref[...] = _log_sigmoid(z) * (1.0 / GLA_TAU)


def _gla_gate(xb, w_low, w_a2, b_a, tm):
    n, d = xb.shape
    kw = w_a2.shape[1]
    return pl.pallas_call(
        _gla_gate_kernel,
        grid=(n // tm,),
        in_specs=[pl.BlockSpec((tm, d), lambda i: (i, 0)),
                  pl.BlockSpec(w_low.shape, lambda i: (0, 0)),
                  pl.BlockSpec(w_a2.shape, lambda i: (0, 0)),
                  pl.BlockSpec((1, kw), lambda i: (0, 0))],
        out_specs=pl.BlockSpec((tm, kw), lambda i: (i, 0)),
        out_shape=jax.ShapeDtypeStruct((n, kw), F32),
        compiler_params=pltpu.CompilerParams(
            dimension_semantics=("parallel",), vmem_limit_bytes=_vmem(32)),
        name="gla_gate",
    )(xb, w_low, w_a2, b_a)


def _gla_kernel(q_ref, k_ref, v_ref, la_ref, g_ref, ng_ref, o_ref, state_ref, *, scale):
    seq, hk = q_ref.shape
    hv = v_ref.shape[1]
    ck = GLA_CHUNK
    state_ref[...] = jnp.zeros_like(state_ref)
    row = lax.broadcasted_iota(jnp.int32, (ck, ck), 0)
    col = lax.broadcasted_iota(jnp.int32, (ck, ck), 1)
    causal = row >= col
    tri = jnp.where(causal, 1.0, 0.0).astype(BF16)
    ones = jnp.ones((ck, LANES), BF16)

    def chunk(c, _):
        r0 = pl.multiple_of(c * ck, ck)
        q = q_ref[pl.ds(r0, ck), :].astype(F32) * scale
        k = k_ref[pl.ds(r0, ck), :].astype(F32)
        v = v_ref[pl.ds(r0, ck), :]
        la_hi, la_lo = _split_bf16(la_ref[pl.ds(r0, ck), :])
        b = _dot(tri, la_hi) + _dot(tri, la_lo)
        b_last = b[ck - 1:ck, :]
        qt = (q * jnp.exp(b)).astype(BF16)
        kt = (k * jnp.exp(-b)).astype(BF16)
        kend = (k * jnp.exp(b_last - b)).astype(BF16)
        attn = jnp.where(causal, _dot_nt(qt, kt), 0.0).astype(BF16)
        state = state_ref[...]
        o = _dot(attn, v) + _dot(qt, state.astype(BF16))
        dcol = jnp.exp(_dot_tn(la_hi, ones) + _dot_tn(la_lo, ones))
        decay = jnp.concatenate([dcol] * (hv // LANES), axis=1)
        state_ref[...] = state * decay + _dot_tn(kend, v)
        o = o * lax.rsqrt(jnp.mean(o * o, axis=-1, keepdims=True) + LN_EPS)
        gate = g_ref[pl.ds(r0, ck), :].astype(F32)
        o = o * ng_ref[...] * (gate * _sigmoid(gate))
        o_ref[pl.ds(r0, ck), :] = o.astype(o_ref.dtype)
        return 0

    lax.fori_loop(0, seq // ck, chunk, 0)


def _gla(h, la, norm_g, batch, seq):
    n = h.shape[0]
    kw = la.shape[1]
    vw = norm_g.shape[1]
    nh = GLA_HEADS
    hk, hv = kw // nh, vw // nh
    v0 = 2 * kw // hv
    g0 = (2 * kw + vw) // hv
    kern = functools.partial(_gla_kernel, scale=hk ** -0.5)
    return pl.pallas_call(
        kern,
        grid=(batch, nh),
        in_specs=[pl.BlockSpec((seq, hk), lambda b, hh: (b, hh)),
                  pl.BlockSpec((seq, hk), lambda b, hh: (b, nh + hh)),
                  pl.BlockSpec((seq, hv), lambda b, hh: (b, v0 + hh)),
                  pl.BlockSpec((seq, hk), lambda b, hh: (b, hh)),
                  pl.BlockSpec((seq, hv), lambda b, hh: (b, g0 + hh)),
                  pl.BlockSpec((1, hv), lambda b, hh: (0, hh))],
        out_specs=pl.BlockSpec((seq, hv), lambda b, hh: (b, hh)),
        out_shape=jax.ShapeDtypeStruct((n, vw), BF16),
        scratch_shapes=[pltpu.VMEM((hk, hv), F32)],
        compiler_params=pltpu.CompilerParams(
            dimension_semantics=("parallel", "parallel"), vmem_limit_bytes=_vmem(48)),
        name="gla",
    )(h, h, h, la, h, norm_g)


EXPERT_TM = 256


def _moe(xn, xb, route, wgu, wd, ln_g, ln_b):
    n, d = xn.shape
    a = n * TOP_K
    tm = EXPERT_TM
    nb = a // tm + N_EXPERTS
    p = nb * tm
    flat_e = route[:TOP_K].astype(jnp.int32).T.reshape(a)
    flat_g = route[TOP_K:2 * TOP_K].T.reshape(a)
    onehot = (flat_e[:, None] == jnp.arange(N_EXPERTS, dtype=jnp.int32)[None, :]).astype(jnp.int32)
    csum = jnp.cumsum(onehot, axis=0)
    counts = csum[-1]
    rank = jnp.sum(csum * onehot, axis=1) - 1
    nblk = (counts + tm - 1) // tm
    bend = jnp.cumsum(nblk)
    pstart = (bend - nblk) * tm
    pos = pstart[flat_e] + rank
    slot_tok = jnp.zeros((p,), jnp.int32).at[pos].set(jnp.arange(a, dtype=jnp.int32) // TOP_K)
    slot_gate = jnp.zeros((p,), F32).at[pos].set(flat_g)
    n_active = bend[-1:].astype(jnp.int32)
    blk = jnp.minimum(jnp.arange(nb, dtype=jnp.int32), n_active[0] - 1)
    block_expert = jnp.minimum(jnp.searchsorted(bend, blk, side="right"),
                               N_EXPERTS - 1).astype(jnp.int32)
    xs = jnp.take(xb, slot_tok, axis=0)
    gate128 = jnp.broadcast_to(slot_gate[:, None], (p, LANES))
    ys = _expert_ffn(xs, gate128, block_expert, n_active, wgu, wd, tm)
    pos2 = pos.reshape(n, TOP_K)
    y0 = jnp.take(ys, pos2[:, 0], axis=0)
    y1 = jnp.take(ys, pos2[:, 1], axis=0)
    return _combine_ln(xn, y0, y1, ln_g, ln_b, 512)


def kernel(x, even_w_in, even_b_f, even_conv_w, even_conv_b, even_conv_norm_g, even_conv_norm_b, even_w_out, odd_w_in, odd_w_a2, odd_b_a, odd_norm_g, odd_w_out, ln_mix_g, ln_mix_b, ln_ffn_g, ln_ffn_b, router_w, router_bias, expert_w_gate, expert_w_up, expert_w_down):
    batch, seq, d = x.shape
    n = batch * seq
    x2 = x.reshape(n, d)
    fw = FOX_HEADS * FOX_HEAD_DIM
    conv_ch = even_conv_w.shape[-1]
    kw = odd_w_a2.shape[-1]
    vw = odd_norm_g.shape[-1]
    row = lambda t: t.reshape(1, -1)

    wr_hi, wr_lo = _split_bf16(router_w.T)
    rbias = router_bias.reshape(N_EXPERTS, 1).astype(F32)
    wgu = jnp.concatenate([expert_w_gate, expert_w_up], axis=-1).astype(BF16)
    wd = expert_w_down.astype(BF16)

    w_in = even_w_in[0]
    w_main = jnp.concatenate([w_in[:, :3 * fw], w_in[:, 3 * fw + FOX_HEADS:]], axis=1).astype(BF16)
    wf_t = w_in[:, 3 * fw:3 * fw + FOX_HEADS].T.astype(BF16)
    h = _matmul(x2, w_main, BF16, 1024, 1024)
    c = _fox_gate(x2, wf_t, even_b_f[0].reshape(FOX_HEADS, 1), batch, seq, 512)
    att = _fox_attention(h, c, batch, seq, 256)
    a_col = 3 * fw // CONV_GROUP
    u = _conv_module(h, a_col, a_col + conv_ch // CONV_GROUP, even_conv_w[0, :, 0, :],
                     row(even_conv_b[0]), row(even_conv_norm_g[0]), row(even_conv_norm_b[0]),
                     batch, seq)
    xn, xb, route = _outproj_ln_route(att, 0, u, 0, fw, even_w_out[0].astype(BF16), x2,
                                      row(ln_mix_g[0]), row(ln_mix_b[0]), wr_hi, wr_lo, rbias, 512)
    xn, xb = _moe(xn, xb, route, wgu[0], wd[0], row(ln_ffn_g[0]), row(ln_ffn_b[0]))

    w_in = odd_w_in[0]
    h = _matmul(xb, w_in[:, :2 * kw + 2 * vw].astype(BF16), BF16, 1024, 1024)
    w_low = jnp.zeros((d, LANES), F32).at[:, :GLA_LOW_RANK].set(w_in[:, 2 * kw + 2 * vw:]).astype(BF16)
    w_a2 = jnp.zeros((LANES, kw), F32).at[:GLA_LOW_RANK].set(odd_w_a2[0]).astype(BF16)
    la = _gla_gate(xb, w_low, w_a2, row(odd_b_a[0]), 1024)
    o = _gla(h, la, row(odd_norm_g[0]), batch, seq)
    half = vw // 2
    xn, xb, route = _outproj_ln_route(o, 0, o, 1, half, odd_w_out[0].astype(BF16), xn,
                                      row(ln_mix_g[1]), row(ln_mix_b[1]), wr_hi, wr_lo, rbias, 512)
    xn, xb = _moe(xn, xb, route, wgu[1], wd[1], row(ln_ffn_g[1]), row(ln_ffn_b[1]))
    return xn.reshape(batch, seq, d)
```

```python
import functools

import jax
import jax.numpy as jnp
from jax import lax
from jax.experimental import pallas as pl
from jax.experimental.pallas import tpu as pltpu

F32 = jnp.float32
BF16 = jnp.bfloat16

DEPTH = 2
ALPHA = (2 * DEPTH) ** 0.25
LN_EPS = 1e-5
FOX_HEADS = 8
FOX_HEAD_DIM = 128
CONV_WIDTH = 31
CONV_GROUP = 128
GLA_HEADS = 4
GLA_LOW_RANK = 16
GLA_TAU = 16.0
GLA_CHUNK = 64
N_EXPERTS = 16
N_GROUPS = 4
EXPERTS_PER_GROUP = N_EXPERTS // N_GROUPS
TOP_K = 2

LANES = 128
V7X_VMEM_BYTES = 64 * 1024 * 1024
NEG_BIG = -1e30
LOG2E = 1.4426950408889634


def _vmem(mib):
    assert mib * 1024 * 1024 < V7X_VMEM_BYTES
    return mib * 1024 * 1024


def _sigmoid(z):
    return 1.0 / (1.0 + jnp.exp(-z))


def _log_sigmoid(z):
    return jnp.minimum(z, 0.0) - jnp.log1p(jnp.exp(-jnp.abs(z)))


def _dot(a, b):
    return jnp.dot(a, b, preferred_element_type=F32)


def _dot_nt(a, b):
    return lax.dot_general(a, b, (((1,), (1,)), ((), ())), preferred_element_type=F32)


def _dot_tn(a, b):
    return lax.dot_general(a, b, (((0,), (0,)), ((), ())), preferred_element_type=F32)


def _split_bf16(v):
    hi = v.astype(BF16)
    lo = (v - hi.astype(F32)).astype(BF16)
    return hi, lo


def _mm_kernel(x_ref, w_ref, o_ref):
    o_ref[...] = _dot(x_ref[...].astype(BF16), w_ref[...]).astype(o_ref.dtype)


def _matmul(x, w, out_dtype, tm, tn):
    m, k = x.shape
    n = w.shape[1]
    assert m % tm == 0 and n % tn == 0
    return pl.pallas_call(
        _mm_kernel,
        grid=(m // tm, n // tn),
        in_specs=[pl.BlockSpec((tm, k), lambda i, j: (i, 0)),
                  pl.BlockSpec((k, tn), lambda i, j: (0, j))],
        out_specs=pl.BlockSpec((tm, tn), lambda i, j: (i, j)),
        out_shape=jax.ShapeDtypeStruct((m, n), out_dtype),
        compiler_params=pltpu.CompilerParams(
            dimension_semantics=("parallel", "parallel"), vmem_limit_bytes=_vmem(52)),
        name="dense_proj",
    )(x, w)


def _fox_gate_kernel(x_ref, wf_ref, bf_ref, c_ref, carry_ref):
    j = pl.program_id(1)

    @pl.when(j == 0)
    def _():
        carry_ref[...] = jnp.zeros_like(carry_ref)

    z = _dot_nt(wf_ref[...], x_ref[...].astype(BF16)) + bf_ref[...]
    lf = _log_sigmoid(z)
    ts = lf.shape[1]
    lane = lax.broadcasted_iota(jnp.int32, lf.shape, 1)
    sh = 1
    while sh < ts:
        lf = lf + jnp.where(lane >= sh, pltpu.roll(lf, sh, axis=1), 0.0)
        sh *= 2
    c = lf + carry_ref[...]
    c_ref[0] = c * LOG2E
    carry_ref[...] = c[:, ts - 1:ts]


def _fox_gate(x2, wf_t, b_f, batch, seq, ts):
    heads = wf_t.shape[0]
    d = x2.shape[1]
    nj = seq // ts
    return pl.pallas_call(
        _fox_gate_kernel,
        grid=(batch, nj),
        in_specs=[pl.BlockSpec((ts, d), lambda b, j: (b * nj + j, 0)),
                  pl.BlockSpec((heads, d), lambda b, j: (0, 0)),
                  pl.BlockSpec((heads, 1), lambda b, j: (0, 0))],
        out_specs=pl.BlockSpec((1, heads, ts), lambda b, j: (b, 0, j)),
        out_shape=jax.ShapeDtypeStruct((batch, heads, seq), F32),
        scratch_shapes=[pltpu.VMEM((heads, 1), F32)],
        compiler_params=pltpu.CompilerParams(
            dimension_semantics=("parallel", "arbitrary"), vmem_limit_bytes=_vmem(40)),
        name="fox_gate",
    )(x2, wf_t, b_f)


FOX_HEADS_PER_STEP = 4


def _fox_attn_kernel(q_ref, k_ref, v_ref, c_ref, o_ref, *, tile, heads):
    i = pl.program_id(2)
    hd = FOX_HEAD_DIM

    def scores(g, j):
        r0 = pl.multiple_of(j * tile, tile)
        kj = k_ref[pl.ds(r0, tile), g * hd:(g + 1) * hd]
        vj = v_ref[pl.ds(r0, tile), g * hd:(g + 1) * hd]
        s = _dot_nt(q_ref[:, g * hd:(g + 1) * hd], kj) - c_ref[g, j]
        return s, vj

    def update(carry, s, vj):
        m, l, acc = carry
        m_new = jnp.maximum(m, jnp.max(s, axis=-1, keepdims=True))
        a = jnp.exp2(m - m_new)
        p = jnp.exp2(s - m_new)
        l = a * l + jnp.sum(p, axis=-1, keepdims=True)
        acc = a * acc + _dot(p.astype(BF16), vj)
        return m_new, l, acc

    def body(j, carries):
        return tuple(update(carries[g], *scores(g, j)) for g in range(heads))

    init = tuple((jnp.full((tile, 1), NEG_BIG, F32), jnp.zeros((tile, 1), F32),
                  jnp.zeros((tile, hd), F32)) for _ in range(heads))
    carries = lax.fori_loop(0, i, body, init)
    row = lax.broadcasted_iota(jnp.int32, (tile, tile), 0)
    col = lax.broadcasted_iota(jnp.int32, (tile, tile), 1)
    for g in range(heads):
        s, vj = scores(g, i)
        s = jnp.where(row >= col, s, NEG_BIG)
        _, l, acc = update(carries[g], s, vj)
        o_ref[:, g * hd:(g + 1) * hd] = (acc * (1.0 / l)).astype(o_ref.dtype)


def _fox_attention(h, c, batch, seq, tile):
    n = h.shape[0]
    hd = FOX_HEAD_DIM
    nh = FOX_HEADS
    gh = FOX_HEADS_PER_STEP
    ng = nh // gh
    nq = seq // tile
    c4 = c.reshape(batch * nh, nq, 1, tile)
    kern = functools.partial(_fox_attn_kernel, tile=tile, heads=gh)
    return pl.pallas_call(
        kern,
        grid=(batch, ng, nq),
        in_specs=[pl.BlockSpec((tile, gh * hd), lambda b, hh, i: (b * nq + i, hh)),
                  pl.BlockSpec((seq, gh * hd), lambda b, hh, i: (b, ng + hh)),
                  pl.BlockSpec((seq, gh * hd), lambda b, hh, i: (b, 2 * ng + hh)),
                  pl.BlockSpec((gh, nq, 1, tile), lambda b, hh, i: (b * ng + hh, 0, 0, 0))],
        out_specs=pl.BlockSpec((tile, gh * hd), lambda b, hh, i: (b * nq + i, hh)),
        out_shape=jax.ShapeDtypeStruct((n, nh * hd), BF16),
        compiler_params=pltpu.CompilerParams(
            dimension_semantics=("parallel", "parallel", "arbitrary"), vmem_limit_bytes=_vmem(40)),
        name="fox_attention",
    )(h, h, h, c4)


CONV_PAD = 32
CONV_ROWS = 256


def _conv_kernel(a_ref, g_ref, w_ref, cb_ref, ng_ref, nb_ref, o_ref, pad_ref):
    seq = a_ref.shape[0]
    pad_ref[pl.ds(0, CONV_PAD), :] = jnp.zeros((CONV_PAD, LANES), F32)
    pad_ref[pl.ds(CONV_PAD, seq), :] = a_ref[...].astype(F32) * _sigmoid(g_ref[...].astype(F32))
    off = CONV_PAD - (CONV_WIDTH - 1)
    for r in range(seq // CONV_ROWS):
        base = r * CONV_ROWS
        acc = jnp.zeros((CONV_ROWS, LANES), F32)
        for j in range(CONV_WIDTH):
            acc = acc + pad_ref[pl.ds(base + off + j, CONV_ROWS), :] * w_ref[pl.ds(j, 1), :]
        y = acc + cb_ref[...]
        mu = jnp.mean(y, axis=-1, keepdims=True)
        yc = y - mu
        var = jnp.mean(yc * yc, axis=-1, keepdims=True)
        yn = yc * lax.rsqrt(var + LN_EPS) * ng_ref[...] + nb_ref[...]
        o_ref[pl.ds(base, CONV_ROWS), :] = (yn * _sigmoid(yn)).astype(o_ref.dtype)


def _conv_module(h, a_col, g_col, conv_w, conv_b, cn_g, cn_b, batch, seq):
    n = h.shape[0]
    ch = conv_w.shape[1]
    ng = ch // CONV_GROUP
    assert seq % CONV_ROWS == 0
    vec = lambda: pl.BlockSpec((1, CONV_GROUP), lambda b, g: (0, g))
    return pl.pallas_call(
        _conv_kernel,
        grid=(batch, ng),
        in_specs=[pl.BlockSpec((seq, CONV_GROUP), lambda b, g: (b, a_col + g)),
                  pl.BlockSpec((seq, CONV_GROUP), lambda b, g: (b, g_col + g)),
                  pl.BlockSpec((CONV_WIDTH, CONV_GROUP), lambda b, g: (0, g)),
                  vec(), vec(), vec()],
        out_specs=pl.BlockSpec((seq, CONV_GROUP), lambda b, g: (b, g)),
        out_shape=jax.ShapeDtypeStruct((n, ch), BF16),
        scratch_shapes=[pltpu.VMEM((seq + CONV_PAD, LANES), F32)],
        compiler_params=pltpu.CompilerParams(
            dimension_semantics=("parallel", "parallel"), vmem_limit_bytes=_vmem(32)),
        name="conv_module",
    )(h, h, conv_w, conv_b, cn_g, cn_b)


def _layer_norm(y, g, b):
    mu = jnp.mean(y, axis=-1, keepdims=True)
    yc = y - mu
    var = jnp.mean(yc * yc, axis=-1, keepdims=True)
    return yc * lax.rsqrt(var + LN_EPS) * g + b


def _route_rows(xn, wr_hi, wr_lo, rbias, count_ref):
    x_hi, x_lo = _split_bf16(xn)
    logits = _dot_nt(wr_hi, x_hi) + _dot_nt(wr_hi, x_lo) + _dot_nt(wr_lo, x_hi)
    scores = _sigmoid(logits)
    sel = scores + rbias
    s = [sel[e:e + 1, :] for e in range(N_EXPERTS)]
    r = [scores[e:e + 1, :] for e in range(N_EXPERTS)]
    pg = EXPERTS_PER_GROUP

    def top2_sum(vals):
        best = None
        for a in range(len(vals)):
            for b in range(a + 1, len(vals)):
                t = vals[a] + vals[b]
                best = t if best is None else jnp.maximum(best, t)
        return best

    gs = [top2_sum(s[g * pg:(g + 1) * pg]) for g in range(N_GROUPS)]
    best, grp = gs[0], jnp.zeros_like(gs[0], dtype=jnp.int32)
    for g in range(1, N_GROUPS):
        upd = gs[g] > best
        best = jnp.where(upd, gs[g], best)
        grp = jnp.where(upd, g, grp)

    def pick_group(vals, k):
        out = vals[(N_GROUPS - 1) * pg + k]
        for g in range(N_GROUPS - 2, -1, -1):
            out = jnp.where(grp == g, vals[g * pg + k], out)
        return out

    v = [pick_group(s, k) for k in range(pg)]
    w = [pick_group(r, k) for k in range(pg)]
    b1, i1, w1 = v[0], jnp.zeros_like(grp), w[0]
    for k in range(1, pg):
        upd = v[k] > b1
        b1 = jnp.where(upd, v[k], b1)
        i1 = jnp.where(upd, k, i1)
        w1 = jnp.where(upd, w[k], w1)
    b2 = jnp.full_like(b1, -jnp.inf)
    i2, w2 = jnp.zeros_like(grp), jnp.zeros_like(w1)
    for k in range(pg):
        upd = jnp.logical_and(i1 != k, v[k] > b2)
        b2 = jnp.where(upd, v[k], b2)
        i2 = jnp.where(upd, k, i2)
        w2 = jnp.where(upd, w[k], w2)
    tot = w1 + w2
    e1 = grp * pg + i1
    e2 = grp * pg + i2
    eid = lax.broadcasted_iota(jnp.int32, scores.shape, 0)
    oh1 = jnp.where(eid == e1, 1.0, 0.0)
    oh2 = jnp.where(eid == e2, 1.0, 0.0)
    cnt = oh1 + oh2
    tm = cnt.shape[1]
    lane = lax.broadcasted_iota(jnp.int32, cnt.shape, 1)
    incl = cnt
    sh = 1
    while sh < tm:
        incl = incl + jnp.where(lane >= sh, pltpu.roll(incl, sh, axis=1), 0.0)
        sh *= 2
    before = incl - cnt + count_ref[...]
    count_ref[...] = count_ref[...] + incl[:, tm - 1:tm]
    rank1 = jnp.sum(oh1 * before, axis=0, keepdims=True)
    rank2 = jnp.sum(oh2 * before, axis=0, keepdims=True)
    rows = [e1.astype(F32), e2.astype(F32), w1 / tot, w2 / tot, rank1, rank2]
    rows += [jnp.zeros_like(w1)] * (8 - len(rows))
    return jnp.concatenate(rows, axis=0)


def _outproj_kernel(a1_ref, a2_ref, w_ref, x_ref, g_ref, b_ref, wrh_ref, wrl_ref, rb_ref,
                    xn_ref, xb_ref, rt_ref, rc_ref, count_ref):
    @pl.when(pl.program_id(0) == 0)
    def _():
        count_ref[...] = jnp.zeros_like(count_ref)

    k1 = a1_ref.shape[1]
    k2 = a2_ref.shape[1]
    mix = _dot(a1_ref[...], w_ref[pl.ds(0, k1), :]) + _dot(a2_ref[...], w_ref[pl.ds(k1, k2), :])
    xn = _layer_norm(ALPHA * x_ref[...] + mix, g_ref[...], b_ref[...])
    xn_ref[...] = xn
    xb_ref[...] = xn.astype(BF16)
    rows = _route_rows(xn, wrh_ref[...], wrl_ref[...], rb_ref[...], count_ref)
    rt_ref[...] = rows
    rc_ref[...] = rows.T


def _outproj_ln_route(a1, a1_col, a2, a2_col, kw, w, xres, ln_g, ln_b, wr_hi, wr_lo, rbias, tm):
    n, d = xres.shape
    full = lambda shape: pl.BlockSpec(shape, lambda i: (0, 0))
    return pl.pallas_call(
        _outproj_kernel,
        grid=(n // tm,),
        in_specs=[pl.BlockSpec((tm, kw), lambda i: (i, a1_col)),
                  pl.BlockSpec((tm, kw), lambda i: (i, a2_col)),
                  full(w.shape),
                  pl.BlockSpec((tm, d), lambda i: (i, 0)),
                  full((1, d)), full((1, d)),
                  full(wr_hi.shape), full(wr_lo.shape), full(rbias.shape)],
        out_specs=[pl.BlockSpec((tm, d), lambda i: (i, 0)),
                   pl.BlockSpec((tm, d), lambda i: (i, 0)),
                   pl.BlockSpec((8, tm), lambda i: (0, i)),
                   pl.BlockSpec((tm, 8), lambda i: (i, 0))],
        out_shape=[jax.ShapeDtypeStruct((n, d), F32),
                   jax.ShapeDtypeStruct((n, d), BF16),
                   jax.ShapeDtypeStruct((8, n), F32),
                   jax.ShapeDtypeStruct((n, 8), F32)],
        scratch_shapes=[pltpu.VMEM((N_EXPERTS, 1), F32)],
        compiler_params=pltpu.CompilerParams(
            dimension_semantics=("arbitrary",), vmem_limit_bytes=_vmem(52)),
        name="outproj_ln_route",
    )(a1, a2, w, xres, ln_g, ln_b, wr_hi, wr_lo, rbias)


def _expert_kernel(be_ref, na_ref, xs_ref, wgu_ref, wd_ref, o_ref):
    del be_ref

    @pl.when(pl.program_id(0) < na_ref[0])
    def _():
        de = wd_ref.shape[0]
        gu = _dot(xs_ref[...], wgu_ref[...])
        g = gu[:, :de]
        u = gu[:, de:]
        hdn = (g * _sigmoid(g)) * u
        o_ref[...] = _dot(hdn.astype(BF16), wd_ref[...]).astype(o_ref.dtype)


def _expert_ffn(xs, block_expert, n_active, wgu, wd, tm):
    p, d = xs.shape
    nb = p // tm
    de = wd.shape[1]
    row = lambda i, be, na: (jnp.minimum(i, na[0] - 1), 0)
    return pl.pallas_call(
        _expert_kernel,
        grid_spec=pltpu.PrefetchScalarGridSpec(
            num_scalar_prefetch=2,
            grid=(nb,),
            in_specs=[pl.BlockSpec((tm, d), row),
                      pl.BlockSpec((None, d, 2 * de), lambda i, be, na: (be[i], 0, 0)),
                      pl.BlockSpec((None, de, d), lambda i, be, na: (be[i], 0, 0))],
            out_specs=pl.BlockSpec((tm, d), lambda i, be, na: (jnp.where(i < na[0], i, nb), 0))),
        out_shape=jax.ShapeDtypeStruct((p + tm, d), BF16),
        compiler_params=pltpu.CompilerParams(
            dimension_semantics=("arbitrary",), vmem_limit_bytes=_vmem(58)),
        name="expert_ffn",
    )(block_expert, n_active, xs, wgu, wd)


ROUTE_GATE_COL = 2


def _combine_kernel(x_ref, y0_ref, y1_ref, rc_ref, g_ref, b_ref, xn_ref, xb_ref):
    g0 = rc_ref[:, ROUTE_GATE_COL:ROUTE_GATE_COL + 1]
    g1 = rc_ref[:, ROUTE_GATE_COL + 1:ROUTE_GATE_COL + 2]
    y = ALPHA * x_ref[...] + (g0 * y0_ref[...].astype(F32) + g1 * y1_ref[...].astype(F32))
    xn = _layer_norm(y, g_ref[...], b_ref[...])
    xn_ref[...] = xn
    xb_ref[...] = xn.astype(BF16)


def _combine_ln(x, y01, route_cols, ln_g, ln_b, tm):
    n, d = x.shape
    nt = n // tm
    row = lambda: pl.BlockSpec((tm, d), lambda i: (i, 0))
    vec = lambda: pl.BlockSpec((1, d), lambda i: (0, 0))
    return pl.pallas_call(
        _combine_kernel,
        grid=(nt,),
        in_specs=[row(), row(), pl.BlockSpec((tm, d), lambda i: (nt + i, 0)),
                  pl.BlockSpec((tm, 8), lambda i: (i, 0)), vec(), vec()],
        out_specs=[row(), row()],
        out_shape=[jax.ShapeDtypeStruct((n, d), F32), jax.ShapeDtypeStruct((n, d), BF16)],
        compiler_params=pltpu.CompilerParams(
            dimension_semantics=("parallel",), vmem_limit_bytes=_vmem(48)),
        name="combine_ln",
    )(x, y01, y01, route_cols, ln_g, ln_b)


def _gla_gate_kernel(x_ref, wl_ref, w2_ref, ba_ref, o_ref):
    low = _dot(x_ref[...], wl_ref[...]).astype(BF16)
    z = _dot(low, w2_ref[...]) + ba_ref[...]
    o_ref[...] = _log_sigmoid(z) * (1.0 / GLA_TAU)


def _gla_gate(xb, w_low, w_a2, b_a, tm):
    n, d = xb.shape
    kw = w_a2.shape[1]
    return pl.pallas_call(
        _gla_gate_kernel,
        grid=(n // tm,),
        in_specs=[pl.BlockSpec((tm, d), lambda i: (i, 0)),
                  pl.BlockSpec(w_low.shape, lambda i: (0, 0)),
                  pl.BlockSpec(w_a2.shape, lambda i: (0, 0)),
                  pl.BlockSpec((1, kw), lambda i: (0, 0))],
        out_specs=pl.BlockSpec((tm, kw), lambda i: (i, 0)),
        out_shape=jax.ShapeDtypeStruct((n, kw), F32),
        compiler_params=pltpu.CompilerParams(
            dimension_semantics=("parallel",), vmem_limit_bytes=_vmem(32)),
        name="gla_gate",
    )(xb, w_low, w_a2, b_a)


GLA_ROWS = 512


def _gla_kernel(q_ref, k_ref, v_ref, la_ref, g_ref, ng_ref, o_ref, state_ref, *, scale):
    rows, kw = q_ref.shape
    nh = state_ref.shape[0]
    hk, hv = state_ref.shape[1], state_ref.shape[2]
    ck = GLA_CHUNK

    @pl.when(pl.program_id(1) == 0)
    def _():
        state_ref[...] = jnp.zeros_like(state_ref)

    row = lax.broadcasted_iota(jnp.int32, (ck, ck), 0)
    col = lax.broadcasted_iota(jnp.int32, (ck, ck), 1)
    causal = row >= col
    tri = jnp.where(causal, 1.0, 0.0).astype(BF16)
    ones = jnp.ones((ck, LANES), BF16)

    def head_chunk(hh, r0):
        ks = slice(hh * hk, (hh + 1) * hk)
        vs = slice(hh * hv, (hh + 1) * hv)
        q = q_ref[pl.ds(r0, ck), ks].astype(F32) * scale
        k = k_ref[pl.ds(r0, ck), ks].astype(F32)
        v = v_ref[pl.ds(r0, ck), vs]
        la_hi, la_lo = _split_bf16(la_ref[pl.ds(r0, ck), ks])
        b = _dot(tri, la_hi) + _dot(tri, la_lo)
        b_last = b[ck - 1:ck, :]
        qt = (q * jnp.exp(b)).astype(BF16)
        kt = (k * jnp.exp(-b)).astype(BF16)
        kend = (k * jnp.exp(b_last - b)).astype(BF16)
        attn = jnp.where(causal, _dot_nt(qt, kt), 0.0).astype(BF16)
        state = state_ref[hh]
        o = _dot(attn, v) + _dot(qt, state.astype(BF16))
        dcol = jnp.exp(_dot_tn(la_hi, ones) + _dot_tn(la_lo, ones))
        decay = jnp.concatenate([dcol] * (hv // LANES), axis=1)
        state_ref[hh] = state * decay + _dot_tn(kend, v)
        o = o * lax.rsqrt(jnp.mean(o * o, axis=-1, keepdims=True) + LN_EPS)
        gate = g_ref[pl.ds(r0, ck), vs].astype(F32)
        o = o * ng_ref[:, vs] * (gate * _sigmoid(gate))
        o_ref[pl.ds(r0, ck), vs] = o.astype(o_ref.dtype)

    def chunk(c, _):
        r0 = pl.multiple_of(c * ck, ck)
        for hh in range(nh):
            head_chunk(hh, r0)
        return 0

    lax.fori_loop(0, rows // ck, chunk, 0)


def _gla(h, la, norm_g, batch, seq):
    n = h.shape[0]
    kw = la.shape[1]
    vw = norm_g.shape[1]
    nh = GLA_HEADS
    hk, hv = kw // nh, vw // nh
    rows = GLA_ROWS
    nj = seq // rows
    kern = functools.partial(_gla_kernel, scale=hk ** -0.5)
    rowblk = lambda b, j: b * nj + j
    return pl.pallas_call(
        kern,
        grid=(batch, nj),
        in_specs=[pl.BlockSpec((rows, kw), lambda b, j: (rowblk(b, j), 0)),
                  pl.BlockSpec((rows, kw), lambda b, j: (rowblk(b, j), 1)),
                  pl.BlockSpec((rows, vw), lambda b, j: (rowblk(b, j), 2 * kw // vw)),
                  pl.BlockSpec((rows, kw), lambda b, j: (rowblk(b, j), 0)),
                  pl.BlockSpec((rows, vw), lambda b, j: (rowblk(b, j), 2 * kw // vw + 1)),
                  pl.BlockSpec((1, vw), lambda b, j: (0, 0))],
        out_specs=pl.BlockSpec((rows, vw), lambda b, j: (rowblk(b, j), 0)),
        out_shape=jax.ShapeDtypeStruct((n, vw), BF16),
        scratch_shapes=[pltpu.VMEM((nh, hk, hv), F32)],
        compiler_params=pltpu.CompilerParams(
            dimension_semantics=("parallel", "arbitrary"), vmem_limit_bytes=_vmem(48)),
        name="gla",
    )(h, h, h, la, h, norm_g)


EXPERT_TM = 256


def _moe(xn, xb, route, route_cols, wgu, wd, ln_g, ln_b):
    n, d = xn.shape
    a = n * TOP_K
    tm = EXPERT_TM
    nb = a // tm + N_EXPERTS
    p = nb * tm
    eidx = route[:TOP_K].astype(jnp.int32)
    rank = route[2 * TOP_K:3 * TOP_K].astype(jnp.int32)
    flat_e = eidx.T.reshape(a)
    sorted_e, sorted_a = lax.sort((flat_e, jnp.arange(a, dtype=jnp.int32)), num_keys=1)
    experts = jnp.arange(N_EXPERTS, dtype=jnp.int32)
    starts = jnp.searchsorted(sorted_e, experts, side="left").astype(jnp.int32)
    counts = jnp.searchsorted(sorted_e, experts, side="right").astype(jnp.int32) - starts
    nblk = (counts + tm - 1) // tm
    bend = jnp.cumsum(nblk)
    pstart = (bend - nblk) * tm
    n_active = bend[-1:].astype(jnp.int32)
    blk = jnp.minimum(jnp.arange(nb, dtype=jnp.int32), n_active[0] - 1)
    block_expert = jnp.minimum(jnp.searchsorted(bend, blk, side="right"),
                               N_EXPERTS - 1).astype(jnp.int32)
    slot_e = jnp.repeat(block_expert, tm)
    slot_r = jnp.arange(p, dtype=jnp.int32) - pstart[slot_e]
    valid = slot_r < counts[slot_e]
    src = jnp.minimum(starts[slot_e] + slot_r, a - 1)
    slot_tok = jnp.where(valid, sorted_a[src] // TOP_K, 0)
    xs = jnp.take(xb, slot_tok, axis=0)
    ys = _expert_ffn(xs, block_expert, n_active, wgu, wd, tm)
    pos = (pstart[eidx] + rank).reshape(a)
    y01 = jnp.take(ys, pos, axis=0)
    return _combine_ln(xn, y01, route_cols, ln_g, ln_b, 512)


def kernel(x, even_w_in, even_b_f, even_conv_w, even_conv_b, even_conv_norm_g, even_conv_norm_b, even_w_out, odd_w_in, odd_w_a2, odd_b_a, odd_norm_g, odd_w_out, ln_mix_g, ln_mix_b, ln_ffn_g, ln_ffn_b, router_w, router_bias, expert_w_gate, expert_w_up, expert_w_down):
    batch, seq, d = x.shape
    n = batch * seq
    x2 = x.reshape(n, d)
    fw = FOX_HEADS * FOX_HEAD_DIM
    conv_ch = even_conv_w.shape[-1]
    kw = odd_w_a2.shape[-1]
    vw = odd_norm_g.shape[-1]
    row = lambda t: t.reshape(1, -1)

    wr_hi, wr_lo = _split_bf16(router_w.T)
    rbias = router_bias.reshape(N_EXPERTS, 1).astype(F32)
    wgu = jnp.concatenate([expert_w_gate, expert_w_up], axis=-1).astype(BF16)
    wd = expert_w_down.astype(BF16)

    w_in = even_w_in[0]
    q_scale = LOG2E * FOX_HEAD_DIM ** -0.5
    w_main = jnp.concatenate([w_in[:, :fw] * q_scale, w_in[:, fw:3 * fw],
                              w_in[:, 3 * fw + FOX_HEADS:]], axis=1).astype(BF16)
    wf_t = w_in[:, 3 * fw:3 * fw + FOX_HEADS].T.astype(BF16)
    h = _matmul(x2, w_main, BF16, 1024, 1024)
    c = _fox_gate(x2, wf_t, even_b_f[0].reshape(FOX_HEADS, 1), batch, seq, 512)
    att = _fox_attention(h, c, batch, seq, 256)
    a_col = 3 * fw // CONV_GROUP
    u = _conv_module(h, a_col, a_col + conv_ch // CONV_GROUP, even_conv_w[0, :, 0, :],
                     row(even_conv_b[0]), row(even_conv_norm_g[0]), row(even_conv_norm_b[0]),
                     batch, seq)
    xn, xb, route, rcols = _outproj_ln_route(att, 0, u, 0, fw, even_w_out[0].astype(BF16), x2,
                                             row(ln_mix_g[0]), row(ln_mix_b[0]),
                                             wr_hi, wr_lo, rbias, 512)
    xn, xb = _moe(xn, xb, route, rcols, wgu[0], wd[0], row(ln_ffn_g[0]), row(ln_ffn_b[0]))

    w_in = odd_w_in[0]
    h = _matmul(xb, w_in[:, :2 * kw + 2 * vw].astype(BF16), BF16, 1024, 1024)
    w_low = jnp.zeros((d, LANES), F32).at[:, :GLA_LOW_RANK].set(w_in[:, 2 * kw + 2 * vw:]).astype(BF16)
    w_a2 = jnp.zeros((LANES, kw), F32).at[:GLA_LOW_RANK].set(odd_w_a2[0]).astype(BF16)
    la = _gla_gate(xb, w_low, w_a2, row(odd_b_a[0]), 1024)
    o = _gla(h, la, row(odd_norm_g[0]), batch, seq)
    half = vw // 2
    xn, xb, route, rcols = _outproj_ln_route(o, 0, o, 1, half, odd_w_out[0].astype(BF16), xn,
                                             row(ln_mix_g[1]), row(ln_mix_b[1]),
                                             wr_hi, wr_lo, rbias, 512)
    xn, xb = _moe(xn, xb, route, rcols, wgu[1], wd[1], row(ln_ffn_g[1]), row(ln_ffn_b[1]))
    return xn.reshape(batch, seq, d)
```

```python
import functools

import jax
import jax.numpy as jnp
from jax import lax
from jax.experimental import pallas as pl
from jax.experimental.pallas import tpu as pltpu

F32 = jnp.float32
BF16 = jnp.bfloat16

DEPTH = 2
ALPHA = (2 * DEPTH) ** 0.25
LN_EPS = 1e-5
FOX_HEADS = 8
FOX_HEAD_DIM = 128
CONV_WIDTH = 31
CONV_GROUP = 128
GLA_HEADS = 4
GLA_LOW_RANK = 16
GLA_TAU = 16.0
GLA_CHUNK = 64
N_EXPERTS = 16
N_GROUPS = 4
EXPERTS_PER_GROUP = N_EXPERTS // N_GROUPS
TOP_K = 2

LANES = 128
V7X_VMEM_BYTES = 64 * 1024 * 1024
NEG_BIG = -1e30
LOG2E = 1.4426950408889634


def _vmem(mib):
    assert mib * 1024 * 1024 < V7X_VMEM_BYTES
    return mib * 1024 * 1024


def _sigmoid(z):
    return 1.0 / (1.0 + jnp.exp(-z))


def _log_sigmoid(z):
    return jnp.minimum(z, 0.0) - jnp.log1p(jnp.exp(-jnp.abs(z)))


def _dot(a, b):
    return jnp.dot(a, b, preferred_element_type=F32)


def _dot_nt(a, b):
    return lax.dot_general(a, b, (((1,), (1,)), ((), ())), preferred_element_type=F32)


def _dot_tn(a, b):
    return lax.dot_general(a, b, (((0,), (0,)), ((), ())), preferred_element_type=F32)


def _split_bf16(v):
    hi = v.astype(BF16)
    lo = (v - hi.astype(F32)).astype(BF16)
    return hi, lo


def _mm_kernel(x_ref, w_ref, o_ref):
    o_ref[...] = _dot(x_ref[...].astype(BF16), w_ref[...]).astype(o_ref.dtype)


def _matmul(x, w, out_dtype, tm, tn):
    m, k = x.shape
    n = w.shape[1]
    assert m % tm == 0 and n % tn == 0
    return pl.pallas_call(
        _mm_kernel,
        grid=(m // tm, n // tn),
        in_specs=[pl.BlockSpec((tm, k), lambda i, j: (i, 0)),
                  pl.BlockSpec((k, tn), lambda i, j: (0, j))],
        out_specs=pl.BlockSpec((tm, tn), lambda i, j: (i, j)),
        out_shape=jax.ShapeDtypeStruct((m, n), out_dtype),
        compiler_params=pltpu.CompilerParams(
            dimension_semantics=("parallel", "parallel"), vmem_limit_bytes=_vmem(52)),
        name="dense_proj",
    )(x, w)


def _fox_gate_kernel(x_ref, wf_ref, bf_ref, c_ref, carry_ref):
    j = pl.program_id(1)

    @pl.when(j == 0)
    def _():
        carry_ref[...] = jnp.zeros_like(carry_ref)

    z = _dot_nt(wf_ref[...], x_ref[...].astype(BF16)) + bf_ref[...]
    lf = _log_sigmoid(z)
    ts = lf.shape[1]
    lane = lax.broadcasted_iota(jnp.int32, lf.shape, 1)
    sh = 1
    while sh < ts:
        lf = lf + jnp.where(lane >= sh, pltpu.roll(lf, sh, axis=1), 0.0)
        sh *= 2
    c = lf + carry_ref[...]
    c_ref[0] = c * LOG2E
    carry_ref[...] = c[:, ts - 1:ts]


def _fox_gate(x2, wf_t, b_f, batch, seq, ts):
    heads = wf_t.shape[0]
    d = x2.shape[1]
    nj = seq // ts
    return pl.pallas_call(
        _fox_gate_kernel,
        grid=(batch, nj),
        in_specs=[pl.BlockSpec((ts, d), lambda b, j: (b * nj + j, 0)),
                  pl.BlockSpec((heads, d), lambda b, j: (0, 0)),
                  pl.BlockSpec((heads, 1), lambda b, j: (0, 0))],
        out_specs=pl.BlockSpec((1, heads, ts), lambda b, j: (b, 0, j)),
        out_shape=jax.ShapeDtypeStruct((batch, heads, seq), F32),
        scratch_shapes=[pltpu.VMEM((heads, 1), F32)],
        compiler_params=pltpu.CompilerParams(
            dimension_semantics=("parallel", "arbitrary"), vmem_limit_bytes=_vmem(40)),
        name="fox_gate",
    )(x2, wf_t, b_f)


FOX_HEADS_PER_STEP = 4


def _fox_attn_kernel(q_ref, k_ref, v_ref, c_ref, o_ref, *, tile, heads):
    i = pl.program_id(2)
    hd = FOX_HEAD_DIM

    def scores(g, j):
        r0 = pl.multiple_of(j * tile, tile)
        kj = k_ref[pl.ds(r0, tile), g * hd:(g + 1) * hd]
        vj = v_ref[pl.ds(r0, tile), g * hd:(g + 1) * hd]
        s = _dot_nt(q_ref[:, g * hd:(g + 1) * hd], kj) - c_ref[g, j]
        return s, vj

    def update(carry, s, vj):
        m, l, acc = carry
        m_new = jnp.maximum(m, jnp.max(s, axis=-1, keepdims=True))
        a = jnp.exp2(m - m_new)
        p = jnp.exp2(s - m_new)
        l = a * l + jnp.sum(p, axis=-1, keepdims=True)
        acc = a * acc + _dot(p.astype(BF16), vj)
        return m_new, l, acc

    def body(j, carries):
        return tuple(update(carries[g], *scores(g, j)) for g in range(heads))

    init = tuple((jnp.full((tile, 1), NEG_BIG, F32), jnp.zeros((tile, 1), F32),
                  jnp.zeros((tile, hd), F32)) for _ in range(heads))
    carries = lax.fori_loop(0, i, body, init)
    row = lax.broadcasted_iota(jnp.int32, (tile, tile), 0)
    col = lax.broadcasted_iota(jnp.int32, (tile, tile), 1)
    for g in range(heads):
        s, vj = scores(g, i)
        s = jnp.where(row >= col, s, NEG_BIG)
        _, l, acc = update(carries[g], s, vj)
        o_ref[:, g * hd:(g + 1) * hd] = (acc * (1.0 / l)).astype(o_ref.dtype)


def _fox_attention(h, c, batch, seq, tile):
    n = h.shape[0]
    hd = FOX_HEAD_DIM
    nh = FOX_HEADS
    gh = FOX_HEADS_PER_STEP
    ng = nh // gh
    nq = seq // tile
    c4 = c.reshape(batch * nh, nq, 1, tile)
    kern = functools.partial(_fox_attn_kernel, tile=tile, heads=gh)
    return pl.pallas_call(
        kern,
        grid=(batch, ng, nq),
        in_specs=[pl.BlockSpec((tile, gh * hd), lambda b, hh, i: (b * nq + i, hh)),
                  pl.BlockSpec((seq, gh * hd), lambda b, hh, i: (b, ng + hh)),
                  pl.BlockSpec((seq, gh * hd), lambda b, hh, i: (b, 2 * ng + hh)),
                  pl.BlockSpec((gh, nq, 1, tile), lambda b, hh, i: (b * ng + hh, 0, 0, 0))],
        out_specs=pl.BlockSpec((tile, gh * hd), lambda b, hh, i: (b * nq + i, hh)),
        out_shape=jax.ShapeDtypeStruct((n, nh * hd), BF16),
        compiler_params=pltpu.CompilerParams(
            dimension_semantics=("parallel", "parallel", "arbitrary"), vmem_limit_bytes=_vmem(40)),
        name="fox_attention",
    )(h, h, h, c4)


CONV_PAD = 32
CONV_ROWS = 256


def _conv_kernel(a_ref, g_ref, w_ref, cb_ref, ng_ref, nb_ref, o_ref, pad_ref):
    seq = a_ref.shape[0]
    pad_ref[pl.ds(0, CONV_PAD), :] = jnp.zeros((CONV_PAD, LANES), F32)
    pad_ref[pl.ds(CONV_PAD, seq), :] = a_ref[...].astype(F32) * _sigmoid(g_ref[...].astype(F32))
    off = CONV_PAD - (CONV_WIDTH - 1)
    for r in range(seq // CONV_ROWS):
        base = r * CONV_ROWS
        acc = jnp.zeros((CONV_ROWS, LANES), F32)
        for j in range(CONV_WIDTH):
            acc = acc + pad_ref[pl.ds(base + off + j, CONV_ROWS), :] * w_ref[pl.ds(j, 1), :]
        y = acc + cb_ref[...]
        mu = jnp.mean(y, axis=-1, keepdims=True)
        yc = y - mu
        var = jnp.mean(yc * yc, axis=-1, keepdims=True)
        yn = yc * lax.rsqrt(var + LN_EPS) * ng_ref[...] + nb_ref[...]
        o_ref[pl.ds(base, CONV_ROWS), :] = (yn * _sigmoid(yn)).astype(o_ref.dtype)


def _conv_module(h, a_col, g_col, conv_w, conv_b, cn_g, cn_b, batch, seq):
    n = h.shape[0]
    ch = conv_w.shape[1]
    ng = ch // CONV_GROUP
    assert seq % CONV_ROWS == 0
    vec = lambda: pl.BlockSpec((1, CONV_GROUP), lambda b, g: (0, g))
    return pl.pallas_call(
        _conv_kernel,
        grid=(batch, ng),
        in_specs=[pl.BlockSpec((seq, CONV_GROUP), lambda b, g: (b, a_col + g)),
                  pl.BlockSpec((seq, CONV_GROUP), lambda b, g: (b, g_col + g)),
                  pl.BlockSpec((CONV_WIDTH, CONV_GROUP), lambda b, g: (0, g)),
                  vec(), vec(), vec()],
        out_specs=pl.BlockSpec((seq, CONV_GROUP), lambda b, g: (b, g)),
        out_shape=jax.ShapeDtypeStruct((n, ch), BF16),
        scratch_shapes=[pltpu.VMEM((seq + CONV_PAD, LANES), F32)],
        compiler_params=pltpu.CompilerParams(
            dimension_semantics=("parallel", "parallel"), vmem_limit_bytes=_vmem(32)),
        name="conv_module",
    )(h, h, conv_w, conv_b, cn_g, cn_b)


def _layer_norm(y, g, b):
    mu = jnp.mean(y, axis=-1, keepdims=True)
    yc = y - mu
    var = jnp.mean(yc * yc, axis=-1, keepdims=True)
    return yc * lax.rsqrt(var + LN_EPS) * g + b


def _route_rows(xn, wr_hi, wr_lo, rbias, count_ref):
    x_hi, x_lo = _split_bf16(xn)
    logits = _dot_nt(wr_hi, x_hi) + _dot_nt(wr_hi, x_lo) + _dot_nt(wr_lo, x_hi)
    scores = _sigmoid(logits)
    sel = scores + rbias
    s = [sel[e:e + 1, :] for e in range(N_EXPERTS)]
    r = [scores[e:e + 1, :] for e in range(N_EXPERTS)]
    pg = EXPERTS_PER_GROUP

    def top2_sum(vals):
        best = None
        for a in range(len(vals)):
            for b in range(a + 1, len(vals)):
                t = vals[a] + vals[b]
                best = t if best is None else jnp.maximum(best, t)
        return best

    gs = [top2_sum(s[g * pg:(g + 1) * pg]) for g in range(N_GROUPS)]
    best, grp = gs[0], jnp.zeros_like(gs[0], dtype=jnp.int32)
    for g in range(1, N_GROUPS):
        upd = gs[g] > best
        best = jnp.where(upd, gs[g], best)
        grp = jnp.where(upd, g, grp)

    def pick_group(vals, k):
        out = vals[(N_GROUPS - 1) * pg + k]
        for g in range(N_GROUPS - 2, -1, -1):
            out = jnp.where(grp == g, vals[g * pg + k], out)
        return out

    v = [pick_group(s, k) for k in range(pg)]
    w = [pick_group(r, k) for k in range(pg)]
    b1, i1, w1 = v[0], jnp.zeros_like(grp), w[0]
    for k in range(1, pg):
        upd = v[k] > b1
        b1 = jnp.where(upd, v[k], b1)
        i1 = jnp.where(upd, k, i1)
        w1 = jnp.where(upd, w[k], w1)
    b2 = jnp.full_like(b1, -jnp.inf)
    i2, w2 = jnp.zeros_like(grp), jnp.zeros_like(w1)
    for k in range(pg):
        upd = jnp.logical_and(i1 != k, v[k] > b2)
        b2 = jnp.where(upd, v[k], b2)
        i2 = jnp.where(upd, k, i2)
        w2 = jnp.where(upd, w[k], w2)
    tot = w1 + w2
    e1 = grp * pg + i1
    e2 = grp * pg + i2
    eid = lax.broadcasted_iota(jnp.int32, scores.shape, 0)
    oh1 = jnp.where(eid == e1, 1.0, 0.0)
    oh2 = jnp.where(eid == e2, 1.0, 0.0)
    cnt = oh1 + oh2
    tm = cnt.shape[1]
    lane = lax.broadcasted_iota(jnp.int32, cnt.shape, 1)
    incl = cnt
    sh = 1
    while sh < tm:
        incl = incl + jnp.where(lane >= sh, pltpu.roll(incl, sh, axis=1), 0.0)
        sh *= 2
    before = incl - cnt + count_ref[...]
    count_ref[...] = count_ref[...] + incl[:, tm - 1:tm]
    rank1 = jnp.sum(oh1 * before, axis=0, keepdims=True)
    rank2 = jnp.sum(oh2 * before, axis=0, keepdims=True)
    rows = [e1.astype(F32), e2.astype(F32), w1 / tot, w2 / tot, rank1, rank2]
    rows += [jnp.zeros_like(w1)] * (8 - len(rows))
    return jnp.concatenate(rows, axis=0)


def _outproj_kernel(a1_ref, a2_ref, w_ref, x_ref, g_ref, b_ref, wrh_ref, wrl_ref, rb_ref,
                    xn_ref, xb_ref, rt_ref, rc_ref, count_ref):
    @pl.when(pl.program_id(0) == 0)
    def _():
        count_ref[...] = jnp.zeros_like(count_ref)

    k1 = a1_ref.shape[1]
    k2 = a2_ref.shape[1]
    mix = _dot(a1_ref[...], w_ref[pl.ds(0, k1), :]) + _dot(a2_ref[...], w_ref[pl.ds(k1, k2), :])
    xn = _layer_norm(ALPHA * x_ref[...] + mix, g_ref[...], b_ref[...])
    xn_ref[...] = xn
    xb_ref[...] = xn.astype(BF16)
    rows = _route_rows(xn, wrh_ref[...], wrl_ref[...], rb_ref[...], count_ref)
    rt_ref[...] = rows
    rc_ref[...] = rows.T


def _outproj_ln_route(a1, a1_col, a2, a2_col, kw, w, xres, ln_g, ln_b, wr_hi, wr_lo, rbias, tm):
    n, d = xres.shape
    full = lambda shape: pl.BlockSpec(shape, lambda i: (0, 0))
    return pl.pallas_call(
        _outproj_kernel,
        grid=(n // tm,),
        in_specs=[pl.BlockSpec((tm, kw), lambda i: (i, a1_col)),
                  pl.BlockSpec((tm, kw), lambda i: (i, a2_col)),
                  full(w.shape),
                  pl.BlockSpec((tm, d), lambda i: (i, 0)),
                  full((1, d)), full((1, d)),
                  full(wr_hi.shape), full(wr_lo.shape), full(rbias.shape)],
        out_specs=[pl.BlockSpec((tm, d), lambda i: (i, 0)),
                   pl.BlockSpec((tm, d), lambda i: (i, 0)),
                   pl.BlockSpec((8, tm), lambda i: (0, i)),
                   pl.BlockSpec((tm, 8), lambda i: (i, 0))],
        out_shape=[jax.ShapeDtypeStruct((n, d), F32),
                   jax.ShapeDtypeStruct((n, d), BF16),
                   jax.ShapeDtypeStruct((8, n), F32),
                   jax.ShapeDtypeStruct((n, 8), F32)],
        scratch_shapes=[pltpu.VMEM((N_EXPERTS, 1), F32)],
        compiler_params=pltpu.CompilerParams(
            dimension_semantics=("arbitrary",), vmem_limit_bytes=_vmem(52)),
        name="outproj_ln_route",
    )(a1, a2, w, xres, ln_g, ln_b, wr_hi, wr_lo, rbias)


EXPERT_CHUNK_ROWS = 128
EXPERT_STAGES = 4


def _expert_kernel(be_ref, na_ref, kb_ref, nk_ref, nx_ref, sl_ref,
                   xs_ref, wg_hbm, wu_hbm, wd_hbm, o_ref,
                   wgu_ref, wdn_ref, stg_ref, sem, *, layer):
    i = pl.program_id(0)
    d = xs_ref.shape[1]
    de = wdn_ref.shape[1]
    ch = EXPERT_CHUNK_ROWS
    n_in = d // ch
    n_dn = de // ch
    n_chunks = 2 * n_in + n_dn

    def chunk_copy(e, c):
        st = c % EXPERT_STAGES

        def gate_up(w_hbm, first, col0):
            r0 = pl.multiple_of((c - first) * ch, ch)
            cp = pltpu.make_async_copy(w_hbm.at[layer, e, pl.ds(r0, ch), :],
                                       stg_ref.at[st, :, pl.ds(0, de)], sem.at[st])

            def convert(slot):
                wgu_ref[slot, pl.ds(r0, ch), pl.ds(col0, de)] = stg_ref[st, :, pl.ds(0, de)].astype(BF16)
            return cp, convert

        def down():
            r0 = pl.multiple_of((c - 2 * n_in) * ch, ch)
            cp = pltpu.make_async_copy(wd_hbm.at[layer, e, pl.ds(r0, ch), :],
                                       stg_ref.at[st], sem.at[st])

            def convert(slot):
                wdn_ref[slot, pl.ds(r0, ch), :] = stg_ref[st].astype(BF16)
            return cp, convert

        return ((c < n_in, lambda: gate_up(wg_hbm, 0, 0)),
                (jnp.logical_and(c >= n_in, c < 2 * n_in), lambda: gate_up(wu_hbm, n_in, de)),
                (c >= 2 * n_in, down))

    def start_chunk(e, c):
        for cond, make in chunk_copy(e, c):
            @pl.when(cond)
            def _():
                make()[0].start()

    def finish_chunk(e, c, slot):
        for cond, make in chunk_copy(e, c):
            @pl.when(cond)
            def _():
                cp, convert = make()
                cp.wait()
                convert(slot)

    def prime(e):
        for c in range(EXPERT_STAGES):
            start_chunk(e, jnp.int32(c))

    def process(e, slot, c_lo, c_hi):
        def body(c, _):
            finish_chunk(e, c, slot)

            @pl.when(c + EXPERT_STAGES < n_chunks)
            def _():
                start_chunk(e, c + EXPERT_STAGES)
            return 0
        lax.fori_loop(c_lo, c_hi, body, 0)

    @pl.when(i >= na_ref[0])
    def _():
        o_ref[...] = jnp.zeros_like(o_ref)

    @pl.when(i < na_ref[0])
    def _():
        e = be_ref[i]
        kb = kb_ref[i]
        nk = nk_ref[i]
        nxt = nx_ref[i]
        slot = sl_ref[i]

        @pl.when(i == 0)
        def _():
            prime(e)
            process(e, slot, 0, n_chunks)

        @pl.when(jnp.logical_and(kb == 0, nxt >= 0))
        def _():
            prime(nxt)

        gu = _dot(xs_ref[...], wgu_ref[slot])
        g = gu[:, :de]
        u = gu[:, de:]
        hdn = (g * _sigmoid(g)) * u
        o_ref[...] = _dot(hdn.astype(BF16), wdn_ref[slot]).astype(o_ref.dtype)

        @pl.when(nxt >= 0)
        def _():
            process(nxt, 1 - slot, (n_chunks * kb) // nk, (n_chunks * (kb + 1)) // nk)


def _expert_ffn(xs, tables, w_gate, w_up, w_down, layer, tm):
    p, d = xs.shape
    nb = p // tm
    de = w_down.shape[2]
    assert d % EXPERT_CHUNK_ROWS == 0 and de % EXPERT_CHUNK_ROWS == 0 and de % LANES == 0
    row = lambda i, be, na, *_: (jnp.minimum(i, na[0] - 1), 0)
    hbm = lambda: pl.BlockSpec(memory_space=pl.ANY)
    return pl.pallas_call(
        functools.partial(_expert_kernel, layer=layer),
        grid_spec=pltpu.PrefetchScalarGridSpec(
            num_scalar_prefetch=len(tables),
            grid=(nb,),
            in_specs=[pl.BlockSpec((tm, d), row), hbm(), hbm(), hbm()],
            out_specs=pl.BlockSpec((tm, d), lambda i, *_: (i, 0)),
            scratch_shapes=[pltpu.VMEM((2, d, 2 * de), BF16),
                            pltpu.VMEM((2, de, d), BF16),
                            pltpu.VMEM((EXPERT_STAGES, EXPERT_CHUNK_ROWS, d), F32),
                            pltpu.SemaphoreType.DMA((EXPERT_STAGES,))]),
        out_shape=jax.ShapeDtypeStruct((p, d), BF16),
        compiler_params=pltpu.CompilerParams(
            dimension_semantics=("arbitrary",), vmem_limit_bytes=_vmem(60)),
        name="expert_ffn",
    )(*tables, xs, w_gate, w_up, w_down)


ROUTE_GATE_COL = 2


def _combine_kernel(x_ref, y0_ref, y1_ref, rc_ref, g_ref, b_ref, xn_ref, xb_ref):
    g0 = rc_ref[:, ROUTE_GATE_COL:ROUTE_GATE_COL + 1]
    g1 = rc_ref[:, ROUTE_GATE_COL + 1:ROUTE_GATE_COL + 2]
    y = ALPHA * x_ref[...] + (g0 * y0_ref[...].astype(F32) + g1 * y1_ref[...].astype(F32))
    xn = _layer_norm(y, g_ref[...], b_ref[...])
    xn_ref[...] = xn
    xb_ref[...] = xn.astype(BF16)


def _combine_ln(x, y01, route_cols, ln_g, ln_b, tm):
    n, d = x.shape
    nt = n // tm
    row = lambda: pl.BlockSpec((tm, d), lambda i: (i, 0))
    vec = lambda: pl.BlockSpec((1, d), lambda i: (0, 0))
    return pl.pallas_call(
        _combine_kernel,
        grid=(nt,),
        in_specs=[row(), row(), pl.BlockSpec((tm, d), lambda i: (nt + i, 0)),
                  pl.BlockSpec((tm, 8), lambda i: (i, 0)), vec(), vec()],
        out_specs=[row(), row()],
        out_shape=[jax.ShapeDtypeStruct((n, d), F32), jax.ShapeDtypeStruct((n, d), BF16)],
        compiler_params=pltpu.CompilerParams(
            dimension_semantics=("parallel",), vmem_limit_bytes=_vmem(48)),
        name="combine_ln",
    )(x, y01, y01, route_cols, ln_g, ln_b)


def _gla_gate_kernel(x_ref, wl_ref, w2_ref, ba_ref, o_ref):
    low = _dot(x_ref[...], wl_ref[...]).astype(BF16)
    z = _dot(low, w2_ref[...]) + ba_ref[...]
    o_ref[...] = _log_sigmoid(z) * (1.0 / GLA_TAU)


def _gla_gate(xb, w_low, w_a2, b_a, tm):
    n, d = xb.shape
    kw = w_a2.shape[1]
    return pl.pallas_call(
        _gla_gate_kernel,
        grid=(n // tm,),
        in_specs=[pl.BlockSpec((tm, d), lambda i: (i, 0)),
                  pl.BlockSpec(w_low.shape, lambda i: (0, 0)),
                  pl.BlockSpec(w_a2.shape, lambda i: (0, 0)),
                  pl.BlockSpec((1, kw), lambda i: (0, 0))],
        out_specs=pl.BlockSpec((tm, kw), lambda i: (i, 0)),
        out_shape=jax.ShapeDtypeStruct((n, kw), F32),
        compiler_params=pltpu.CompilerParams(
            dimension_semantics=("parallel",), vmem_limit_bytes=_vmem(32)),
        name="gla_gate",
    )(xb, w_low, w_a2, b_a)


GLA_ROWS = 512


def _gla_kernel(q_ref, k_ref, v_ref, la_ref, g_ref, ng_ref, o_ref, state_ref, *, scale):
    rows, kw = q_ref.shape
    nh = state_ref.shape[0]
    hk, hv = state_ref.shape[1], state_ref.shape[2]
    ck = GLA_CHUNK

    @pl.when(pl.program_id(1) == 0)
    def _():
        state_ref[...] = jnp.zeros_like(state_ref)

    row = lax.broadcasted_iota(jnp.int32, (ck, ck), 0)
    col = lax.broadcasted_iota(jnp.int32, (ck, ck), 1)
    causal = row >= col
    tri = jnp.where(causal, 1.0, 0.0).astype(BF16)
    ones = jnp.ones((ck, LANES), BF16)

    def head_chunk(hh, r0):
        ks = slice(hh * hk, (hh + 1) * hk)
        vs = slice(hh * hv, (hh + 1) * hv)
        q = q_ref[pl.ds(r0, ck), ks].astype(F32) * scale
        k = k_ref[pl.ds(r0, ck), ks].astype(F32)
        v = v_ref[pl.ds(r0, ck), vs]
        la_hi, la_lo = _split_bf16(la_ref[pl.ds(r0, ck), ks])
        b = _dot(tri, la_hi) + _dot(tri, la_lo)
        b_last = b[ck - 1:ck, :]
        qt = (q * jnp.exp(b)).astype(BF16)
        kt = (k * jnp.exp(-b)).astype(BF16)
        kend = (k * jnp.exp(b_last - b)).astype(BF16)
        attn = jnp.where(causal, _dot_nt(qt, kt), 0.0).astype(BF16)
        state = state_ref[hh]
        o = _dot(attn, v) + _dot(qt, state.astype(BF16))
        dcol = jnp.exp(_dot_tn(la_hi, ones) + _dot_tn(la_lo, ones))
        decay = jnp.concatenate([dcol] * (hv // LANES), axis=1)
        state_ref[hh] = state * decay + _dot_tn(kend, v)
        o = o * lax.rsqrt(jnp.mean(o * o, axis=-1, keepdims=True) + LN_EPS)
        gate = g_ref[pl.ds(r0, ck), vs].astype(F32)
        o = o * ng_ref[:, vs] * (gate * _sigmoid(gate))
        o_ref[pl.ds(r0, ck), vs] = o.astype(o_ref.dtype)

    def chunk(c, _):
        r0 = pl.multiple_of(c * ck, ck)
        for hh in range(nh):
            head_chunk(hh, r0)
        return 0

    lax.fori_loop(0, rows // ck, chunk, 0)


def _gla(h, la, norm_g, batch, seq):
    n = h.shape[0]
    kw = la.shape[1]
    vw = norm_g.shape[1]
    nh = GLA_HEADS
    hk, hv = kw // nh, vw // nh
    rows = GLA_ROWS
    nj = seq // rows
    kern = functools.partial(_gla_kernel, scale=hk ** -0.5)
    rowblk = lambda b, j: b * nj + j
    return pl.pallas_call(
        kern,
        grid=(batch, nj),
        in_specs=[pl.BlockSpec((rows, kw), lambda b, j: (rowblk(b, j), 0)),
                  pl.BlockSpec((rows, kw), lambda b, j: (rowblk(b, j), 1)),
                  pl.BlockSpec((rows, vw), lambda b, j: (rowblk(b, j), 2 * kw // vw)),
                  pl.BlockSpec((rows, kw), lambda b, j: (rowblk(b, j), 0)),
                  pl.BlockSpec((rows, vw), lambda b, j: (rowblk(b, j), 2 * kw // vw + 1)),
                  pl.BlockSpec((1, vw), lambda b, j: (0, 0))],
        out_specs=pl.BlockSpec((rows, vw), lambda b, j: (rowblk(b, j), 0)),
        out_shape=jax.ShapeDtypeStruct((n, vw), BF16),
        scratch_shapes=[pltpu.VMEM((nh, hk, hv), F32)],
        compiler_params=pltpu.CompilerParams(
            dimension_semantics=("parallel", "arbitrary"), vmem_limit_bytes=_vmem(48)),
        name="gla",
    )(h, h, h, la, h, norm_g)


EXPERT_TM = 256


def _moe(xn, xb, route, route_cols, w_gate, w_up, w_down, layer, ln_g, ln_b):
    n, d = xn.shape
    a = n * TOP_K
    tm = EXPERT_TM
    nb = a // tm + N_EXPERTS
    p = nb * tm
    i32 = jnp.int32
    eidx = route[:TOP_K].astype(i32)
    rank = route[2 * TOP_K:3 * TOP_K].astype(i32)
    experts = jnp.arange(N_EXPERTS, dtype=i32)
    is_e = eidx[None] == experts[:, None, None]
    counts = jnp.sum(is_e.astype(i32), axis=(1, 2))
    starts = jnp.cumsum(counts) - counts
    nblk = (counts + tm - 1) // tm
    bend = jnp.cumsum(nblk)
    pstart = (bend - nblk) * tm
    n_active = bend[-1:].astype(i32)
    blk = jnp.minimum(jnp.arange(nb, dtype=i32), n_active[0] - 1)
    block_expert = jnp.minimum(jnp.sum((bend[None, :] <= blk[:, None]).astype(i32), axis=1),
                               N_EXPERTS - 1)
    assign = jnp.arange(n, dtype=i32)[None, :] * TOP_K + jnp.arange(TOP_K, dtype=i32)[:, None]
    sorted_a = lax.sort((eidx * a + assign).reshape(a)) % a
    sorted_a = jnp.concatenate([sorted_a, jnp.zeros((tm,), i32)])
    blk_r0 = jnp.arange(nb, dtype=i32) * tm - pstart[block_expert]
    blk_src = jnp.clip(starts[block_expert] + blk_r0, 0, a)
    blk_left = counts[block_expert] - blk_r0
    slot_a = jax.vmap(lambda s: lax.dynamic_slice(sorted_a, (s,), (tm,)))(blk_src)
    valid = jnp.arange(tm, dtype=i32)[None, :] < blk_left[:, None]
    slot_tok = jnp.where(valid, slot_a // TOP_K, 0).reshape(p)
    xs = jnp.take(xb, slot_tok, axis=0)
    nonempty = nblk > 0
    later = jnp.logical_and(experts[None, :] > experts[:, None], nonempty[None, :])
    nxt_e = jnp.min(jnp.where(later, experts[None, :], N_EXPERTS), axis=1)
    nxt_e = jnp.where(nxt_e < N_EXPERTS, nxt_e, -1)
    slot_e = (jnp.cumsum(nonempty.astype(i32)) - 1) % 2
    tables = (block_expert, n_active,
              blk - (bend - nblk)[block_expert], nblk[block_expert],
              nxt_e[block_expert], slot_e[block_expert])
    ys = _expert_ffn(xs, tuple(t.astype(i32) for t in tables), w_gate, w_up, w_down, layer, tm)
    pos = rank + jnp.sum(jnp.where(is_e, pstart[:, None, None], 0), axis=0)
    y01 = jnp.take(ys, pos.reshape(a), axis=0)
    return _combine_ln(xn, y01, route_cols, ln_g, ln_b, 512)


def kernel(x, even_w_in, even_b_f, even_conv_w, even_conv_b, even_conv_norm_g, even_conv_norm_b, even_w_out, odd_w_in, odd_w_a2, odd_b_a, odd_norm_g, odd_w_out, ln_mix_g, ln_mix_b, ln_ffn_g, ln_ffn_b, router_w, router_bias, expert_w_gate, expert_w_up, expert_w_down):
    batch, seq, d = x.shape
    n = batch * seq
    x2 = x.reshape(n, d)
    fw = FOX_HEADS * FOX_HEAD_DIM
    conv_ch = even_conv_w.shape[-1]
    kw = odd_w_a2.shape[-1]
    vw = odd_norm_g.shape[-1]
    row = lambda t: t.reshape(1, -1)

    wr_hi, wr_lo = _split_bf16(router_w.T)
    rbias = router_bias.reshape(N_EXPERTS, 1).astype(F32)
    experts_w = (expert_w_gate, expert_w_up, expert_w_down)

    w_in = even_w_in[0]
    q_scale = LOG2E * FOX_HEAD_DIM ** -0.5
    w_main = jnp.concatenate([w_in[:, :fw] * q_scale, w_in[:, fw:3 * fw],
                              w_in[:, 3 * fw + FOX_HEADS:]], axis=1).astype(BF16)
    wf_t = w_in[:, 3 * fw:3 * fw + FOX_HEADS].T.astype(BF16)
    h = _matmul(x2, w_main, BF16, 1024, 1024)
    c = _fox_gate(x2, wf_t, even_b_f[0].reshape(FOX_HEADS, 1), batch, seq, 512)
    att = _fox_attention(h, c, batch, seq, 256)
    a_col = 3 * fw // CONV_GROUP
    u = _conv_module(h, a_col, a_col + conv_ch // CONV_GROUP, even_conv_w[0, :, 0, :],
                     row(even_conv_b[0]), row(even_conv_norm_g[0]), row(even_conv_norm_b[0]),
                     batch, seq)
    xn, xb, route, rcols = _outproj_ln_route(att, 0, u, 0, fw, even_w_out[0].astype(BF16), x2,
                                             row(ln_mix_g[0]), row(ln_mix_b[0]),
                                             wr_hi, wr_lo, rbias, 512)
    xn, xb = _moe(xn, xb, route, rcols, *experts_w, 0, row(ln_ffn_g[0]), row(ln_ffn_b[0]))

    w_in = odd_w_in[0]
    h = _matmul(xb, w_in[:, :2 * kw + 2 * vw].astype(BF16), BF16, 1024, 1024)
    w_low = jnp.zeros((d, LANES), F32).at[:, :GLA_LOW_RANK].set(w_in[:, 2 * kw + 2 * vw:]).astype(BF16)
    w_a2 = jnp.zeros((LANES, kw), F32).at[:GLA_LOW_RANK].set(odd_w_a2[0]).astype(BF16)
    la = _gla_gate(xb, w_low, w_a2, row(odd_b_a[0]), 1024)
    o = _gla(h, la, row(odd_norm_g[0]), batch, seq)
    half = vw // 2
    xn, xb, route, rcols = _outproj_ln_route(o, 0, o, 1, half, odd_w_out[0].astype(BF16), xn,
                                             row(ln_mix_g[1]), row(ln_mix_b[1]),
                                             wr_hi, wr_lo, rbias, 512)
    xn, xb = _moe(xn, xb, route, rcols, *experts_w, 1, row(ln_ffn_g[1]), row(ln_ffn_b[1]))
    return xn.reshape(batch, seq, d)
```

```python
import functools

import jax
import jax.numpy as jnp
from jax import lax
from jax.experimental import pallas as pl
from jax.experimental.pallas import tpu as pltpu

F32 = jnp.float32
BF16 = jnp.bfloat16

DEPTH = 2
ALPHA = (2 * DEPTH) ** 0.25
LN_EPS = 1e-5
FOX_HEADS = 8
FOX_HEAD_DIM = 128
CONV_WIDTH = 31
CONV_GROUP = 128
GLA_HEADS = 4
GLA_LOW_RANK = 16
GLA_TAU = 16.0
GLA_CHUNK = 64
N_EXPERTS = 16
N_GROUPS = 4
EXPERTS_PER_GROUP = N_EXPERTS // N_GROUPS
TOP_K = 2

LANES = 128
V7X_VMEM_BYTES = 64 * 1024 * 1024
NEG_BIG = -1e30
LOG2E = 1.4426950408889634


def _vmem(mib):
    assert mib * 1024 * 1024 < V7X_VMEM_BYTES
    return mib * 1024 * 1024


def _sigmoid(z):
    return 1.0 / (1.0 + jnp.exp(-z))


def _log_sigmoid(z):
    return jnp.minimum(z, 0.0) - jnp.log1p(jnp.exp(-jnp.abs(z)))


def _dot(a, b):
    return jnp.dot(a, b, preferred_element_type=F32)


def _dot_nt(a, b):
    return lax.dot_general(a, b, (((1,), (1,)), ((), ())), preferred_element_type=F32)


def _dot_tn(a, b):
    return lax.dot_general(a, b, (((0,), (0,)), ((), ())), preferred_element_type=F32)


def _split_bf16(v):
    hi = v.astype(BF16)
    lo = (v - hi.astype(F32)).astype(BF16)
    return hi, lo


def _mm_kernel(x_ref, w_ref, o_ref):
    o_ref[...] = _dot(x_ref[...].astype(BF16), w_ref[...]).astype(o_ref.dtype)


def _matmul(x, w, out_dtype, tm, tn):
    m, k = x.shape
    n = w.shape[1]
    assert m % tm == 0 and n % tn == 0
    return pl.pallas_call(
        _mm_kernel,
        grid=(m // tm, n // tn),
        in_specs=[pl.BlockSpec((tm, k), lambda i, j: (i, 0)),
                  pl.BlockSpec((k, tn), lambda i, j: (0, j))],
        out_specs=pl.BlockSpec((tm, tn), lambda i, j: (i, j)),
        out_shape=jax.ShapeDtypeStruct((m, n), out_dtype),
        compiler_params=pltpu.CompilerParams(
            dimension_semantics=("parallel", "parallel"), vmem_limit_bytes=_vmem(52)),
        name="dense_proj",
    )(x, w)


def _fox_gate_kernel(x_ref, wf_ref, bf_ref, c_ref, carry_ref):
    j = pl.program_id(1)

    @pl.when(j == 0)
    def _():
        carry_ref[...] = jnp.zeros_like(carry_ref)

    z = _dot_nt(wf_ref[...], x_ref[...].astype(BF16)) + bf_ref[...]
    lf = _log_sigmoid(z)
    ts = lf.shape[1]
    lane = lax.broadcasted_iota(jnp.int32, lf.shape, 1)
    sh = 1
    while sh < ts:
        lf = lf + jnp.where(lane >= sh, pltpu.roll(lf, sh, axis=1), 0.0)
        sh *= 2
    c = lf + carry_ref[...]
    c_ref[0] = c * LOG2E
    carry_ref[...] = c[:, ts - 1:ts]


def _fox_gate(x2, wf_t, b_f, batch, seq, ts):
    heads = wf_t.shape[0]
    d = x2.shape[1]
    nj = seq // ts
    return pl.pallas_call(
        _fox_gate_kernel,
        grid=(batch, nj),
        in_specs=[pl.BlockSpec((ts, d), lambda b, j: (b * nj + j, 0)),
                  pl.BlockSpec((heads, d), lambda b, j: (0, 0)),
                  pl.BlockSpec((heads, 1), lambda b, j: (0, 0))],
        out_specs=pl.BlockSpec((1, heads, ts), lambda b, j: (b, 0, j)),
        out_shape=jax.ShapeDtypeStruct((batch, heads, seq), F32),
        scratch_shapes=[pltpu.VMEM((heads, 1), F32)],
        compiler_params=pltpu.CompilerParams(
            dimension_semantics=("parallel", "arbitrary"), vmem_limit_bytes=_vmem(40)),
        name="fox_gate",
    )(x2, wf_t, b_f)


FOX_HEADS_PER_STEP = 4


def _fox_attn_kernel(q_ref, k_ref, v_ref, c_ref, o_ref, *, tile, heads):
    i = pl.program_id(2)
    hd = FOX_HEAD_DIM

    def scores(g, j):
        r0 = pl.multiple_of(j * tile, tile)
        kj = k_ref[pl.ds(r0, tile), g * hd:(g + 1) * hd]
        vj = v_ref[pl.ds(r0, tile), g * hd:(g + 1) * hd]
        s = _dot_nt(q_ref[:, g * hd:(g + 1) * hd], kj) - c_ref[g, j]
        return s, vj

    def update(carry, s, vj):
        m, l, acc = carry
        m_new = jnp.maximum(m, jnp.max(s, axis=-1, keepdims=True))
        a = jnp.exp2(m - m_new)
        p = jnp.exp2(s - m_new)
        l = a * l + jnp.sum(p, axis=-1, keepdims=True)
        acc = a * acc + _dot(p.astype(BF16), vj)
        return m_new, l, acc

    def body(j, carries):
        return tuple(update(carries[g], *scores(g, j)) for g in range(heads))

    init = tuple((jnp.full((tile, 1), NEG_BIG, F32), jnp.zeros((tile, 1), F32),
                  jnp.zeros((tile, hd), F32)) for _ in range(heads))
    carries = lax.fori_loop(0, i, body, init)
    row = lax.broadcasted_iota(jnp.int32, (tile, tile), 0)
    col = lax.broadcasted_iota(jnp.int32, (tile, tile), 1)
    for g in range(heads):
        s, vj = scores(g, i)
        s = jnp.where(row >= col, s, NEG_BIG)
        _, l, acc = update(carries[g], s, vj)
        o_ref[:, g * hd:(g + 1) * hd] = (acc * (1.0 / l)).astype(o_ref.dtype)


def _fox_attention(h, c, batch, seq, tile):
    n = h.shape[0]
    hd = FOX_HEAD_DIM
    nh = FOX_HEADS
    gh = FOX_HEADS_PER_STEP
    ng = nh // gh
    nq = seq // tile
    c4 = c.reshape(batch * nh, nq, 1, tile)
    kern = functools.partial(_fox_attn_kernel, tile=tile, heads=gh)
    return pl.pallas_call(
        kern,
        grid=(batch, ng, nq),
        in_specs=[pl.BlockSpec((tile, gh * hd), lambda b, hh, i: (b * nq + i, hh)),
                  pl.BlockSpec((seq, gh * hd), lambda b, hh, i: (b, ng + hh)),
                  pl.BlockSpec((seq, gh * hd), lambda b, hh, i: (b, 2 * ng + hh)),
                  pl.BlockSpec((gh, nq, 1, tile), lambda b, hh, i: (b * ng + hh, 0, 0, 0))],
        out_specs=pl.BlockSpec((tile, gh * hd), lambda b, hh, i: (b * nq + i, hh)),
        out_shape=jax.ShapeDtypeStruct((n, nh * hd), BF16),
        compiler_params=pltpu.CompilerParams(
            dimension_semantics=("parallel", "parallel", "arbitrary"), vmem_limit_bytes=_vmem(40)),
        name="fox_attention",
    )(h, h, h, c4)


CONV_PAD = 32
CONV_ROWS = 256


def _conv_kernel(a_ref, g_ref, w_ref, cb_ref, ng_ref, nb_ref, o_ref, pad_ref):
    seq = a_ref.shape[0]
    pad_ref[pl.ds(0, CONV_PAD), :] = jnp.zeros((CONV_PAD, LANES), F32)
    pad_ref[pl.ds(CONV_PAD, seq), :] = a_ref[...].astype(F32) * _sigmoid(g_ref[...].astype(F32))
    off = CONV_PAD - (CONV_WIDTH - 1)
    for r in range(seq // CONV_ROWS):
        base = r * CONV_ROWS
        acc = jnp.zeros((CONV_ROWS, LANES), F32)
        for j in range(CONV_WIDTH):
            acc = acc + pad_ref[pl.ds(base + off + j, CONV_ROWS), :] * w_ref[pl.ds(j, 1), :]
        y = acc + cb_ref[...]
        mu = jnp.mean(y, axis=-1, keepdims=True)
        yc = y - mu
        var = jnp.mean(yc * yc, axis=-1, keepdims=True)
        yn = yc * lax.rsqrt(var + LN_EPS) * ng_ref[...] + nb_ref[...]
        o_ref[pl.ds(base, CONV_ROWS), :] = (yn * _sigmoid(yn)).astype(o_ref.dtype)


def _conv_module(h, a_col, g_col, conv_w, conv_b, cn_g, cn_b, batch, seq):
    n = h.shape[0]
    ch = conv_w.shape[1]
    ng = ch // CONV_GROUP
    assert seq % CONV_ROWS == 0
    vec = lambda: pl.BlockSpec((1, CONV_GROUP), lambda b, g: (0, g))
    return pl.pallas_call(
        _conv_kernel,
        grid=(batch, ng),
        in_specs=[pl.BlockSpec((seq, CONV_GROUP), lambda b, g: (b, a_col + g)),
                  pl.BlockSpec((seq, CONV_GROUP), lambda b, g: (b, g_col + g)),
                  pl.BlockSpec((CONV_WIDTH, CONV_GROUP), lambda b, g: (0, g)),
                  vec(), vec(), vec()],
        out_specs=pl.BlockSpec((seq, CONV_GROUP), lambda b, g: (b, g)),
        out_shape=jax.ShapeDtypeStruct((n, ch), BF16),
        scratch_shapes=[pltpu.VMEM((seq + CONV_PAD, LANES), F32)],
        compiler_params=pltpu.CompilerParams(
            dimension_semantics=("parallel", "parallel"), vmem_limit_bytes=_vmem(32)),
        name="conv_module",
    )(h, h, conv_w, conv_b, cn_g, cn_b)


def _layer_norm(y, g, b):
    mu = jnp.mean(y, axis=-1, keepdims=True)
    yc = y - mu
    var = jnp.mean(yc * yc, axis=-1, keepdims=True)
    return yc * lax.rsqrt(var + LN_EPS) * g + b


def _route_rows(xn, wr_hi, wr_lo, rbias, count_ref):
    x_hi, x_lo = _split_bf16(xn)
    logits = _dot_nt(wr_hi, x_hi) + _dot_nt(wr_hi, x_lo) + _dot_nt(wr_lo, x_hi)
    scores = _sigmoid(logits)
    sel = scores + rbias
    s = [sel[e:e + 1, :] for e in range(N_EXPERTS)]
    r = [scores[e:e + 1, :] for e in range(N_EXPERTS)]
    pg = EXPERTS_PER_GROUP

    def top2_sum(vals):
        best = None
        for a in range(len(vals)):
            for b in range(a + 1, len(vals)):
                t = vals[a] + vals[b]
                best = t if best is None else jnp.maximum(best, t)
        return best

    gs = [top2_sum(s[g * pg:(g + 1) * pg]) for g in range(N_GROUPS)]
    best, grp = gs[0], jnp.zeros_like(gs[0], dtype=jnp.int32)
    for g in range(1, N_GROUPS):
        upd = gs[g] > best
        best = jnp.where(upd, gs[g], best)
        grp = jnp.where(upd, g, grp)

    def pick_group(vals, k):
        out = vals[(N_GROUPS - 1) * pg + k]
        for g in range(N_GROUPS - 2, -1, -1):
            out = jnp.where(grp == g, vals[g * pg + k], out)
        return out

    v = [pick_group(s, k) for k in range(pg)]
    w = [pick_group(r, k) for k in range(pg)]
    b1, i1, w1 = v[0], jnp.zeros_like(grp), w[0]
    for k in range(1, pg):
        upd = v[k] > b1
        b1 = jnp.where(upd, v[k], b1)
        i1 = jnp.where(upd, k, i1)
        w1 = jnp.where(upd, w[k], w1)
    b2 = jnp.full_like(b1, -jnp.inf)
    i2, w2 = jnp.zeros_like(grp), jnp.zeros_like(w1)
    for k in range(pg):
        upd = jnp.logical_and(i1 != k, v[k] > b2)
        b2 = jnp.where(upd, v[k], b2)
        i2 = jnp.where(upd, k, i2)
        w2 = jnp.where(upd, w[k], w2)
    tot = w1 + w2
    e1 = grp * pg + i1
    e2 = grp * pg + i2
    eid = lax.broadcasted_iota(jnp.int32, scores.shape, 0)
    oh1 = jnp.where(eid == e1, 1.0, 0.0)
    oh2 = jnp.where(eid == e2, 1.0, 0.0)
    cnt = oh1 + oh2
    tm = cnt.shape[1]
    lane = lax.broadcasted_iota(jnp.int32, cnt.shape, 1)
    incl = cnt
    sh = 1
    while sh < tm:
        incl = incl + jnp.where(lane >= sh, pltpu.roll(incl, sh, axis=1), 0.0)
        sh *= 2
    before = incl - cnt + count_ref[...]
    count_ref[...] = count_ref[...] + incl[:, tm - 1:tm]
    rank1 = jnp.sum(oh1 * before, axis=0, keepdims=True)
    rank2 = jnp.sum(oh2 * before, axis=0, keepdims=True)
    rows = [e1.astype(F32), e2.astype(F32), w1 / tot, w2 / tot, rank1, rank2]
    rows += [jnp.zeros_like(w1)] * (8 - len(rows))
    return jnp.concatenate(rows, axis=0)


def _outproj_kernel(a1_ref, a2_ref, w_ref, x_ref, g_ref, b_ref, wrh_ref, wrl_ref, rb_ref,
                    xn_ref, rt_ref, rc_ref, cnt_ref, count_ref):
    @pl.when(pl.program_id(0) == 0)
    def _():
        count_ref[...] = jnp.zeros_like(count_ref)

    k1 = a1_ref.shape[1]
    k2 = a2_ref.shape[1]
    mix = _dot(a1_ref[...], w_ref[pl.ds(0, k1), :]) + _dot(a2_ref[...], w_ref[pl.ds(k1, k2), :])
    xn = _layer_norm(ALPHA * x_ref[...] + mix, g_ref[...], b_ref[...])
    xn_ref[...] = xn
    rows = _route_rows(xn, wrh_ref[...], wrl_ref[...], rb_ref[...], count_ref)
    rt_ref[...] = rows
    rc_ref[...] = rows.T
    cnt_ref[...] = jnp.broadcast_to(count_ref[...], cnt_ref.shape)


def _outproj_ln_route(a1, a1_col, a2, a2_col, kw, w, xres, ln_g, ln_b, wr_hi, wr_lo, rbias, tm):
    n, d = xres.shape
    full = lambda shape: pl.BlockSpec(shape, lambda i: (0, 0))
    return pl.pallas_call(
        _outproj_kernel,
        grid=(n // tm,),
        in_specs=[pl.BlockSpec((tm, kw), lambda i: (i, a1_col)),
                  pl.BlockSpec((tm, kw), lambda i: (i, a2_col)),
                  full(w.shape),
                  pl.BlockSpec((tm, d), lambda i: (i, 0)),
                  full((1, d)), full((1, d)),
                  full(wr_hi.shape), full(wr_lo.shape), full(rbias.shape)],
        out_specs=[pl.BlockSpec((tm, d), lambda i: (i, 0)),
                   pl.BlockSpec((8, tm), lambda i: (0, i)),
                   pl.BlockSpec((tm, 8), lambda i: (i, 0)),
                   pl.BlockSpec((N_EXPERTS, LANES), lambda i: (0, 0))],
        out_shape=[jax.ShapeDtypeStruct((n, d), F32),
                   jax.ShapeDtypeStruct((8, n), F32),
                   jax.ShapeDtypeStruct((n, 8), F32),
                   jax.ShapeDtypeStruct((N_EXPERTS, LANES), F32)],
        scratch_shapes=[pltpu.VMEM((N_EXPERTS, 1), F32)],
        compiler_params=pltpu.CompilerParams(
            dimension_semantics=("arbitrary",), vmem_limit_bytes=_vmem(52)),
        name="outproj_ln_route",
    )(a1, a2, w, xres, ln_g, ln_b, wr_hi, wr_lo, rbias)


EXPERT_CHUNK_ROWS = 128
EXPERT_STAGES = 4


TOK_WINDOW = 1024


def _expert_kernel(be_ref, na_ref, kb_ref, nk_ref, nx_ref, sl_ref, src_ref,
                   tok_hbm, x_hbm, wg_hbm, wu_hbm, wd_hbm, o_ref,
                   wgu_ref, wdn_ref, stg_ref, sem, xbuf_ref, tok_ref, gsem, tsem, *, layer):
    i = pl.program_id(0)
    tm, d = o_ref.shape
    de = wdn_ref.shape[1]
    na = na_ref[0]

    def tok_copy(j):
        base = pl.multiple_of((src_ref[j] // TOK_WINDOW) * TOK_WINDOW, TOK_WINDOW)
        return pltpu.make_async_copy(tok_hbm.at[pl.ds(base, 2 * TOK_WINDOW)],
                                     tok_ref.at[j % 2], tsem.at[j % 2])

    def start_rows(j):
        off = src_ref[j] % TOK_WINDOW
        buf = j % 2

        def body(r, _):
            tok = tok_ref[buf, off + r]
            pltpu.make_async_copy(x_hbm.at[pl.ds(tok, 1), :], xbuf_ref.at[buf, pl.ds(r, 1), :],
                                  gsem.at[buf]).start()
            return 0
        lax.fori_loop(0, tm, body, 0, unroll=8)

    def wait_rows(j):
        pltpu.make_async_copy(x_hbm.at[pl.ds(0, tm), :], xbuf_ref.at[j % 2], gsem.at[j % 2]).wait()
    ch = EXPERT_CHUNK_ROWS
    n_in = d // ch
    n_dn = de // ch
    n_chunks = 2 * n_in + n_dn

    def chunk_copy(e, c):
        st = c % EXPERT_STAGES

        def gate_up(w_hbm, first, col0):
            r0 = pl.multiple_of((c - first) * ch, ch)
            cp = pltpu.make_async_copy(w_hbm.at[layer, e, pl.ds(r0, ch), :],
                                       stg_ref.at[st, :, pl.ds(0, de)], sem.at[st])

            def convert(slot):
                wgu_ref[slot, pl.ds(r0, ch), pl.ds(col0, de)] = stg_ref[st, :, pl.ds(0, de)].astype(BF16)
            return cp, convert

        def down():
            r0 = pl.multiple_of((c - 2 * n_in) * ch, ch)
            cp = pltpu.make_async_copy(wd_hbm.at[layer, e, pl.ds(r0, ch), :],
                                       stg_ref.at[st], sem.at[st])

            def convert(slot):
                wdn_ref[slot, pl.ds(r0, ch), :] = stg_ref[st].astype(BF16)
            return cp, convert

        return ((c < n_in, lambda: gate_up(wg_hbm, 0, 0)),
                (jnp.logical_and(c >= n_in, c < 2 * n_in), lambda: gate_up(wu_hbm, n_in, de)),
                (c >= 2 * n_in, down))

    def start_chunk(e, c):
        for cond, make in chunk_copy(e, c):
            @pl.when(cond)
            def _():
                make()[0].start()

    def finish_chunk(e, c, slot):
        for cond, make in chunk_copy(e, c):
            @pl.when(cond)
            def _():
                cp, convert = make()
                cp.wait()
                convert(slot)

    def prime(e):
        for c in range(EXPERT_STAGES):
            start_chunk(e, jnp.int32(c))

    def process(e, slot, c_lo, c_hi):
        def body(c, _):
            finish_chunk(e, c, slot)

            @pl.when(c + EXPERT_STAGES < n_chunks)
            def _():
                start_chunk(e, c + EXPERT_STAGES)
            return 0
        lax.fori_loop(c_lo, c_hi, body, 0)

    @pl.when(i >= na)
    def _():
        o_ref[...] = jnp.zeros_like(o_ref)

    @pl.when(i < na)
    def _():
        e = be_ref[i]
        kb = kb_ref[i]
        nk = nk_ref[i]
        nxt = nx_ref[i]
        slot = sl_ref[i]

        @pl.when(i == 0)
        def _():
            tok_copy(0).start()
            prime(e)
            tok_copy(0).wait()
            start_rows(0)

            @pl.when(na > 1)
            def _():
                tok_copy(1).start()
            process(e, slot, 0, n_chunks)

        @pl.when(i + 1 < na)
        def _():
            tok_copy(i + 1).wait()
            start_rows(i + 1)

        @pl.when(i + 2 < na)
        def _():
            tok_copy(i + 2).start()

        @pl.when(jnp.logical_and(kb == 0, nxt >= 0))
        def _():
            prime(nxt)

        wait_rows(i)
        gu = _dot(xbuf_ref[i % 2].astype(BF16), wgu_ref[slot])
        g = gu[:, :de]
        u = gu[:, de:]
        hdn = (g * _sigmoid(g)) * u
        o_ref[...] = _dot(hdn.astype(BF16), wdn_ref[slot])

        @pl.when(nxt >= 0)
        def _():
            process(nxt, 1 - slot, (n_chunks * kb) // nk, (n_chunks * (kb + 1)) // nk)


def _expert_ffn(xn, sorted_tok, tables, w_gate, w_up, w_down, layer, tm, nb):
    d = xn.shape[1]
    de = w_down.shape[2]
    assert d % EXPERT_CHUNK_ROWS == 0 and de % EXPERT_CHUNK_ROWS == 0 and de % LANES == 0
    assert tm <= TOK_WINDOW
    hbm = lambda: pl.BlockSpec(memory_space=pl.ANY)
    return pl.pallas_call(
        functools.partial(_expert_kernel, layer=layer),
        grid_spec=pltpu.PrefetchScalarGridSpec(
            num_scalar_prefetch=len(tables),
            grid=(nb,),
            in_specs=[hbm(), hbm(), hbm(), hbm(), hbm()],
            out_specs=pl.BlockSpec((tm, d), lambda i, *_: (i, 0)),
            scratch_shapes=[pltpu.VMEM((2, d, 2 * de), BF16),
                            pltpu.VMEM((2, de, d), BF16),
                            pltpu.VMEM((EXPERT_STAGES, EXPERT_CHUNK_ROWS, d), F32),
                            pltpu.SemaphoreType.DMA((EXPERT_STAGES,)),
                            pltpu.VMEM((2, tm, d), F32),
                            pltpu.SMEM((2, 2 * TOK_WINDOW), jnp.int32),
                            pltpu.SemaphoreType.DMA((2,)),
                            pltpu.SemaphoreType.DMA((2,))]),
        out_shape=jax.ShapeDtypeStruct((nb * tm, d), F32),
        compiler_params=pltpu.CompilerParams(
            dimension_semantics=("arbitrary",), vmem_limit_bytes=_vmem(60)),
        name="expert_ffn",
    )(*tables, sorted_tok, xn, w_gate, w_up, w_down)


ROUTE_GATE_COL = 2


def _combine_kernel(pos_ref, posn_ref, x_ref, ys_hbm, rc_ref, g_ref, b_ref, xn_ref, xb_ref,
                    ybuf_ref, sem):
    i = pl.program_id(0)
    nt = pl.num_programs(0)
    tm, d = x_ref.shape

    def start_rows(p_ref, buf):
        def body(r, _):
            for k in range(TOP_K):
                pltpu.make_async_copy(ys_hbm.at[pl.ds(p_ref[0, 0, k * tm + r], 1), :],
                                      ybuf_ref.at[buf, k, pl.ds(r, 1), :], sem.at[buf]).start()
            return 0
        lax.fori_loop(0, tm, body, 0, unroll=4)

    @pl.when(i == 0)
    def _():
        start_rows(pos_ref, 0)

    @pl.when(i + 1 < nt)
    def _():
        start_rows(posn_ref, (i + 1) % 2)

    buf = i % 2
    for k in range(TOP_K):
        pltpu.make_async_copy(ys_hbm.at[pl.ds(0, tm), :], ybuf_ref.at[buf, k], sem.at[buf]).wait()
    g0 = rc_ref[:, ROUTE_GATE_COL:ROUTE_GATE_COL + 1]
    g1 = rc_ref[:, ROUTE_GATE_COL + 1:ROUTE_GATE_COL + 2]
    y = ALPHA * x_ref[...] + (g0 * ybuf_ref[buf, 0] + g1 * ybuf_ref[buf, 1])
    xn = _layer_norm(y, g_ref[...], b_ref[...])
    xn_ref[...] = xn
    xb_ref[...] = xn.astype(BF16)


def _combine_ln(x, ys, pos, route_cols, ln_g, ln_b, tm):
    n, d = x.shape
    nt = n // tm
    pos_t = pos.reshape(TOP_K, nt, tm).transpose(1, 0, 2).reshape(nt, 1, TOP_K * tm)
    row = lambda: pl.BlockSpec((tm, d), lambda i: (i, 0))
    vec = lambda: pl.BlockSpec((1, d), lambda i: (0, 0))
    smem = lambda imap: pl.BlockSpec((1, 1, TOP_K * tm), imap, memory_space=pltpu.SMEM)
    return pl.pallas_call(
        _combine_kernel,
        grid=(nt,),
        in_specs=[smem(lambda i: (i, 0, 0)), smem(lambda i: (jnp.minimum(i + 1, nt - 1), 0, 0)),
                  row(), pl.BlockSpec(memory_space=pl.ANY),
                  pl.BlockSpec((tm, 8), lambda i: (i, 0)), vec(), vec()],
        out_specs=[row(), row()],
        out_shape=[jax.ShapeDtypeStruct((n, d), F32), jax.ShapeDtypeStruct((n, d), BF16)],
        scratch_shapes=[pltpu.VMEM((2, TOP_K, tm, d), F32), pltpu.SemaphoreType.DMA((2,))],
        compiler_params=pltpu.CompilerParams(
            dimension_semantics=("arbitrary",), vmem_limit_bytes=_vmem(48)),
        name="combine_ln",
    )(pos_t, pos_t, x, ys, route_cols, ln_g, ln_b)


def _gla_gate_kernel(x_ref, wl_ref, w2_ref, ba_ref, o_ref):
    low = _dot(x_ref[...], wl_ref[...]).astype(BF16)
    z = _dot(low, w2_ref[...]) + ba_ref[...]
    o_ref[...] = _log_sigmoid(z) * (1.0 / GLA_TAU)


def _gla_gate(xb, w_low, w_a2, b_a, tm):
    n, d = xb.shape
    kw = w_a2.shape[1]
    return pl.pallas_call(
        _gla_gate_kernel,
        grid=(n // tm,),
        in_specs=[pl.BlockSpec((tm, d), lambda i: (i, 0)),
                  pl.BlockSpec(w_low.shape, lambda i: (0, 0)),
                  pl.BlockSpec(w_a2.shape, lambda i: (0, 0)),
                  pl.BlockSpec((1, kw), lambda i: (0, 0))],
        out_specs=pl.BlockSpec((tm, kw), lambda i: (i, 0)),
        out_shape=jax.ShapeDtypeStruct((n, kw), F32),
        compiler_params=pltpu.CompilerParams(
            dimension_semantics=("parallel",), vmem_limit_bytes=_vmem(32)),
        name="gla_gate",
    )(xb, w_low, w_a2, b_a)


GLA_ROWS = 512


def _gla_kernel(q_ref, k_ref, v_ref, la_ref, g_ref, ng_ref, o_ref, state_ref, *, scale):
    rows, kw = q_ref.shape
    nh = state_ref.shape[0]
    hk, hv = state_ref.shape[1], state_ref.shape[2]
    ck = GLA_CHUNK

    @pl.when(pl.program_id(1) == 0)
    def _():
        state_ref[...] = jnp.zeros_like(state_ref)

    row = lax.broadcasted_iota(jnp.int32, (ck, ck), 0)
    col = lax.broadcasted_iota(jnp.int32, (ck, ck), 1)
    causal = row >= col
    tri = jnp.where(causal, 1.0, 0.0).astype(BF16)
    ones = jnp.ones((ck, LANES), BF16)

    def head_chunk(hh, r0):
        ks = slice(hh * hk, (hh + 1) * hk)
        vs = slice(hh * hv, (hh + 1) * hv)
        q = q_ref[pl.ds(r0, ck), ks].astype(F32) * scale
        k = k_ref[pl.ds(r0, ck), ks].astype(F32)
        v = v_ref[pl.ds(r0, ck), vs]
        la_hi, la_lo = _split_bf16(la_ref[pl.ds(r0, ck), ks])
        b = _dot(tri, la_hi) + _dot(tri, la_lo)
        b_last = b[ck - 1:ck, :]
        qt = (q * jnp.exp(b)).astype(BF16)
        kt = (k * jnp.exp(-b)).astype(BF16)
        kend = (k * jnp.exp(b_last - b)).astype(BF16)
        attn = jnp.where(causal, _dot_nt(qt, kt), 0.0).astype(BF16)
        state = state_ref[hh]
        o = _dot(attn, v) + _dot(qt, state.astype(BF16))
        dcol = jnp.exp(_dot_tn(la_hi, ones) + _dot_tn(la_lo, ones))
        decay = jnp.concatenate([dcol] * (hv // LANES), axis=1)
        state_ref[hh] = state * decay + _dot_tn(kend, v)
        o = o * lax.rsqrt(jnp.mean(o * o, axis=-1, keepdims=True) + LN_EPS)
        gate = g_ref[pl.ds(r0, ck), vs].astype(F32)
        o = o * ng_ref[:, vs] * (gate * _sigmoid(gate))
        o_ref[pl.ds(r0, ck), vs] = o.astype(o_ref.dtype)

    def chunk(c, _):
        r0 = pl.multiple_of(c * ck, ck)
        for hh in range(nh):
            head_chunk(hh, r0)
        return 0

    lax.fori_loop(0, rows // ck, chunk, 0)


def _gla(h, la, norm_g, batch, seq):
    n = h.shape[0]
    kw = la.shape[1]
    vw = norm_g.shape[1]
    nh = GLA_HEADS
    hk, hv = kw // nh, vw // nh
    rows = GLA_ROWS
    nj = seq // rows
    kern = functools.partial(_gla_kernel, scale=hk ** -0.5)
    rowblk = lambda b, j: b * nj + j
    return pl.pallas_call(
        kern,
        grid=(batch, nj),
        in_specs=[pl.BlockSpec((rows, kw), lambda b, j: (rowblk(b, j), 0)),
                  pl.BlockSpec((rows, kw), lambda b, j: (rowblk(b, j), 1)),
                  pl.BlockSpec((rows, vw), lambda b, j: (rowblk(b, j), 2 * kw // vw)),
                  pl.BlockSpec((rows, kw), lambda b, j: (rowblk(b, j), 0)),
                  pl.BlockSpec((rows, vw), lambda b, j: (rowblk(b, j), 2 * kw // vw + 1)),
                  pl.BlockSpec((1, vw), lambda b, j: (0, 0))],
        out_specs=pl.BlockSpec((rows, vw), lambda b, j: (rowblk(b, j), 0)),
        out_shape=jax.ShapeDtypeStruct((n, vw), BF16),
        scratch_shapes=[pltpu.VMEM((nh, hk, hv), F32)],
        compiler_params=pltpu.CompilerParams(
            dimension_semantics=("parallel", "arbitrary"), vmem_limit_bytes=_vmem(48)),
        name="gla",
    )(h, h, h, la, h, norm_g)


EXPERT_TM = 256
COMBINE_TM = 256


def _moe(xn, route, route_cols, counts, w_gate, w_up, w_down, layer, ln_g, ln_b):
    n, d = xn.shape
    a = n * TOP_K
    tm = EXPERT_TM
    nb = a // tm + N_EXPERTS
    i32 = jnp.int32
    eidx = route[:TOP_K].astype(i32)
    rank = route[2 * TOP_K:3 * TOP_K].astype(i32)
    experts = jnp.arange(N_EXPERTS, dtype=i32)
    counts = counts[:, 0].astype(i32)
    starts = jnp.cumsum(counts) - counts
    nblk = (counts + tm - 1) // tm
    bend = jnp.cumsum(nblk)
    pstart = (bend - nblk) * tm
    n_active = bend[-1:].astype(i32)
    blk = jnp.minimum(jnp.arange(nb, dtype=i32), n_active[0] - 1)
    block_expert = jnp.minimum(jnp.sum((bend[None, :] <= blk[:, None]).astype(i32), axis=1),
                               N_EXPERTS - 1)
    assign = jnp.arange(n, dtype=i32)[None, :] * TOP_K + jnp.arange(TOP_K, dtype=i32)[:, None]
    sorted_tok = (lax.sort((eidx * a + assign).reshape(a)) % a) // TOP_K
    sorted_tok = jnp.concatenate([sorted_tok, jnp.zeros((2 * TOK_WINDOW,), i32)])
    kb = blk - (bend - nblk)[block_expert]
    nonempty = nblk > 0
    later = jnp.logical_and(experts[None, :] > experts[:, None], nonempty[None, :])
    nxt_e = jnp.min(jnp.where(later, experts[None, :], N_EXPERTS), axis=1)
    nxt_e = jnp.where(nxt_e < N_EXPERTS, nxt_e, -1)
    slot_e = (jnp.cumsum(nonempty.astype(i32)) - 1) % 2
    tables = (block_expert, n_active, kb, nblk[block_expert], nxt_e[block_expert],
              slot_e[block_expert], starts[block_expert] + kb * tm)
    ys = _expert_ffn(xn, sorted_tok, tuple(t.astype(i32) for t in tables),
                     w_gate, w_up, w_down, layer, tm, nb)
    pstart_of = jnp.zeros_like(eidx)
    for e in range(N_EXPERTS):
        pstart_of = jnp.where(eidx == e, pstart[e], pstart_of)
    return _combine_ln(xn, ys, pstart_of + rank, route_cols, ln_g, ln_b, COMBINE_TM)


def kernel(x, even_w_in, even_b_f, even_conv_w, even_conv_b, even_conv_norm_g, even_conv_norm_b, even_w_out, odd_w_in, odd_w_a2, odd_b_a, odd_norm_g, odd_w_out, ln_mix_g, ln_mix_b, ln_ffn_g, ln_ffn_b, router_w, router_bias, expert_w_gate, expert_w_up, expert_w_down):
    batch, seq, d = x.shape
    n = batch * seq
    x2 = x.reshape(n, d)
    fw = FOX_HEADS * FOX_HEAD_DIM
    conv_ch = even_conv_w.shape[-1]
    kw = odd_w_a2.shape[-1]
    vw = odd_norm_g.shape[-1]
    row = lambda t: t.reshape(1, -1)

    wr_hi, wr_lo = _split_bf16(router_w.T)
    rbias = router_bias.reshape(N_EXPERTS, 1).astype(F32)
    experts_w = (expert_w_gate, expert_w_up, expert_w_down)

    w_in = even_w_in[0]
    q_scale = LOG2E * FOX_HEAD_DIM ** -0.5
    w_main = jnp.concatenate([w_in[:, :fw] * q_scale, w_in[:, fw:3 * fw],
                              w_in[:, 3 * fw + FOX_HEADS:]], axis=1).astype(BF16)
    wf_t = w_in[:, 3 * fw:3 * fw + FOX_HEADS].T.astype(BF16)
    h = _matmul(x2, w_main, BF16, 1024, 1024)
    c = _fox_gate(x2, wf_t, even_b_f[0].reshape(FOX_HEADS, 1), batch, seq, 512)
    att = _fox_attention(h, c, batch, seq, 256)
    a_col = 3 * fw // CONV_GROUP
    u = _conv_module(h, a_col, a_col + conv_ch // CONV_GROUP, even_conv_w[0, :, 0, :],
                     row(even_conv_b[0]), row(even_conv_norm_g[0]), row(even_conv_norm_b[0]),
                     batch, seq)
    xn, route, rcols, cnt = _outproj_ln_route(att, 0, u, 0, fw, even_w_out[0].astype(BF16), x2,
                                              row(ln_mix_g[0]), row(ln_mix_b[0]),
                                              wr_hi, wr_lo, rbias, 512)
    xn, xb = _moe(xn, route, rcols, cnt, *experts_w, 0, row(ln_ffn_g[0]), row(ln_ffn_b[0]))

    w_in = odd_w_in[0]
    h = _matmul(xb, w_in[:, :2 * kw + 2 * vw].astype(BF16), BF16, 1024, 1024)
    w_low = jnp.zeros((d, LANES), F32).at[:, :GLA_LOW_RANK].set(w_in[:, 2 * kw + 2 * vw:]).astype(BF16)
    w_a2 = jnp.zeros((LANES, kw), F32).at[:GLA_LOW_RANK].set(odd_w_a2[0]).astype(BF16)
    la = _gla_gate(xb, w_low, w_a2, row(odd_b_a[0]), 1024)
    o = _gla(h, la, row(odd_norm_g[0]), batch, seq)
    half = vw // 2
    xn, route, rcols, cnt = _outproj_ln_route(o, 0, o, 1, half, odd_w_out[0].astype(BF16), xn,
                                              row(ln_mix_g[1]), row(ln_mix_b[1]),
                                              wr_hi, wr_lo, rbias, 512)
    xn, xb = _moe(xn, route, rcols, cnt, *experts_w, 1, row(ln_ffn_g[1]), row(ln_ffn_b[1]))
    return xn.reshape(batch, seq, d)
```

```python
import functools

import jax
import jax.numpy as jnp
from jax import lax
from jax.experimental import pallas as pl
from jax.experimental.pallas import tpu as pltpu

F32 = jnp.float32
BF16 = jnp.bfloat16

DEPTH = 2
ALPHA = (2 * DEPTH) ** 0.25
LN_EPS = 1e-5
FOX_HEADS = 8
FOX_HEAD_DIM = 128
CONV_WIDTH = 31
CONV_GROUP = 128
GLA_HEADS = 4
GLA_LOW_RANK = 16
GLA_TAU = 16.0
GLA_CHUNK = 64
N_EXPERTS = 16
N_GROUPS = 4
EXPERTS_PER_GROUP = N_EXPERTS // N_GROUPS
TOP_K = 2

LANES = 128
V7X_VMEM_BYTES = 64 * 1024 * 1024
NEG_BIG = -1e30
LOG2E = 1.4426950408889634


def _vmem(mib):
    assert mib * 1024 * 1024 < V7X_VMEM_BYTES
    return mib * 1024 * 1024


def _sigmoid(z):
    return 1.0 / (1.0 + jnp.exp(-z))


def _log_sigmoid(z):
    return jnp.minimum(z, 0.0) - jnp.log1p(jnp.exp(-jnp.abs(z)))


def _dot(a, b):
    return jnp.dot(a, b, preferred_element_type=F32)


def _dot_nt(a, b):
    return lax.dot_general(a, b, (((1,), (1,)), ((), ())), preferred_element_type=F32)


def _dot_tn(a, b):
    return lax.dot_general(a, b, (((0,), (0,)), ((), ())), preferred_element_type=F32)


def _split_bf16(v):
    hi = v.astype(BF16)
    lo = (v - hi.astype(F32)).astype(BF16)
    return hi, lo


def _mm_kernel(x_ref, w_ref, o_ref):
    o_ref[...] = _dot(x_ref[...].astype(BF16), w_ref[...]).astype(o_ref.dtype)


def _matmul(x, w, out_dtype, tm, tn):
    m, k = x.shape
    n = w.shape[1]
    assert m % tm == 0 and n % tn == 0
    return pl.pallas_call(
        _mm_kernel,
        grid=(m // tm, n // tn),
        in_specs=[pl.BlockSpec((tm, k), lambda i, j: (i, 0)),
                  pl.BlockSpec((k, tn), lambda i, j: (0, j))],
        out_specs=pl.BlockSpec((tm, tn), lambda i, j: (i, j)),
        out_shape=jax.ShapeDtypeStruct((m, n), out_dtype),
        compiler_params=pltpu.CompilerParams(
            dimension_semantics=("parallel", "parallel"), vmem_limit_bytes=_vmem(52)),
        name="dense_proj",
    )(x, w)


def _fox_gate_kernel(x_ref, wf_ref, bf_ref, c_ref, carry_ref):
    j = pl.program_id(1)

    @pl.when(j == 0)
    def _():
        carry_ref[...] = jnp.zeros_like(carry_ref)

    z = _dot_nt(wf_ref[...], x_ref[...].astype(BF16)) + bf_ref[...]
    lf = _log_sigmoid(z)
    ts = lf.shape[1]
    lane = lax.broadcasted_iota(jnp.int32, lf.shape, 1)
    sh = 1
    while sh < ts:
        lf = lf + jnp.where(lane >= sh, pltpu.roll(lf, sh, axis=1), 0.0)
        sh *= 2
    c = lf + carry_ref[...]
    c_ref[0] = c * LOG2E
    carry_ref[...] = c[:, ts - 1:ts]


def _fox_gate(x2, wf_t, b_f, batch, seq, ts):
    heads = wf_t.shape[0]
    d = x2.shape[1]
    nj = seq // ts
    return pl.pallas_call(
        _fox_gate_kernel,
        grid=(batch, nj),
        in_specs=[pl.BlockSpec((ts, d), lambda b, j: (b * nj + j, 0)),
                  pl.BlockSpec((heads, d), lambda b, j: (0, 0)),
                  pl.BlockSpec((heads, 1), lambda b, j: (0, 0))],
        out_specs=pl.BlockSpec((1, heads, ts), lambda b, j: (b, 0, j)),
        out_shape=jax.ShapeDtypeStruct((batch, heads, seq), F32),
        scratch_shapes=[pltpu.VMEM((heads, 1), F32)],
        compiler_params=pltpu.CompilerParams(
            dimension_semantics=("parallel", "arbitrary"), vmem_limit_bytes=_vmem(40)),
        name="fox_gate",
    )(x2, wf_t, b_f)


FOX_HEADS_PER_STEP = 4


def _fox_attn_kernel(q_ref, k_ref, v_ref, c_ref, o_ref, *, tile, heads):
    i = pl.program_id(2)
    hd = FOX_HEAD_DIM

    def scores(g, j):
        r0 = pl.multiple_of(j * tile, tile)
        kj = k_ref[pl.ds(r0, tile), g * hd:(g + 1) * hd]
        vj = v_ref[pl.ds(r0, tile), g * hd:(g + 1) * hd]
        s = _dot_nt(q_ref[:, g * hd:(g + 1) * hd], kj) - c_ref[g, j]
        return s, vj

    def update(carry, s, vj):
        m, l, acc = carry
        m_new = jnp.maximum(m, jnp.max(s, axis=-1, keepdims=True))
        a = jnp.exp2(m - m_new)
        p = jnp.exp2(s - m_new)
        l = a * l + jnp.sum(p, axis=-1, keepdims=True)
        acc = a * acc + _dot(p.astype(BF16), vj)
        return m_new, l, acc

    def body(j, carries):
        return tuple(update(carries[g], *scores(g, j)) for g in range(heads))

    init = tuple((jnp.full((tile, 1), NEG_BIG, F32), jnp.zeros((tile, 1), F32),
                  jnp.zeros((tile, hd), F32)) for _ in range(heads))
    carries = lax.fori_loop(0, i, body, init)
    row = lax.broadcasted_iota(jnp.int32, (tile, tile), 0)
    col = lax.broadcasted_iota(jnp.int32, (tile, tile), 1)
    for g in range(heads):
        s, vj = scores(g, i)
        s = jnp.where(row >= col, s, NEG_BIG)
        _, l, acc = update(carries[g], s, vj)
        o_ref[:, g * hd:(g + 1) * hd] = (acc * (1.0 / l)).astype(o_ref.dtype)


def _fox_attention(h, c, batch, seq, tile):
    n = h.shape[0]
    hd = FOX_HEAD_DIM
    nh = FOX_HEADS
    gh = FOX_HEADS_PER_STEP
    ng = nh // gh
    nq = seq // tile
    c4 = c.reshape(batch * nh, nq, 1, tile)
    kern = functools.partial(_fox_attn_kernel, tile=tile, heads=gh)
    return pl.pallas_call(
        kern,
        grid=(batch, ng, nq),
        in_specs=[pl.BlockSpec((tile, gh * hd), lambda b, hh, i: (b * nq + i, hh)),
                  pl.BlockSpec((seq, gh * hd), lambda b, hh, i: (b, ng + hh)),
                  pl.BlockSpec((seq, gh * hd), lambda b, hh, i: (b, 2 * ng + hh)),
                  pl.BlockSpec((gh, nq, 1, tile), lambda b, hh, i: (b * ng + hh, 0, 0, 0))],
        out_specs=pl.BlockSpec((tile, gh * hd), lambda b, hh, i: (b * nq + i, hh)),
        out_shape=jax.ShapeDtypeStruct((n, nh * hd), BF16),
        compiler_params=pltpu.CompilerParams(
            dimension_semantics=("parallel", "parallel", "arbitrary"), vmem_limit_bytes=_vmem(40)),
        name="fox_attention",
    )(h, h, h, c4)


CONV_PAD = 32
CONV_ROWS = 256


def _conv_kernel(a_ref, g_ref, w_ref, cb_ref, ng_ref, nb_ref, o_ref, pad_ref):
    seq = a_ref.shape[0]
    pad_ref[pl.ds(0, CONV_PAD), :] = jnp.zeros((CONV_PAD, LANES), F32)
    pad_ref[pl.ds(CONV_PAD, seq), :] = a_ref[...].astype(F32) * _sigmoid(g_ref[...].astype(F32))
    off = CONV_PAD - (CONV_WIDTH - 1)
    for r in range(seq // CONV_ROWS):
        base = r * CONV_ROWS
        acc = jnp.zeros((CONV_ROWS, LANES), F32)
        for j in range(CONV_WIDTH):
            acc = acc + pad_ref[pl.ds(base + off + j, CONV_ROWS), :] * w_ref[pl.ds(j, 1), :]
        y = acc + cb_ref[...]
        mu = jnp.mean(y, axis=-1, keepdims=True)
        yc = y - mu
        var = jnp.mean(yc * yc, axis=-1, keepdims=True)
        yn = yc * lax.rsqrt(var + LN_EPS) * ng_ref[...] + nb_ref[...]
        o_ref[pl.ds(base, CONV_ROWS), :] = (yn * _sigmoid(yn)).astype(o_ref.dtype)


def _conv_module(h, a_col, g_col, conv_w, conv_b, cn_g, cn_b, batch, seq):
    n = h.shape[0]
    ch = conv_w.shape[1]
    ng = ch // CONV_GROUP
    assert seq % CONV_ROWS == 0
    vec = lambda: pl.BlockSpec((1, CONV_GROUP), lambda b, g: (0, g))
    return pl.pallas_call(
        _conv_kernel,
        grid=(batch, ng),
        in_specs=[pl.BlockSpec((seq, CONV_GROUP), lambda b, g: (b, a_col + g)),
                  pl.BlockSpec((seq, CONV_GROUP), lambda b, g: (b, g_col + g)),
                  pl.BlockSpec((CONV_WIDTH, CONV_GROUP), lambda b, g: (0, g)),
                  vec(), vec(), vec()],
        out_specs=pl.BlockSpec((seq, CONV_GROUP), lambda b, g: (b, g)),
        out_shape=jax.ShapeDtypeStruct((n, ch), BF16),
        scratch_shapes=[pltpu.VMEM((seq + CONV_PAD, LANES), F32)],
        compiler_params=pltpu.CompilerParams(
            dimension_semantics=("parallel", "parallel"), vmem_limit_bytes=_vmem(32)),
        name="conv_module",
    )(h, h, conv_w, conv_b, cn_g, cn_b)


def _layer_norm(y, g, b):
    mu = jnp.mean(y, axis=-1, keepdims=True)
    yc = y - mu
    var = jnp.mean(yc * yc, axis=-1, keepdims=True)
    return yc * lax.rsqrt(var + LN_EPS) * g + b


def _route_rows(xn, wr_hi, wr_lo, rbias, count_ref):
    x_hi, x_lo = _split_bf16(xn)
    logits = _dot_nt(wr_hi, x_hi) + _dot_nt(wr_hi, x_lo) + _dot_nt(wr_lo, x_hi)
    scores = _sigmoid(logits)
    sel = scores + rbias
    s = [sel[e:e + 1, :] for e in range(N_EXPERTS)]
    r = [scores[e:e + 1, :] for e in range(N_EXPERTS)]
    pg = EXPERTS_PER_GROUP

    def top2_sum(vals):
        best = None
        for a in range(len(vals)):
            for b in range(a + 1, len(vals)):
                t = vals[a] + vals[b]
                best = t if best is None else jnp.maximum(best, t)
        return best

    gs = [top2_sum(s[g * pg:(g + 1) * pg]) for g in range(N_GROUPS)]
    best, grp = gs[0], jnp.zeros_like(gs[0], dtype=jnp.int32)
    for g in range(1, N_GROUPS):
        upd = gs[g] > best
        best = jnp.where(upd, gs[g], best)
        grp = jnp.where(upd, g, grp)

    def pick_group(vals, k):
        out = vals[(N_GROUPS - 1) * pg + k]
        for g in range(N_GROUPS - 2, -1, -1):
            out = jnp.where(grp == g, vals[g * pg + k], out)
        return out

    v = [pick_group(s, k) for k in range(pg)]
    w = [pick_group(r, k) for k in range(pg)]
    b1, i1, w1 = v[0], jnp.zeros_like(grp), w[0]
    for k in range(1, pg):
        upd = v[k] > b1
        b1 = jnp.where(upd, v[k], b1)
        i1 = jnp.where(upd, k, i1)
        w1 = jnp.where(upd, w[k], w1)
    b2 = jnp.full_like(b1, -jnp.inf)
    i2, w2 = jnp.zeros_like(grp), jnp.zeros_like(w1)
    for k in range(pg):
        upd = jnp.logical_and(i1 != k, v[k] > b2)
        b2 = jnp.where(upd, v[k], b2)
        i2 = jnp.where(upd, k, i2)
        w2 = jnp.where(upd, w[k], w2)
    tot = w1 + w2
    e1 = grp * pg + i1
    e2 = grp * pg + i2
    eid = lax.broadcasted_iota(jnp.int32, scores.shape, 0)
    oh1 = jnp.where(eid == e1, 1.0, 0.0)
    oh2 = jnp.where(eid == e2, 1.0, 0.0)
    cnt = oh1 + oh2
    tm = cnt.shape[1]
    lane = lax.broadcasted_iota(jnp.int32, cnt.shape, 1)
    incl = cnt
    sh = 1
    while sh < tm:
        incl = incl + jnp.where(lane >= sh, pltpu.roll(incl, sh, axis=1), 0.0)
        sh *= 2
    before = incl - cnt + count_ref[...]
    count_ref[...] = count_ref[...] + incl[:, tm - 1:tm]
    rank1 = jnp.sum(oh1 * before, axis=0, keepdims=True)
    rank2 = jnp.sum(oh2 * before, axis=0, keepdims=True)
    rows = [e1.astype(F32), e2.astype(F32), w1 / tot, w2 / tot, rank1, rank2]
    rows += [jnp.zeros_like(w1)] * (8 - len(rows))
    return jnp.concatenate(rows, axis=0)


def _outproj_kernel(a1_ref, a2_ref, w_ref, x_ref, g_ref, b_ref, wrh_ref, wrl_ref, rb_ref,
                    xn_ref, rt_ref, rc_ref, cnt_ref, count_ref):
    @pl.when(pl.program_id(0) == 0)
    def _():
        count_ref[...] = jnp.zeros_like(count_ref)

    k1 = a1_ref.shape[1]
    k2 = a2_ref.shape[1]
    mix = _dot(a1_ref[...], w_ref[pl.ds(0, k1), :]) + _dot(a2_ref[...], w_ref[pl.ds(k1, k2), :])
    xn = _layer_norm(ALPHA * x_ref[...] + mix, g_ref[...], b_ref[...])
    xn_ref[...] = xn
    rows = _route_rows(xn, wrh_ref[...], wrl_ref[...], rb_ref[...], count_ref)
    rt_ref[...] = rows
    rc_ref[...] = rows.T
    cnt_ref[...] = jnp.broadcast_to(count_ref[...], cnt_ref.shape)


def _outproj_ln_route(a1, a1_col, a2, a2_col, kw, w, xres, ln_g, ln_b, wr_hi, wr_lo, rbias, tm):
    n, d = xres.shape
    full = lambda shape: pl.BlockSpec(shape, lambda i: (0, 0))
    return pl.pallas_call(
        _outproj_kernel,
        grid=(n // tm,),
        in_specs=[pl.BlockSpec((tm, kw), lambda i: (i, a1_col)),
                  pl.BlockSpec((tm, kw), lambda i: (i, a2_col)),
                  full(w.shape),
                  pl.BlockSpec((tm, d), lambda i: (i, 0)),
                  full((1, d)), full((1, d)),
                  full(wr_hi.shape), full(wr_lo.shape), full(rbias.shape)],
        out_specs=[pl.BlockSpec((tm, d), lambda i: (i, 0)),
                   pl.BlockSpec((8, tm), lambda i: (0, i)),
                   pl.BlockSpec((tm, 8), lambda i: (i, 0)),
                   pl.BlockSpec((N_EXPERTS, LANES), lambda i: (0, 0))],
        out_shape=[jax.ShapeDtypeStruct((n, d), F32),
                   jax.ShapeDtypeStruct((8, n), F32),
                   jax.ShapeDtypeStruct((n, 8), F32),
                   jax.ShapeDtypeStruct((N_EXPERTS, LANES), F32)],
        scratch_shapes=[pltpu.VMEM((N_EXPERTS, 1), F32)],
        compiler_params=pltpu.CompilerParams(
            dimension_semantics=("arbitrary",), vmem_limit_bytes=_vmem(52)),
        name="outproj_ln_route",
    )(a1, a2, w, xres, ln_g, ln_b, wr_hi, wr_lo, rbias)


EXPERT_CHUNK_ROWS = 128
EXPERT_STAGES = 4


TOK_WINDOW = 1024


def _expert_kernel(be_ref, na_ref, kb_ref, nk_ref, nx_ref, sl_ref, src_ref,
                   tok_hbm, x_hbm, wg_hbm, wu_hbm, wd_hbm, o_ref,
                   wgu_ref, wdn_ref, stg_ref, sem, xbuf_ref, tok_ref, gsem, tsem, *, layer):
    i = pl.program_id(0)
    tm, d = o_ref.shape
    de = wdn_ref.shape[1]
    na = na_ref[0]

    def tok_copy(j):
        base = pl.multiple_of((src_ref[j] // TOK_WINDOW) * TOK_WINDOW, TOK_WINDOW)
        dst = pl.multiple_of((j % 2) * (2 * TOK_WINDOW), 2 * TOK_WINDOW)
        return pltpu.make_async_copy(tok_hbm.at[pl.ds(base, 2 * TOK_WINDOW)],
                                     tok_ref.at[pl.ds(dst, 2 * TOK_WINDOW)], tsem.at[j % 2])

    def start_rows(j):
        buf = j % 2
        first = buf * (2 * TOK_WINDOW) + src_ref[j] % TOK_WINDOW
        for r in range(tm):
            tok = tok_ref[first + r]
            pltpu.make_async_copy(x_hbm.at[pl.ds(tok, 1), :], xbuf_ref.at[buf, pl.ds(r, 1), :],
                                  gsem.at[buf]).start()

    def wait_rows(j):
        pltpu.make_async_copy(x_hbm.at[pl.ds(0, tm), :], xbuf_ref.at[j % 2], gsem.at[j % 2]).wait()
    ch = EXPERT_CHUNK_ROWS
    n_in = d // ch
    n_dn = de // ch
    n_chunks = 2 * n_in + n_dn

    def chunk_copy(e, c):
        st = c % EXPERT_STAGES

        def gate_up(w_hbm, first, col0):
            r0 = pl.multiple_of((c - first) * ch, ch)
            cp = pltpu.make_async_copy(w_hbm.at[layer, e, pl.ds(r0, ch), :],
                                       stg_ref.at[st, :, pl.ds(0, de)], sem.at[st])

            def convert(slot):
                wgu_ref[slot, pl.ds(r0, ch), pl.ds(col0, de)] = stg_ref[st, :, pl.ds(0, de)].astype(BF16)
            return cp, convert

        def down():
            r0 = pl.multiple_of((c - 2 * n_in) * ch, ch)
            cp = pltpu.make_async_copy(wd_hbm.at[layer, e, pl.ds(r0, ch), :],
                                       stg_ref.at[st], sem.at[st])

            def convert(slot):
                wdn_ref[slot, pl.ds(r0, ch), :] = stg_ref[st].astype(BF16)
            return cp, convert

        return ((c < n_in, lambda: gate_up(wg_hbm, 0, 0)),
                (jnp.logical_and(c >= n_in, c < 2 * n_in), lambda: gate_up(wu_hbm, n_in, de)),
                (c >= 2 * n_in, down))

    def start_chunk(e, c):
        for cond, make in chunk_copy(e, c):
            @pl.when(cond)
            def _():
                make()[0].start()

    def finish_chunk(e, c, slot):
        for cond, make in chunk_copy(e, c):
            @pl.when(cond)
            def _():
                cp, convert = make()
                cp.wait()
                convert(slot)

    def prime(e):
        for c in range(EXPERT_STAGES):
            start_chunk(e, jnp.int32(c))

    def process(e, slot, c_lo, c_hi):
        def body(c, _):
            finish_chunk(e, c, slot)

            @pl.when(c + EXPERT_STAGES < n_chunks)
            def _():
                start_chunk(e, c + EXPERT_STAGES)
            return 0
        lax.fori_loop(c_lo, c_hi, body, 0)

    @pl.when(i >= na)
    def _():
        o_ref[...] = jnp.zeros_like(o_ref)

    @pl.when(i < na)
    def _():
        e = be_ref[i]
        kb = kb_ref[i]
        nk = nk_ref[i]
        nxt = nx_ref[i]
        slot = sl_ref[i]

        @pl.when(i == 0)
        def _():
            tok_copy(0).start()
            prime(e)
            tok_copy(0).wait()
            start_rows(0)

            @pl.when(na > 1)
            def _():
                tok_copy(1).start()
            process(e, slot, 0, n_chunks)

        @pl.when(jnp.logical_and(kb == 0, nxt >= 0))
        def _():
            prime(nxt)

        @pl.when(i + 1 < na)
        def _():
            tok_copy(i + 1).wait()
            start_rows(i + 1)

        @pl.when(i + 2 < na)
        def _():
            tok_copy(i + 2).start()

        wait_rows(i)
        gu = _dot(xbuf_ref[i % 2].astype(BF16), wgu_ref[slot])
        g = gu[:, :de]
        u = gu[:, de:]
        hdn = (g * _sigmoid(g)) * u
        o_ref[...] = _dot(hdn.astype(BF16), wdn_ref[slot])

        @pl.when(nxt >= 0)
        def _():
            process(nxt, 1 - slot, (n_chunks * kb) // nk, (n_chunks * (kb + 1)) // nk)


def _expert_ffn(xn, sorted_tok, tables, w_gate, w_up, w_down, layer, tm, nb):
    d = xn.shape[1]
    de = w_down.shape[2]
    assert d % EXPERT_CHUNK_ROWS == 0 and de % EXPERT_CHUNK_ROWS == 0 and de % LANES == 0
    assert tm <= TOK_WINDOW
    hbm = lambda: pl.BlockSpec(memory_space=pl.ANY)
    return pl.pallas_call(
        functools.partial(_expert_kernel, layer=layer),
        grid_spec=pltpu.PrefetchScalarGridSpec(
            num_scalar_prefetch=len(tables),
            grid=(nb,),
            in_specs=[hbm(), hbm(), hbm(), hbm(), hbm()],
            out_specs=pl.BlockSpec((tm, d), lambda i, *_: (i, 0)),
            scratch_shapes=[pltpu.VMEM((2, d, 2 * de), BF16),
                            pltpu.VMEM((2, de, d), BF16),
                            pltpu.VMEM((EXPERT_STAGES, EXPERT_CHUNK_ROWS, d), F32),
                            pltpu.SemaphoreType.DMA((EXPERT_STAGES,)),
                            pltpu.VMEM((2, tm, d), F32),
                            pltpu.SMEM((2 * 2 * TOK_WINDOW,), jnp.int32),
                            pltpu.SemaphoreType.DMA((2,)),
                            pltpu.SemaphoreType.DMA((2,))]),
        out_shape=jax.ShapeDtypeStruct((nb * tm, d), F32),
        compiler_params=pltpu.CompilerParams(
            dimension_semantics=("arbitrary",), vmem_limit_bytes=_vmem(60)),
        name="expert_ffn",
    )(*tables, sorted_tok, xn, w_gate, w_up, w_down)


ROUTE_GATE_COL = 2


def _combine_kernel(pos_ref, posn_ref, x_ref, ys_hbm, rc_ref, g_ref, b_ref, xn_ref, xb_ref,
                    ybuf_ref, sem):
    i = pl.program_id(0)
    nt = pl.num_programs(0)
    tm, d = x_ref.shape

    def start_rows(p_ref, buf):
        for r in range(tm):
            for k in range(TOP_K):
                pltpu.make_async_copy(ys_hbm.at[pl.ds(p_ref[0, 0, k * tm + r], 1), :],
                                      ybuf_ref.at[buf, k, pl.ds(r, 1), :], sem.at[buf]).start()

    @pl.when(i == 0)
    def _():
        start_rows(pos_ref, 0)

    @pl.when(i + 1 < nt)
    def _():
        start_rows(posn_ref, (i + 1) % 2)

    buf = i % 2
    for k in range(TOP_K):
        pltpu.make_async_copy(ys_hbm.at[pl.ds(0, tm), :], ybuf_ref.at[buf, k], sem.at[buf]).wait()
    g0 = rc_ref[:, ROUTE_GATE_COL:ROUTE_GATE_COL + 1]
    g1 = rc_ref[:, ROUTE_GATE_COL + 1:ROUTE_GATE_COL + 2]
    y = ALPHA * x_ref[...] + (g0 * ybuf_ref[buf, 0] + g1 * ybuf_ref[buf, 1])
    xn = _layer_norm(y, g_ref[...], b_ref[...])
    xn_ref[...] = xn
    xb_ref[...] = xn.astype(BF16)


def _combine_ln(x, ys, pos, route_cols, ln_g, ln_b, tm):
    n, d = x.shape
    nt = n // tm
    pos_t = pos.reshape(TOP_K, nt, tm).transpose(1, 0, 2).reshape(nt, 1, TOP_K * tm)
    row = lambda: pl.BlockSpec((tm, d), lambda i: (i, 0))
    vec = lambda: pl.BlockSpec((1, d), lambda i: (0, 0))
    smem = lambda imap: pl.BlockSpec((1, 1, TOP_K * tm), imap, memory_space=pltpu.SMEM)
    return pl.pallas_call(
        _combine_kernel,
        grid=(nt,),
        in_specs=[smem(lambda i: (i, 0, 0)), smem(lambda i: (jnp.minimum(i + 1, nt - 1), 0, 0)),
                  row(), pl.BlockSpec(memory_space=pl.ANY),
                  pl.BlockSpec((tm, 8), lambda i: (i, 0)), vec(), vec()],
        out_specs=[row(), row()],
        out_shape=[jax.ShapeDtypeStruct((n, d), F32), jax.ShapeDtypeStruct((n, d), BF16)],
        scratch_shapes=[pltpu.VMEM((2, TOP_K, tm, d), F32), pltpu.SemaphoreType.DMA((2,))],
        compiler_params=pltpu.CompilerParams(
            dimension_semantics=("arbitrary",), vmem_limit_bytes=_vmem(48)),
        name="combine_ln",
    )(pos_t, pos_t, x, ys, route_cols, ln_g, ln_b)


def _gla_gate_kernel(x_ref, wl_ref, w2_ref, ba_ref, o_ref):
    low = _dot(x_ref[...], wl_ref[...]).astype(BF16)
    z = _dot(low, w2_ref[...]) + ba_ref[...]
    o_ref[...] = _log_sigmoid(z) * (1.0 / GLA_TAU)


def _gla_gate(xb, w_low, w_a2, b_a, tm):
    n, d = xb.shape
    kw = w_a2.shape[1]
    return pl.pallas_call(
        _gla_gate_kernel,
        grid=(n // tm,),
        in_specs=[pl.BlockSpec((tm, d), lambda i: (i, 0)),
                  pl.BlockSpec(w_low.shape, lambda i: (0, 0)),
                  pl.BlockSpec(w_a2.shape, lambda i: (0, 0)),
                  pl.BlockSpec((1, kw), lambda i: (0, 0))],
        out_specs=pl.BlockSpec((tm, kw), lambda i: (i, 0)),
        out_shape=jax.ShapeDtypeStruct((n, kw), F32),
        compiler_params=pltpu.CompilerParams(
            dimension_semantics=("parallel",), vmem_limit_bytes=_vmem(32)),
        name="gla_gate",
    )(xb, w_low, w_a2, b_a)


GLA_ROWS = 512


def _gla_kernel(q_ref, k_ref, v_ref, la_ref, g_ref, ng_ref, o_ref, state_ref, *, scale):
    rows, kw = q_ref.shape
    nh = state_ref.shape[0]
    hk, hv = state_ref.shape[1], state_ref.shape[2]
    ck = GLA_CHUNK

    @pl.when(pl.program_id(1) == 0)
    def _():
        state_ref[...] = jnp.zeros_like(state_ref)

    row = lax.broadcasted_iota(jnp.int32, (ck, ck), 0)
    col = lax.broadcasted_iota(jnp.int32, (ck, ck), 1)
    causal = row >= col
    tri = jnp.where(causal, 1.0, 0.0).astype(BF16)
    ones = jnp.ones((ck, LANES), BF16)

    def head_chunk(hh, r0):
        ks = slice(hh * hk, (hh + 1) * hk)
        vs = slice(hh * hv, (hh + 1) * hv)
        q = q_ref[pl.ds(r0, ck), ks].astype(F32) * scale
        k = k_ref[pl.ds(r0, ck), ks].astype(F32)
        v = v_ref[pl.ds(r0, ck), vs]
        la_hi, la_lo = _split_bf16(la_ref[pl.ds(r0, ck), ks])
        b = _dot(tri, la_hi) + _dot(tri, la_lo)
        b_last = b[ck - 1:ck, :]
        qt = (q * jnp.exp(b)).astype(BF16)
        kt = (k * jnp.exp(-b)).astype(BF16)
        kend = (k * jnp.exp(b_last - b)).astype(BF16)
        attn = jnp.where(causal, _dot_nt(qt, kt), 0.0).astype(BF16)
        state = state_ref[hh]
        o = _dot(attn, v) + _dot(qt, state.astype(BF16))
        dcol = jnp.exp(_dot_tn(la_hi, ones) + _dot_tn(la_lo, ones))
        decay = jnp.concatenate([dcol] * (hv // LANES), axis=1)
        state_ref[hh] = state * decay + _dot_tn(kend, v)
        o = o * lax.rsqrt(jnp.mean(o * o, axis=-1, keepdims=True) + LN_EPS)
        gate = g_ref[pl.ds(r0, ck), vs].astype(F32)
        o = o * ng_ref[:, vs] * (gate * _sigmoid(gate))
        o_ref[pl.ds(r0, ck), vs] = o.astype(o_ref.dtype)

    def chunk(c, _):
        r0 = pl.multiple_of(c * ck, ck)
        for hh in range(nh):
            head_chunk(hh, r0)
        return 0

    lax.fori_loop(0, rows // ck, chunk, 0)


def _gla(h, la, norm_g, batch, seq):
    n = h.shape[0]
    kw = la.shape[1]
    vw = norm_g.shape[1]
    nh = GLA_HEADS
    hk, hv = kw // nh, vw // nh
    rows = GLA_ROWS
    nj = seq // rows
    kern = functools.partial(_gla_kernel, scale=hk ** -0.5)
    rowblk = lambda b, j: b * nj + j
    return pl.pallas_call(
        kern,
        grid=(batch, nj),
        in_specs=[pl.BlockSpec((rows, kw), lambda b, j: (rowblk(b, j), 0)),
                  pl.BlockSpec((rows, kw), lambda b, j: (rowblk(b, j), 1)),
                  pl.BlockSpec((rows, vw), lambda b, j: (rowblk(b, j), 2 * kw // vw)),
                  pl.BlockSpec((rows, kw), lambda b, j: (rowblk(b, j), 0)),
                  pl.BlockSpec((rows, vw), lambda b, j: (rowblk(b, j), 2 * kw // vw + 1)),
                  pl.BlockSpec((1, vw), lambda b, j: (0, 0))],
        out_specs=pl.BlockSpec((rows, vw), lambda b, j: (rowblk(b, j), 0)),
        out_shape=jax.ShapeDtypeStruct((n, vw), BF16),
        scratch_shapes=[pltpu.VMEM((nh, hk, hv), F32)],
        compiler_params=pltpu.CompilerParams(
            dimension_semantics=("parallel", "arbitrary"), vmem_limit_bytes=_vmem(48)),
        name="gla",
    )(h, h, h, la, h, norm_g)


EXPERT_TM = 256
COMBINE_TM = 256


def _moe(xn, route, route_cols, counts, w_gate, w_up, w_down, layer, ln_g, ln_b):
    n, d = xn.shape
    a = n * TOP_K
    tm = EXPERT_TM
    nb = a // tm + N_EXPERTS
    i32 = jnp.int32
    eidx = route[:TOP_K].astype(i32)
    rank = route[2 * TOP_K:3 * TOP_K].astype(i32)
    experts = jnp.arange(N_EXPERTS, dtype=i32)
    counts = counts[:, 0].astype(i32)
    starts = jnp.cumsum(counts) - counts
    nblk = (counts + tm - 1) // tm
    bend = jnp.cumsum(nblk)
    pstart = (bend - nblk) * tm
    n_active = bend[-1:].astype(i32)
    blk = jnp.minimum(jnp.arange(nb, dtype=i32), n_active[0] - 1)
    block_expert = jnp.minimum(jnp.sum((bend[None, :] <= blk[:, None]).astype(i32), axis=1),
                               N_EXPERTS - 1)
    assign = jnp.arange(n, dtype=i32)[None, :] * TOP_K + jnp.arange(TOP_K, dtype=i32)[:, None]
    sorted_tok = (lax.sort((eidx * a + assign).reshape(a)) % a) // TOP_K
    sorted_tok = jnp.concatenate([sorted_tok, jnp.zeros((2 * TOK_WINDOW,), i32)])
    kb = blk - (bend - nblk)[block_expert]
    nonempty = nblk > 0
    later = jnp.logical_and(experts[None, :] > experts[:, None], nonempty[None, :])
    nxt_e = jnp.min(jnp.where(later, experts[None, :], N_EXPERTS), axis=1)
    nxt_e = jnp.where(nxt_e < N_EXPERTS, nxt_e, -1)
    slot_e = (jnp.cumsum(nonempty.astype(i32)) - 1) % 2
    tables = (block_expert, n_active, kb, nblk[block_expert], nxt_e[block_expert],
              slot_e[block_expert], starts[block_expert] + kb * tm)
    ys = _expert_ffn(xn, sorted_tok, tuple(t.astype(i32) for t in tables),
                     w_gate, w_up, w_down, layer, tm, nb)
    pstart_of = jnp.zeros_like(eidx)
    for e in range(N_EXPERTS):
        pstart_of = jnp.where(eidx == e, pstart[e], pstart_of)
    return _combine_ln(xn, ys, pstart_of + rank, route_cols, ln_g, ln_b, COMBINE_TM)


def kernel(x, even_w_in, even_b_f, even_conv_w, even_conv_b, even_conv_norm_g, even_conv_norm_b, even_w_out, odd_w_in, odd_w_a2, odd_b_a, odd_norm_g, odd_w_out, ln_mix_g, ln_mix_b, ln_ffn_g, ln_ffn_b, router_w, router_bias, expert_w_gate, expert_w_up, expert_w_down):
    batch, seq, d = x.shape
    n = batch * seq
    x2 = x.reshape(n, d)
    fw = FOX_HEADS * FOX_HEAD_DIM
    conv_ch = even_conv_w.shape[-1]
    kw = odd_w_a2.shape[-1]
    vw = odd_norm_g.shape[-1]
    row = lambda t: t.reshape(1, -1)

    wr_hi, wr_lo = _split_bf16(router_w.T)
    rbias = router_bias.reshape(N_EXPERTS, 1).astype(F32)
    experts_w = (expert_w_gate, expert_w_up, expert_w_down)

    w_in = even_w_in[0]
    q_scale = LOG2E * FOX_HEAD_DIM ** -0.5
    w_main = jnp.concatenate([w_in[:, :fw] * q_scale, w_in[:, fw:3 * fw],
                              w_in[:, 3 * fw + FOX_HEADS:]], axis=1).astype(BF16)
    wf_t = w_in[:, 3 * fw:3 * fw + FOX_HEADS].T.astype(BF16)
    h = _matmul(x2, w_main, BF16, 1024, 1024)
    c = _fox_gate(x2, wf_t, even_b_f[0].reshape(FOX_HEADS, 1), batch, seq, 512)
    att = _fox_attention(h, c, batch, seq, 256)
    a_col = 3 * fw // CONV_GROUP
    u = _conv_module(h, a_col, a_col + conv_ch // CONV_GROUP, even_conv_w[0, :, 0, :],
                     row(even_conv_b[0]), row(even_conv_norm_g[0]), row(even_conv_norm_b[0]),
                     batch, seq)
    xn, route, rcols, cnt = _outproj_ln_route(att, 0, u, 0, fw, even_w_out[0].astype(BF16), x2,
                                              row(ln_mix_g[0]), row(ln_mix_b[0]),
                                              wr_hi, wr_lo, rbias, 512)
    xn, xb = _moe(xn, route, rcols, cnt, *experts_w, 0, row(ln_ffn_g[0]), row(ln_ffn_b[0]))

    w_in = odd_w_in[0]
    h = _matmul(xb, w_in[:, :2 * kw + 2 * vw].astype(BF16), BF16, 1024, 1024)
    w_low = jnp.zeros((d, LANES), F32).at[:, :GLA_LOW_RANK].set(w_in[:, 2 * kw + 2 * vw:]).astype(BF16)
    w_a2 = jnp.zeros((LANES, kw), F32).at[:GLA_LOW_RANK].set(odd_w_a2[0]).astype(BF16)
    la = _gla_gate(xb, w_low, w_a2, row(odd_b_a[0]), 1024)
    o = _gla(h, la, row(odd_norm_g[0]), batch, seq)
    half = vw // 2
    xn, route, rcols, cnt = _outproj_ln_route(o, 0, o, 1, half, odd_w_out[0].astype(BF16), xn,
                                              row(ln_mix_g[1]), row(ln_mix_b[1]),
                                              wr_hi, wr_lo, rbias, 512)
    xn, xb = _moe(xn, route, rcols, cnt, *experts_w, 1, row(ln_ffn_g[1]), row(ln_ffn_b[1]))
    return xn.reshape(batch, seq, d)
```

```python
import functools

import jax
import jax.numpy as jnp
from jax import lax
from jax.experimental import pallas as pl
from jax.experimental.pallas import tpu as pltpu

F32 = jnp.float32
BF16 = jnp.bfloat16

DEPTH = 2
ALPHA = (2 * DEPTH) ** 0.25
LN_EPS = 1e-5
FOX_HEADS = 8
FOX_HEAD_DIM = 128
CONV_WIDTH = 31
CONV_GROUP = 128
GLA_HEADS = 4
GLA_LOW_RANK = 16
GLA_TAU = 16.0
GLA_CHUNK = 64
N_EXPERTS = 16
N_GROUPS = 4
EXPERTS_PER_GROUP = N_EXPERTS // N_GROUPS
TOP_K = 2

LANES = 128
V7X_VMEM_BYTES = 64 * 1024 * 1024
NEG_BIG = -1e30
LOG2E = 1.4426950408889634


def _vmem(mib):
    assert mib * 1024 * 1024 < V7X_VMEM_BYTES
    return mib * 1024 * 1024


def _sigmoid(z):
    return 1.0 / (1.0 + jnp.exp(-z))


def _log_sigmoid(z):
    return jnp.minimum(z, 0.0) - jnp.log1p(jnp.exp(-jnp.abs(z)))


def _dot(a, b):
    return jnp.dot(a, b, preferred_element_type=F32)


def _dot_nt(a, b):
    return lax.dot_general(a, b, (((1,), (1,)), ((), ())), preferred_element_type=F32)


def _dot_tn(a, b):
    return lax.dot_general(a, b, (((0,), (0,)), ((), ())), preferred_element_type=F32)


def _split_bf16(v):
    hi = v.astype(BF16)
    lo = (v - hi.astype(F32)).astype(BF16)
    return hi, lo


def _mm_kernel(x_ref, w_ref, o_ref):
    o_ref[...] = _dot(x_ref[...].astype(BF16), w_ref[...]).astype(o_ref.dtype)


def _matmul(x, w, out_dtype, tm, tn):
    m, k = x.shape
    n = w.shape[1]
    assert m % tm == 0 and n % tn == 0
    return pl.pallas_call(
        _mm_kernel,
        grid=(m // tm, n // tn),
        in_specs=[pl.BlockSpec((tm, k), lambda i, j: (i, 0)),
                  pl.BlockSpec((k, tn), lambda i, j: (0, j))],
        out_specs=pl.BlockSpec((tm, tn), lambda i, j: (i, j)),
        out_shape=jax.ShapeDtypeStruct((m, n), out_dtype),
        compiler_params=pltpu.CompilerParams(
            dimension_semantics=("parallel", "parallel"), vmem_limit_bytes=_vmem(52)),
        name="dense_proj",
    )(x, w)


def _fox_gate_kernel(x_ref, wf_ref, bf_ref, c_ref, carry_ref):
    j = pl.program_id(1)

    @pl.when(j == 0)
    def _():
        carry_ref[...] = jnp.zeros_like(carry_ref)

    z = _dot(x_ref[...].astype(BF16), wf_ref[...]) + bf_ref[...]
    lf = _log_sigmoid(z)
    ts = lf.shape[0]
    row = lax.broadcasted_iota(jnp.int32, lf.shape, 0)
    sh = 1
    while sh < ts:
        lf = lf + jnp.where(row >= sh, pltpu.roll(lf, sh, axis=0), 0.0)
        sh *= 2
    c = lf + carry_ref[...]
    c_ref[...] = c * LOG2E
    carry_ref[...] = c[ts - 1:ts, :]


def _fox_gate(x2, wf, b_f, batch, seq, ts):
    d = x2.shape[1]
    nj = seq // ts
    return pl.pallas_call(
        _fox_gate_kernel,
        grid=(batch, nj),
        in_specs=[pl.BlockSpec((ts, d), lambda b, j: (b * nj + j, 0)),
                  pl.BlockSpec((d, LANES), lambda b, j: (0, 0)),
                  pl.BlockSpec((1, LANES), lambda b, j: (0, 0))],
        out_specs=pl.BlockSpec((ts, LANES), lambda b, j: (b * nj + j, 0)),
        out_shape=jax.ShapeDtypeStruct((batch * seq, LANES), F32),
        scratch_shapes=[pltpu.VMEM((1, LANES), F32)],
        compiler_params=pltpu.CompilerParams(
            dimension_semantics=("parallel", "arbitrary"), vmem_limit_bytes=_vmem(40)),
        name="fox_gate",
    )(x2, wf, b_f)


FOX_HEADS_PER_STEP = 4


def _fox_attn_kernel(q_ref, k_ref, v_ref, c_ref, o_ref, vt_ref, crep_ref, *, tile, heads):
    hh = pl.program_id(1)
    i = pl.program_id(2)
    hd = FOX_HEAD_DIM
    nq = vt_ref.shape[1]

    @pl.when(i == 0)
    def _():
        lane = lax.broadcasted_iota(jnp.int32, (tile, LANES), 1)
        for j in range(nq):
            cj = c_ref[pl.ds(j * tile, tile), :]
            for g in range(heads):
                col = jnp.sum(jnp.where(lane == hh * heads + g, cj, 0.0), axis=1, keepdims=True)
                crep_ref[g, j] = jnp.broadcast_to(col, (tile, LANES))
                vj = v_ref[pl.ds(j * tile, tile), g * hd:(g + 1) * hd]
                vt_ref[g, j] = vj.astype(F32).T.astype(BF16)

    def scores(g, j):
        r0 = pl.multiple_of(j * tile, tile)
        kj = k_ref[pl.ds(r0, tile), g * hd:(g + 1) * hd]
        cj = jnp.concatenate([crep_ref[g, j]] * (tile // LANES), axis=1)
        return _dot_nt(kj, q_ref[:, g * hd:(g + 1) * hd]) - cj

    def update(g, j, carry, st):
        m, l, acc = carry
        m_new = jnp.maximum(m, jnp.max(st, axis=0, keepdims=True))
        a = jnp.exp2(m - m_new)
        p = jnp.exp2(st - m_new)
        l = a * l + jnp.sum(p, axis=0, keepdims=True)
        acc = a * acc + _dot(vt_ref[g, j], p.astype(BF16))
        return m_new, l, acc

    def body(j, carries):
        sts = [scores(g, j) for g in range(heads)]
        return tuple(update(g, j, carries[g], sts[g]) for g in range(heads))

    init = tuple((jnp.full((1, tile), NEG_BIG, F32), jnp.zeros((1, tile), F32),
                  jnp.zeros((hd, tile), F32)) for _ in range(heads))
    carries = lax.fori_loop(0, i, body, init)
    key = lax.broadcasted_iota(jnp.int32, (tile, tile), 0)
    qry = lax.broadcasted_iota(jnp.int32, (tile, tile), 1)
    sts = [jnp.where(key <= qry, scores(g, i), NEG_BIG) for g in range(heads)]
    for g in range(heads):
        _, l, acc = update(g, i, carries[g], sts[g])
        o_ref[:, g * hd:(g + 1) * hd] = (acc * (1.0 / l)).T.astype(o_ref.dtype)


def _fox_attention(h, c, batch, seq, tile):
    n = h.shape[0]
    hd = FOX_HEAD_DIM
    nh = FOX_HEADS
    gh = FOX_HEADS_PER_STEP
    ng = nh // gh
    nq = seq // tile
    kern = functools.partial(_fox_attn_kernel, tile=tile, heads=gh)
    return pl.pallas_call(
        kern,
        grid=(batch, ng, nq),
        in_specs=[pl.BlockSpec((tile, gh * hd), lambda b, hh, i: (b * nq + i, hh)),
                  pl.BlockSpec((seq, gh * hd), lambda b, hh, i: (b, ng + hh)),
                  pl.BlockSpec((seq, gh * hd), lambda b, hh, i: (b, 2 * ng + hh)),
                  pl.BlockSpec((seq, LANES), lambda b, hh, i: (b, 0))],
        out_specs=pl.BlockSpec((tile, gh * hd), lambda b, hh, i: (b * nq + i, hh)),
        out_shape=jax.ShapeDtypeStruct((n, nh * hd), BF16),
        scratch_shapes=[pltpu.VMEM((gh, nq, hd, tile), BF16),
                        pltpu.VMEM((gh, nq, tile, LANES), F32)],
        compiler_params=pltpu.CompilerParams(
            dimension_semantics=("parallel", "parallel", "arbitrary"), vmem_limit_bytes=_vmem(40)),
        name="fox_attention",
    )(h, h, h, c)


CONV_PAD = 32
CONV_ROWS = 256


def _conv_kernel(a_ref, g_ref, w_ref, cb_ref, ng_ref, nb_ref, o_ref, pad_ref):
    seq = a_ref.shape[0]
    pad_ref[pl.ds(0, CONV_PAD), :] = jnp.zeros((CONV_PAD, LANES), F32)
    pad_ref[pl.ds(CONV_PAD, seq), :] = a_ref[...].astype(F32) * _sigmoid(g_ref[...].astype(F32))
    off = CONV_PAD - (CONV_WIDTH - 1)
    for r in range(seq // CONV_ROWS):
        base = r * CONV_ROWS
        acc = jnp.zeros((CONV_ROWS, LANES), F32)
        for j in range(CONV_WIDTH):
            acc = acc + pad_ref[pl.ds(base + off + j, CONV_ROWS), :] * w_ref[pl.ds(j, 1), :]
        y = acc + cb_ref[...]
        mu = jnp.mean(y, axis=-1, keepdims=True)
        yc = y - mu
        var = jnp.mean(yc * yc, axis=-1, keepdims=True)
        yn = yc * lax.rsqrt(var + LN_EPS) * ng_ref[...] + nb_ref[...]
        o_ref[pl.ds(base, CONV_ROWS), :] = (yn * _sigmoid(yn)).astype(o_ref.dtype)


def _conv_module(h, a_col, g_col, conv_w, conv_b, cn_g, cn_b, batch, seq):
    n = h.shape[0]
    ch = conv_w.shape[1]
    ng = ch // CONV_GROUP
    assert seq % CONV_ROWS == 0
    vec = lambda: pl.BlockSpec((1, CONV_GROUP), lambda b, g: (0, g))
    return pl.pallas_call(
        _conv_kernel,
        grid=(batch, ng),
        in_specs=[pl.BlockSpec((seq, CONV_GROUP), lambda b, g: (b, a_col + g)),
                  pl.BlockSpec((seq, CONV_GROUP), lambda b, g: (b, g_col + g)),
                  pl.BlockSpec((CONV_WIDTH, CONV_GROUP), lambda b, g: (0, g)),
                  vec(), vec(), vec()],
        out_specs=pl.BlockSpec((seq, CONV_GROUP), lambda b, g: (b, g)),
        out_shape=jax.ShapeDtypeStruct((n, ch), BF16),
        scratch_shapes=[pltpu.VMEM((seq + CONV_PAD, LANES), F32)],
        compiler_params=pltpu.CompilerParams(
            dimension_semantics=("parallel", "parallel"), vmem_limit_bytes=_vmem(32)),
        name="conv_module",
    )(h, h, conv_w, conv_b, cn_g, cn_b)


def _layer_norm(y, g, b):
    mu = jnp.mean(y, axis=-1, keepdims=True)
    yc = y - mu
    var = jnp.mean(yc * yc, axis=-1, keepdims=True)
    return yc * lax.rsqrt(var + LN_EPS) * g + b


def _route_rows(xn, wr_hi, wr_lo, rbias, count_ref):
    x_hi, x_lo = _split_bf16(xn)
    logits = _dot_nt(wr_hi, x_hi) + _dot_nt(wr_hi, x_lo) + _dot_nt(wr_lo, x_hi)
    scores = _sigmoid(logits)
    sel = scores + rbias
    s = [sel[e:e + 1, :] for e in range(N_EXPERTS)]
    r = [scores[e:e + 1, :] for e in range(N_EXPERTS)]
    pg = EXPERTS_PER_GROUP

    def top2_sum(vals):
        best = None
        for a in range(len(vals)):
            for b in range(a + 1, len(vals)):
                t = vals[a] + vals[b]
                best = t if best is None else jnp.maximum(best, t)
        return best

    gs = [top2_sum(s[g * pg:(g + 1) * pg]) for g in range(N_GROUPS)]
    best, grp = gs[0], jnp.zeros_like(gs[0], dtype=jnp.int32)
    for g in range(1, N_GROUPS):
        upd = gs[g] > best
        best = jnp.where(upd, gs[g], best)
        grp = jnp.where(upd, g, grp)

    def pick_group(vals, k):
        out = vals[(N_GROUPS - 1) * pg + k]
        for g in range(N_GROUPS - 2, -1, -1):
            out = jnp.where(grp == g, vals[g * pg + k], out)
        return out

    v = [pick_group(s, k) for k in range(pg)]
    w = [pick_group(r, k) for k in range(pg)]
    b1, i1, w1 = v[0], jnp.zeros_like(grp), w[0]
    for k in range(1, pg):
        upd = v[k] > b1
        b1 = jnp.where(upd, v[k], b1)
        i1 = jnp.where(upd, k, i1)
        w1 = jnp.where(upd, w[k], w1)
    b2 = jnp.full_like(b1, -jnp.inf)
    i2, w2 = jnp.zeros_like(grp), jnp.zeros_like(w1)
    for k in range(pg):
        upd = jnp.logical_and(i1 != k, v[k] > b2)
        b2 = jnp.where(upd, v[k], b2)
        i2 = jnp.where(upd, k, i2)
        w2 = jnp.where(upd, w[k], w2)
    tot = w1 + w2
    e1 = grp * pg + i1
    e2 = grp * pg + i2
    eid = lax.broadcasted_iota(jnp.int32, scores.shape, 0)
    oh1 = jnp.where(eid == e1, 1.0, 0.0)
    oh2 = jnp.where(eid == e2, 1.0, 0.0)
    cnt = oh1 + oh2
    tm = cnt.shape[1]
    lane = lax.broadcasted_iota(jnp.int32, cnt.shape, 1)
    incl = cnt
    sh = 1
    while sh < tm:
        incl = incl + jnp.where(lane >= sh, pltpu.roll(incl, sh, axis=1), 0.0)
        sh *= 2
    before = incl - cnt + count_ref[...]
    count_ref[...] = count_ref[...] + incl[:, tm - 1:tm]
    rank1 = jnp.sum(oh1 * before, axis=0, keepdims=True)
    rank2 = jnp.sum(oh2 * before, axis=0, keepdims=True)
    rows = [e1.astype(F32), e2.astype(F32), w1 / tot, w2 / tot, rank1, rank2]
    rows += [jnp.zeros_like(w1)] * (8 - len(rows))
    return jnp.concatenate(rows, axis=0)


def _outproj_kernel(a1_ref, a2_ref, w_ref, x_ref, g_ref, b_ref, wrh_ref, wrl_ref, rb_ref,
                    xn_ref, rt_ref, rc_ref, cnt_ref, count_ref):
    @pl.when(pl.program_id(0) == 0)
    def _():
        count_ref[...] = jnp.zeros_like(count_ref)

    k1 = a1_ref.shape[1]
    k2 = a2_ref.shape[1]
    mix = _dot(a1_ref[...], w_ref[pl.ds(0, k1), :]) + _dot(a2_ref[...], w_ref[pl.ds(k1, k2), :])
    xn = _layer_norm(ALPHA * x_ref[...] + mix, g_ref[...], b_ref[...])
    xn_ref[...] = xn
    rows = _route_rows(xn, wrh_ref[...], wrl_ref[...], rb_ref[...], count_ref)
    rt_ref[...] = rows
    rc_ref[...] = rows.T
    cnt_ref[...] = jnp.broadcast_to(count_ref[...], cnt_ref.shape)


def _outproj_ln_route(a1, a1_col, a2, a2_col, kw, w, xres, ln_g, ln_b, wr_hi, wr_lo, rbias, tm):
    n, d = xres.shape
    full = lambda shape: pl.BlockSpec(shape, lambda i: (0, 0))
    return pl.pallas_call(
        _outproj_kernel,
        grid=(n // tm,),
        in_specs=[pl.BlockSpec((tm, kw), lambda i: (i, a1_col)),
                  pl.BlockSpec((tm, kw), lambda i: (i, a2_col)),
                  full(w.shape),
                  pl.BlockSpec((tm, d), lambda i: (i, 0)),
                  full((1, d)), full((1, d)),
                  full(wr_hi.shape), full(wr_lo.shape), full(rbias.shape)],
        out_specs=[pl.BlockSpec((tm, d), lambda i: (i, 0)),
                   pl.BlockSpec((8, tm), lambda i: (0, i)),
                   pl.BlockSpec((tm, 8), lambda i: (i, 0)),
                   pl.BlockSpec((N_EXPERTS, LANES), lambda i: (0, 0))],
        out_shape=[jax.ShapeDtypeStruct((n, d), F32),
                   jax.ShapeDtypeStruct((8, n), F32),
                   jax.ShapeDtypeStruct((n, 8), F32),
                   jax.ShapeDtypeStruct((N_EXPERTS, LANES), F32)],
        scratch_shapes=[pltpu.VMEM((N_EXPERTS, 1), F32)],
        compiler_params=pltpu.CompilerParams(
            dimension_semantics=("arbitrary",), vmem_limit_bytes=_vmem(52)),
        name="outproj_ln_route",
    )(a1, a2, w, xres, ln_g, ln_b, wr_hi, wr_lo, rbias)


EXPERT_CHUNK_ROWS = 128
EXPERT_STAGES = 4


TOK_WINDOW = 1024


def _expert_kernel(be_ref, na_ref, kb_ref, nk_ref, nx_ref, sl_ref, src_ref,
                   tok_hbm, x_hbm, wg_hbm, wu_hbm, wd_hbm, o_ref,
                   wgu_ref, wdn_ref, stg_ref, sem, xbuf_ref, tok_ref, gsem, tsem, *, layer):
    i = pl.program_id(0)
    tm, d = o_ref.shape
    de = wdn_ref.shape[1]
    na = na_ref[0]

    def tok_copy(j):
        base = pl.multiple_of((src_ref[j] // TOK_WINDOW) * TOK_WINDOW, TOK_WINDOW)
        dst = pl.multiple_of((j % 2) * (2 * TOK_WINDOW), 2 * TOK_WINDOW)
        return pltpu.make_async_copy(tok_hbm.at[pl.ds(base, 2 * TOK_WINDOW)],
                                     tok_ref.at[pl.ds(dst, 2 * TOK_WINDOW)], tsem.at[j % 2])

    def start_rows(j):
        buf = j % 2
        first = buf * (2 * TOK_WINDOW) + src_ref[j] % TOK_WINDOW
        for r in range(tm):
            tok = tok_ref[first + r]
            pltpu.make_async_copy(x_hbm.at[pl.ds(tok, 1), :], xbuf_ref.at[buf, pl.ds(r, 1), :],
                                  gsem.at[buf]).start()

    def wait_rows(j):
        pltpu.make_async_copy(x_hbm.at[pl.ds(0, tm), :], xbuf_ref.at[j % 2], gsem.at[j % 2]).wait()
    ch = EXPERT_CHUNK_ROWS
    n_in = d // ch
    n_dn = de // ch
    n_chunks = 2 * n_in + n_dn

    def chunk_copy(e, c):
        st = c % EXPERT_STAGES

        def gate_up(w_hbm, first, col0):
            r0 = pl.multiple_of((c - first) * ch, ch)
            cp = pltpu.make_async_copy(w_hbm.at[layer, e, pl.ds(r0, ch), :],
                                       stg_ref.at[st, :, pl.ds(0, de)], sem.at[st])

            def convert(slot):
                wgu_ref[slot, pl.ds(r0, ch), pl.ds(col0, de)] = stg_ref[st, :, pl.ds(0, de)].astype(BF16)
            return cp, convert

        def down():
            r0 = pl.multiple_of((c - 2 * n_in) * ch, ch)
            cp = pltpu.make_async_copy(wd_hbm.at[layer, e, pl.ds(r0, ch), :],
                                       stg_ref.at[st], sem.at[st])

            def convert(slot):
                wdn_ref[slot, pl.ds(r0, ch), :] = stg_ref[st].astype(BF16)
            return cp, convert

        return ((c < n_in, lambda: gate_up(wg_hbm, 0, 0)),
                (jnp.logical_and(c >= n_in, c < 2 * n_in), lambda: gate_up(wu_hbm, n_in, de)),
                (c >= 2 * n_in, down))

    def start_chunk(e, c):
        for cond, make in chunk_copy(e, c):
            @pl.when(cond)
            def _():
                make()[0].start()

    def finish_chunk(e, c, slot):
        for cond, make in chunk_copy(e, c):
            @pl.when(cond)
            def _():
                cp, convert = make()
                cp.wait()
                convert(slot)

    def prime(e):
        for c in range(EXPERT_STAGES):
            start_chunk(e, jnp.int32(c))

    def process(e, slot, c_lo, c_hi):
        def body(c, _):
            finish_chunk(e, c, slot)

            @pl.when(c + EXPERT_STAGES < n_chunks)
            def _():
                start_chunk(e, c + EXPERT_STAGES)
            return 0
        lax.fori_loop(c_lo, c_hi, body, 0)

    @pl.when(i >= na)
    def _():
        o_ref[...] = jnp.zeros_like(o_ref)

    @pl.when(i < na)
    def _():
        e = be_ref[i]
        kb = kb_ref[i]
        nk = nk_ref[i]
        nxt = nx_ref[i]
        slot = sl_ref[i]

        @pl.when(i == 0)
        def _():
            tok_copy(0).start()
            prime(e)
            tok_copy(0).wait()
            start_rows(0)

            @pl.when(na > 1)
            def _():
                tok_copy(1).start()
            process(e, slot, 0, n_chunks)

        @pl.when(jnp.logical_and(kb == 0, nxt >= 0))
        def _():
            prime(nxt)

        @pl.when(i + 1 < na)
        def _():
            tok_copy(i + 1).wait()
            start_rows(i + 1)

        @pl.when(i + 2 < na)
        def _():
            tok_copy(i + 2).start()

        wait_rows(i)
        gu = _dot(xbuf_ref[i % 2].astype(BF16), wgu_ref[slot])
        g = gu[:, :de]
        u = gu[:, de:]
        hdn = (g * _sigmoid(g)) * u
        o_ref[...] = _dot(hdn.astype(BF16), wdn_ref[slot])

        @pl.when(nxt >= 0)
        def _():
            process(nxt, 1 - slot, (n_chunks * kb) // nk, (n_chunks * (kb + 1)) // nk)


def _expert_ffn(xn, sorted_tok, tables, w_gate, w_up, w_down, layer, tm, nb):
    d = xn.shape[1]
    de = w_down.shape[2]
    assert d % EXPERT_CHUNK_ROWS == 0 and de % EXPERT_CHUNK_ROWS == 0 and de % LANES == 0
    assert tm <= TOK_WINDOW
    hbm = lambda: pl.BlockSpec(memory_space=pl.ANY)
    return pl.pallas_call(
        functools.partial(_expert_kernel, layer=layer),
        grid_spec=pltpu.PrefetchScalarGridSpec(
            num_scalar_prefetch=len(tables),
            grid=(nb,),
            in_specs=[hbm(), hbm(), hbm(), hbm(), hbm()],
            out_specs=pl.BlockSpec((tm, d), lambda i, *_: (i, 0)),
            scratch_shapes=[pltpu.VMEM((2, d, 2 * de), BF16),
                            pltpu.VMEM((2, de, d), BF16),
                            pltpu.VMEM((EXPERT_STAGES, EXPERT_CHUNK_ROWS, d), F32),
                            pltpu.SemaphoreType.DMA((EXPERT_STAGES,)),
                            pltpu.VMEM((2, tm, d), F32),
                            pltpu.SMEM((2 * 2 * TOK_WINDOW,), jnp.int32),
                            pltpu.SemaphoreType.DMA((2,)),
                            pltpu.SemaphoreType.DMA((2,))]),
        out_shape=jax.ShapeDtypeStruct((nb * tm, d), F32),
        compiler_params=pltpu.CompilerParams(
            dimension_semantics=("arbitrary",), vmem_limit_bytes=_vmem(60)),
        name="expert_ffn",
    )(*tables, sorted_tok, xn, w_gate, w_up, w_down)


ROUTE_GATE_COL = 2


def _combine_kernel(pos_ref, posn_ref, x_ref, ys_hbm, rc_ref, g_ref, b_ref, xn_ref, xb_ref,
                    ybuf_ref, sem):
    i = pl.program_id(0)
    nt = pl.num_programs(0)
    tm, d = x_ref.shape

    def start_rows(p_ref, buf):
        for r in range(tm):
            for k in range(TOP_K):
                pltpu.make_async_copy(ys_hbm.at[pl.ds(p_ref[0, 0, k * tm + r], 1), :],
                                      ybuf_ref.at[buf, k, pl.ds(r, 1), :], sem.at[buf]).start()

    @pl.when(i == 0)
    def _():
        start_rows(pos_ref, 0)

    @pl.when(i + 1 < nt)
    def _():
        start_rows(posn_ref, (i + 1) % 2)

    buf = i % 2
    for k in range(TOP_K):
        pltpu.make_async_copy(ys_hbm.at[pl.ds(0, tm), :], ybuf_ref.at[buf, k], sem.at[buf]).wait()
    g0 = rc_ref[:, ROUTE_GATE_COL:ROUTE_GATE_COL + 1]
    g1 = rc_ref[:, ROUTE_GATE_COL + 1:ROUTE_GATE_COL + 2]
    y = ALPHA * x_ref[...] + (g0 * ybuf_ref[buf, 0] + g1 * ybuf_ref[buf, 1])
    xn = _layer_norm(y, g_ref[...], b_ref[...])
    xn_ref[...] = xn
    xb_ref[...] = xn.astype(BF16)


def _combine_ln(x, ys, pos, route_cols, ln_g, ln_b, tm):
    n, d = x.shape
    nt = n // tm
    pos_t = pos.reshape(TOP_K, nt, tm).transpose(1, 0, 2).reshape(nt, 1, TOP_K * tm)
    row = lambda: pl.BlockSpec((tm, d), lambda i: (i, 0))
    vec = lambda: pl.BlockSpec((1, d), lambda i: (0, 0))
    smem = lambda imap: pl.BlockSpec((1, 1, TOP_K * tm), imap, memory_space=pltpu.SMEM)
    return pl.pallas_call(
        _combine_kernel,
        grid=(nt,),
        in_specs=[smem(lambda i: (i, 0, 0)), smem(lambda i: (jnp.minimum(i + 1, nt - 1), 0, 0)),
                  row(), pl.BlockSpec(memory_space=pl.ANY),
                  pl.BlockSpec((tm, 8), lambda i: (i, 0)), vec(), vec()],
        out_specs=[row(), row()],
        out_shape=[jax.ShapeDtypeStruct((n, d), F32), jax.ShapeDtypeStruct((n, d), BF16)],
        scratch_shapes=[pltpu.VMEM((2, TOP_K, tm, d), F32), pltpu.SemaphoreType.DMA((2,))],
        compiler_params=pltpu.CompilerParams(
            dimension_semantics=("arbitrary",), vmem_limit_bytes=_vmem(48)),
        name="combine_ln",
    )(pos_t, pos_t, x, ys, route_cols, ln_g, ln_b)


def _gla_gate_kernel(x_ref, wl_ref, w2_ref, ba_ref, o_ref):
    low = _dot(x_ref[...], wl_ref[...]).astype(BF16)
    z = _dot(low, w2_ref[...]) + ba_ref[...]
    o_ref[...] = _log_sigmoid(z) * (1.0 / GLA_TAU)


def _gla_gate(xb, w_low, w_a2, b_a, tm):
    n, d = xb.shape
    kw = w_a2.shape[1]
    return pl.pallas_call(
        _gla_gate_kernel,
        grid=(n // tm,),
        in_specs=[pl.BlockSpec((tm, d), lambda i: (i, 0)),
                  pl.BlockSpec(w_low.shape, lambda i: (0, 0)),
                  pl.BlockSpec(w_a2.shape, lambda i: (0, 0)),
                  pl.BlockSpec((1, kw), lambda i: (0, 0))],
        out_specs=pl.BlockSpec((tm, kw), lambda i: (i, 0)),
        out_shape=jax.ShapeDtypeStruct((n, kw), F32),
        compiler_params=pltpu.CompilerParams(
            dimension_semantics=("parallel",), vmem_limit_bytes=_vmem(32)),
        name="gla_gate",
    )(xb, w_low, w_a2, b_a)


GLA_ROWS = 512


def _gla_kernel(q_ref, k_ref, v_ref, la_ref, g_ref, ng_ref, o_ref, state_ref, *, scale):
    rows, kw = q_ref.shape
    nh = state_ref.shape[0]
    hk, hv = state_ref.shape[1], state_ref.shape[2]
    ck = GLA_CHUNK

    @pl.when(pl.program_id(1) == 0)
    def _():
        state_ref[...] = jnp.zeros_like(state_ref)

    row = lax.broadcasted_iota(jnp.int32, (ck, ck), 0)
    col = lax.broadcasted_iota(jnp.int32, (ck, ck), 1)
    causal = row >= col
    tri = jnp.where(causal, 1.0, 0.0).astype(BF16)
    ones = jnp.ones((ck, LANES), BF16)

    def chunk(c, _):
        r0 = pl.multiple_of(c * ck, ck)
        heads = range(nh)
        ks = [slice(hh * hk, (hh + 1) * hk) for hh in heads]
        vs = [slice(hh * hv, (hh + 1) * hv) for hh in heads]
        la = [_split_bf16(la_ref[pl.ds(r0, ck), ks[hh]]) for hh in heads]
        b = [_dot(tri, la[hh][0]) + _dot(tri, la[hh][1]) for hh in heads]
        dsum = [_dot_tn(la[hh][0], ones) + _dot_tn(la[hh][1], ones) for hh in heads]
        qt, kt, kend = [], [], []
        for hh in heads:
            q = q_ref[pl.ds(r0, ck), ks[hh]].astype(F32) * scale
            k = k_ref[pl.ds(r0, ck), ks[hh]].astype(F32)
            b_last = b[hh][ck - 1:ck, :]
            qt.append((q * jnp.exp(b[hh])).astype(BF16))
            kt.append((k * jnp.exp(-b[hh])).astype(BF16))
            kend.append((k * jnp.exp(b_last - b[hh])).astype(BF16))
        v = [v_ref[pl.ds(r0, ck), vs[hh]] for hh in heads]
        state = [state_ref[hh] for hh in heads]
        attn = [_dot_nt(qt[hh], kt[hh]) for hh in heads]
        inter = [_dot(qt[hh], state[hh].astype(BF16)) for hh in heads]
        kv = [_dot_tn(kend[hh], v[hh]) for hh in heads]
        for hh in heads:
            decay = jnp.concatenate([jnp.exp(dsum[hh])] * (hv // LANES), axis=1)
            state_ref[hh] = state[hh] * decay + kv[hh]
        for hh in heads:
            o = _dot(jnp.where(causal, attn[hh], 0.0).astype(BF16), v[hh]) + inter[hh]
            o = o * lax.rsqrt(jnp.mean(o * o, axis=-1, keepdims=True) + LN_EPS)
            gate = g_ref[pl.ds(r0, ck), vs[hh]].astype(F32)
            o = o * ng_ref[:, vs[hh]] * (gate * _sigmoid(gate))
            o_ref[pl.ds(r0, ck), vs[hh]] = o.astype(o_ref.dtype)
        return 0

    lax.fori_loop(0, rows // ck, chunk, 0)


def _gla(h, la, norm_g, batch, seq):
    n = h.shape[0]
    kw = la.shape[1]
    vw = norm_g.shape[1]
    nh = GLA_HEADS
    hk, hv = kw // nh, vw // nh
    rows = GLA_ROWS
    nj = seq // rows
    kern = functools.partial(_gla_kernel, scale=hk ** -0.5)
    rowblk = lambda b, j: b * nj + j
    return pl.pallas_call(
        kern,
        grid=(batch, nj),
        in_specs=[pl.BlockSpec((rows, kw), lambda b, j: (rowblk(b, j), 0)),
                  pl.BlockSpec((rows, kw), lambda b, j: (rowblk(b, j), 1)),
                  pl.BlockSpec((rows, vw), lambda b, j: (rowblk(b, j), 2 * kw // vw)),
                  pl.BlockSpec((rows, kw), lambda b, j: (rowblk(b, j), 0)),
                  pl.BlockSpec((rows, vw), lambda b, j: (rowblk(b, j), 2 * kw // vw + 1)),
                  pl.BlockSpec((1, vw), lambda b, j: (0, 0))],
        out_specs=pl.BlockSpec((rows, vw), lambda b, j: (rowblk(b, j), 0)),
        out_shape=jax.ShapeDtypeStruct((n, vw), BF16),
        scratch_shapes=[pltpu.VMEM((nh, hk, hv), F32)],
        compiler_params=pltpu.CompilerParams(
            dimension_semantics=("parallel", "arbitrary"), vmem_limit_bytes=_vmem(48)),
        name="gla",
    )(h, h, h, la, h, norm_g)


EXPERT_TM = 256
COMBINE_TM = 256


def _moe(xn, route, route_cols, counts, w_gate, w_up, w_down, layer, ln_g, ln_b):
    n, d = xn.shape
    a = n * TOP_K
    tm = EXPERT_TM
    nb = a // tm + N_EXPERTS
    i32 = jnp.int32
    eidx = route[:TOP_K].astype(i32)
    rank = route[2 * TOP_K:3 * TOP_K].astype(i32)
    experts = jnp.arange(N_EXPERTS, dtype=i32)
    counts = counts[:, 0].astype(i32)
    starts = jnp.cumsum(counts) - counts
    nblk = (counts + tm - 1) // tm
    bend = jnp.cumsum(nblk)
    pstart = (bend - nblk) * tm
    n_active = bend[-1:].astype(i32)
    blk = jnp.minimum(jnp.arange(nb, dtype=i32), n_active[0] - 1)
    block_expert = jnp.minimum(jnp.sum((bend[None, :] <= blk[:, None]).astype(i32), axis=1),
                               N_EXPERTS - 1)
    assign = jnp.arange(n, dtype=i32)[None, :] * TOP_K + jnp.arange(TOP_K, dtype=i32)[:, None]
    sorted_tok = (lax.sort((eidx * a + assign).reshape(a)) % a) // TOP_K
    sorted_tok = jnp.concatenate([sorted_tok, jnp.zeros((2 * TOK_WINDOW,), i32)])
    kb = blk - (bend - nblk)[block_expert]
    nonempty = nblk > 0
    later = jnp.logical_and(experts[None, :] > experts[:, None], nonempty[None, :])
    nxt_e = jnp.min(jnp.where(later, experts[None, :], N_EXPERTS), axis=1)
    nxt_e = jnp.where(nxt_e < N_EXPERTS, nxt_e, -1)
    slot_e = (jnp.cumsum(nonempty.astype(i32)) - 1) % 2
    tables = (block_expert, n_active, kb, nblk[block_expert], nxt_e[block_expert],
              slot_e[block_expert], starts[block_expert] + kb * tm)
    ys = _expert_ffn(xn, sorted_tok, tuple(t.astype(i32) for t in tables),
                     w_gate, w_up, w_down, layer, tm, nb)
    pstart_of = jnp.zeros_like(eidx)
    for e in range(N_EXPERTS):
        pstart_of = jnp.where(eidx == e, pstart[e], pstart_of)
    return _combine_ln(xn, ys, pstart_of + rank, route_cols, ln_g, ln_b, COMBINE_TM)


def kernel(x, even_w_in, even_b_f, even_conv_w, even_conv_b, even_conv_norm_g, even_conv_norm_b, even_w_out, odd_w_in, odd_w_a2, odd_b_a, odd_norm_g, odd_w_out, ln_mix_g, ln_mix_b, ln_ffn_g, ln_ffn_b, router_w, router_bias, expert_w_gate, expert_w_up, expert_w_down):
    batch, seq, d = x.shape
    n = batch * seq
    x2 = x.reshape(n, d)
    fw = FOX_HEADS * FOX_HEAD_DIM
    conv_ch = even_conv_w.shape[-1]
    kw = odd_w_a2.shape[-1]
    vw = odd_norm_g.shape[-1]
    row = lambda t: t.reshape(1, -1)

    wr_hi, wr_lo = _split_bf16(router_w.T)
    rbias = router_bias.reshape(N_EXPERTS, 1).astype(F32)
    experts_w = (expert_w_gate, expert_w_up, expert_w_down)

    w_in = even_w_in[0]
    q_scale = LOG2E * FOX_HEAD_DIM ** -0.5
    w_main = jnp.concatenate([w_in[:, :fw] * q_scale, w_in[:, fw:3 * fw],
                              w_in[:, 3 * fw + FOX_HEADS:]], axis=1).astype(BF16)
    wf = jnp.zeros((d, LANES), F32).at[:, :FOX_HEADS].set(w_in[:, 3 * fw:3 * fw + FOX_HEADS])
    b_f = jnp.zeros((1, LANES), F32).at[0, :FOX_HEADS].set(even_b_f[0])
    h = _matmul(x2, w_main, BF16, 1024, 1024)
    c = _fox_gate(x2, wf.astype(BF16), b_f, batch, seq, 512)
    att = _fox_attention(h, c, batch, seq, 256)
    a_col = 3 * fw // CONV_GROUP
    u = _conv_module(h, a_col, a_col + conv_ch // CONV_GROUP, even_conv_w[0, :, 0, :],
                     row(even_conv_b[0]), row(even_conv_norm_g[0]), row(even_conv_norm_b[0]),
                     batch, seq)
    xn, route, rcols, cnt = _outproj_ln_route(att, 0, u, 0, fw, even_w_out[0].astype(BF16), x2,
                                              row(ln_mix_g[0]), row(ln_mix_b[0]),
                                              wr_hi, wr_lo, rbias, 512)
    xn, xb = _moe(xn, route, rcols, cnt, *experts_w, 0, row(ln_ffn_g[0]), row(ln_ffn_b[0]))

    w_in = odd_w_in[0]
    h = _matmul(xb, w_in[:, :2 * kw + 2 * vw].astype(BF16), BF16, 1024, 1024)
    w_low = jnp.zeros((d, LANES), F32).at[:, :GLA_LOW_RANK].set(w_in[:, 2 * kw + 2 * vw:]).astype(BF16)
    w_a2 = jnp.zeros((LANES, kw), F32).at[:GLA_LOW_RANK].set(odd_w_a2[0]).astype(BF16)
    la = _gla_gate(xb, w_low, w_a2, row(odd_b_a[0]), 1024)
    o = _gla(h, la, row(odd_norm_g[0]), batch, seq)
    half = vw // 2
    xn, route, rcols, cnt = _outproj_ln_route(o, 0, o, 1, half, odd_w_out[0].astype(BF16), xn,
                                              row(ln_mix_g[1]), row(ln_mix_b[1]),
                                              wr_hi, wr_lo, rbias, 512)
    xn, xb = _moe(xn, route, rcols, cnt, *experts_w, 1, row(ln_ffn_g[1]), row(ln_ffn_b[1]))
    return xn.reshape(batch, seq, d)
```

```python
import functools

import jax
import jax.numpy as jnp
from jax import lax
from jax.experimental import pallas as pl
from jax.experimental.pallas import tpu as pltpu

F32 = jnp.float32
BF16 = jnp.bfloat16

DEPTH = 2
ALPHA = (2 * DEPTH) ** 0.25
LN_EPS = 1e-5
FOX_HEADS = 8
FOX_HEAD_DIM = 128
CONV_WIDTH = 31
CONV_GROUP = 128
GLA_HEADS = 4
GLA_LOW_RANK = 16
GLA_TAU = 16.0
GLA_CHUNK = 64
N_EXPERTS = 16
N_GROUPS = 4
EXPERTS_PER_GROUP = N_EXPERTS // N_GROUPS
TOP_K = 2

LANES = 128
V7X_VMEM_BYTES = 64 * 1024 * 1024
NEG_BIG = -1e30
LOG2E = 1.4426950408889634


def _vmem(mib):
    assert mib * 1024 * 1024 < V7X_VMEM_BYTES
    return mib * 1024 * 1024


def _sigmoid(z):
    return 1.0 / (1.0 + jnp.exp(-z))


def _log_sigmoid(z):
    return jnp.minimum(z, 0.0) - jnp.log1p(jnp.exp(-jnp.abs(z)))


def _dot(a, b):
    return jnp.dot(a, b, preferred_element_type=F32)


def _dot_nt(a, b):
    return lax.dot_general(a, b, (((1,), (1,)), ((), ())), preferred_element_type=F32)


def _dot_tn(a, b):
    return lax.dot_general(a, b, (((0,), (0,)), ((), ())), preferred_element_type=F32)


def _split_bf16(v):
    hi = v.astype(BF16)
    lo = (v - hi.astype(F32)).astype(BF16)
    return hi, lo


def _mm_kernel(x_ref, w_ref, o_ref):
    o_ref[...] = _dot(x_ref[...].astype(BF16), w_ref[...]).astype(o_ref.dtype)


def _matmul(x, w, out_dtype, tm, tn):
    m, k = x.shape
    n = w.shape[1]
    assert m % tm == 0 and n % tn == 0
    return pl.pallas_call(
        _mm_kernel,
        grid=(m // tm, n // tn),
        in_specs=[pl.BlockSpec((tm, k), lambda i, j: (i, 0)),
                  pl.BlockSpec((k, tn), lambda i, j: (0, j))],
        out_specs=pl.BlockSpec((tm, tn), lambda i, j: (i, j)),
        out_shape=jax.ShapeDtypeStruct((m, n), out_dtype),
        compiler_params=pltpu.CompilerParams(
            dimension_semantics=("parallel", "parallel"), vmem_limit_bytes=_vmem(52)),
        name="dense_proj",
    )(x, w)


def _fox_gate_kernel(x_ref, wf_ref, bf_ref, c_ref, carry_ref):
    j = pl.program_id(1)

    @pl.when(j == 0)
    def _():
        carry_ref[...] = jnp.zeros_like(carry_ref)

    z = _dot(x_ref[...].astype(BF16), wf_ref[...]) + bf_ref[...]
    lf = _log_sigmoid(z)
    ts = lf.shape[0]
    row = lax.broadcasted_iota(jnp.int32, lf.shape, 0)
    sh = 1
    while sh < ts:
        lf = lf + jnp.where(row >= sh, pltpu.roll(lf, sh, axis=0), 0.0)
        sh *= 2
    c = lf + carry_ref[...]
    c_ref[...] = c * LOG2E
    carry_ref[...] = c[ts - 1:ts, :]


def _fox_gate(x2, wf, b_f, batch, seq, ts):
    d = x2.shape[1]
    nj = seq // ts
    return pl.pallas_call(
        _fox_gate_kernel,
        grid=(batch, nj),
        in_specs=[pl.BlockSpec((ts, d), lambda b, j: (b * nj + j, 0)),
                  pl.BlockSpec((d, LANES), lambda b, j: (0, 0)),
                  pl.BlockSpec((1, LANES), lambda b, j: (0, 0))],
        out_specs=pl.BlockSpec((ts, LANES), lambda b, j: (b * nj + j, 0)),
        out_shape=jax.ShapeDtypeStruct((batch * seq, LANES), F32),
        scratch_shapes=[pltpu.VMEM((1, LANES), F32)],
        compiler_params=pltpu.CompilerParams(
            dimension_semantics=("parallel", "arbitrary"), vmem_limit_bytes=_vmem(40)),
        name="fox_gate",
    )(x2, wf, b_f)


FOX_HEADS_PER_STEP = 8


def _fox_attn_kernel(q_ref, k_ref, v_ref, c_ref, o_ref, vt_ref, crep_ref, *, tile, heads):
    hh = pl.program_id(1)
    i = pl.program_id(2)
    hd = FOX_HEAD_DIM
    nq = vt_ref.shape[1]

    @pl.when(i == 0)
    def _():
        lane = lax.broadcasted_iota(jnp.int32, (tile, LANES), 1)
        for j in range(nq):
            cj = c_ref[pl.ds(j * tile, tile), :]
            for g in range(heads):
                col = jnp.sum(jnp.where(lane == hh * heads + g, cj, 0.0), axis=1, keepdims=True)
                crep_ref[g, j] = jnp.broadcast_to(col, (tile, LANES))
                vj = v_ref[pl.ds(j * tile, tile), g * hd:(g + 1) * hd]
                vt_ref[g, j] = vj.astype(F32).T.astype(BF16)

    def scores(g, j):
        r0 = pl.multiple_of(j * tile, tile)
        kj = k_ref[pl.ds(r0, tile), g * hd:(g + 1) * hd]
        cj = jnp.concatenate([crep_ref[g, j]] * (tile // LANES), axis=1)
        return _dot_nt(kj, q_ref[:, g * hd:(g + 1) * hd]) - cj

    def update(g, j, carry, st):
        m, l, acc = carry
        m_new = jnp.maximum(m, jnp.max(st, axis=0, keepdims=True))
        a = jnp.exp2(m - m_new)
        p = jnp.exp2(st - m_new)
        l = a * l + jnp.sum(p, axis=0, keepdims=True)
        acc = a * acc + _dot(vt_ref[g, j], p.astype(BF16))
        return m_new, l, acc

    def body(j, carries):
        sts = [scores(g, j) for g in range(heads)]
        return tuple(update(g, j, carries[g], sts[g]) for g in range(heads))

    init = tuple((jnp.full((1, tile), NEG_BIG, F32), jnp.zeros((1, tile), F32),
                  jnp.zeros((hd, tile), F32)) for _ in range(heads))
    carries = lax.fori_loop(0, i, body, init)
    key = lax.broadcasted_iota(jnp.int32, (tile, tile), 0)
    qry = lax.broadcasted_iota(jnp.int32, (tile, tile), 1)
    sts = [jnp.where(key <= qry, scores(g, i), NEG_BIG) for g in range(heads)]
    for g in range(heads):
        _, l, acc = update(g, i, carries[g], sts[g])
        o_ref[:, g * hd:(g + 1) * hd] = (acc * (1.0 / l)).T.astype(o_ref.dtype)


def _fox_attention(h, c, batch, seq, tile):
    n = h.shape[0]
    hd = FOX_HEAD_DIM
    nh = FOX_HEADS
    gh = FOX_HEADS_PER_STEP
    ng = nh // gh
    nq = seq // tile
    kern = functools.partial(_fox_attn_kernel, tile=tile, heads=gh)
    return pl.pallas_call(
        kern,
        grid=(batch, ng, nq),
        in_specs=[pl.BlockSpec((tile, gh * hd), lambda b, hh, i: (b * nq + i, hh)),
                  pl.BlockSpec((seq, gh * hd), lambda b, hh, i: (b, ng + hh)),
                  pl.BlockSpec((seq, gh * hd), lambda b, hh, i: (b, 2 * ng + hh)),
                  pl.BlockSpec((seq, LANES), lambda b, hh, i: (b, 0))],
        out_specs=pl.BlockSpec((tile, gh * hd), lambda b, hh, i: (b * nq + i, hh)),
        out_shape=jax.ShapeDtypeStruct((n, nh * hd), BF16),
        scratch_shapes=[pltpu.VMEM((gh, nq, hd, tile), BF16),
                        pltpu.VMEM((gh, nq, tile, LANES), F32)],
        compiler_params=pltpu.CompilerParams(
            dimension_semantics=("parallel", "parallel", "arbitrary"), vmem_limit_bytes=_vmem(40)),
        name="fox_attention",
    )(h, h, h, c)


CONV_PAD = 32
CONV_ROWS = 256


def _conv_kernel(a_ref, g_ref, w_ref, cb_ref, ng_ref, nb_ref, o_ref, pad_ref):
    seq = a_ref.shape[0]
    pad_ref[pl.ds(0, CONV_PAD), :] = jnp.zeros((CONV_PAD, LANES), F32)
    pad_ref[pl.ds(CONV_PAD, seq), :] = a_ref[...].astype(F32) * _sigmoid(g_ref[...].astype(F32))
    off = CONV_PAD - (CONV_WIDTH - 1)
    for r in range(seq // CONV_ROWS):
        base = r * CONV_ROWS
        acc = jnp.zeros((CONV_ROWS, LANES), F32)
        for j in range(CONV_WIDTH):
            acc = acc + pad_ref[pl.ds(base + off + j, CONV_ROWS), :] * w_ref[pl.ds(j, 1), :]
        y = acc + cb_ref[...]
        mu = jnp.mean(y, axis=-1, keepdims=True)
        yc = y - mu
        var = jnp.mean(yc * yc, axis=-1, keepdims=True)
        yn = yc * lax.rsqrt(var + LN_EPS) * ng_ref[...] + nb_ref[...]
        o_ref[pl.ds(base, CONV_ROWS), :] = (yn * _sigmoid(yn)).astype(o_ref.dtype)


def _conv_module(h, a_col, g_col, conv_w, conv_b, cn_g, cn_b, batch, seq):
    n = h.shape[0]
    ch = conv_w.shape[1]
    ng = ch // CONV_GROUP
    assert seq % CONV_ROWS == 0
    vec = lambda: pl.BlockSpec((1, CONV_GROUP), lambda b, g: (0, g))
    return pl.pallas_call(
        _conv_kernel,
        grid=(batch, ng),
        in_specs=[pl.BlockSpec((seq, CONV_GROUP), lambda b, g: (b, a_col + g)),
                  pl.BlockSpec((seq, CONV_GROUP), lambda b, g: (b, g_col + g)),
                  pl.BlockSpec((CONV_WIDTH, CONV_GROUP), lambda b, g: (0, g)),
                  vec(), vec(), vec()],
        out_specs=pl.BlockSpec((seq, CONV_GROUP), lambda b, g: (b, g)),
        out_shape=jax.ShapeDtypeStruct((n, ch), BF16),
        scratch_shapes=[pltpu.VMEM((seq + CONV_PAD, LANES), F32)],
        compiler_params=pltpu.CompilerParams(
            dimension_semantics=("parallel", "parallel"), vmem_limit_bytes=_vmem(32)),
        name="conv_module",
    )(h, h, conv_w, conv_b, cn_g, cn_b)


def _layer_norm(y, g, b):
    mu = jnp.mean(y, axis=-1, keepdims=True)
    yc = y - mu
    var = jnp.mean(yc * yc, axis=-1, keepdims=True)
    return yc * lax.rsqrt(var + LN_EPS) * g + b


def _route_rows(xn, wr_hi, wr_lo, rbias, count_ref):
    x_hi, x_lo = _split_bf16(xn)
    logits = _dot_nt(wr_hi, x_hi) + _dot_nt(wr_hi, x_lo) + _dot_nt(wr_lo, x_hi)
    scores = _sigmoid(logits)
    sel = scores + rbias
    s = [sel[e:e + 1, :] for e in range(N_EXPERTS)]
    r = [scores[e:e + 1, :] for e in range(N_EXPERTS)]
    pg = EXPERTS_PER_GROUP

    def top2_sum(vals):
        best = None
        for a in range(len(vals)):
            for b in range(a + 1, len(vals)):
                t = vals[a] + vals[b]
                best = t if best is None else jnp.maximum(best, t)
        return best

    gs = [top2_sum(s[g * pg:(g + 1) * pg]) for g in range(N_GROUPS)]
    best, grp = gs[0], jnp.zeros_like(gs[0], dtype=jnp.int32)
    for g in range(1, N_GROUPS):
        upd = gs[g] > best
        best = jnp.where(upd, gs[g], best)
        grp = jnp.where(upd, g, grp)

    def pick_group(vals, k):
        out = vals[(N_GROUPS - 1) * pg + k]
        for g in range(N_GROUPS - 2, -1, -1):
            out = jnp.where(grp == g, vals[g * pg + k], out)
        return out

    v = [pick_group(s, k) for k in range(pg)]
    w = [pick_group(r, k) for k in range(pg)]
    b1, i1, w1 = v[0], jnp.zeros_like(grp), w[0]
    for k in range(1, pg):
        upd = v[k] > b1
        b1 = jnp.where(upd, v[k], b1)
        i1 = jnp.where(upd, k, i1)
        w1 = jnp.where(upd, w[k], w1)
    b2 = jnp.full_like(b1, -jnp.inf)
    i2, w2 = jnp.zeros_like(grp), jnp.zeros_like(w1)
    for k in range(pg):
        upd = jnp.logical_and(i1 != k, v[k] > b2)
        b2 = jnp.where(upd, v[k], b2)
        i2 = jnp.where(upd, k, i2)
        w2 = jnp.where(upd, w[k], w2)
    tot = w1 + w2
    e1 = grp * pg + i1
    e2 = grp * pg + i2
    eid = lax.broadcasted_iota(jnp.int32, scores.shape, 0)
    oh1 = jnp.where(eid == e1, 1.0, 0.0)
    oh2 = jnp.where(eid == e2, 1.0, 0.0)
    cnt = oh1 + oh2
    tm = cnt.shape[1]
    lane = lax.broadcasted_iota(jnp.int32, cnt.shape, 1)
    incl = cnt
    sh = 1
    while sh < tm:
        incl = incl + jnp.where(lane >= sh, pltpu.roll(incl, sh, axis=1), 0.0)
        sh *= 2
    before = incl - cnt + count_ref[...]
    count_ref[...] = count_ref[...] + incl[:, tm - 1:tm]
    rank1 = jnp.sum(oh1 * before, axis=0, keepdims=True)
    rank2 = jnp.sum(oh2 * before, axis=0, keepdims=True)
    rows = [e1.astype(F32), e2.astype(F32), w1 / tot, w2 / tot, rank1, rank2]
    rows += [jnp.zeros_like(w1)] * (8 - len(rows))
    return jnp.concatenate(rows, axis=0)


OUTPROJ_SUBTILES = 2


def _outproj_kernel(a1_ref, a2_ref, w_ref, x_ref, g_ref, b_ref, wrh_ref, wrl_ref, rb_ref,
                    xn_ref, rt_ref, rc_ref, cnt_ref, count_ref):
    @pl.when(pl.program_id(0) == 0)
    def _():
        count_ref[...] = jnp.zeros_like(count_ref)

    k1 = a1_ref.shape[1]
    k2 = a2_ref.shape[1]
    tm = x_ref.shape[0]
    sub = tm // OUTPROJ_SUBTILES
    rs = [pl.ds(t * sub, sub) for t in range(OUTPROJ_SUBTILES)]
    mix = [_dot(a1_ref[r, :], w_ref[pl.ds(0, k1), :]) + _dot(a2_ref[r, :], w_ref[pl.ds(k1, k2), :])
           for r in rs]
    for t, r in enumerate(rs):
        xn = _layer_norm(ALPHA * x_ref[r, :] + mix[t], g_ref[...], b_ref[...])
        xn_ref[r, :] = xn
        rows = _route_rows(xn, wrh_ref[...], wrl_ref[...], rb_ref[...], count_ref)
        rt_ref[:, r] = rows
        rc_ref[r, :] = rows.T
    cnt_ref[...] = jnp.broadcast_to(count_ref[...], cnt_ref.shape)


def _outproj_ln_route(a1, a1_col, a2, a2_col, kw, w, xres, ln_g, ln_b, wr_hi, wr_lo, rbias, tm):
    n, d = xres.shape
    full = lambda shape: pl.BlockSpec(shape, lambda i: (0, 0))
    return pl.pallas_call(
        _outproj_kernel,
        grid=(n // tm,),
        in_specs=[pl.BlockSpec((tm, kw), lambda i: (i, a1_col)),
                  pl.BlockSpec((tm, kw), lambda i: (i, a2_col)),
                  full(w.shape),
                  pl.BlockSpec((tm, d), lambda i: (i, 0)),
                  full((1, d)), full((1, d)),
                  full(wr_hi.shape), full(wr_lo.shape), full(rbias.shape)],
        out_specs=[pl.BlockSpec((tm, d), lambda i: (i, 0)),
                   pl.BlockSpec((8, tm), lambda i: (0, i)),
                   pl.BlockSpec((tm, 8), lambda i: (i, 0)),
                   pl.BlockSpec((N_EXPERTS, LANES), lambda i: (0, 0))],
        out_shape=[jax.ShapeDtypeStruct((n, d), F32),
                   jax.ShapeDtypeStruct((8, n), F32),
                   jax.ShapeDtypeStruct((n, 8), F32),
                   jax.ShapeDtypeStruct((N_EXPERTS, LANES), F32)],
        scratch_shapes=[pltpu.VMEM((N_EXPERTS, 1), F32)],
        compiler_params=pltpu.CompilerParams(
            dimension_semantics=("arbitrary",), vmem_limit_bytes=_vmem(52)),
        name="outproj_ln_route",
    )(a1, a2, w, xres, ln_g, ln_b, wr_hi, wr_lo, rbias)


EXPERT_CHUNK_ROWS = 128
EXPERT_STAGES = 4


TOK_WINDOW = 1024


def _expert_kernel(be_ref, na_ref, kb_ref, nk_ref, nx_ref, sl_ref, src_ref,
                   tok_hbm, x_hbm, wg_hbm, wu_hbm, wd_hbm, o_ref,
                   wgu_ref, wdn_ref, stg_ref, sem, xbuf_ref, tok_ref, gsem, tsem, *, layer):
    i = pl.program_id(0)
    tm, d = o_ref.shape
    de = wdn_ref.shape[1]
    na = na_ref[0]

    def tok_copy(j):
        base = pl.multiple_of((src_ref[j] // TOK_WINDOW) * TOK_WINDOW, TOK_WINDOW)
        dst = pl.multiple_of((j % 2) * (2 * TOK_WINDOW), 2 * TOK_WINDOW)
        return pltpu.make_async_copy(tok_hbm.at[pl.ds(base, 2 * TOK_WINDOW)],
                                     tok_ref.at[pl.ds(dst, 2 * TOK_WINDOW)], tsem.at[j % 2])

    def start_rows(j):
        buf = j % 2
        first = buf * (2 * TOK_WINDOW) + src_ref[j] % TOK_WINDOW
        for r in range(tm):
            tok = tok_ref[first + r]
            pltpu.make_async_copy(x_hbm.at[pl.ds(tok, 1), :], xbuf_ref.at[buf, pl.ds(r, 1), :],
                                  gsem.at[buf]).start()

    def wait_rows(j):
        pltpu.make_async_copy(x_hbm.at[pl.ds(0, tm), :], xbuf_ref.at[j % 2], gsem.at[j % 2]).wait()
    ch = EXPERT_CHUNK_ROWS
    n_in = d // ch
    n_dn = de // ch
    n_chunks = 2 * n_in + n_dn

    def chunk_copy(e, c):
        st = c % EXPERT_STAGES

        def gate_up(w_hbm, first, col0):
            r0 = pl.multiple_of((c - first) * ch, ch)
            cp = pltpu.make_async_copy(w_hbm.at[layer, e, pl.ds(r0, ch), :],
                                       stg_ref.at[st, :, pl.ds(0, de)], sem.at[st])

            def convert(slot):
                wgu_ref[slot, pl.ds(r0, ch), pl.ds(col0, de)] = stg_ref[st, :, pl.ds(0, de)].astype(BF16)
            return cp, convert

        def down():
            r0 = pl.multiple_of((c - 2 * n_in) * ch, ch)
            cp = pltpu.make_async_copy(wd_hbm.at[layer, e, pl.ds(r0, ch), :],
                                       stg_ref.at[st], sem.at[st])

            def convert(slot):
                wdn_ref[slot, pl.ds(r0, ch), :] = stg_ref[st].astype(BF16)
            return cp, convert

        return ((c < n_in, lambda: gate_up(wg_hbm, 0, 0)),
                (jnp.logical_and(c >= n_in, c < 2 * n_in), lambda: gate_up(wu_hbm, n_in, de)),
                (c >= 2 * n_in, down))

    def start_chunk(e, c):
        for cond, make in chunk_copy(e, c):
            @pl.when(cond)
            def _():
                make()[0].start()

    def finish_chunk(e, c, slot):
        for cond, make in chunk_copy(e, c):
            @pl.when(cond)
            def _():
                cp, convert = make()
                cp.wait()
                convert(slot)

    def prime(e):
        for c in range(EXPERT_STAGES):
            start_chunk(e, jnp.int32(c))

    def process(e, slot, c_lo, c_hi):
        def body(c, _):
            finish_chunk(e, c, slot)

            @pl.when(c + EXPERT_STAGES < n_chunks)
            def _():
                start_chunk(e, c + EXPERT_STAGES)
            return 0
        lax.fori_loop(c_lo, c_hi, body, 0)

    @pl.when(i >= na)
    def _():
        o_ref[...] = jnp.zeros_like(o_ref)

    @pl.when(i < na)
    def _():
        e = be_ref[i]
        kb = kb_ref[i]
        nk = nk_ref[i]
        nxt = nx_ref[i]
        slot = sl_ref[i]

        @pl.when(i == 0)
        def _():
            tok_copy(0).start()
            prime(e)
            tok_copy(0).wait()
            start_rows(0)
            tok_copy(1).start()
            process(e, slot, 0, n_chunks)

        @pl.when(jnp.logical_and(kb == 0, nxt >= 0))
        def _():
            prime(nxt)

        tok_copy(i + 1).wait()
        start_rows(i + 1)
        tok_copy(i + 2).start()
        wait_rows(i)
        gu = _dot(xbuf_ref[i % 2].astype(BF16), wgu_ref[slot])
        g = gu[:, :de]
        u = gu[:, de:]
        hdn = (g * _sigmoid(g)) * u
        o_ref[...] = _dot(hdn.astype(BF16), wdn_ref[slot])

        @pl.when(i == na - 1)
        def _():
            wait_rows(i + 1)
            tok_copy(i + 2).wait()

        @pl.when(nxt >= 0)
        def _():
            process(nxt, 1 - slot, (n_chunks * kb) // nk, (n_chunks * (kb + 1)) // nk)


def _expert_ffn(xn, sorted_tok, tables, w_gate, w_up, w_down, layer, tm, nb):
    d = xn.shape[1]
    de = w_down.shape[2]
    assert d % EXPERT_CHUNK_ROWS == 0 and de % EXPERT_CHUNK_ROWS == 0 and de % LANES == 0
    assert tm <= TOK_WINDOW
    hbm = lambda: pl.BlockSpec(memory_space=pl.ANY)
    return pl.pallas_call(
        functools.partial(_expert_kernel, layer=layer),
        grid_spec=pltpu.PrefetchScalarGridSpec(
            num_scalar_prefetch=len(tables),
            grid=(nb,),
            in_specs=[hbm(), hbm(), hbm(), hbm(), hbm()],
            out_specs=pl.BlockSpec((tm, d), lambda i, *_: (i, 0)),
            scratch_shapes=[pltpu.VMEM((2, d, 2 * de), BF16),
                            pltpu.VMEM((2, de, d), BF16),
                            pltpu.VMEM((EXPERT_STAGES, EXPERT_CHUNK_ROWS, d), F32),
                            pltpu.SemaphoreType.DMA((EXPERT_STAGES,)),
                            pltpu.VMEM((2, tm, d), F32),
                            pltpu.SMEM((2 * 2 * TOK_WINDOW,), jnp.int32),
                            pltpu.SemaphoreType.DMA((2,)),
                            pltpu.SemaphoreType.DMA((2,))]),
        out_shape=jax.ShapeDtypeStruct((nb * tm, d), F32),
        compiler_params=pltpu.CompilerParams(
            dimension_semantics=("arbitrary",), vmem_limit_bytes=_vmem(60)),
        name="expert_ffn",
    )(*tables, sorted_tok, xn, w_gate, w_up, w_down)


ROUTE_GATE_COL = 2


def _combine_kernel(pos_ref, posn_ref, x_ref, ys_hbm, rc_ref, g_ref, b_ref, xn_ref, xb_ref,
                    ybuf_ref, sem):
    i = pl.program_id(0)
    nt = pl.num_programs(0)
    tm, d = x_ref.shape

    def start_rows(p_ref, buf):
        for r in range(tm):
            for k in range(TOP_K):
                pltpu.make_async_copy(ys_hbm.at[pl.ds(p_ref[0, 0, k * tm + r], 1), :],
                                      ybuf_ref.at[buf, k, pl.ds(r, 1), :], sem.at[buf]).start()

    @pl.when(i == 0)
    def _():
        start_rows(pos_ref, 0)

    @pl.when(i + 1 < nt)
    def _():
        start_rows(posn_ref, (i + 1) % 2)

    buf = i % 2
    for k in range(TOP_K):
        pltpu.make_async_copy(ys_hbm.at[pl.ds(0, tm), :], ybuf_ref.at[buf, k], sem.at[buf]).wait()
    g0 = rc_ref[:, ROUTE_GATE_COL:ROUTE_GATE_COL + 1]
    g1 = rc_ref[:, ROUTE_GATE_COL + 1:ROUTE_GATE_COL + 2]
    y = ALPHA * x_ref[...] + (g0 * ybuf_ref[buf, 0] + g1 * ybuf_ref[buf, 1])
    xn = _layer_norm(y, g_ref[...], b_ref[...])
    xn_ref[...] = xn
    xb_ref[...] = xn.astype(BF16)


POS_TILES_PER_STEP = 8


def _slot_pos_kernel(rt_ref, ps_ref, o_ref, *, tm):
    eid = lax.broadcasted_iota(jnp.int32, (N_EXPERTS, rt_ref.shape[1]), 0)
    for k in range(TOP_K):
        e = rt_ref[k:k + 1, :].astype(jnp.int32)
        base = jnp.sum(jnp.where(eid == e, ps_ref[...], 0.0), axis=0, keepdims=True)
        pos = (base + rt_ref[2 * TOP_K + k:2 * TOP_K + k + 1, :]).astype(jnp.int32)
        for t in range(o_ref.shape[0]):
            o_ref[t, :, pl.ds(k * tm, tm)] = pos[:, t * tm:(t + 1) * tm]


def _slot_positions(route, pstart, tm):
    n = route.shape[1]
    nt = n // tm
    step = POS_TILES_PER_STEP if nt % POS_TILES_PER_STEP == 0 else 1
    return pl.pallas_call(
        functools.partial(_slot_pos_kernel, tm=tm),
        grid=(nt // step,),
        in_specs=[pl.BlockSpec((8, step * tm), lambda i: (0, i)),
                  pl.BlockSpec((N_EXPERTS, 1), lambda i: (0, 0))],
        out_specs=pl.BlockSpec((step, 1, TOP_K * tm), lambda i: (i, 0, 0)),
        out_shape=jax.ShapeDtypeStruct((nt, 1, TOP_K * tm), jnp.int32),
        compiler_params=pltpu.CompilerParams(dimension_semantics=("parallel",)),
        name="slot_positions",
    )(route, pstart)


def _combine_ln(x, ys, pos_t, route_cols, ln_g, ln_b, tm):
    n, d = x.shape
    nt = n // tm
    row = lambda: pl.BlockSpec((tm, d), lambda i: (i, 0))
    vec = lambda: pl.BlockSpec((1, d), lambda i: (0, 0))
    smem = lambda imap: pl.BlockSpec((1, 1, TOP_K * tm), imap, memory_space=pltpu.SMEM)
    return pl.pallas_call(
        _combine_kernel,
        grid=(nt,),
        in_specs=[smem(lambda i: (i, 0, 0)), smem(lambda i: (jnp.minimum(i + 1, nt - 1), 0, 0)),
                  row(), pl.BlockSpec(memory_space=pl.ANY),
                  pl.BlockSpec((tm, 8), lambda i: (i, 0)), vec(), vec()],
        out_specs=[row(), row()],
        out_shape=[jax.ShapeDtypeStruct((n, d), F32), jax.ShapeDtypeStruct((n, d), BF16)],
        scratch_shapes=[pltpu.VMEM((2, TOP_K, tm, d), F32), pltpu.SemaphoreType.DMA((2,))],
        compiler_params=pltpu.CompilerParams(
            dimension_semantics=("arbitrary",), vmem_limit_bytes=_vmem(48)),
        name="combine_ln",
    )(pos_t, pos_t, x, ys, route_cols, ln_g, ln_b)


def _gla_gate_kernel(x_ref, wl_ref, w2_ref, ba_ref, o_ref):
    low = _dot(x_ref[...], wl_ref[...]).astype(BF16)
    z = _dot(low, w2_ref[...]) + ba_ref[...]
    o_ref[...] = _log_sigmoid(z) * (1.0 / GLA_TAU)


def _gla_gate(xb, w_low, w_a2, b_a, tm):
    n, d = xb.shape
    kw = w_a2.shape[1]
    return pl.pallas_call(
        _gla_gate_kernel,
        grid=(n // tm,),
        in_specs=[pl.BlockSpec((tm, d), lambda i: (i, 0)),
                  pl.BlockSpec(w_low.shape, lambda i: (0, 0)),
                  pl.BlockSpec(w_a2.shape, lambda i: (0, 0)),
                  pl.BlockSpec((1, kw), lambda i: (0, 0))],
        out_specs=pl.BlockSpec((tm, kw), lambda i: (i, 0)),
        out_shape=jax.ShapeDtypeStruct((n, kw), F32),
        compiler_params=pltpu.CompilerParams(
            dimension_semantics=("parallel",), vmem_limit_bytes=_vmem(32)),
        name="gla_gate",
    )(xb, w_low, w_a2, b_a)


GLA_ROWS = 512


def _gla_kernel(q_ref, k_ref, v_ref, la_ref, g_ref, ng_ref, o_ref, state_ref, *, scale):
    rows, kw = q_ref.shape
    nh = state_ref.shape[0]
    hk, hv = state_ref.shape[1], state_ref.shape[2]
    ck = GLA_CHUNK

    @pl.when(pl.program_id(1) == 0)
    def _():
        state_ref[...] = jnp.zeros_like(state_ref)

    row = lax.broadcasted_iota(jnp.int32, (ck, ck), 0)
    col = lax.broadcasted_iota(jnp.int32, (ck, ck), 1)
    causal = row >= col
    tri = jnp.where(causal, 1.0, 0.0).astype(BF16)
    ones = jnp.ones((ck, LANES), BF16)

    def chunk(c, _):
        r0 = pl.multiple_of(c * ck, ck)
        heads = range(nh)
        ks = [slice(hh * hk, (hh + 1) * hk) for hh in heads]
        vs = [slice(hh * hv, (hh + 1) * hv) for hh in heads]
        la = [_split_bf16(la_ref[pl.ds(r0, ck), ks[hh]]) for hh in heads]
        b = [_dot(tri, la[hh][0]) + _dot(tri, la[hh][1]) for hh in heads]
        dsum = [_dot_tn(la[hh][0], ones) + _dot_tn(la[hh][1], ones) for hh in heads]
        qt, kt, kend = [], [], []
        for hh in heads:
            q = q_ref[pl.ds(r0, ck), ks[hh]].astype(F32) * scale
            k = k_ref[pl.ds(r0, ck), ks[hh]].astype(F32)
            b_last = b[hh][ck - 1:ck, :]
            qt.append((q * jnp.exp(b[hh])).astype(BF16))
            kt.append((k * jnp.exp(-b[hh])).astype(BF16))
            kend.append((k * jnp.exp(b_last - b[hh])).astype(BF16))
        v = [v_ref[pl.ds(r0, ck), vs[hh]] for hh in heads]
        state = [state_ref[hh] for hh in heads]
        attn = [_dot_nt(qt[hh], kt[hh]) for hh in heads]
        inter = [_dot(qt[hh], state[hh].astype(BF16)) for hh in heads]
        kv = [_dot_tn(kend[hh], v[hh]) for hh in heads]
        for hh in heads:
            decay = jnp.concatenate([jnp.exp(dsum[hh])] * (hv // LANES), axis=1)
            state_ref[hh] = state[hh] * decay + kv[hh]
        for hh in heads:
            o = _dot(jnp.where(causal, attn[hh], 0.0).astype(BF16), v[hh]) + inter[hh]
            o = o * lax.rsqrt(jnp.mean(o * o, axis=-1, keepdims=True) + LN_EPS)
            gate = g_ref[pl.ds(r0, ck), vs[hh]].astype(F32)
            o = o * ng_ref[:, vs[hh]] * (gate * _sigmoid(gate))
            o_ref[pl.ds(r0, ck), vs[hh]] = o.astype(o_ref.dtype)
        return 0

    lax.fori_loop(0, rows // ck, chunk, 0)


def _gla(h, la, norm_g, batch, seq):
    n = h.shape[0]
    kw = la.shape[1]
    vw = norm_g.shape[1]
    nh = GLA_HEADS
    hk, hv = kw // nh, vw // nh
    rows = GLA_ROWS
    nj = seq // rows
    kern = functools.partial(_gla_kernel, scale=hk ** -0.5)
    rowblk = lambda b, j: b * nj + j
    return pl.pallas_call(
        kern,
        grid=(batch, nj),
        in_specs=[pl.BlockSpec((rows, kw), lambda b, j: (rowblk(b, j), 0)),
                  pl.BlockSpec((rows, kw), lambda b, j: (rowblk(b, j), 1)),
                  pl.BlockSpec((rows, vw), lambda b, j: (rowblk(b, j), 2 * kw // vw)),
                  pl.BlockSpec((rows, kw), lambda b, j: (rowblk(b, j), 0)),
                  pl.BlockSpec((rows, vw), lambda b, j: (rowblk(b, j), 2 * kw // vw + 1)),
                  pl.BlockSpec((1, vw), lambda b, j: (0, 0))],
        out_specs=pl.BlockSpec((rows, vw), lambda b, j: (rowblk(b, j), 0)),
        out_shape=jax.ShapeDtypeStruct((n, vw), BF16),
        scratch_shapes=[pltpu.VMEM((nh, hk, hv), F32)],
        compiler_params=pltpu.CompilerParams(
            dimension_semantics=("parallel", "arbitrary"), vmem_limit_bytes=_vmem(48)),
        name="gla",
    )(h, h, h, la, h, norm_g)


EXPERT_TM = 256
COMBINE_TM = 256


def _moe(xn, route, route_cols, counts, w_gate, w_up, w_down, layer, ln_g, ln_b):
    n, d = xn.shape
    a = n * TOP_K
    tm = EXPERT_TM
    nb = a // tm + N_EXPERTS
    i32 = jnp.int32
    eidx = route[:TOP_K].astype(i32)
    experts = jnp.arange(N_EXPERTS, dtype=i32)
    counts = counts[:, 0].astype(i32)
    starts = jnp.cumsum(counts) - counts
    nblk = (counts + tm - 1) // tm
    bend = jnp.cumsum(nblk)
    pstart = (bend - nblk) * tm
    n_active = bend[-1:].astype(i32)
    blk = jnp.minimum(jnp.arange(nb, dtype=i32), n_active[0] - 1)
    block_expert = jnp.minimum(jnp.sum((bend[None, :] <= blk[:, None]).astype(i32), axis=1),
                               N_EXPERTS - 1)
    assign = jnp.arange(n, dtype=i32)[None, :] * TOP_K + jnp.arange(TOP_K, dtype=i32)[:, None]
    sorted_tok = (lax.sort((eidx * a + assign).reshape(a)) % a) // TOP_K
    sorted_tok = jnp.concatenate([sorted_tok, jnp.zeros((2 * TOK_WINDOW,), i32)])
    kb = blk - (bend - nblk)[block_expert]
    nonempty = nblk > 0
    later = jnp.logical_and(experts[None, :] > experts[:, None], nonempty[None, :])
    nxt_e = jnp.min(jnp.where(later, experts[None, :], N_EXPERTS), axis=1)
    nxt_e = jnp.where(nxt_e < N_EXPERTS, nxt_e, -1)
    slot_e = (jnp.cumsum(nonempty.astype(i32)) - 1) % 2
    blk_src = jnp.concatenate([starts[block_expert] + kb * tm, jnp.zeros((2,), i32)])
    tables = (block_expert, n_active, kb, nblk[block_expert], nxt_e[block_expert],
              slot_e[block_expert], blk_src)
    ys = _expert_ffn(xn, sorted_tok, tuple(t.astype(i32) for t in tables),
                     w_gate, w_up, w_down, layer, tm, nb)
    pos = _slot_positions(route, pstart.reshape(N_EXPERTS, 1).astype(F32), COMBINE_TM)
    return _combine_ln(xn, ys, pos, route_cols, ln_g, ln_b, COMBINE_TM)


def kernel(x, even_w_in, even_b_f, even_conv_w, even_conv_b, even_conv_norm_g, even_conv_norm_b, even_w_out, odd_w_in, odd_w_a2, odd_b_a, odd_norm_g, odd_w_out, ln_mix_g, ln_mix_b, ln_ffn_g, ln_ffn_b, router_w, router_bias, expert_w_gate, expert_w_up, expert_w_down):
    batch, seq, d = x.shape
    n = batch * seq
    x2 = x.reshape(n, d)
    fw = FOX_HEADS * FOX_HEAD_DIM
    conv_ch = even_conv_w.shape[-1]
    kw = odd_w_a2.shape[-1]
    vw = odd_norm_g.shape[-1]
    row = lambda t: t.reshape(1, -1)

    wr_hi, wr_lo = _split_bf16(router_w.T)
    rbias = router_bias.reshape(N_EXPERTS, 1).astype(F32)
    experts_w = (expert_w_gate, expert_w_up, expert_w_down)

    w_in = even_w_in[0]
    q_scale = LOG2E * FOX_HEAD_DIM ** -0.5
    w_main = jnp.concatenate([w_in[:, :fw] * q_scale, w_in[:, fw:3 * fw],
                              w_in[:, 3 * fw + FOX_HEADS:]], axis=1).astype(BF16)
    wf = jnp.zeros((d, LANES), F32).at[:, :FOX_HEADS].set(w_in[:, 3 * fw:3 * fw + FOX_HEADS])
    b_f = jnp.zeros((1, LANES), F32).at[0, :FOX_HEADS].set(even_b_f[0])
    h = _matmul(x2, w_main, BF16, 1024, 1024)
    c = _fox_gate(x2, wf.astype(BF16), b_f, batch, seq, 512)
    att = _fox_attention(h, c, batch, seq, 256)
    a_col = 3 * fw // CONV_GROUP
    u = _conv_module(h, a_col, a_col + conv_ch // CONV_GROUP, even_conv_w[0, :, 0, :],
                     row(even_conv_b[0]), row(even_conv_norm_g[0]), row(even_conv_norm_b[0]),
                     batch, seq)
    xn, route, rcols, cnt = _outproj_ln_route(att, 0, u, 0, fw, even_w_out[0].astype(BF16), x2,
                                              row(ln_mix_g[0]), row(ln_mix_b[0]),
                                              wr_hi, wr_lo, rbias, 512)
    xn, xb = _moe(xn, route, rcols, cnt, *experts_w, 0, row(ln_ffn_g[0]), row(ln_ffn_b[0]))

    w_in = odd_w_in[0]
    h = _matmul(xb, w_in[:, :2 * kw + 2 * vw].astype(BF16), BF16, 1024, 1024)
    w_low = jnp.zeros((d, LANES), F32).at[:, :GLA_LOW_RANK].set(w_in[:, 2 * kw + 2 * vw:]).astype(BF16)
    w_a2 = jnp.zeros((LANES, kw), F32).at[:GLA_LOW_RANK].set(odd_w_a2[0]).astype(BF16)
    la = _gla_gate(xb, w_low, w_a2, row(odd_b_a[0]), 1024)
    o = _gla(h, la, row(odd_norm_g[0]), batch, seq)
    half = vw // 2
    xn, route, rcols, cnt = _outproj_ln_route(o, 0, o, 1, half, odd_w_out[0].astype(BF16), xn,
                                              row(ln_mix_g[1]), row(ln_mix_b[1]),
                                              wr_hi, wr_lo, rbias, 512)
    xn, xb = _moe(xn, route, rcols, cnt, *experts_w, 1, row(ln_ffn_g[1]), row(ln_ffn_b[1]))
    return xn.reshape(batch, seq, d)
```

```python
import functools

import jax
import jax.numpy as jnp
from jax import lax
from jax.experimental import pallas as pl
from jax.experimental.pallas import tpu as pltpu

F32 = jnp.float32
BF16 = jnp.bfloat16

DEPTH = 2
ALPHA = (2 * DEPTH) ** 0.25
LN_EPS = 1e-5
FOX_HEADS = 8
FOX_HEAD_DIM = 128
CONV_WIDTH = 31
CONV_GROUP = 128
GLA_HEADS = 4
GLA_LOW_RANK = 16
GLA_TAU = 16.0
GLA_CHUNK = 64
N_EXPERTS = 16
N_GROUPS = 4
EXPERTS_PER_GROUP = N_EXPERTS // N_GROUPS
TOP_K = 2

LANES = 128
V7X_VMEM_BYTES = 64 * 1024 * 1024
NEG_BIG = -1e30
LOG2E = 1.4426950408889634


def _vmem(mib):
    assert mib * 1024 * 1024 < V7X_VMEM_BYTES
    return mib * 1024 * 1024


def _sigmoid(z):
    return 1.0 / (1.0 + jnp.exp(-z))


def _log_sigmoid(z):
    return jnp.minimum(z, 0.0) - jnp.log1p(jnp.exp(-jnp.abs(z)))


def _dot(a, b):
    return jnp.dot(a, b, preferred_element_type=F32)


def _dot_nt(a, b):
    return lax.dot_general(a, b, (((1,), (1,)), ((), ())), preferred_element_type=F32)


def _dot_tn(a, b):
    return lax.dot_general(a, b, (((0,), (0,)), ((), ())), preferred_element_type=F32)


def _split_bf16(v):
    hi = v.astype(BF16)
    lo = (v - hi.astype(F32)).astype(BF16)
    return hi, lo


def _mm_kernel(x_ref, w_ref, o_ref, wb_ref, *, first_scale):
    @pl.when(pl.program_id(1) == 0)
    def _():
        wb_ref[...] = w_ref[...].astype(BF16)

    acc = _dot(x_ref[...].astype(BF16), wb_ref[...])
    if first_scale is not None:
        acc = acc * jnp.where(pl.program_id(0) == 0, first_scale, 1.0)
    o_ref[...] = acc.astype(o_ref.dtype)


def _matmul(x, w, n_cols, out_dtype, tm, tn, first_scale=None):
    m, k = x.shape
    assert m % tm == 0 and n_cols % tn == 0 and n_cols <= w.shape[2]
    return pl.pallas_call(
        functools.partial(_mm_kernel, first_scale=first_scale),
        grid=(n_cols // tn, m // tm),
        in_specs=[pl.BlockSpec((tm, k), lambda j, i: (i, 0)),
                  pl.BlockSpec((None, k, tn), lambda j, i: (0, 0, j))],
        out_specs=pl.BlockSpec((tm, tn), lambda j, i: (i, j)),
        out_shape=jax.ShapeDtypeStruct((m, n_cols), out_dtype),
        scratch_shapes=[pltpu.VMEM((k, tn), BF16)],
        compiler_params=pltpu.CompilerParams(
            dimension_semantics=("parallel", "arbitrary"), vmem_limit_bytes=_vmem(52)),
        name="dense_proj",
    )(x, w)


def _fox_gate_kernel(x_ref, wf_ref, bf_ref, c_ref, carry_ref):
    j = pl.program_id(1)

    @pl.when(j == 0)
    def _():
        carry_ref[...] = jnp.zeros_like(carry_ref)

    z = _dot(x_ref[...].astype(BF16), wf_ref[...]) + bf_ref[...]
    lf = _log_sigmoid(z)
    ts = lf.shape[0]
    row = lax.broadcasted_iota(jnp.int32, lf.shape, 0)
    sh = 1
    while sh < ts:
        lf = lf + jnp.where(row >= sh, pltpu.roll(lf, sh, axis=0), 0.0)
        sh *= 2
    c = lf + carry_ref[...]
    c_ref[...] = c * LOG2E
    carry_ref[...] = c[ts - 1:ts, :]


def _fox_gate(x2, wf, b_f, batch, seq, ts):
    d = x2.shape[1]
    nj = seq // ts
    return pl.pallas_call(
        _fox_gate_kernel,
        grid=(batch, nj),
        in_specs=[pl.BlockSpec((ts, d), lambda b, j: (b * nj + j, 0)),
                  pl.BlockSpec((d, LANES), lambda b, j: (0, 0)),
                  pl.BlockSpec((1, LANES), lambda b, j: (0, 0))],
        out_specs=pl.BlockSpec((ts, LANES), lambda b, j: (b * nj + j, 0)),
        out_shape=jax.ShapeDtypeStruct((batch * seq, LANES), F32),
        scratch_shapes=[pltpu.VMEM((1, LANES), F32)],
        compiler_params=pltpu.CompilerParams(
            dimension_semantics=("parallel", "arbitrary"), vmem_limit_bytes=_vmem(40)),
        name="fox_gate",
    )(x2, wf, b_f)


FOX_HEADS_PER_STEP = 8


def _fox_attn_kernel(q_ref, k_ref, v_ref, c_ref, o_ref, vt_ref, crep_ref, *, tile, heads):
    hh = pl.program_id(1)
    i = pl.program_id(2)
    hd = FOX_HEAD_DIM
    nq = vt_ref.shape[1]

    @pl.when(i == 0)
    def _():
        lane = lax.broadcasted_iota(jnp.int32, (tile, LANES), 1)
        for j in range(nq):
            cj = c_ref[pl.ds(j * tile, tile), :]
            for g in range(heads):
                col = jnp.sum(jnp.where(lane == hh * heads + g, cj, 0.0), axis=1, keepdims=True)
                crep_ref[g, j] = jnp.broadcast_to(col, (tile, LANES))
                vj = v_ref[pl.ds(j * tile, tile), g * hd:(g + 1) * hd]
                vt_ref[g, j] = vj.astype(F32).T.astype(BF16)

    def scores(g, j):
        r0 = pl.multiple_of(j * tile, tile)
        kj = k_ref[pl.ds(r0, tile), g * hd:(g + 1) * hd]
        cj = jnp.concatenate([crep_ref[g, j]] * (tile // LANES), axis=1)
        return _dot_nt(kj, q_ref[:, g * hd:(g + 1) * hd]) - cj

    def update(g, j, carry, st):
        m, l, acc = carry
        m_new = jnp.maximum(m, jnp.max(st, axis=0, keepdims=True))
        a = jnp.exp2(m - m_new)
        p = jnp.exp2(st - m_new)
        l = a * l + jnp.sum(p, axis=0, keepdims=True)
        acc = a * acc + _dot(vt_ref[g, j], p.astype(BF16))
        return m_new, l, acc

    def body(j, carries):
        sts = [scores(g, j) for g in range(heads)]
        return tuple(update(g, j, carries[g], sts[g]) for g in range(heads))

    init = tuple((jnp.full((1, tile), NEG_BIG, F32), jnp.zeros((1, tile), F32),
                  jnp.zeros((hd, tile), F32)) for _ in range(heads))
    carries = lax.fori_loop(0, i, body, init)
    key = lax.broadcasted_iota(jnp.int32, (tile, tile), 0)
    qry = lax.broadcasted_iota(jnp.int32, (tile, tile), 1)
    sts = [jnp.where(key <= qry, scores(g, i), NEG_BIG) for g in range(heads)]
    for g in range(heads):
        _, l, acc = update(g, i, carries[g], sts[g])
        o_ref[:, g * hd:(g + 1) * hd] = (acc * (1.0 / l)).T.astype(o_ref.dtype)


def _fox_attention(h, c, batch, seq, tile):
    n = h.shape[0]
    hd = FOX_HEAD_DIM
    nh = FOX_HEADS
    gh = FOX_HEADS_PER_STEP
    ng = nh // gh
    nq = seq // tile
    kern = functools.partial(_fox_attn_kernel, tile=tile, heads=gh)
    return pl.pallas_call(
        kern,
        grid=(batch, ng, nq),
        in_specs=[pl.BlockSpec((tile, gh * hd), lambda b, hh, i: (b * nq + i, hh)),
                  pl.BlockSpec((seq, gh * hd), lambda b, hh, i: (b, ng + hh)),
                  pl.BlockSpec((seq, gh * hd), lambda b, hh, i: (b, 2 * ng + hh)),
                  pl.BlockSpec((seq, LANES), lambda b, hh, i: (b, 0))],
        out_specs=pl.BlockSpec((tile, gh * hd), lambda b, hh, i: (b * nq + i, hh)),
        out_shape=jax.ShapeDtypeStruct((n, nh * hd), BF16),
        scratch_shapes=[pltpu.VMEM((gh, nq, hd, tile), BF16),
                        pltpu.VMEM((gh, nq, tile, LANES), F32)],
        compiler_params=pltpu.CompilerParams(
            dimension_semantics=("parallel", "parallel", "arbitrary"), vmem_limit_bytes=_vmem(40)),
        name="fox_attention",
    )(h, h, h, c)


CONV_PAD = 32
CONV_ROWS = 256


def _conv_kernel(a_ref, g_ref, w_ref, cb_ref, ng_ref, nb_ref, o_ref, pad_ref):
    seq = a_ref.shape[0]
    pad_ref[pl.ds(0, CONV_PAD), :] = jnp.zeros((CONV_PAD, LANES), F32)
    pad_ref[pl.ds(CONV_PAD, seq), :] = a_ref[...].astype(F32) * _sigmoid(g_ref[...].astype(F32))
    off = CONV_PAD - (CONV_WIDTH - 1)
    for r in range(seq // CONV_ROWS):
        base = r * CONV_ROWS
        acc = jnp.zeros((CONV_ROWS, LANES), F32)
        for j in range(CONV_WIDTH):
            acc = acc + pad_ref[pl.ds(base + off + j, CONV_ROWS), :] * w_ref[pl.ds(j, 1), :]
        y = acc + cb_ref[...]
        mu = jnp.mean(y, axis=-1, keepdims=True)
        yc = y - mu
        var = jnp.mean(yc * yc, axis=-1, keepdims=True)
        yn = yc * lax.rsqrt(var + LN_EPS) * ng_ref[...] + nb_ref[...]
        o_ref[pl.ds(base, CONV_ROWS), :] = (yn * _sigmoid(yn)).astype(o_ref.dtype)


def _conv_module(h, a_col, g_col, conv_w, conv_b, cn_g, cn_b, batch, seq):
    n = h.shape[0]
    ch = conv_w.shape[1]
    ng = ch // CONV_GROUP
    assert seq % CONV_ROWS == 0
    vec = lambda: pl.BlockSpec((1, CONV_GROUP), lambda b, g: (0, g))
    return pl.pallas_call(
        _conv_kernel,
        grid=(batch, ng),
        in_specs=[pl.BlockSpec((seq, CONV_GROUP), lambda b, g: (b, a_col + g)),
                  pl.BlockSpec((seq, CONV_GROUP), lambda b, g: (b, g_col + g)),
                  pl.BlockSpec((CONV_WIDTH, CONV_GROUP), lambda b, g: (0, g)),
                  vec(), vec(), vec()],
        out_specs=pl.BlockSpec((seq, CONV_GROUP), lambda b, g: (b, g)),
        out_shape=jax.ShapeDtypeStruct((n, ch), BF16),
        scratch_shapes=[pltpu.VMEM((seq + CONV_PAD, LANES), F32)],
        compiler_params=pltpu.CompilerParams(
            dimension_semantics=("parallel", "parallel"), vmem_limit_bytes=_vmem(32)),
        name="conv_module",
    )(h, h, conv_w, conv_b, cn_g, cn_b)


def _layer_norm(y, g, b):
    mu = jnp.mean(y, axis=-1, keepdims=True)
    yc = y - mu
    var = jnp.mean(yc * yc, axis=-1, keepdims=True)
    return yc * lax.rsqrt(var + LN_EPS) * g + b


def _route_rows(xn, wr_hi, wr_lo, rbias, count_ref):
    x_hi, x_lo = _split_bf16(xn)
    logits = _dot_nt(wr_hi, x_hi) + _dot_nt(wr_hi, x_lo) + _dot_nt(wr_lo, x_hi)
    scores = _sigmoid(logits)
    sel = scores + rbias
    s = [sel[e:e + 1, :] for e in range(N_EXPERTS)]
    r = [scores[e:e + 1, :] for e in range(N_EXPERTS)]
    pg = EXPERTS_PER_GROUP

    def top2_sum(vals):
        best = None
        for a in range(len(vals)):
            for b in range(a + 1, len(vals)):
                t = vals[a] + vals[b]
                best = t if best is None else jnp.maximum(best, t)
        return best

    gs = [top2_sum(s[g * pg:(g + 1) * pg]) for g in range(N_GROUPS)]
    best, grp = gs[0], jnp.zeros_like(gs[0], dtype=jnp.int32)
    for g in range(1, N_GROUPS):
        upd = gs[g] > best
        best = jnp.where(upd, gs[g], best)
        grp = jnp.where(upd, g, grp)

    def pick_group(vals, k):
        out = vals[(N_GROUPS - 1) * pg + k]
        for g in range(N_GROUPS - 2, -1, -1):
            out = jnp.where(grp == g, vals[g * pg + k], out)
        return out

    v = [pick_group(s, k) for k in range(pg)]
    w = [pick_group(r, k) for k in range(pg)]
    b1, i1, w1 = v[0], jnp.zeros_like(grp), w[0]
    for k in range(1, pg):
        upd = v[k] > b1
        b1 = jnp.where(upd, v[k], b1)
        i1 = jnp.where(upd, k, i1)
        w1 = jnp.where(upd, w[k], w1)
    b2 = jnp.full_like(b1, -jnp.inf)
    i2, w2 = jnp.zeros_like(grp), jnp.zeros_like(w1)
    for k in range(pg):
        upd = jnp.logical_and(i1 != k, v[k] > b2)
        b2 = jnp.where(upd, v[k], b2)
        i2 = jnp.where(upd, k, i2)
        w2 = jnp.where(upd, w[k], w2)
    tot = w1 + w2
    e1 = grp * pg + i1
    e2 = grp * pg + i2
    eid = lax.broadcasted_iota(jnp.int32, scores.shape, 0)
    oh1 = jnp.where(eid == e1, 1.0, 0.0)
    oh2 = jnp.where(eid == e2, 1.0, 0.0)
    cnt = oh1 + oh2
    tm = cnt.shape[1]
    lane = lax.broadcasted_iota(jnp.int32, cnt.shape, 1)
    incl = cnt
    sh = 1
    while sh < tm:
        incl = incl + jnp.where(lane >= sh, pltpu.roll(incl, sh, axis=1), 0.0)
        sh *= 2
    before = incl - cnt + count_ref[...]
    count_ref[...] = count_ref[...] + incl[:, tm - 1:tm]
    rank1 = jnp.sum(oh1 * before, axis=0, keepdims=True)
    rank2 = jnp.sum(oh2 * before, axis=0, keepdims=True)
    rows = [e1.astype(F32), e2.astype(F32), w1 / tot, w2 / tot, rank1, rank2]
    rows += [jnp.zeros_like(w1)] * (8 - len(rows))
    return jnp.concatenate(rows, axis=0)


OUTPROJ_SUBTILES = 2


def _outproj_kernel(a1_ref, a2_ref, w_ref, x_ref, g_ref, b_ref, wrh_ref, wrl_ref, rb_ref,
                    xn_ref, rt_ref, rc_ref, cnt_ref, count_ref):
    @pl.when(pl.program_id(0) == 0)
    def _():
        count_ref[...] = jnp.zeros_like(count_ref)

    k1 = a1_ref.shape[1]
    k2 = a2_ref.shape[1]
    tm = x_ref.shape[0]
    sub = tm // OUTPROJ_SUBTILES
    rs = [pl.ds(t * sub, sub) for t in range(OUTPROJ_SUBTILES)]
    mix = [_dot(a1_ref[r, :], w_ref[pl.ds(0, k1), :]) + _dot(a2_ref[r, :], w_ref[pl.ds(k1, k2), :])
           for r in rs]
    for t, r in enumerate(rs):
        xn = _layer_norm(ALPHA * x_ref[r, :] + mix[t], g_ref[...], b_ref[...])
        xn_ref[r, :] = xn
        rows = _route_rows(xn, wrh_ref[...], wrl_ref[...], rb_ref[...], count_ref)
        rt_ref[:, r] = rows
        rc_ref[r, :] = rows.T
    cnt_ref[...] = jnp.broadcast_to(count_ref[...], cnt_ref.shape)


def _outproj_ln_route(a1, a1_col, a2, a2_col, kw, w, xres, ln_g, ln_b, wr_hi, wr_lo, rbias, tm):
    n, d = xres.shape
    full = lambda shape: pl.BlockSpec(shape, lambda i: (0, 0))
    return pl.pallas_call(
        _outproj_kernel,
        grid=(n // tm,),
        in_specs=[pl.BlockSpec((tm, kw), lambda i: (i, a1_col)),
                  pl.BlockSpec((tm, kw), lambda i: (i, a2_col)),
                  full(w.shape),
                  pl.BlockSpec((tm, d), lambda i: (i, 0)),
                  full((1, d)), full((1, d)),
                  full(wr_hi.shape), full(wr_lo.shape), full(rbias.shape)],
        out_specs=[pl.BlockSpec((tm, d), lambda i: (i, 0)),
                   pl.BlockSpec((8, tm), lambda i: (0, i)),
                   pl.BlockSpec((tm, 8), lambda i: (i, 0)),
                   pl.BlockSpec((N_EXPERTS, LANES), lambda i: (0, 0))],
        out_shape=[jax.ShapeDtypeStruct((n, d), F32),
                   jax.ShapeDtypeStruct((8, n), F32),
                   jax.ShapeDtypeStruct((n, 8), F32),
                   jax.ShapeDtypeStruct((N_EXPERTS, LANES), F32)],
        scratch_shapes=[pltpu.VMEM((N_EXPERTS, 1), F32)],
        compiler_params=pltpu.CompilerParams(
            dimension_semantics=("arbitrary",), vmem_limit_bytes=_vmem(52)),
        name="outproj_ln_route",
    )(a1, a2, w, xres, ln_g, ln_b, wr_hi, wr_lo, rbias)


EXPERT_CHUNK_ROWS = 128
EXPERT_STAGES = 4


TOK_WINDOW = 1024


def _expert_kernel(be_ref, na_ref, kb_ref, nk_ref, nx_ref, sl_ref, src_ref,
                   tok_hbm, x_hbm, wg_hbm, wu_hbm, wd_hbm, o_ref,
                   wgu_ref, wdn_ref, stg_ref, sem, xbuf_ref, tok_ref, gsem, tsem, *, layer):
    i = pl.program_id(0)
    tm, d = o_ref.shape
    de = wdn_ref.shape[1]
    na = na_ref[0]

    def tok_copy(j):
        base = pl.multiple_of((src_ref[j] // TOK_WINDOW) * TOK_WINDOW, TOK_WINDOW)
        dst = pl.multiple_of((j % 2) * (2 * TOK_WINDOW), 2 * TOK_WINDOW)
        return pltpu.make_async_copy(tok_hbm.at[pl.ds(base, 2 * TOK_WINDOW)],
                                     tok_ref.at[pl.ds(dst, 2 * TOK_WINDOW)], tsem.at[j % 2])

    def start_rows(j):
        buf = j % 2
        first = buf * (2 * TOK_WINDOW) + src_ref[j] % TOK_WINDOW
        for r in range(tm):
            tok = tok_ref[first + r]
            pltpu.make_async_copy(x_hbm.at[pl.ds(tok, 1), :], xbuf_ref.at[buf, pl.ds(r, 1), :],
                                  gsem.at[buf]).start()

    def wait_rows(j):
        pltpu.make_async_copy(x_hbm.at[pl.ds(0, tm), :], xbuf_ref.at[j % 2], gsem.at[j % 2]).wait()
    ch = EXPERT_CHUNK_ROWS
    n_in = d // ch
    n_dn = de // ch
    n_chunks = 2 * n_in + n_dn

    def chunk_copy(e, c):
        st = c % EXPERT_STAGES

        def gate_up(w_hbm, first, col0):
            r0 = pl.multiple_of((c - first) * ch, ch)
            cp = pltpu.make_async_copy(w_hbm.at[layer, e, pl.ds(r0, ch), :],
                                       stg_ref.at[st, :, pl.ds(0, de)], sem.at[st])

            def convert(slot):
                wgu_ref[slot, pl.ds(r0, ch), pl.ds(col0, de)] = stg_ref[st, :, pl.ds(0, de)].astype(BF16)
            return cp, convert

        def down():
            r0 = pl.multiple_of((c - 2 * n_in) * ch, ch)
            cp = pltpu.make_async_copy(wd_hbm.at[layer, e, pl.ds(r0, ch), :],
                                       stg_ref.at[st], sem.at[st])

            def convert(slot):
                wdn_ref[slot, pl.ds(r0, ch), :] = stg_ref[st].astype(BF16)
            return cp, convert

        return ((c < n_in, lambda: gate_up(wg_hbm, 0, 0)),
                (jnp.logical_and(c >= n_in, c < 2 * n_in), lambda: gate_up(wu_hbm, n_in, de)),
                (c >= 2 * n_in, down))

    def start_chunk(e, c):
        for cond, make in chunk_copy(e, c):
            @pl.when(cond)
            def _():
                make()[0].start()

    def finish_chunk(e, c, slot):
        for cond, make in chunk_copy(e, c):
            @pl.when(cond)
            def _():
                cp, convert = make()
                cp.wait()
                convert(slot)

    def prime(e):
        for c in range(EXPERT_STAGES):
            start_chunk(e, jnp.int32(c))

    def process(e, slot, c_lo, c_hi):
        def body(c, _):
            finish_chunk(e, c, slot)

            @pl.when(c + EXPERT_STAGES < n_chunks)
            def _():
                start_chunk(e, c + EXPERT_STAGES)
            return 0
        lax.fori_loop(c_lo, c_hi, body, 0)

    @pl.when(i >= na)
    def _():
        o_ref[...] = jnp.zeros_like(o_ref)

    @pl.when(i < na)
    def _():
        e = be_ref[i]
        kb = kb_ref[i]
        nk = nk_ref[i]
        nxt = nx_ref[i]
        slot = sl_ref[i]

        @pl.when(i == 0)
        def _():
            tok_copy(0).start()
            prime(e)
            tok_copy(0).wait()
            start_rows(0)
            tok_copy(1).start()
            process(e, slot, 0, n_chunks)

        @pl.when(jnp.logical_and(kb == 0, nxt >= 0))
        def _():
            prime(nxt)

        tok_copy(i + 1).wait()
        start_rows(i + 1)
        tok_copy(i + 2).start()
        wait_rows(i)
        gu = _dot(xbuf_ref[i % 2].astype(BF16), wgu_ref[slot])
        g = gu[:, :de]
        u = gu[:, de:]
        hdn = (g * _sigmoid(g)) * u
        o_ref[...] = _dot(hdn.astype(BF16), wdn_ref[slot])

        @pl.when(i == na - 1)
        def _():
            wait_rows(i + 1)
            tok_copy(i + 2).wait()

        @pl.when(nxt >= 0)
        def _():
            process(nxt, 1 - slot, (n_chunks * kb) // nk, (n_chunks * (kb + 1)) // nk)


def _expert_ffn(xn, sorted_tok, tables, w_gate, w_up, w_down, layer, tm, nb):
    d = xn.shape[1]
    de = w_down.shape[2]
    assert d % EXPERT_CHUNK_ROWS == 0 and de % EXPERT_CHUNK_ROWS == 0 and de % LANES == 0
    assert tm <= TOK_WINDOW
    hbm = lambda: pl.BlockSpec(memory_space=pl.ANY)
    return pl.pallas_call(
        functools.partial(_expert_kernel, layer=layer),
        grid_spec=pltpu.PrefetchScalarGridSpec(
            num_scalar_prefetch=len(tables),
            grid=(nb,),
            in_specs=[hbm(), hbm(), hbm(), hbm(), hbm()],
            out_specs=pl.BlockSpec((tm, d), lambda i, *_: (i, 0)),
            scratch_shapes=[pltpu.VMEM((2, d, 2 * de), BF16),
                            pltpu.VMEM((2, de, d), BF16),
                            pltpu.VMEM((EXPERT_STAGES, EXPERT_CHUNK_ROWS, d), F32),
                            pltpu.SemaphoreType.DMA((EXPERT_STAGES,)),
                            pltpu.VMEM((2, tm, d), F32),
                            pltpu.SMEM((2 * 2 * TOK_WINDOW,), jnp.int32),
                            pltpu.SemaphoreType.DMA((2,)),
                            pltpu.SemaphoreType.DMA((2,))]),
        out_shape=jax.ShapeDtypeStruct((nb * tm, d), F32),
        compiler_params=pltpu.CompilerParams(
            dimension_semantics=("arbitrary",), vmem_limit_bytes=_vmem(60)),
        name="expert_ffn",
    )(*tables, sorted_tok, xn, w_gate, w_up, w_down)


ROUTE_GATE_COL = 2


def _combine_kernel(pos_ref, posn_ref, x_ref, ys_hbm, rc_ref, g_ref, b_ref, xn_ref, *rest):
    xb_ref = rest[0] if len(rest) == 3 else None
    ybuf_ref, sem = rest[-2:]
    i = pl.program_id(0)
    nt = pl.num_programs(0)
    tm, d = x_ref.shape

    def start_rows(p_ref, buf):
        for r in range(tm):
            for k in range(TOP_K):
                pltpu.make_async_copy(ys_hbm.at[pl.ds(p_ref[0, 0, k * tm + r], 1), :],
                                      ybuf_ref.at[buf, k, pl.ds(r, 1), :], sem.at[buf]).start()

    @pl.when(i == 0)
    def _():
        start_rows(pos_ref, 0)

    @pl.when(i + 1 < nt)
    def _():
        start_rows(posn_ref, (i + 1) % 2)

    buf = i % 2
    for k in range(TOP_K):
        pltpu.make_async_copy(ys_hbm.at[pl.ds(0, tm), :], ybuf_ref.at[buf, k], sem.at[buf]).wait()
    g0 = rc_ref[:, ROUTE_GATE_COL:ROUTE_GATE_COL + 1]
    g1 = rc_ref[:, ROUTE_GATE_COL + 1:ROUTE_GATE_COL + 2]
    y = ALPHA * x_ref[...] + (g0 * ybuf_ref[buf, 0] + g1 * ybuf_ref[buf, 1])
    xn = _layer_norm(y, g_ref[...], b_ref[...])
    xn_ref[...] = xn
    if xb_ref is not None:
        xb_ref[...] = xn.astype(BF16)


POS_TILES_PER_STEP = 8


def _slot_pos_kernel(rt_ref, ps_ref, o_ref, *, tm):
    eid = lax.broadcasted_iota(jnp.int32, (N_EXPERTS, rt_ref.shape[1]), 0)
    for k in range(TOP_K):
        e = rt_ref[k:k + 1, :].astype(jnp.int32)
        base = jnp.sum(jnp.where(eid == e, ps_ref[...], 0.0), axis=0, keepdims=True)
        pos = (base + rt_ref[2 * TOP_K + k:2 * TOP_K + k + 1, :]).astype(jnp.int32)
        for t in range(o_ref.shape[0]):
            o_ref[t, :, pl.ds(k * tm, tm)] = pos[:, t * tm:(t + 1) * tm]


def _slot_positions(route, pstart, tm):
    n = route.shape[1]
    nt = n // tm
    step = POS_TILES_PER_STEP if nt % POS_TILES_PER_STEP == 0 else 1
    return pl.pallas_call(
        functools.partial(_slot_pos_kernel, tm=tm),
        grid=(nt // step,),
        in_specs=[pl.BlockSpec((8, step * tm), lambda i: (0, i)),
                  pl.BlockSpec((N_EXPERTS, 1), lambda i: (0, 0))],
        out_specs=pl.BlockSpec((step, 1, TOP_K * tm), lambda i: (i, 0, 0)),
        out_shape=jax.ShapeDtypeStruct((nt, 1, TOP_K * tm), jnp.int32),
        compiler_params=pltpu.CompilerParams(dimension_semantics=("parallel",)),
        name="slot_positions",
    )(route, pstart)


def _combine_ln(x, ys, pos_t, route_cols, ln_g, ln_b, tm, with_bf16):
    n, d = x.shape
    nt = n // tm
    n_out = 2 if with_bf16 else 1
    row = lambda: pl.BlockSpec((tm, d), lambda i: (i, 0))
    vec = lambda: pl.BlockSpec((1, d), lambda i: (0, 0))
    smem = lambda imap: pl.BlockSpec((1, 1, TOP_K * tm), imap, memory_space=pltpu.SMEM)
    return pl.pallas_call(
        _combine_kernel,
        grid=(nt,),
        in_specs=[smem(lambda i: (i, 0, 0)), smem(lambda i: (jnp.minimum(i + 1, nt - 1), 0, 0)),
                  row(), pl.BlockSpec(memory_space=pl.ANY),
                  pl.BlockSpec((tm, 8), lambda i: (i, 0)), vec(), vec()],
        out_specs=[row(), row()][:n_out],
        out_shape=[jax.ShapeDtypeStruct((n, d), F32), jax.ShapeDtypeStruct((n, d), BF16)][:n_out],
        scratch_shapes=[pltpu.VMEM((2, TOP_K, tm, d), F32), pltpu.SemaphoreType.DMA((2,))],
        compiler_params=pltpu.CompilerParams(
            dimension_semantics=("arbitrary",), vmem_limit_bytes=_vmem(48)),
        name="combine_ln",
    )(pos_t, pos_t, x, ys, route_cols, ln_g, ln_b)


def _gla_gate_kernel(x_ref, wl_ref, w2_ref, ba_ref, o_ref):
    low = _dot(x_ref[...], wl_ref[...]).astype(BF16)
    z = _dot(low, w2_ref[...]) + ba_ref[...]
    o_ref[...] = _log_sigmoid(z) * (1.0 / GLA_TAU)


def _gla_gate(xb, w_low, w_a2, b_a, tm):
    n, d = xb.shape
    kw = w_a2.shape[1]
    return pl.pallas_call(
        _gla_gate_kernel,
        grid=(n // tm,),
        in_specs=[pl.BlockSpec((tm, d), lambda i: (i, 0)),
                  pl.BlockSpec(w_low.shape, lambda i: (0, 0)),
                  pl.BlockSpec(w_a2.shape, lambda i: (0, 0)),
                  pl.BlockSpec((1, kw), lambda i: (0, 0))],
        out_specs=pl.BlockSpec((tm, kw), lambda i: (i, 0)),
        out_shape=jax.ShapeDtypeStruct((n, kw), F32),
        compiler_params=pltpu.CompilerParams(
            dimension_semantics=("parallel",), vmem_limit_bytes=_vmem(32)),
        name="gla_gate",
    )(xb, w_low, w_a2, b_a)


GLA_ROWS = 512


def _gla_kernel(q_ref, k_ref, v_ref, la_ref, g_ref, ng_ref, o_ref, state_ref, *, scale):
    rows, kw = q_ref.shape
    nh = state_ref.shape[0]
    hk, hv = state_ref.shape[1], state_ref.shape[2]
    ck = GLA_CHUNK

    @pl.when(pl.program_id(1) == 0)
    def _():
        state_ref[...] = jnp.zeros_like(state_ref)

    row = lax.broadcasted_iota(jnp.int32, (ck, ck), 0)
    col = lax.broadcasted_iota(jnp.int32, (ck, ck), 1)
    causal = row >= col
    tri = jnp.where(causal, 1.0, 0.0).astype(BF16)
    ones = jnp.ones((ck, LANES), BF16)

    def chunk(c, _):
        r0 = pl.multiple_of(c * ck, ck)
        heads = range(nh)
        ks = [slice(hh * hk, (hh + 1) * hk) for hh in heads]
        vs = [slice(hh * hv, (hh + 1) * hv) for hh in heads]
        la = [_split_bf16(la_ref[pl.ds(r0, ck), ks[hh]]) for hh in heads]
        b = [_dot(tri, la[hh][0]) + _dot(tri, la[hh][1]) for hh in heads]
        dsum = [_dot_tn(la[hh][0], ones) + _dot_tn(la[hh][1], ones) for hh in heads]
        qt, kt, kend = [], [], []
        for hh in heads:
            q = q_ref[pl.ds(r0, ck), ks[hh]].astype(F32) * scale
            k = k_ref[pl.ds(r0, ck), ks[hh]].astype(F32)
            b_last = b[hh][ck - 1:ck, :]
            qt.append((q * jnp.exp(b[hh])).astype(BF16))
            kt.append((k * jnp.exp(-b[hh])).astype(BF16))
            kend.append((k * jnp.exp(b_last - b[hh])).astype(BF16))
        v = [v_ref[pl.ds(r0, ck), vs[hh]] for hh in heads]
        state = [state_ref[hh] for hh in heads]
        attn = [_dot_nt(qt[hh], kt[hh]) for hh in heads]
        inter = [_dot(qt[hh], state[hh].astype(BF16)) for hh in heads]
        kv = [_dot_tn(kend[hh], v[hh]) for hh in heads]
        for hh in heads:
            decay = jnp.concatenate([jnp.exp(dsum[hh])] * (hv // LANES), axis=1)
            state_ref[hh] = state[hh] * decay + kv[hh]
        for hh in heads:
            o = _dot(jnp.where(causal, attn[hh], 0.0).astype(BF16), v[hh]) + inter[hh]
            o = o * lax.rsqrt(jnp.mean(o * o, axis=-1, keepdims=True) + LN_EPS)
            gate = g_ref[pl.ds(r0, ck), vs[hh]].astype(F32)
            o = o * ng_ref[:, vs[hh]] * (gate * _sigmoid(gate))
            o_ref[pl.ds(r0, ck), vs[hh]] = o.astype(o_ref.dtype)
        return 0

    lax.fori_loop(0, rows // ck, chunk, 0, unroll=2)


def _gla(h, la, norm_g, batch, seq):
    n = h.shape[0]
    kw = la.shape[1]
    vw = norm_g.shape[1]
    nh = GLA_HEADS
    hk, hv = kw // nh, vw // nh
    rows = GLA_ROWS
    nj = seq // rows
    kern = functools.partial(_gla_kernel, scale=hk ** -0.5)
    rowblk = lambda b, j: b * nj + j
    return pl.pallas_call(
        kern,
        grid=(batch, nj),
        in_specs=[pl.BlockSpec((rows, kw), lambda b, j: (rowblk(b, j), 0)),
                  pl.BlockSpec((rows, kw), lambda b, j: (rowblk(b, j), 1)),
                  pl.BlockSpec((rows, vw), lambda b, j: (rowblk(b, j), 2 * kw // vw)),
                  pl.BlockSpec((rows, kw), lambda b, j: (rowblk(b, j), 0)),
                  pl.BlockSpec((rows, vw), lambda b, j: (rowblk(b, j), 2 * kw // vw + 1)),
                  pl.BlockSpec((1, vw), lambda b, j: (0, 0))],
        out_specs=pl.BlockSpec((rows, vw), lambda b, j: (rowblk(b, j), 0)),
        out_shape=jax.ShapeDtypeStruct((n, vw), BF16),
        scratch_shapes=[pltpu.VMEM((nh, hk, hv), F32)],
        compiler_params=pltpu.CompilerParams(
            dimension_semantics=("parallel", "arbitrary"), vmem_limit_bytes=_vmem(48)),
        name="gla",
    )(h, h, h, la, h, norm_g)


EXPERT_TM = 256
COMBINE_TM = 256


def _moe(xn, route, route_cols, counts, w_gate, w_up, w_down, layer, ln_g, ln_b, with_bf16):
    n, d = xn.shape
    a = n * TOP_K
    tm = EXPERT_TM
    nb = a // tm + N_EXPERTS
    i32 = jnp.int32
    eidx = route[:TOP_K].astype(i32)
    experts = jnp.arange(N_EXPERTS, dtype=i32)
    counts = counts[:, 0].astype(i32)
    starts = jnp.cumsum(counts) - counts
    nblk = (counts + tm - 1) // tm
    bend = jnp.cumsum(nblk)
    pstart = (bend - nblk) * tm
    n_active = bend[-1:].astype(i32)
    blk = jnp.minimum(jnp.arange(nb, dtype=i32), n_active[0] - 1)
    block_expert = jnp.minimum(jnp.sum((bend[None, :] <= blk[:, None]).astype(i32), axis=1),
                               N_EXPERTS - 1)
    assign = jnp.arange(n, dtype=i32)[None, :] * TOP_K + jnp.arange(TOP_K, dtype=i32)[:, None]
    sorted_tok = (lax.sort((eidx * a + assign).reshape(a)) % a) // TOP_K
    sorted_tok = jnp.concatenate([sorted_tok, jnp.zeros((2 * TOK_WINDOW,), i32)])
    kb = blk - (bend - nblk)[block_expert]
    nonempty = nblk > 0
    later = jnp.logical_and(experts[None, :] > experts[:, None], nonempty[None, :])
    nxt_e = jnp.min(jnp.where(later, experts[None, :], N_EXPERTS), axis=1)
    nxt_e = jnp.where(nxt_e < N_EXPERTS, nxt_e, -1)
    slot_e = (jnp.cumsum(nonempty.astype(i32)) - 1) % 2
    blk_src = jnp.concatenate([starts[block_expert] + kb * tm, jnp.zeros((2,), i32)])
    tables = (block_expert, n_active, kb, nblk[block_expert], nxt_e[block_expert],
              slot_e[block_expert], blk_src)
    ys = _expert_ffn(xn, sorted_tok, tuple(t.astype(i32) for t in tables),
                     w_gate, w_up, w_down, layer, tm, nb)
    pos = _slot_positions(route, pstart.reshape(N_EXPERTS, 1).astype(F32), COMBINE_TM)
    return _combine_ln(xn, ys, pos, route_cols, ln_g, ln_b, COMBINE_TM, with_bf16)


def kernel(x, even_w_in, even_b_f, even_conv_w, even_conv_b, even_conv_norm_g, even_conv_norm_b, even_w_out, odd_w_in, odd_w_a2, odd_b_a, odd_norm_g, odd_w_out, ln_mix_g, ln_mix_b, ln_ffn_g, ln_ffn_b, router_w, router_bias, expert_w_gate, expert_w_up, expert_w_down):
    batch, seq, d = x.shape
    n = batch * seq
    x2 = x.reshape(n, d)
    fw = FOX_HEADS * FOX_HEAD_DIM
    conv_ch = even_conv_w.shape[-1]
    kw = odd_w_a2.shape[-1]
    vw = odd_norm_g.shape[-1]
    row = lambda t: t.reshape(1, -1)

    wr_hi, wr_lo = _split_bf16(router_w.T)
    rbias = router_bias.reshape(N_EXPERTS, 1).astype(F32)
    experts_w = (expert_w_gate, expert_w_up, expert_w_down)

    w_in = even_w_in[0]
    q_scale = LOG2E * FOX_HEAD_DIM ** -0.5
    wf = jnp.zeros((d, LANES), F32).at[:, :FOX_HEADS].set(w_in[:, 3 * fw:3 * fw + FOX_HEADS])
    b_f = jnp.zeros((1, LANES), F32).at[0, :FOX_HEADS].set(even_b_f[0])
    h = _matmul(x2, even_w_in, 3 * fw, BF16, 512, fw, first_scale=q_scale)
    w_glu = even_w_in[:, :, 3 * fw + FOX_HEADS:]
    h_glu = _matmul(x2, w_glu, 2 * conv_ch, BF16, 512, conv_ch)
    c = _fox_gate(x2, wf.astype(BF16), b_f, batch, seq, 512)
    att = _fox_attention(h, c, batch, seq, 256)
    u = _conv_module(h_glu, 0, conv_ch // CONV_GROUP, even_conv_w[0, :, 0, :],
                     row(even_conv_b[0]), row(even_conv_norm_g[0]), row(even_conv_norm_b[0]),
                     batch, seq)
    xn, route, rcols, cnt = _outproj_ln_route(att, 0, u, 0, fw, even_w_out[0].astype(BF16), x2,
                                              row(ln_mix_g[0]), row(ln_mix_b[0]),
                                              wr_hi, wr_lo, rbias, 512)
    xn, xb = _moe(xn, route, rcols, cnt, *experts_w, 0, row(ln_ffn_g[0]), row(ln_ffn_b[0]), True)

    w_in = odd_w_in[0]
    h = _matmul(xb, odd_w_in, 2 * kw + 2 * vw, BF16, 1024, 1024)
    w_low = jnp.zeros((d, LANES), F32).at[:, :GLA_LOW_RANK].set(w_in[:, 2 * kw + 2 * vw:]).astype(BF16)
    w_a2 = jnp.zeros((LANES, kw), F32).at[:GLA_LOW_RANK].set(odd_w_a2[0]).astype(BF16)
    la = _gla_gate(xb, w_low, w_a2, row(odd_b_a[0]), 1024)
    o = _gla(h, la, row(odd_norm_g[0]), batch, seq)
    half = vw // 2
    xn, route, rcols, cnt = _outproj_ln_route(o, 0, o, 1, half, odd_w_out[0].astype(BF16), xn,
                                              row(ln_mix_g[1]), row(ln_mix_b[1]),
                                              wr_hi, wr_lo, rbias, 512)
    (xn,) = _moe(xn, route, rcols, cnt, *experts_w, 1, row(ln_ffn_g[1]), row(ln_ffn_b[1]), False)
    return xn.reshape(batch, seq, d)
```

```python
import functools

import jax
import jax.numpy as jnp
from jax import lax
from jax.experimental import pallas as pl
from jax.experimental.pallas import tpu as pltpu

F32 = jnp.float32
BF16 = jnp.bfloat16

DEPTH = 2
ALPHA = (2 * DEPTH) ** 0.25
LN_EPS = 1e-5
FOX_HEADS = 8
FOX_HEAD_DIM = 128
CONV_WIDTH = 31
CONV_GROUP = 128
GLA_HEADS = 4
GLA_LOW_RANK = 16
GLA_TAU = 16.0
GLA_CHUNK = 64
N_EXPERTS = 16
N_GROUPS = 4
EXPERTS_PER_GROUP = N_EXPERTS // N_GROUPS
TOP_K = 2

LANES = 128
V7X_VMEM_BYTES = 64 * 1024 * 1024
NEG_BIG = -1e30
LOG2E = 1.4426950408889634


def _vmem(mib):
    assert mib * 1024 * 1024 < V7X_VMEM_BYTES
    return mib * 1024 * 1024


def _sigmoid(z):
    return 1.0 / (1.0 + jnp.exp(-z))


def _log_sigmoid(z):
    return jnp.minimum(z, 0.0) - jnp.log1p(jnp.exp(-jnp.abs(z)))


def _dot(a, b):
    return jnp.dot(a, b, preferred_element_type=F32)


def _dot_nt(a, b):
    return lax.dot_general(a, b, (((1,), (1,)), ((), ())), preferred_element_type=F32)


def _dot_tn(a, b):
    return lax.dot_general(a, b, (((0,), (0,)), ((), ())), preferred_element_type=F32)


def _split_bf16(v):
    hi = v.astype(BF16)
    lo = (v - hi.astype(F32)).astype(BF16)
    return hi, lo


def _mm_kernel(x_ref, w_ref, o_ref, wb_ref, *, first_scale):
    @pl.when(pl.program_id(1) == 0)
    def _():
        wb_ref[...] = w_ref[...].T.astype(BF16)

    acc = _dot(x_ref[...].astype(BF16), wb_ref[...])
    if first_scale is not None:
        acc = acc * jnp.where(pl.program_id(0) == 0, first_scale, 1.0)
    o_ref[...] = acc.astype(o_ref.dtype)


def _matmul(x, w_t, n_cols, out_dtype, tm, tn, first_scale=None):
    m, k = x.shape
    assert m % tm == 0 and n_cols % tn == 0 and n_cols <= w_t.shape[1] and k == w_t.shape[2]
    return pl.pallas_call(
        functools.partial(_mm_kernel, first_scale=first_scale),
        grid=(n_cols // tn, m // tm),
        in_specs=[pl.BlockSpec((tm, k), lambda j, i: (i, 0)),
                  pl.BlockSpec((None, tn, k), lambda j, i: (0, j, 0))],
        out_specs=pl.BlockSpec((tm, tn), lambda j, i: (i, j)),
        out_shape=jax.ShapeDtypeStruct((m, n_cols), out_dtype),
        scratch_shapes=[pltpu.VMEM((k, tn), BF16)],
        compiler_params=pltpu.CompilerParams(
            dimension_semantics=("parallel", "arbitrary"), vmem_limit_bytes=_vmem(52)),
        name="dense_proj",
    )(x, w_t)


def _fox_gate_kernel(x_ref, wf_ref, bf_ref, c_ref, carry_ref):
    j = pl.program_id(1)

    @pl.when(j == 0)
    def _():
        carry_ref[...] = jnp.zeros_like(carry_ref)

    z = _dot(x_ref[...].astype(BF16), wf_ref[...]) + bf_ref[...]
    lf = _log_sigmoid(z)
    ts = lf.shape[0]
    row = lax.broadcasted_iota(jnp.int32, lf.shape, 0)
    sh = 1
    while sh < ts:
        lf = lf + jnp.where(row >= sh, pltpu.roll(lf, sh, axis=0), 0.0)
        sh *= 2
    c = lf + carry_ref[...]
    c_ref[...] = c * LOG2E
    carry_ref[...] = c[ts - 1:ts, :]


def _fox_gate(x2, wf, b_f, batch, seq, ts):
    d = x2.shape[1]
    nj = seq // ts
    return pl.pallas_call(
        _fox_gate_kernel,
        grid=(batch, nj),
        in_specs=[pl.BlockSpec((ts, d), lambda b, j: (b * nj + j, 0)),
                  pl.BlockSpec((d, LANES), lambda b, j: (0, 0)),
                  pl.BlockSpec((1, LANES), lambda b, j: (0, 0))],
        out_specs=pl.BlockSpec((ts, LANES), lambda b, j: (b * nj + j, 0)),
        out_shape=jax.ShapeDtypeStruct((batch * seq, LANES), F32),
        scratch_shapes=[pltpu.VMEM((1, LANES), F32)],
        compiler_params=pltpu.CompilerParams(
            dimension_semantics=("parallel", "arbitrary"), vmem_limit_bytes=_vmem(40)),
        name="fox_gate",
    )(x2, wf, b_f)


FOX_HEADS_PER_STEP = 8


def _fox_attn_kernel(q_ref, k_ref, v_ref, c_ref, o_ref, vt_ref, crep_ref, *, tile, heads):
    hh = pl.program_id(1)
    i = pl.program_id(2)
    hd = FOX_HEAD_DIM
    nq = vt_ref.shape[1]

    @pl.when(i == 0)
    def _():
        lane = lax.broadcasted_iota(jnp.int32, (tile, LANES), 1)
        for j in range(nq):
            cj = c_ref[pl.ds(j * tile, tile), :]
            for g in range(heads):
                col = jnp.sum(jnp.where(lane == hh * heads + g, cj, 0.0), axis=1, keepdims=True)
                crep_ref[g, j] = jnp.broadcast_to(col, (tile, LANES))
                vj = v_ref[pl.ds(j * tile, tile), g * hd:(g + 1) * hd]
                vt_ref[g, j] = vj.astype(F32).T.astype(BF16)

    def scores(g, j):
        r0 = pl.multiple_of(j * tile, tile)
        kj = k_ref[pl.ds(r0, tile), g * hd:(g + 1) * hd]
        cj = jnp.concatenate([crep_ref[g, j]] * (tile // LANES), axis=1)
        return _dot_nt(kj, q_ref[:, g * hd:(g + 1) * hd]) - cj

    def update(g, j, carry, st):
        m, l, acc = carry
        m_new = jnp.maximum(m, jnp.max(st, axis=0, keepdims=True))
        a = jnp.exp2(m - m_new)
        p = jnp.exp2(st - m_new)
        l = a * l + jnp.sum(p, axis=0, keepdims=True)
        acc = a * acc + _dot(vt_ref[g, j], p.astype(BF16))
        return m_new, l, acc

    def body(j, carries):
        sts = [scores(g, j) for g in range(heads)]
        return tuple(update(g, j, carries[g], sts[g]) for g in range(heads))

    init = tuple((jnp.full((1, tile), NEG_BIG, F32), jnp.zeros((1, tile), F32),
                  jnp.zeros((hd, tile), F32)) for _ in range(heads))
    carries = lax.fori_loop(0, i, body, init)
    key = lax.broadcasted_iota(jnp.int32, (tile, tile), 0)
    qry = lax.broadcasted_iota(jnp.int32, (tile, tile), 1)
    sts = [jnp.where(key <= qry, scores(g, i), NEG_BIG) for g in range(heads)]
    for g in range(heads):
        _, l, acc = update(g, i, carries[g], sts[g])
        o_ref[:, g * hd:(g + 1) * hd] = (acc * (1.0 / l)).T.astype(o_ref.dtype)


def _fox_attention(h, c, batch, seq, tile):
    n = h.shape[0]
    hd = FOX_HEAD_DIM
    nh = FOX_HEADS
    gh = FOX_HEADS_PER_STEP
    ng = nh // gh
    nq = seq // tile
    kern = functools.partial(_fox_attn_kernel, tile=tile, heads=gh)
    return pl.pallas_call(
        kern,
        grid=(batch, ng, nq),
        in_specs=[pl.BlockSpec((tile, gh * hd), lambda b, hh, i: (b * nq + i, hh)),
                  pl.BlockSpec((seq, gh * hd), lambda b, hh, i: (b, ng + hh)),
                  pl.BlockSpec((seq, gh * hd), lambda b, hh, i: (b, 2 * ng + hh)),
                  pl.BlockSpec((seq, LANES), lambda b, hh, i: (b, 0))],
        out_specs=pl.BlockSpec((tile, gh * hd), lambda b, hh, i: (b * nq + i, hh)),
        out_shape=jax.ShapeDtypeStruct((n, nh * hd), BF16),
        scratch_shapes=[pltpu.VMEM((gh, nq, hd, tile), BF16),
                        pltpu.VMEM((gh, nq, tile, LANES), F32)],
        compiler_params=pltpu.CompilerParams(
            dimension_semantics=("parallel", "parallel", "arbitrary"), vmem_limit_bytes=_vmem(40)),
        name="fox_attention",
    )(h, h, h, c)


CONV_PAD = 32
CONV_ROWS = 256


def _conv_kernel(a_ref, g_ref, w_ref, cb_ref, ng_ref, nb_ref, o_ref, pad_ref):
    seq = a_ref.shape[0]
    pad_ref[pl.ds(0, CONV_PAD), :] = jnp.zeros((CONV_PAD, LANES), F32)
    pad_ref[pl.ds(CONV_PAD, seq), :] = a_ref[...].astype(F32) * _sigmoid(g_ref[...].astype(F32))
    off = CONV_PAD - (CONV_WIDTH - 1)
    for r in range(seq // CONV_ROWS):
        base = r * CONV_ROWS
        acc = jnp.zeros((CONV_ROWS, LANES), F32)
        for j in range(CONV_WIDTH):
            acc = acc + pad_ref[pl.ds(base + off + j, CONV_ROWS), :] * w_ref[pl.ds(j, 1), :]
        y = acc + cb_ref[...]
        mu = jnp.mean(y, axis=-1, keepdims=True)
        yc = y - mu
        var = jnp.mean(yc * yc, axis=-1, keepdims=True)
        yn = yc * lax.rsqrt(var + LN_EPS) * ng_ref[...] + nb_ref[...]
        o_ref[pl.ds(base, CONV_ROWS), :] = (yn * _sigmoid(yn)).astype(o_ref.dtype)


def _conv_module(h, a_col, g_col, conv_w, conv_b, cn_g, cn_b, batch, seq):
    n = h.shape[0]
    ch = conv_w.shape[1]
    ng = ch // CONV_GROUP
    assert seq % CONV_ROWS == 0
    vec = lambda: pl.BlockSpec((1, CONV_GROUP), lambda b, g: (0, g))
    return pl.pallas_call(
        _conv_kernel,
        grid=(batch, ng),
        in_specs=[pl.BlockSpec((seq, CONV_GROUP), lambda b, g: (b, a_col + g)),
                  pl.BlockSpec((seq, CONV_GROUP), lambda b, g: (b, g_col + g)),
                  pl.BlockSpec((CONV_WIDTH, CONV_GROUP), lambda b, g: (0, g)),
                  vec(), vec(), vec()],
        out_specs=pl.BlockSpec((seq, CONV_GROUP), lambda b, g: (b, g)),
        out_shape=jax.ShapeDtypeStruct((n, ch), BF16),
        scratch_shapes=[pltpu.VMEM((seq + CONV_PAD, LANES), F32)],
        compiler_params=pltpu.CompilerParams(
            dimension_semantics=("parallel", "parallel"), vmem_limit_bytes=_vmem(32)),
        name="conv_module",
    )(h, h, conv_w, conv_b, cn_g, cn_b)


def _layer_norm(y, g, b):
    mu = jnp.mean(y, axis=-1, keepdims=True)
    yc = y - mu
    var = jnp.mean(yc * yc, axis=-1, keepdims=True)
    return yc * lax.rsqrt(var + LN_EPS) * g + b


def _route_rows(xn, wr_hi, wr_lo, rbias, count_ref):
    x_hi, x_lo = _split_bf16(xn)
    logits = _dot_nt(wr_hi, x_hi) + _dot_nt(wr_hi, x_lo) + _dot_nt(wr_lo, x_hi)
    scores = _sigmoid(logits)
    sel = scores + rbias
    s = [sel[e:e + 1, :] for e in range(N_EXPERTS)]
    r = [scores[e:e + 1, :] for e in range(N_EXPERTS)]
    pg = EXPERTS_PER_GROUP

    def top2_sum(vals):
        best = None
        for a in range(len(vals)):
            for b in range(a + 1, len(vals)):
                t = vals[a] + vals[b]
                best = t if best is None else jnp.maximum(best, t)
        return best

    gs = [top2_sum(s[g * pg:(g + 1) * pg]) for g in range(N_GROUPS)]
    best, grp = gs[0], jnp.zeros_like(gs[0], dtype=jnp.int32)
    for g in range(1, N_GROUPS):
        upd = gs[g] > best
        best = jnp.where(upd, gs[g], best)
        grp = jnp.where(upd, g, grp)

    def pick_group(vals, k):
        out = vals[(N_GROUPS - 1) * pg + k]
        for g in range(N_GROUPS - 2, -1, -1):
            out = jnp.where(grp == g, vals[g * pg + k], out)
        return out

    v = [pick_group(s, k) for k in range(pg)]
    w = [pick_group(r, k) for k in range(pg)]
    b1, i1, w1 = v[0], jnp.zeros_like(grp), w[0]
    for k in range(1, pg):
        upd = v[k] > b1
        b1 = jnp.where(upd, v[k], b1)
        i1 = jnp.where(upd, k, i1)
        w1 = jnp.where(upd, w[k], w1)
    b2 = jnp.full_like(b1, -jnp.inf)
    i2, w2 = jnp.zeros_like(grp), jnp.zeros_like(w1)
    for k in range(pg):
        upd = jnp.logical_and(i1 != k, v[k] > b2)
        b2 = jnp.where(upd, v[k], b2)
        i2 = jnp.where(upd, k, i2)
        w2 = jnp.where(upd, w[k], w2)
    tot = w1 + w2
    e1 = grp * pg + i1
    e2 = grp * pg + i2
    eid = lax.broadcasted_iota(jnp.int32, scores.shape, 0)
    oh1 = jnp.where(eid == e1, 1.0, 0.0)
    oh2 = jnp.where(eid == e2, 1.0, 0.0)
    cnt = oh1 + oh2
    tm = cnt.shape[1]
    lane = lax.broadcasted_iota(jnp.int32, cnt.shape, 1)
    incl = cnt
    sh = 1
    while sh < tm:
        incl = incl + jnp.where(lane >= sh, pltpu.roll(incl, sh, axis=1), 0.0)
        sh *= 2
    before = incl - cnt + count_ref[...]
    count_ref[...] = count_ref[...] + incl[:, tm - 1:tm]
    rank1 = jnp.sum(oh1 * before, axis=0, keepdims=True)
    rank2 = jnp.sum(oh2 * before, axis=0, keepdims=True)
    rows = [e1.astype(F32), e2.astype(F32), w1 / tot, w2 / tot, rank1, rank2]
    rows += [jnp.zeros_like(w1)] * (8 - len(rows))
    return jnp.concatenate(rows, axis=0)


OUTPROJ_SUBTILES = 2


def _outproj_kernel(a1_ref, a2_ref, w_ref, x_ref, g_ref, b_ref, wrh_ref, wrl_ref, rb_ref,
                    xn_ref, rt_ref, rc_ref, cnt_ref, count_ref):
    @pl.when(pl.program_id(0) == 0)
    def _():
        count_ref[...] = jnp.zeros_like(count_ref)

    k1 = a1_ref.shape[1]
    k2 = a2_ref.shape[1]
    tm = x_ref.shape[0]
    sub = tm // OUTPROJ_SUBTILES
    rs = [pl.ds(t * sub, sub) for t in range(OUTPROJ_SUBTILES)]
    mix = [_dot(a1_ref[r, :], w_ref[pl.ds(0, k1), :]) + _dot(a2_ref[r, :], w_ref[pl.ds(k1, k2), :])
           for r in rs]
    for t, r in enumerate(rs):
        xn = _layer_norm(ALPHA * x_ref[r, :] + mix[t], g_ref[...], b_ref[...])
        xn_ref[r, :] = xn
        rows = _route_rows(xn, wrh_ref[...], wrl_ref[...], rb_ref[...], count_ref)
        rt_ref[:, r] = rows
        rc_ref[r, :] = rows.T
    cnt_ref[...] = jnp.broadcast_to(count_ref[...], cnt_ref.shape)


def _outproj_ln_route(a1, a1_col, a2, a2_col, kw, w, xres, ln_g, ln_b, wr_hi, wr_lo, rbias, tm):
    n, d = xres.shape
    full = lambda shape: pl.BlockSpec(shape, lambda i: (0, 0))
    return pl.pallas_call(
        _outproj_kernel,
        grid=(n // tm,),
        in_specs=[pl.BlockSpec((tm, kw), lambda i: (i, a1_col)),
                  pl.BlockSpec((tm, kw), lambda i: (i, a2_col)),
                  full(w.shape),
                  pl.BlockSpec((tm, d), lambda i: (i, 0)),
                  full((1, d)), full((1, d)),
                  full(wr_hi.shape), full(wr_lo.shape), full(rbias.shape)],
        out_specs=[pl.BlockSpec((tm, d), lambda i: (i, 0)),
                   pl.BlockSpec((8, tm), lambda i: (0, i)),
                   pl.BlockSpec((tm, 8), lambda i: (i, 0)),
                   pl.BlockSpec((N_EXPERTS, LANES), lambda i: (0, 0))],
        out_shape=[jax.ShapeDtypeStruct((n, d), F32),
                   jax.ShapeDtypeStruct((8, n), F32),
                   jax.ShapeDtypeStruct((n, 8), F32),
                   jax.ShapeDtypeStruct((N_EXPERTS, LANES), F32)],
        scratch_shapes=[pltpu.VMEM((N_EXPERTS, 1), F32)],
        compiler_params=pltpu.CompilerParams(
            dimension_semantics=("arbitrary",), vmem_limit_bytes=_vmem(52)),
        name="outproj_ln_route",
    )(a1, a2, w, xres, ln_g, ln_b, wr_hi, wr_lo, rbias)


EXPERT_CHUNK_ROWS = 128
EXPERT_STAGES = 4


TOK_WINDOW = 1024


def _expert_kernel(be_ref, na_ref, kb_ref, nk_ref, nx_ref, sl_ref, src_ref,
                   tok_hbm, x_hbm, wg_hbm, wu_hbm, wd_hbm, o_ref,
                   wgu_ref, wdn_ref, stg_ref, sem, xbuf_ref, tok_ref, gsem, tsem, *, layer):
    i = pl.program_id(0)
    tm, d = o_ref.shape
    de = wdn_ref.shape[1]
    na = na_ref[0]

    def tok_copy(j):
        base = pl.multiple_of((src_ref[j] // TOK_WINDOW) * TOK_WINDOW, TOK_WINDOW)
        dst = pl.multiple_of((j % 2) * (2 * TOK_WINDOW), 2 * TOK_WINDOW)
        return pltpu.make_async_copy(tok_hbm.at[pl.ds(base, 2 * TOK_WINDOW)],
                                     tok_ref.at[pl.ds(dst, 2 * TOK_WINDOW)], tsem.at[j % 2])

    def start_rows(j):
        buf = j % 2
        first = buf * (2 * TOK_WINDOW) + src_ref[j] % TOK_WINDOW
        for r in range(tm):
            tok = tok_ref[first + r]
            pltpu.make_async_copy(x_hbm.at[pl.ds(tok, 1), :], xbuf_ref.at[buf, pl.ds(r, 1), :],
                                  gsem.at[buf]).start()

    def wait_rows(j):
        pltpu.make_async_copy(x_hbm.at[pl.ds(0, tm), :], xbuf_ref.at[j % 2], gsem.at[j % 2]).wait()
    ch = EXPERT_CHUNK_ROWS
    n_in = d // ch
    n_dn = de // ch
    n_chunks = 2 * n_in + n_dn

    def chunk_copy(e, c):
        st = c % EXPERT_STAGES

        def gate_up(w_hbm, first, col0):
            r0 = pl.multiple_of((c - first) * ch, ch)
            cp = pltpu.make_async_copy(w_hbm.at[layer, e, pl.ds(r0, ch), :],
                                       stg_ref.at[st, :, pl.ds(0, de)], sem.at[st])

            def convert(slot):
                wgu_ref[slot, pl.ds(r0, ch), pl.ds(col0, de)] = stg_ref[st, :, pl.ds(0, de)].astype(BF16)
            return cp, convert

        def down():
            r0 = pl.multiple_of((c - 2 * n_in) * ch, ch)
            cp = pltpu.make_async_copy(wd_hbm.at[layer, e, pl.ds(r0, ch), :],
                                       stg_ref.at[st], sem.at[st])

            def convert(slot):
                wdn_ref[slot, pl.ds(r0, ch), :] = stg_ref[st].astype(BF16)
            return cp, convert

        return ((c < n_in, lambda: gate_up(wg_hbm, 0, 0)),
                (jnp.logical_and(c >= n_in, c < 2 * n_in), lambda: gate_up(wu_hbm, n_in, de)),
                (c >= 2 * n_in, down))

    def start_chunk(e, c):
        for cond, make in chunk_copy(e, c):
            @pl.when(cond)
            def _():
                make()[0].start()

    def finish_chunk(e, c, slot):
        for cond, make in chunk_copy(e, c):
            @pl.when(cond)
            def _():
                cp, convert = make()
                cp.wait()
                convert(slot)

    def prime(e):
        for c in range(EXPERT_STAGES):
            start_chunk(e, jnp.int32(c))

    def process(e, slot, c_lo, c_hi):
        def body(c, _):
            finish_chunk(e, c, slot)

            @pl.when(c + EXPERT_STAGES < n_chunks)
            def _():
                start_chunk(e, c + EXPERT_STAGES)
            return 0
        lax.fori_loop(c_lo, c_hi, body, 0)

    @pl.when(i >= na)
    def _():
        o_ref[...] = jnp.zeros_like(o_ref)

    @pl.when(i < na)
    def _():
        e = be_ref[i]
        kb = kb_ref[i]
        nk = nk_ref[i]
        nxt = nx_ref[i]
        slot = sl_ref[i]

        @pl.when(i == 0)
        def _():
            tok_copy(0).start()
            prime(e)
            tok_copy(0).wait()
            start_rows(0)
            tok_copy(1).start()
            process(e, slot, 0, n_chunks)

        @pl.when(jnp.logical_and(kb == 0, nxt >= 0))
        def _():
            prime(nxt)

        tok_copy(i + 1).wait()
        start_rows(i + 1)
        tok_copy(i + 2).start()
        wait_rows(i)
        gu = _dot(xbuf_ref[i % 2].astype(BF16), wgu_ref[slot])
        g = gu[:, :de]
        u = gu[:, de:]
        hdn = (g * _sigmoid(g)) * u
        o_ref[...] = _dot(hdn.astype(BF16), wdn_ref[slot])

        @pl.when(i == na - 1)
        def _():
            wait_rows(i + 1)
            tok_copy(i + 2).wait()

        @pl.when(nxt >= 0)
        def _():
            process(nxt, 1 - slot, (n_chunks * kb) // nk, (n_chunks * (kb + 1)) // nk)


def _expert_ffn(xn, sorted_tok, tables, w_gate, w_up, w_down, layer, tm, nb):
    d = xn.shape[1]
    de = w_down.shape[2]
    assert d % EXPERT_CHUNK_ROWS == 0 and de % EXPERT_CHUNK_ROWS == 0 and de % LANES == 0
    assert tm <= TOK_WINDOW
    hbm = lambda: pl.BlockSpec(memory_space=pl.ANY)
    return pl.pallas_call(
        functools.partial(_expert_kernel, layer=layer),
        grid_spec=pltpu.PrefetchScalarGridSpec(
            num_scalar_prefetch=len(tables),
            grid=(nb,),
            in_specs=[hbm(), hbm(), hbm(), hbm(), hbm()],
            out_specs=pl.BlockSpec((tm, d), lambda i, *_: (i, 0)),
            scratch_shapes=[pltpu.VMEM((2, d, 2 * de), BF16),
                            pltpu.VMEM((2, de, d), BF16),
                            pltpu.VMEM((EXPERT_STAGES, EXPERT_CHUNK_ROWS, d), F32),
                            pltpu.SemaphoreType.DMA((EXPERT_STAGES,)),
                            pltpu.VMEM((2, tm, d), F32),
                            pltpu.SMEM((2 * 2 * TOK_WINDOW,), jnp.int32),
                            pltpu.SemaphoreType.DMA((2,)),
                            pltpu.SemaphoreType.DMA((2,))]),
        out_shape=jax.ShapeDtypeStruct((nb * tm, d), F32),
        compiler_params=pltpu.CompilerParams(
            dimension_semantics=("arbitrary",), vmem_limit_bytes=_vmem(60)),
        name="expert_ffn",
    )(*tables, sorted_tok, xn, w_gate, w_up, w_down)


ROUTE_GATE_COL = 2


def _combine_kernel(pos_ref, posn_ref, x_ref, ys_hbm, rc_ref, g_ref, b_ref, xn_ref, *rest):
    xb_ref = rest[0] if len(rest) == 3 else None
    ybuf_ref, sem = rest[-2:]
    i = pl.program_id(0)
    nt = pl.num_programs(0)
    tm, d = x_ref.shape

    def start_rows(p_ref, buf):
        for r in range(tm):
            for k in range(TOP_K):
                pltpu.make_async_copy(ys_hbm.at[pl.ds(p_ref[0, 0, k * tm + r], 1), :],
                                      ybuf_ref.at[buf, k, pl.ds(r, 1), :], sem.at[buf]).start()

    @pl.when(i == 0)
    def _():
        start_rows(pos_ref, 0)

    @pl.when(i + 1 < nt)
    def _():
        start_rows(posn_ref, (i + 1) % 2)

    buf = i % 2
    for k in range(TOP_K):
        pltpu.make_async_copy(ys_hbm.at[pl.ds(0, tm), :], ybuf_ref.at[buf, k], sem.at[buf]).wait()
    g0 = rc_ref[:, ROUTE_GATE_COL:ROUTE_GATE_COL + 1]
    g1 = rc_ref[:, ROUTE_GATE_COL + 1:ROUTE_GATE_COL + 2]
    y = ALPHA * x_ref[...] + (g0 * ybuf_ref[buf, 0] + g1 * ybuf_ref[buf, 1])
    xn = _layer_norm(y, g_ref[...], b_ref[...])
    xn_ref[...] = xn
    if xb_ref is not None:
        xb_ref[...] = xn.astype(BF16)


POS_TILES_PER_STEP = 8


def _slot_pos_kernel(rt_ref, ps_ref, o_ref, *, tm):
    eid = lax.broadcasted_iota(jnp.int32, (N_EXPERTS, rt_ref.shape[1]), 0)
    for k in range(TOP_K):
        e = rt_ref[k:k + 1, :].astype(jnp.int32)
        base = jnp.sum(jnp.where(eid == e, ps_ref[...], 0.0), axis=0, keepdims=True)
        pos = (base + rt_ref[2 * TOP_K + k:2 * TOP_K + k + 1, :]).astype(jnp.int32)
        for t in range(o_ref.shape[0]):
            o_ref[t, :, pl.ds(k * tm, tm)] = pos[:, t * tm:(t + 1) * tm]


def _slot_positions(route, pstart, tm):
    n = route.shape[1]
    nt = n // tm
    step = POS_TILES_PER_STEP if nt % POS_TILES_PER_STEP == 0 else 1
    return pl.pallas_call(
        functools.partial(_slot_pos_kernel, tm=tm),
        grid=(nt // step,),
        in_specs=[pl.BlockSpec((8, step * tm), lambda i: (0, i)),
                  pl.BlockSpec((N_EXPERTS, 1), lambda i: (0, 0))],
        out_specs=pl.BlockSpec((step, 1, TOP_K * tm), lambda i: (i, 0, 0)),
        out_shape=jax.ShapeDtypeStruct((nt, 1, TOP_K * tm), jnp.int32),
        compiler_params=pltpu.CompilerParams(dimension_semantics=("parallel",)),
        name="slot_positions",
    )(route, pstart)


def _combine_ln(x, ys, pos_t, route_cols, ln_g, ln_b, tm, with_bf16):
    n, d = x.shape
    nt = n // tm
    n_out = 2 if with_bf16 else 1
    row = lambda: pl.BlockSpec((tm, d), lambda i: (i, 0))
    vec = lambda: pl.BlockSpec((1, d), lambda i: (0, 0))
    smem = lambda imap: pl.BlockSpec((1, 1, TOP_K * tm), imap, memory_space=pltpu.SMEM)
    return pl.pallas_call(
        _combine_kernel,
        grid=(nt,),
        in_specs=[smem(lambda i: (i, 0, 0)), smem(lambda i: (jnp.minimum(i + 1, nt - 1), 0, 0)),
                  row(), pl.BlockSpec(memory_space=pl.ANY),
                  pl.BlockSpec((tm, 8), lambda i: (i, 0)), vec(), vec()],
        out_specs=[row(), row()][:n_out],
        out_shape=[jax.ShapeDtypeStruct((n, d), F32), jax.ShapeDtypeStruct((n, d), BF16)][:n_out],
        scratch_shapes=[pltpu.VMEM((2, TOP_K, tm, d), F32), pltpu.SemaphoreType.DMA((2,))],
        compiler_params=pltpu.CompilerParams(
            dimension_semantics=("arbitrary",), vmem_limit_bytes=_vmem(48)),
        name="combine_ln",
    )(pos_t, pos_t, x, ys, route_cols, ln_g, ln_b)


def _gla_gate_kernel(x_ref, wl_ref, w2_ref, ba_ref, o_ref):
    low = _dot(x_ref[...], wl_ref[...]).astype(BF16)
    z = _dot(low, w2_ref[...]) + ba_ref[...]
    o_ref[...] = _log_sigmoid(z) * (1.0 / GLA_TAU)


def _gla_gate(xb, w_low, w_a2, b_a, tm):
    n, d = xb.shape
    kw = w_a2.shape[1]
    return pl.pallas_call(
        _gla_gate_kernel,
        grid=(n // tm,),
        in_specs=[pl.BlockSpec((tm, d), lambda i: (i, 0)),
                  pl.BlockSpec(w_low.shape, lambda i: (0, 0)),
                  pl.BlockSpec(w_a2.shape, lambda i: (0, 0)),
                  pl.BlockSpec((1, kw), lambda i: (0, 0))],
        out_specs=pl.BlockSpec((tm, kw), lambda i: (i, 0)),
        out_shape=jax.ShapeDtypeStruct((n, kw), F32),
        compiler_params=pltpu.CompilerParams(
            dimension_semantics=("parallel",), vmem_limit_bytes=_vmem(32)),
        name="gla_gate",
    )(xb, w_low, w_a2, b_a)


GLA_ROWS = 512


def _gla_kernel(q_ref, k_ref, v_ref, la_ref, g_ref, ng_ref, o_ref, state_ref, *, scale):
    rows, kw = q_ref.shape
    nh = state_ref.shape[0]
    hk, hv = state_ref.shape[1], state_ref.shape[2]
    ck = GLA_CHUNK

    @pl.when(pl.program_id(1) == 0)
    def _():
        state_ref[...] = jnp.zeros_like(state_ref)

    row = lax.broadcasted_iota(jnp.int32, (ck, ck), 0)
    col = lax.broadcasted_iota(jnp.int32, (ck, ck), 1)
    causal = row >= col
    tri = jnp.where(causal, 1.0, 0.0).astype(BF16)
    ones = jnp.ones((ck, LANES), BF16)

    def chunk(c, _):
        r0 = pl.multiple_of(c * ck, ck)
        heads = range(nh)
        ks = [slice(hh * hk, (hh + 1) * hk) for hh in heads]
        vs = [slice(hh * hv, (hh + 1) * hv) for hh in heads]
        la = [_split_bf16(la_ref[pl.ds(r0, ck), ks[hh]]) for hh in heads]
        b = [_dot(tri, la[hh][0]) + _dot(tri, la[hh][1]) for hh in heads]
        dsum = [_dot_tn(la[hh][0], ones) + _dot_tn(la[hh][1], ones) for hh in heads]
        qt, kt, kend = [], [], []
        for hh in heads:
            q = q_ref[pl.ds(r0, ck), ks[hh]].astype(F32) * scale
            k = k_ref[pl.ds(r0, ck), ks[hh]].astype(F32)
            b_last = b[hh][ck - 1:ck, :]
            qt.append((q * jnp.exp(b[hh])).astype(BF16))
            kt.append((k * jnp.exp(-b[hh])).astype(BF16))
            kend.append((k * jnp.exp(b_last - b[hh])).astype(BF16))
        v = [v_ref[pl.ds(r0, ck), vs[hh]] for hh in heads]
        state = [state_ref[hh] for hh in heads]
        attn = [_dot_nt(qt[hh], kt[hh]) for hh in heads]
        inter = [_dot(qt[hh], state[hh].astype(BF16)) for hh in heads]
        kv = [_dot_tn(kend[hh], v[hh]) for hh in heads]
        for hh in heads:
            decay = jnp.concatenate([jnp.exp(dsum[hh])] * (hv // LANES), axis=1)
            state_ref[hh] = state[hh] * decay + kv[hh]
        for hh in heads:
            o = _dot(jnp.where(causal, attn[hh], 0.0).astype(BF16), v[hh]) + inter[hh]
            o = o * lax.rsqrt(jnp.mean(o * o, axis=-1, keepdims=True) + LN_EPS)
            gate = g_ref[pl.ds(r0, ck), vs[hh]].astype(F32)
            o = o * ng_ref[:, vs[hh]] * (gate * _sigmoid(gate))
            o_ref[pl.ds(r0, ck), vs[hh]] = o.astype(o_ref.dtype)
        return 0

    lax.fori_loop(0, rows // ck, chunk, 0, unroll=2)


def _gla(h, la, norm_g, batch, seq):
    n = h.shape[0]
    kw = la.shape[1]
    vw = norm_g.shape[1]
    nh = GLA_HEADS
    hk, hv = kw // nh, vw // nh
    rows = GLA_ROWS
    nj = seq // rows
    kern = functools.partial(_gla_kernel, scale=hk ** -0.5)
    rowblk = lambda b, j: b * nj + j
    return pl.pallas_call(
        kern,
        grid=(batch, nj),
        in_specs=[pl.BlockSpec((rows, kw), lambda b, j: (rowblk(b, j), 0)),
                  pl.BlockSpec((rows, kw), lambda b, j: (rowblk(b, j), 1)),
                  pl.BlockSpec((rows, vw), lambda b, j: (rowblk(b, j), 2 * kw // vw)),
                  pl.BlockSpec((rows, kw), lambda b, j: (rowblk(b, j), 0)),
                  pl.BlockSpec((rows, vw), lambda b, j: (rowblk(b, j), 2 * kw // vw + 1)),
                  pl.BlockSpec((1, vw), lambda b, j: (0, 0))],
        out_specs=pl.BlockSpec((rows, vw), lambda b, j: (rowblk(b, j), 0)),
        out_shape=jax.ShapeDtypeStruct((n, vw), BF16),
        scratch_shapes=[pltpu.VMEM((nh, hk, hv), F32)],
        compiler_params=pltpu.CompilerParams(
            dimension_semantics=("parallel", "arbitrary"), vmem_limit_bytes=_vmem(48)),
        name="gla",
    )(h, h, h, la, h, norm_g)


EXPERT_TM = 256
COMBINE_TM = 256


def _moe(xn, route, route_cols, counts, w_gate, w_up, w_down, layer, ln_g, ln_b, with_bf16):
    n, d = xn.shape
    a = n * TOP_K
    tm = EXPERT_TM
    nb = a // tm + N_EXPERTS
    i32 = jnp.int32
    eidx = route[:TOP_K].astype(i32)
    experts = jnp.arange(N_EXPERTS, dtype=i32)
    counts = counts[:, 0].astype(i32)
    starts = jnp.cumsum(counts) - counts
    nblk = (counts + tm - 1) // tm
    bend = jnp.cumsum(nblk)
    pstart = (bend - nblk) * tm
    n_active = bend[-1:].astype(i32)
    blk = jnp.minimum(jnp.arange(nb, dtype=i32), n_active[0] - 1)
    block_expert = jnp.minimum(jnp.sum((bend[None, :] <= blk[:, None]).astype(i32), axis=1),
                               N_EXPERTS - 1)
    assign = jnp.arange(n, dtype=i32)[None, :] * TOP_K + jnp.arange(TOP_K, dtype=i32)[:, None]
    sorted_tok = (lax.sort((eidx * a + assign).reshape(a)) % a) // TOP_K
    sorted_tok = jnp.concatenate([sorted_tok, jnp.zeros((2 * TOK_WINDOW,), i32)])
    kb = blk - (bend - nblk)[block_expert]
    nonempty = nblk > 0
    later = jnp.logical_and(experts[None, :] > experts[:, None], nonempty[None, :])
    nxt_e = jnp.min(jnp.where(later, experts[None, :], N_EXPERTS), axis=1)
    nxt_e = jnp.where(nxt_e < N_EXPERTS, nxt_e, -1)
    slot_e = (jnp.cumsum(nonempty.astype(i32)) - 1) % 2
    blk_src = jnp.concatenate([starts[block_expert] + kb * tm, jnp.zeros((2,), i32)])
    tables = (block_expert, n_active, kb, nblk[block_expert], nxt_e[block_expert],
              slot_e[block_expert], blk_src)
    ys = _expert_ffn(xn, sorted_tok, tuple(t.astype(i32) for t in tables),
                     w_gate, w_up, w_down, layer, tm, nb)
    pos = _slot_positions(route, pstart.reshape(N_EXPERTS, 1).astype(F32), COMBINE_TM)
    return _combine_ln(xn, ys, pos, route_cols, ln_g, ln_b, COMBINE_TM, with_bf16)


def kernel(x, even_w_in, even_b_f, even_conv_w, even_conv_b, even_conv_norm_g, even_conv_norm_b, even_w_out, odd_w_in, odd_w_a2, odd_b_a, odd_norm_g, odd_w_out, ln_mix_g, ln_mix_b, ln_ffn_g, ln_ffn_b, router_w, router_bias, expert_w_gate, expert_w_up, expert_w_down):
    batch, seq, d = x.shape
    n = batch * seq
    x2 = x.reshape(n, d)
    fw = FOX_HEADS * FOX_HEAD_DIM
    conv_ch = even_conv_w.shape[-1]
    kw = odd_w_a2.shape[-1]
    vw = odd_norm_g.shape[-1]
    row = lambda t: t.reshape(1, -1)

    wr_hi, wr_lo = _split_bf16(router_w.T)
    rbias = router_bias.reshape(N_EXPERTS, 1).astype(F32)
    experts_w = (expert_w_gate, expert_w_up, expert_w_down)

    w_t = jnp.swapaxes(even_w_in, 1, 2)
    q_scale = LOG2E * FOX_HEAD_DIM ** -0.5
    wf = jnp.zeros((d, LANES), F32).at[:, :FOX_HEADS].set(even_w_in[0, :, 3 * fw:3 * fw + FOX_HEADS])
    b_f = jnp.zeros((1, LANES), F32).at[0, :FOX_HEADS].set(even_b_f[0])
    h = _matmul(x2, w_t, 3 * fw, BF16, 512, fw, first_scale=q_scale)
    h_glu = _matmul(x2, w_t[:, 3 * fw + FOX_HEADS:], 2 * conv_ch, BF16, 512, conv_ch)
    c = _fox_gate(x2, wf.astype(BF16), b_f, batch, seq, 512)
    att = _fox_attention(h, c, batch, seq, 256)
    u = _conv_module(h_glu, 0, conv_ch // CONV_GROUP, even_conv_w[0, :, 0, :],
                     row(even_conv_b[0]), row(even_conv_norm_g[0]), row(even_conv_norm_b[0]),
                     batch, seq)
    xn, route, rcols, cnt = _outproj_ln_route(att, 0, u, 0, fw, even_w_out[0].astype(BF16), x2,
                                              row(ln_mix_g[0]), row(ln_mix_b[0]),
                                              wr_hi, wr_lo, rbias, 512)
    xn, xb = _moe(xn, route, rcols, cnt, *experts_w, 0, row(ln_ffn_g[0]), row(ln_ffn_b[0]), True)

    w_t = jnp.swapaxes(odd_w_in, 1, 2)
    h = _matmul(xb, w_t, 2 * kw + 2 * vw, BF16, 1024, 1024)
    w_low = jnp.zeros((d, LANES), F32).at[:, :GLA_LOW_RANK].set(odd_w_in[0, :, 2 * kw + 2 * vw:])
    w_low = w_low.astype(BF16)
    w_a2 = jnp.zeros((LANES, kw), F32).at[:GLA_LOW_RANK].set(odd_w_a2[0]).astype(BF16)
    la = _gla_gate(xb, w_low, w_a2, row(odd_b_a[0]), 1024)
    o = _gla(h, la, row(odd_norm_g[0]), batch, seq)
    half = vw // 2
    xn, route, rcols, cnt = _outproj_ln_route(o, 0, o, 1, half, odd_w_out[0].astype(BF16), xn,
                                              row(ln_mix_g[1]), row(ln_mix_b[1]),
                                              wr_hi, wr_lo, rbias, 512)
    (xn,) = _moe(xn, route, rcols, cnt, *experts_w, 1, row(ln_ffn_g[1]), row(ln_ffn_b[1]), False)
    return xn.reshape(batch, seq, d)
```

```python
import functools

import jax
import jax.numpy as jnp
from jax import lax
from jax.experimental import pallas as pl
from jax.experimental.pallas import tpu as pltpu

F32 = jnp.float32
BF16 = jnp.bfloat16

DEPTH = 2
ALPHA = (2 * DEPTH) ** 0.25
LN_EPS = 1e-5
FOX_HEADS = 8
FOX_HEAD_DIM = 128
CONV_WIDTH = 31
CONV_GROUP = 128
GLA_HEADS = 4
GLA_LOW_RANK = 16
GLA_TAU = 16.0
GLA_CHUNK = 64
N_EXPERTS = 16
N_GROUPS = 4
EXPERTS_PER_GROUP = N_EXPERTS // N_GROUPS
TOP_K = 2

LANES = 128
V7X_VMEM_BYTES = 64 * 1024 * 1024
NEG_BIG = -1e30
LOG2E = 1.4426950408889634


def _vmem(mib):
    assert mib * 1024 * 1024 < V7X_VMEM_BYTES
    return mib * 1024 * 1024


def _sigmoid(z):
    return 1.0 / (1.0 + jnp.exp(-z))


def _log_sigmoid(z):
    return jnp.minimum(z, 0.0) - jnp.log1p(jnp.exp(-jnp.abs(z)))


def _dot(a, b):
    return jnp.dot(a, b, preferred_element_type=F32)


def _dot_nt(a, b):
    return lax.dot_general(a, b, (((1,), (1,)), ((), ())), preferred_element_type=F32)


def _dot_tn(a, b):
    return lax.dot_general(a, b, (((0,), (0,)), ((), ())), preferred_element_type=F32)


def _split_bf16(v):
    hi = v.astype(BF16)
    lo = (v - hi.astype(F32)).astype(BF16)
    return hi, lo


def _pack_halves(v):
    c = v.shape[1] // 2
    hi = lax.bitcast_convert_type(v[:, :c].astype(BF16).astype(F32), jnp.uint32)
    lo = lax.bitcast_convert_type(v[:, c:].astype(BF16).astype(F32), jnp.uint32)
    return hi | (lo >> 16)


def _unpack_halves(p):
    hi = lax.bitcast_convert_type(p & jnp.uint32(0xFFFF0000), F32)
    lo = lax.bitcast_convert_type(p << 16, F32)
    return hi, lo


def _mm_kernel(x_ref, w_ref, o_ref, wb_ref, *, first_scale):
    @pl.when(pl.program_id(1) == 0)
    def _():
        wb_ref[...] = w_ref[...].T.astype(BF16)

    acc = _dot(x_ref[...].astype(BF16), wb_ref[...])
    if first_scale is not None:
        acc = acc * jnp.where(pl.program_id(0) == 0, first_scale, 1.0)
    o_ref[...] = acc.astype(o_ref.dtype)


def _matmul(x, w_t, n_cols, out_dtype, tm, tn, first_scale=None):
    m, k = x.shape
    assert m % tm == 0 and n_cols % tn == 0 and n_cols <= w_t.shape[1] and k == w_t.shape[2]
    return pl.pallas_call(
        functools.partial(_mm_kernel, first_scale=first_scale),
        grid=(n_cols // tn, m // tm),
        in_specs=[pl.BlockSpec((tm, k), lambda j, i: (i, 0)),
                  pl.BlockSpec((None, tn, k), lambda j, i: (0, j, 0))],
        out_specs=pl.BlockSpec((tm, tn), lambda j, i: (i, j)),
        out_shape=jax.ShapeDtypeStruct((m, n_cols), out_dtype),
        scratch_shapes=[pltpu.VMEM((k, tn), BF16)],
        compiler_params=pltpu.CompilerParams(
            dimension_semantics=("parallel", "arbitrary"), vmem_limit_bytes=_vmem(52)),
        name="dense_proj",
    )(x, w_t)


def _fox_gate_kernel(x_ref, wf_ref, bf_ref, c_ref, carry_ref):
    j = pl.program_id(1)

    @pl.when(j == 0)
    def _():
        carry_ref[...] = jnp.zeros_like(carry_ref)

    z = _dot(x_ref[...].astype(BF16), wf_ref[...]) + bf_ref[...]
    lf = _log_sigmoid(z)
    ts = lf.shape[0]
    row = lax.broadcasted_iota(jnp.int32, lf.shape, 0)
    sh = 1
    while sh < ts:
        lf = lf + jnp.where(row >= sh, pltpu.roll(lf, sh, axis=0), 0.0)
        sh *= 2
    c = lf + carry_ref[...]
    c_ref[...] = c * LOG2E
    carry_ref[...] = c[ts - 1:ts, :]


def _fox_gate(x2, wf, b_f, batch, seq, ts):
    d = x2.shape[1]
    nj = seq // ts
    return pl.pallas_call(
        _fox_gate_kernel,
        grid=(batch, nj),
        in_specs=[pl.BlockSpec((ts, d), lambda b, j: (b * nj + j, 0)),
                  pl.BlockSpec((d, LANES), lambda b, j: (0, 0)),
                  pl.BlockSpec((1, LANES), lambda b, j: (0, 0))],
        out_specs=pl.BlockSpec((ts, LANES), lambda b, j: (b * nj + j, 0)),
        out_shape=jax.ShapeDtypeStruct((batch * seq, LANES), F32),
        scratch_shapes=[pltpu.VMEM((1, LANES), F32)],
        compiler_params=pltpu.CompilerParams(
            dimension_semantics=("parallel", "arbitrary"), vmem_limit_bytes=_vmem(40)),
        name="fox_gate",
    )(x2, wf, b_f)


FOX_HEADS_PER_STEP = 8


def _fox_attn_kernel(q_ref, k_ref, v_ref, c_ref, o_ref, vt_ref, crep_ref, *, tile, heads):
    hh = pl.program_id(1)
    i = pl.program_id(2)
    hd = FOX_HEAD_DIM
    nq = vt_ref.shape[1]

    @pl.when(i == 0)
    def _():
        lane = lax.broadcasted_iota(jnp.int32, (tile, LANES), 1)
        for j in range(nq):
            cj = c_ref[pl.ds(j * tile, tile), :]
            for g in range(heads):
                col = jnp.sum(jnp.where(lane == hh * heads + g, cj, 0.0), axis=1, keepdims=True)
                crep_ref[g, j] = jnp.broadcast_to(col, (tile, LANES))
                vj = v_ref[pl.ds(j * tile, tile), g * hd:(g + 1) * hd]
                vt_ref[g, j] = vj.astype(F32).T.astype(BF16)

    def scores(g, j):
        r0 = pl.multiple_of(j * tile, tile)
        kj = k_ref[pl.ds(r0, tile), g * hd:(g + 1) * hd]
        cj = jnp.concatenate([crep_ref[g, j]] * (tile // LANES), axis=1)
        return _dot_nt(kj, q_ref[:, g * hd:(g + 1) * hd]) - cj

    def update(g, j, carry, st):
        m, l, acc = carry
        m_new = jnp.maximum(m, jnp.max(st, axis=0, keepdims=True))
        a = jnp.exp2(m - m_new)
        p = jnp.exp2(st - m_new)
        l = a * l + jnp.sum(p, axis=0, keepdims=True)
        acc = a * acc + _dot(vt_ref[g, j], p.astype(BF16))
        return m_new, l, acc

    def body(j, carries):
        sts = [scores(g, j) for g in range(heads)]
        return tuple(update(g, j, carries[g], sts[g]) for g in range(heads))

    init = tuple((jnp.full((1, tile), NEG_BIG, F32), jnp.zeros((1, tile), F32),
                  jnp.zeros((hd, tile), F32)) for _ in range(heads))
    carries = lax.fori_loop(0, i, body, init)
    key = lax.broadcasted_iota(jnp.int32, (tile, tile), 0)
    qry = lax.broadcasted_iota(jnp.int32, (tile, tile), 1)
    sts = [jnp.where(key <= qry, scores(g, i), NEG_BIG) for g in range(heads)]
    for g in range(heads):
        _, l, acc = update(g, i, carries[g], sts[g])
        o_ref[:, g * hd:(g + 1) * hd] = (acc * (1.0 / l)).T.astype(o_ref.dtype)


def _fox_attention(h, c, batch, seq, tile):
    n = h.shape[0]
    hd = FOX_HEAD_DIM
    nh = FOX_HEADS
    gh = FOX_HEADS_PER_STEP
    ng = nh // gh
    nq = seq // tile
    kern = functools.partial(_fox_attn_kernel, tile=tile, heads=gh)
    return pl.pallas_call(
        kern,
        grid=(batch, ng, nq),
        in_specs=[pl.BlockSpec((tile, gh * hd), lambda b, hh, i: (b * nq + i, hh)),
                  pl.BlockSpec((seq, gh * hd), lambda b, hh, i: (b, ng + hh)),
                  pl.BlockSpec((seq, gh * hd), lambda b, hh, i: (b, 2 * ng + hh)),
                  pl.BlockSpec((seq, LANES), lambda b, hh, i: (b, 0))],
        out_specs=pl.BlockSpec((tile, gh * hd), lambda b, hh, i: (b * nq + i, hh)),
        out_shape=jax.ShapeDtypeStruct((n, nh * hd), BF16),
        scratch_shapes=[pltpu.VMEM((gh, nq, hd, tile), BF16),
                        pltpu.VMEM((gh, nq, tile, LANES), F32)],
        compiler_params=pltpu.CompilerParams(
            dimension_semantics=("parallel", "parallel", "arbitrary"), vmem_limit_bytes=_vmem(40)),
        name="fox_attention",
    )(h, h, h, c)


CONV_PAD = 32
CONV_ROWS = 256


def _conv_kernel(a_ref, g_ref, w_ref, cb_ref, ng_ref, nb_ref, o_ref, pad_ref):
    seq = a_ref.shape[0]
    pad_ref[pl.ds(0, CONV_PAD), :] = jnp.zeros((CONV_PAD, LANES), F32)
    pad_ref[pl.ds(CONV_PAD, seq), :] = a_ref[...].astype(F32) * _sigmoid(g_ref[...].astype(F32))
    off = CONV_PAD - (CONV_WIDTH - 1)
    for r in range(seq // CONV_ROWS):
        base = r * CONV_ROWS
        acc = jnp.zeros((CONV_ROWS, LANES), F32)
        for j in range(CONV_WIDTH):
            acc = acc + pad_ref[pl.ds(base + off + j, CONV_ROWS), :] * w_ref[pl.ds(j, 1), :]
        y = acc + cb_ref[...]
        mu = jnp.mean(y, axis=-1, keepdims=True)
        yc = y - mu
        var = jnp.mean(yc * yc, axis=-1, keepdims=True)
        yn = yc * lax.rsqrt(var + LN_EPS) * ng_ref[...] + nb_ref[...]
        o_ref[pl.ds(base, CONV_ROWS), :] = (yn * _sigmoid(yn)).astype(o_ref.dtype)


def _conv_module(h, a_col, g_col, conv_w, conv_b, cn_g, cn_b, batch, seq):
    n = h.shape[0]
    ch = conv_w.shape[1]
    ng = ch // CONV_GROUP
    assert seq % CONV_ROWS == 0
    vec = lambda: pl.BlockSpec((1, CONV_GROUP), lambda b, g: (0, g))
    return pl.pallas_call(
        _conv_kernel,
        grid=(batch, ng),
        in_specs=[pl.BlockSpec((seq, CONV_GROUP), lambda b, g: (b, a_col + g)),
                  pl.BlockSpec((seq, CONV_GROUP), lambda b, g: (b, g_col + g)),
                  pl.BlockSpec((CONV_WIDTH, CONV_GROUP), lambda b, g: (0, g)),
                  vec(), vec(), vec()],
        out_specs=pl.BlockSpec((seq, CONV_GROUP), lambda b, g: (b, g)),
        out_shape=jax.ShapeDtypeStruct((n, ch), BF16),
        scratch_shapes=[pltpu.VMEM((seq + CONV_PAD, LANES), F32)],
        compiler_params=pltpu.CompilerParams(
            dimension_semantics=("parallel", "parallel"), vmem_limit_bytes=_vmem(32)),
        name="conv_module",
    )(h, h, conv_w, conv_b, cn_g, cn_b)


def _layer_norm(y, g, b):
    mu = jnp.mean(y, axis=-1, keepdims=True)
    yc = y - mu
    var = jnp.mean(yc * yc, axis=-1, keepdims=True)
    return yc * lax.rsqrt(var + LN_EPS) * g + b


def _route_rows(xn, wr_hi, wr_lo, rbias, count_ref):
    x_hi, x_lo = _split_bf16(xn)
    logits = _dot_nt(wr_hi, x_hi) + _dot_nt(wr_hi, x_lo) + _dot_nt(wr_lo, x_hi)
    scores = _sigmoid(logits)
    sel = scores + rbias
    s = [sel[e:e + 1, :] for e in range(N_EXPERTS)]
    r = [scores[e:e + 1, :] for e in range(N_EXPERTS)]
    pg = EXPERTS_PER_GROUP

    def top2_sum(vals):
        best = None
        for a in range(len(vals)):
            for b in range(a + 1, len(vals)):
                t = vals[a] + vals[b]
                best = t if best is None else jnp.maximum(best, t)
        return best

    gs = [top2_sum(s[g * pg:(g + 1) * pg]) for g in range(N_GROUPS)]
    best, grp = gs[0], jnp.zeros_like(gs[0], dtype=jnp.int32)
    for g in range(1, N_GROUPS):
        upd = gs[g] > best
        best = jnp.where(upd, gs[g], best)
        grp = jnp.where(upd, g, grp)

    def pick_group(vals, k):
        out = vals[(N_GROUPS - 1) * pg + k]
        for g in range(N_GROUPS - 2, -1, -1):
            out = jnp.where(grp == g, vals[g * pg + k], out)
        return out

    v = [pick_group(s, k) for k in range(pg)]
    w = [pick_group(r, k) for k in range(pg)]
    b1, i1, w1 = v[0], jnp.zeros_like(grp), w[0]
    for k in range(1, pg):
        upd = v[k] > b1
        b1 = jnp.where(upd, v[k], b1)
        i1 = jnp.where(upd, k, i1)
        w1 = jnp.where(upd, w[k], w1)
    b2 = jnp.full_like(b1, -jnp.inf)
    i2, w2 = jnp.zeros_like(grp), jnp.zeros_like(w1)
    for k in range(pg):
        upd = jnp.logical_and(i1 != k, v[k] > b2)
        b2 = jnp.where(upd, v[k], b2)
        i2 = jnp.where(upd, k, i2)
        w2 = jnp.where(upd, w[k], w2)
    tot = w1 + w2
    e1 = grp * pg + i1
    e2 = grp * pg + i2
    eid = lax.broadcasted_iota(jnp.int32, scores.shape, 0)
    oh1 = jnp.where(eid == e1, 1.0, 0.0)
    oh2 = jnp.where(eid == e2, 1.0, 0.0)
    cnt = oh1 + oh2
    tm = cnt.shape[1]
    lane = lax.broadcasted_iota(jnp.int32, cnt.shape, 1)
    incl = cnt
    sh = 1
    while sh < tm:
        incl = incl + jnp.where(lane >= sh, pltpu.roll(incl, sh, axis=1), 0.0)
        sh *= 2
    before = incl - cnt + count_ref[...]
    count_ref[...] = count_ref[...] + incl[:, tm - 1:tm]
    rank1 = jnp.sum(oh1 * before, axis=0, keepdims=True)
    rank2 = jnp.sum(oh2 * before, axis=0, keepdims=True)
    rows = [e1.astype(F32), e2.astype(F32), w1 / tot, w2 / tot, rank1, rank2]
    rows += [jnp.zeros_like(w1)] * (8 - len(rows))
    return jnp.concatenate(rows, axis=0)


OUTPROJ_SUBTILES = 2


def _outproj_kernel(a1_ref, a2_ref, w_ref, x_ref, g_ref, b_ref, wrh_ref, wrl_ref, rb_ref,
                    xn_ref, rt_ref, rc_ref, cnt_ref, count_ref):
    @pl.when(pl.program_id(0) == 0)
    def _():
        count_ref[...] = jnp.zeros_like(count_ref)

    k1 = a1_ref.shape[1]
    k2 = a2_ref.shape[1]
    tm = x_ref.shape[0]
    sub = tm // OUTPROJ_SUBTILES
    rs = [pl.ds(t * sub, sub) for t in range(OUTPROJ_SUBTILES)]
    mix = [_dot(a1_ref[r, :], w_ref[pl.ds(0, k1), :]) + _dot(a2_ref[r, :], w_ref[pl.ds(k1, k2), :])
           for r in rs]
    for t, r in enumerate(rs):
        xn = _layer_norm(ALPHA * x_ref[r, :] + mix[t], g_ref[...], b_ref[...])
        xn_ref[r, :] = xn
        rows = _route_rows(xn, wrh_ref[...], wrl_ref[...], rb_ref[...], count_ref)
        rt_ref[:, r] = rows
        rc_ref[r, :] = rows.T
    cnt_ref[...] = jnp.broadcast_to(count_ref[...], cnt_ref.shape)


def _outproj_ln_route(a1, a1_col, a2, a2_col, kw, w, xres, ln_g, ln_b, wr_hi, wr_lo, rbias, tm):
    n, d = xres.shape
    full = lambda shape: pl.BlockSpec(shape, lambda i: (0, 0))
    return pl.pallas_call(
        _outproj_kernel,
        grid=(n // tm,),
        in_specs=[pl.BlockSpec((tm, kw), lambda i: (i, a1_col)),
                  pl.BlockSpec((tm, kw), lambda i: (i, a2_col)),
                  full(w.shape),
                  pl.BlockSpec((tm, d), lambda i: (i, 0)),
                  full((1, d)), full((1, d)),
                  full(wr_hi.shape), full(wr_lo.shape), full(rbias.shape)],
        out_specs=[pl.BlockSpec((tm, d), lambda i: (i, 0)),
                   pl.BlockSpec((8, tm), lambda i: (0, i)),
                   pl.BlockSpec((tm, 8), lambda i: (i, 0)),
                   pl.BlockSpec((N_EXPERTS, LANES), lambda i: (0, 0))],
        out_shape=[jax.ShapeDtypeStruct((n, d), F32),
                   jax.ShapeDtypeStruct((8, n), F32),
                   jax.ShapeDtypeStruct((n, 8), F32),
                   jax.ShapeDtypeStruct((N_EXPERTS, LANES), F32)],
        scratch_shapes=[pltpu.VMEM((N_EXPERTS, 1), F32)],
        compiler_params=pltpu.CompilerParams(
            dimension_semantics=("arbitrary",), vmem_limit_bytes=_vmem(52)),
        name="outproj_ln_route",
    )(a1, a2, w, xres, ln_g, ln_b, wr_hi, wr_lo, rbias)


EXPERT_CHUNK_ROWS = 128
EXPERT_STAGES = 4


TOK_WINDOW = 1024


def _expert_kernel(be_ref, na_ref, kb_ref, nk_ref, nx_ref, sl_ref, src_ref,
                   tok_hbm, x_hbm, wg_hbm, wu_hbm, wd_hbm, o_ref,
                   wgu_ref, wdn_ref, stg_ref, sem, xbuf_ref, tok_ref, gsem, tsem, *, layer):
    i = pl.program_id(0)
    tm = o_ref.shape[0]
    d = wdn_ref.shape[2]
    de = wdn_ref.shape[1]
    na = na_ref[0]

    def tok_copy(j):
        base = pl.multiple_of((src_ref[j] // TOK_WINDOW) * TOK_WINDOW, TOK_WINDOW)
        dst = pl.multiple_of((j % 2) * (2 * TOK_WINDOW), 2 * TOK_WINDOW)
        return pltpu.make_async_copy(tok_hbm.at[pl.ds(base, 2 * TOK_WINDOW)],
                                     tok_ref.at[pl.ds(dst, 2 * TOK_WINDOW)], tsem.at[j % 2])

    def start_rows(j):
        buf = j % 2
        first = buf * (2 * TOK_WINDOW) + src_ref[j] % TOK_WINDOW
        for r in range(tm):
            tok = tok_ref[first + r]
            pltpu.make_async_copy(x_hbm.at[pl.ds(tok, 1), :], xbuf_ref.at[buf, pl.ds(r, 1), :],
                                  gsem.at[buf]).start()

    def wait_rows(j):
        pltpu.make_async_copy(x_hbm.at[pl.ds(0, tm), :], xbuf_ref.at[j % 2], gsem.at[j % 2]).wait()
    ch = EXPERT_CHUNK_ROWS
    n_in = d // ch
    n_dn = de // ch
    n_chunks = 2 * n_in + n_dn

    def chunk_copy(e, c):
        st = c % EXPERT_STAGES

        def gate_up(w_hbm, first, col0):
            r0 = pl.multiple_of((c - first) * ch, ch)
            cp = pltpu.make_async_copy(w_hbm.at[layer, e, pl.ds(r0, ch), :],
                                       stg_ref.at[st, :, pl.ds(0, de)], sem.at[st])

            def convert(slot):
                wgu_ref[slot, pl.ds(r0, ch), pl.ds(col0, de)] = stg_ref[st, :, pl.ds(0, de)].astype(BF16)
            return cp, convert

        def down():
            r0 = pl.multiple_of((c - 2 * n_in) * ch, ch)
            cp = pltpu.make_async_copy(wd_hbm.at[layer, e, pl.ds(r0, ch), :],
                                       stg_ref.at[st], sem.at[st])

            def convert(slot):
                wdn_ref[slot, pl.ds(r0, ch), :] = stg_ref[st].astype(BF16)
            return cp, convert

        return ((c < n_in, lambda: gate_up(wg_hbm, 0, 0)),
                (jnp.logical_and(c >= n_in, c < 2 * n_in), lambda: gate_up(wu_hbm, n_in, de)),
                (c >= 2 * n_in, down))

    def start_chunk(e, c):
        for cond, make in chunk_copy(e, c):
            @pl.when(cond)
            def _():
                make()[0].start()

    def finish_chunk(e, c, slot):
        for cond, make in chunk_copy(e, c):
            @pl.when(cond)
            def _():
                cp, convert = make()
                cp.wait()
                convert(slot)

    def prime(e):
        for c in range(EXPERT_STAGES):
            start_chunk(e, jnp.int32(c))

    def process(e, slot, c_lo, c_hi):
        def body(c, _):
            finish_chunk(e, c, slot)

            @pl.when(c + EXPERT_STAGES < n_chunks)
            def _():
                start_chunk(e, c + EXPERT_STAGES)
            return 0
        lax.fori_loop(c_lo, c_hi, body, 0)

    @pl.when(i >= na)
    def _():
        o_ref[...] = jnp.zeros_like(o_ref)

    @pl.when(i < na)
    def _():
        e = be_ref[i]
        kb = kb_ref[i]
        nk = nk_ref[i]
        nxt = nx_ref[i]
        slot = sl_ref[i]

        @pl.when(i == 0)
        def _():
            tok_copy(0).start()
            prime(e)
            tok_copy(0).wait()
            start_rows(0)
            tok_copy(1).start()
            process(e, slot, 0, n_chunks)

        @pl.when(jnp.logical_and(kb == 0, nxt >= 0))
        def _():
            prime(nxt)

        tok_copy(i + 1).wait()
        start_rows(i + 1)
        tok_copy(i + 2).start()
        wait_rows(i)
        gu = _dot(xbuf_ref[i % 2].astype(BF16), wgu_ref[slot])
        g = gu[:, :de]
        u = gu[:, de:]
        hdn = (g * _sigmoid(g)) * u
        o_ref[...] = _pack_halves(_dot(hdn.astype(BF16), wdn_ref[slot]))

        @pl.when(i == na - 1)
        def _():
            wait_rows(i + 1)
            tok_copy(i + 2).wait()

        @pl.when(nxt >= 0)
        def _():
            process(nxt, 1 - slot, (n_chunks * kb) // nk, (n_chunks * (kb + 1)) // nk)


def _expert_ffn(xn, sorted_tok, tables, w_gate, w_up, w_down, layer, tm, nb):
    d = xn.shape[1]
    de = w_down.shape[2]
    assert d % EXPERT_CHUNK_ROWS == 0 and de % EXPERT_CHUNK_ROWS == 0 and de % LANES == 0
    assert tm <= TOK_WINDOW
    hbm = lambda: pl.BlockSpec(memory_space=pl.ANY)
    return pl.pallas_call(
        functools.partial(_expert_kernel, layer=layer),
        grid_spec=pltpu.PrefetchScalarGridSpec(
            num_scalar_prefetch=len(tables),
            grid=(nb,),
            in_specs=[hbm(), hbm(), hbm(), hbm(), hbm()],
            out_specs=pl.BlockSpec((tm, d // 2), lambda i, *_: (i, 0)),
            scratch_shapes=[pltpu.VMEM((2, d, 2 * de), BF16),
                            pltpu.VMEM((2, de, d), BF16),
                            pltpu.VMEM((EXPERT_STAGES, EXPERT_CHUNK_ROWS, d), F32),
                            pltpu.SemaphoreType.DMA((EXPERT_STAGES,)),
                            pltpu.VMEM((2, tm, d), F32),
                            pltpu.SMEM((2 * 2 * TOK_WINDOW,), jnp.int32),
                            pltpu.SemaphoreType.DMA((2,)),
                            pltpu.SemaphoreType.DMA((2,))]),
        out_shape=jax.ShapeDtypeStruct((nb * tm, d // 2), jnp.uint32),
        compiler_params=pltpu.CompilerParams(
            dimension_semantics=("arbitrary",), vmem_limit_bytes=_vmem(60)),
        name="expert_ffn",
    )(*tables, sorted_tok, xn, w_gate, w_up, w_down)


ROUTE_GATE_COL = 2


def _combine_kernel(pos_ref, posn_ref, x_ref, ys_hbm, rc_ref, g_ref, b_ref, xn_ref, *rest):
    xb_ref = rest[0] if len(rest) == 3 else None
    ybuf_ref, sem = rest[-2:]
    i = pl.program_id(0)
    nt = pl.num_programs(0)
    tm, d = x_ref.shape

    def start_rows(p_ref, buf):
        for r in range(tm):
            for k in range(TOP_K):
                pltpu.make_async_copy(ys_hbm.at[pl.ds(p_ref[0, 0, k * tm + r], 1), :],
                                      ybuf_ref.at[buf, k, pl.ds(r, 1), :], sem.at[buf]).start()

    @pl.when(i == 0)
    def _():
        start_rows(pos_ref, 0)

    @pl.when(i + 1 < nt)
    def _():
        start_rows(posn_ref, (i + 1) % 2)

    buf = i % 2
    for k in range(TOP_K):
        pltpu.make_async_copy(ys_hbm.at[pl.ds(0, tm), :], ybuf_ref.at[buf, k], sem.at[buf]).wait()
    g0 = rc_ref[:, ROUTE_GATE_COL:ROUTE_GATE_COL + 1]
    g1 = rc_ref[:, ROUTE_GATE_COL + 1:ROUTE_GATE_COL + 2]
    left0, right0 = _unpack_halves(ybuf_ref[buf, 0])
    left1, right1 = _unpack_halves(ybuf_ref[buf, 1])
    moe = jnp.concatenate([g0 * left0 + g1 * left1, g0 * right0 + g1 * right1], axis=1)
    xn = _layer_norm(ALPHA * x_ref[...] + moe, g_ref[...], b_ref[...])
    xn_ref[...] = xn
    if xb_ref is not None:
        xb_ref[...] = xn.astype(BF16)


POS_TILES_PER_STEP = 8


def _slot_pos_kernel(rt_ref, ps_ref, o_ref, *, tm):
    eid = lax.broadcasted_iota(jnp.int32, (N_EXPERTS, rt_ref.shape[1]), 0)
    for k in range(TOP_K):
        e = rt_ref[k:k + 1, :].astype(jnp.int32)
        base = jnp.sum(jnp.where(eid == e, ps_ref[...], 0.0), axis=0, keepdims=True)
        pos = (base + rt_ref[2 * TOP_K + k:2 * TOP_K + k + 1, :]).astype(jnp.int32)
        for t in range(o_ref.shape[0]):
            o_ref[t, :, pl.ds(k * tm, tm)] = pos[:, t * tm:(t + 1) * tm]


def _slot_positions(route, pstart, tm):
    n = route.shape[1]
    nt = n // tm
    step = POS_TILES_PER_STEP if nt % POS_TILES_PER_STEP == 0 else 1
    return pl.pallas_call(
        functools.partial(_slot_pos_kernel, tm=tm),
        grid=(nt // step,),
        in_specs=[pl.BlockSpec((8, step * tm), lambda i: (0, i)),
                  pl.BlockSpec((N_EXPERTS, 1), lambda i: (0, 0))],
        out_specs=pl.BlockSpec((step, 1, TOP_K * tm), lambda i: (i, 0, 0)),
        out_shape=jax.ShapeDtypeStruct((nt, 1, TOP_K * tm), jnp.int32),
        compiler_params=pltpu.CompilerParams(dimension_semantics=("parallel",)),
        name="slot_positions",
    )(route, pstart)


def _combine_ln(x, ys, pos_t, route_cols, ln_g, ln_b, tm, with_bf16):
    n, d = x.shape
    nt = n // tm
    n_out = 2 if with_bf16 else 1
    row = lambda: pl.BlockSpec((tm, d), lambda i: (i, 0))
    vec = lambda: pl.BlockSpec((1, d), lambda i: (0, 0))
    smem = lambda imap: pl.BlockSpec((1, 1, TOP_K * tm), imap, memory_space=pltpu.SMEM)
    return pl.pallas_call(
        _combine_kernel,
        grid=(nt,),
        in_specs=[smem(lambda i: (i, 0, 0)), smem(lambda i: (jnp.minimum(i + 1, nt - 1), 0, 0)),
                  row(), pl.BlockSpec(memory_space=pl.ANY),
                  pl.BlockSpec((tm, 8), lambda i: (i, 0)), vec(), vec()],
        out_specs=[row(), row()][:n_out],
        out_shape=[jax.ShapeDtypeStruct((n, d), F32), jax.ShapeDtypeStruct((n, d), BF16)][:n_out],
        scratch_shapes=[pltpu.VMEM((2, TOP_K, tm, d // 2), jnp.uint32),
                        pltpu.SemaphoreType.DMA((2,))],
        compiler_params=pltpu.CompilerParams(
            dimension_semantics=("arbitrary",), vmem_limit_bytes=_vmem(48)),
        name="combine_ln",
    )(pos_t, pos_t, x, ys, route_cols, ln_g, ln_b)


def _gla_gate_kernel(x_ref, wl_ref, w2_ref, ba_ref, o_ref):
    low = _dot(x_ref[...], wl_ref[...]).astype(BF16)
    z = _dot(low, w2_ref[...]) + ba_ref[...]
    o_ref[...] = _log_sigmoid(z) * (1.0 / GLA_TAU)


def _gla_gate(xb, w_low, w_a2, b_a, tm):
    n, d = xb.shape
    kw = w_a2.shape[1]
    return pl.pallas_call(
        _gla_gate_kernel,
        grid=(n // tm,),
        in_specs=[pl.BlockSpec((tm, d), lambda i: (i, 0)),
                  pl.BlockSpec(w_low.shape, lambda i: (0, 0)),
                  pl.BlockSpec(w_a2.shape, lambda i: (0, 0)),
                  pl.BlockSpec((1, kw), lambda i: (0, 0))],
        out_specs=pl.BlockSpec((tm, kw), lambda i: (i, 0)),
        out_shape=jax.ShapeDtypeStruct((n, kw), F32),
        compiler_params=pltpu.CompilerParams(
            dimension_semantics=("parallel",), vmem_limit_bytes=_vmem(32)),
        name="gla_gate",
    )(xb, w_low, w_a2, b_a)


GLA_ROWS = 512


def _gla_kernel(q_ref, k_ref, v_ref, la_ref, g_ref, ng_ref, o_ref, state_ref, *, scale):
    rows, kw = q_ref.shape
    nh = state_ref.shape[0]
    hk, hv = state_ref.shape[1], state_ref.shape[2]
    ck = GLA_CHUNK

    @pl.when(pl.program_id(1) == 0)
    def _():
        state_ref[...] = jnp.zeros_like(state_ref)

    row = lax.broadcasted_iota(jnp.int32, (ck, ck), 0)
    col = lax.broadcasted_iota(jnp.int32, (ck, ck), 1)
    causal = row >= col
    tri = jnp.where(causal, 1.0, 0.0).astype(BF16)
    ones = jnp.ones((ck, LANES), BF16)

    def chunk(c, _):
        r0 = pl.multiple_of(c * ck, ck)
        heads = range(nh)
        ks = [slice(hh * hk, (hh + 1) * hk) for hh in heads]
        vs = [slice(hh * hv, (hh + 1) * hv) for hh in heads]
        la = [_split_bf16(la_ref[pl.ds(r0, ck), ks[hh]]) for hh in heads]
        b = [_dot(tri, la[hh][0]) + _dot(tri, la[hh][1]) for hh in heads]
        dsum = [_dot_tn(la[hh][0], ones) + _dot_tn(la[hh][1], ones) for hh in heads]
        qt, kt, kend = [], [], []
        for hh in heads:
            q = q_ref[pl.ds(r0, ck), ks[hh]].astype(F32) * scale
            k = k_ref[pl.ds(r0, ck), ks[hh]].astype(F32)
            b_last = b[hh][ck - 1:ck, :]
            qt.append((q * jnp.exp(b[hh])).astype(BF16))
            kt.append((k * jnp.exp(-b[hh])).astype(BF16))
            kend.append((k * jnp.exp(b_last - b[hh])).astype(BF16))
        v = [v_ref[pl.ds(r0, ck), vs[hh]] for hh in heads]
        state = [state_ref[hh] for hh in heads]
        attn = [_dot_nt(qt[hh], kt[hh]) for hh in heads]
        inter = [_dot(qt[hh], state[hh].astype(BF16)) for hh in heads]
        kv = [_dot_tn(kend[hh], v[hh]) for hh in heads]
        for hh in heads:
            decay = jnp.concatenate([jnp.exp(dsum[hh])] * (hv // LANES), axis=1)
            state_ref[hh] = state[hh] * decay + kv[hh]
        for hh in heads:
            o = _dot(jnp.where(causal, attn[hh], 0.0).astype(BF16), v[hh]) + inter[hh]
            o = o * lax.rsqrt(jnp.mean(o * o, axis=-1, keepdims=True) + LN_EPS)
            gate = g_ref[pl.ds(r0, ck), vs[hh]].astype(F32)
            o = o * ng_ref[:, vs[hh]] * (gate * _sigmoid(gate))
            o_ref[pl.ds(r0, ck), vs[hh]] = o.astype(o_ref.dtype)
        return 0

    lax.fori_loop(0, rows // ck, chunk, 0, unroll=2)


def _gla(h, la, norm_g, batch, seq):
    n = h.shape[0]
    kw = la.shape[1]
    vw = norm_g.shape[1]
    nh = GLA_HEADS
    hk, hv = kw // nh, vw // nh
    rows = GLA_ROWS
    nj = seq // rows
    kern = functools.partial(_gla_kernel, scale=hk ** -0.5)
    rowblk = lambda b, j: b * nj + j
    return pl.pallas_call(
        kern,
        grid=(batch, nj),
        in_specs=[pl.BlockSpec((rows, kw), lambda b, j: (rowblk(b, j), 0)),
                  pl.BlockSpec((rows, kw), lambda b, j: (rowblk(b, j), 1)),
                  pl.BlockSpec((rows, vw), lambda b, j: (rowblk(b, j), 2 * kw // vw)),
                  pl.BlockSpec((rows, kw), lambda b, j: (rowblk(b, j), 0)),
                  pl.BlockSpec((rows, vw), lambda b, j: (rowblk(b, j), 2 * kw // vw + 1)),
                  pl.BlockSpec((1, vw), lambda b, j: (0, 0))],
        out_specs=pl.BlockSpec((rows, vw), lambda b, j: (rowblk(b, j), 0)),
        out_shape=jax.ShapeDtypeStruct((n, vw), BF16),
        scratch_shapes=[pltpu.VMEM((nh, hk, hv), F32)],
        compiler_params=pltpu.CompilerParams(
            dimension_semantics=("parallel", "arbitrary"), vmem_limit_bytes=_vmem(48)),
        name="gla",
    )(h, h, h, la, h, norm_g)


EXPERT_TM = 256
COMBINE_TM = 256


def _moe(xn, route, route_cols, counts, w_gate, w_up, w_down, layer, ln_g, ln_b, with_bf16):
    n, d = xn.shape
    a = n * TOP_K
    tm = EXPERT_TM
    nb = a // tm + N_EXPERTS
    i32 = jnp.int32
    eidx = route[:TOP_K].astype(i32)
    experts = jnp.arange(N_EXPERTS, dtype=i32)
    counts = counts[:, 0].astype(i32)
    starts = jnp.cumsum(counts) - counts
    nblk = (counts + tm - 1) // tm
    bend = jnp.cumsum(nblk)
    pstart = (bend - nblk) * tm
    n_active = bend[-1:].astype(i32)
    blk = jnp.minimum(jnp.arange(nb, dtype=i32), n_active[0] - 1)
    block_expert = jnp.minimum(jnp.sum((bend[None, :] <= blk[:, None]).astype(i32), axis=1),
                               N_EXPERTS - 1)
    assign = jnp.arange(n, dtype=i32)[None, :] * TOP_K + jnp.arange(TOP_K, dtype=i32)[:, None]
    sorted_tok = (lax.sort((eidx * a + assign).reshape(a)) % a) // TOP_K
    sorted_tok = jnp.concatenate([sorted_tok, jnp.zeros((2 * TOK_WINDOW,), i32)])
    kb = blk - (bend - nblk)[block_expert]
    nonempty = nblk > 0
    later = jnp.logical_and(experts[None, :] > experts[:, None], nonempty[None, :])
    nxt_e = jnp.min(jnp.where(later, experts[None, :], N_EXPERTS), axis=1)
    nxt_e = jnp.where(nxt_e < N_EXPERTS, nxt_e, -1)
    slot_e = (jnp.cumsum(nonempty.astype(i32)) - 1) % 2
    blk_src = jnp.concatenate([starts[block_expert] + kb * tm, jnp.zeros((2,), i32)])
    tables = (block_expert, n_active, kb, nblk[block_expert], nxt_e[block_expert],
              slot_e[block_expert], blk_src)
    ys = _expert_ffn(xn, sorted_tok, tuple(t.astype(i32) for t in tables),
                     w_gate, w_up, w_down, layer, tm, nb)
    pos = _slot_positions(route, pstart.reshape(N_EXPERTS, 1).astype(F32), COMBINE_TM)
    return _combine_ln(xn, ys, pos, route_cols, ln_g, ln_b, COMBINE_TM, with_bf16)


def kernel(x, even_w_in, even_b_f, even_conv_w, even_conv_b, even_conv_norm_g, even_conv_norm_b, even_w_out, odd_w_in, odd_w_a2, odd_b_a, odd_norm_g, odd_w_out, ln_mix_g, ln_mix_b, ln_ffn_g, ln_ffn_b, router_w, router_bias, expert_w_gate, expert_w_up, expert_w_down):
    batch, seq, d = x.shape
    n = batch * seq
    x2 = x.reshape(n, d)
    fw = FOX_HEADS * FOX_HEAD_DIM
    conv_ch = even_conv_w.shape[-1]
    kw = odd_w_a2.shape[-1]
    vw = odd_norm_g.shape[-1]
    row = lambda t: t.reshape(1, -1)

    wr_hi, wr_lo = _split_bf16(router_w.T)
    rbias = router_bias.reshape(N_EXPERTS, 1).astype(F32)
    experts_w = (expert_w_gate, expert_w_up, expert_w_down)

    w_t = jnp.swapaxes(even_w_in, 1, 2)
    q_scale = LOG2E * FOX_HEAD_DIM ** -0.5
    wf = jnp.zeros((d, LANES), F32).at[:, :FOX_HEADS].set(even_w_in[0, :, 3 * fw:3 * fw + FOX_HEADS])
    b_f = jnp.zeros((1, LANES), F32).at[0, :FOX_HEADS].set(even_b_f[0])
    h = _matmul(x2, w_t, 3 * fw, BF16, 512, fw, first_scale=q_scale)
    h_glu = _matmul(x2, w_t[:, 3 * fw + FOX_HEADS:], 2 * conv_ch, BF16, 512, conv_ch)
    c = _fox_gate(x2, wf.astype(BF16), b_f, batch, seq, 512)
    att = _fox_attention(h, c, batch, seq, 256)
    u = _conv_module(h_glu, 0, conv_ch // CONV_GROUP, even_conv_w[0, :, 0, :],
                     row(even_conv_b[0]), row(even_conv_norm_g[0]), row(even_conv_norm_b[0]),
                     batch, seq)
    xn, route, rcols, cnt = _outproj_ln_route(att, 0, u, 0, fw, even_w_out[0].astype(BF16), x2,
                                              row(ln_mix_g[0]), row(ln_mix_b[0]),
                                              wr_hi, wr_lo, rbias, 512)
    xn, xb = _moe(xn, route, rcols, cnt, *experts_w, 0, row(ln_ffn_g[0]), row(ln_ffn_b[0]), True)

    w_t = jnp.swapaxes(odd_w_in, 1, 2)
    h = _matmul(xb, w_t, 2 * kw + 2 * vw, BF16, 1024, 1024)
    w_low = jnp.zeros((d, LANES), F32).at[:, :GLA_LOW_RANK].set(odd_w_in[0, :, 2 * kw + 2 * vw:])
    w_low = w_low.astype(BF16)
    w_a2 = jnp.zeros((LANES, kw), F32).at[:GLA_LOW_RANK].set(odd_w_a2[0]).astype(BF16)
    la = _gla_gate(xb, w_low, w_a2, row(odd_b_a[0]), 1024)
    o = _gla(h, la, row(odd_norm_g[0]), batch, seq)
    half = vw // 2
    xn, route, rcols, cnt = _outproj_ln_route(o, 0, o, 1, half, odd_w_out[0].astype(BF16), xn,
                                              row(ln_mix_g[1]), row(ln_mix_b[1]),
                                              wr_hi, wr_lo, rbias, 512)
    (xn,) = _moe(xn, route, rcols, cnt, *experts_w, 1, row(ln_ffn_g[1]), row(ln_ffn_b[1]), False)
    return xn.reshape(batch, seq, d)
```

```python
import functools

import jax
import jax.numpy as jnp
from jax import lax
from jax.experimental import pallas as pl
from jax.experimental.pallas import tpu as pltpu

F32 = jnp.float32
BF16 = jnp.bfloat16

DEPTH = 2
ALPHA = (2 * DEPTH) ** 0.25
LN_EPS = 1e-5
FOX_HEADS = 8
FOX_HEAD_DIM = 128
CONV_WIDTH = 31
CONV_GROUP = 128
GLA_HEADS = 4
GLA_LOW_RANK = 16
GLA_TAU = 16.0
GLA_CHUNK = 64
N_EXPERTS = 16
N_GROUPS = 4
EXPERTS_PER_GROUP = N_EXPERTS // N_GROUPS
TOP_K = 2

LANES = 128
V7X_VMEM_BYTES = 64 * 1024 * 1024
NEG_BIG = -1e30
LOG2E = 1.4426950408889634


def _vmem(mib):
    assert mib * 1024 * 1024 < V7X_VMEM_BYTES
    return mib * 1024 * 1024


def _sigmoid(z):
    return 1.0 / (1.0 + jnp.exp(-z))


def _log_sigmoid(z):
    return jnp.minimum(z, 0.0) - jnp.log1p(jnp.exp(-jnp.abs(z)))


def _dot(a, b):
    return jnp.dot(a, b, preferred_element_type=F32)


def _dot_nt(a, b):
    return lax.dot_general(a, b, (((1,), (1,)), ((), ())), preferred_element_type=F32)


def _dot_tn(a, b):
    return lax.dot_general(a, b, (((0,), (0,)), ((), ())), preferred_element_type=F32)


def _split_bf16(v):
    hi = v.astype(BF16)
    lo = (v - hi.astype(F32)).astype(BF16)
    return hi, lo


def _mm_kernel(x_ref, w_ref, o_ref, wb_ref, *, first_scale):
    @pl.when(pl.program_id(1) == 0)
    def _():
        wb_ref[...] = w_ref[...].T.astype(BF16)

    acc = _dot(x_ref[...].astype(BF16), wb_ref[...])
    if first_scale is not None:
        acc = acc * jnp.where(pl.program_id(0) == 0, first_scale, 1.0)
    o_ref[...] = acc.astype(o_ref.dtype)


def _matmul(x, w_t, n_cols, out_dtype, tm, tn, first_scale=None):
    m, k = x.shape
    assert m % tm == 0 and n_cols % tn == 0 and n_cols <= w_t.shape[1] and k == w_t.shape[2]
    return pl.pallas_call(
        functools.partial(_mm_kernel, first_scale=first_scale),
        grid=(n_cols // tn, m // tm),
        in_specs=[pl.BlockSpec((tm, k), lambda j, i: (i, 0)),
                  pl.BlockSpec((None, tn, k), lambda j, i: (0, j, 0))],
        out_specs=pl.BlockSpec((tm, tn), lambda j, i: (i, j)),
        out_shape=jax.ShapeDtypeStruct((m, n_cols), out_dtype),
        scratch_shapes=[pltpu.VMEM((k, tn), BF16)],
        compiler_params=pltpu.CompilerParams(
            dimension_semantics=("parallel", "arbitrary"), vmem_limit_bytes=_vmem(52)),
        name="dense_proj",
    )(x, w_t)


def _fox_gate_kernel(x_ref, wf_ref, bf_ref, c_ref, carry_ref):
    j = pl.program_id(1)

    @pl.when(j == 0)
    def _():
        carry_ref[...] = jnp.zeros_like(carry_ref)

    z = _dot(x_ref[...].astype(BF16), wf_ref[...]) + bf_ref[...]
    lf = _log_sigmoid(z)
    ts = lf.shape[0]
    row = lax.broadcasted_iota(jnp.int32, lf.shape, 0)
    sh = 1
    while sh < ts:
        lf = lf + jnp.where(row >= sh, pltpu.roll(lf, sh, axis=0), 0.0)
        sh *= 2
    c = lf + carry_ref[...]
    c_ref[...] = c * LOG2E
    carry_ref[...] = c[ts - 1:ts, :]


def _fox_gate(x2, wf, b_f, batch, seq, ts):
    d = x2.shape[1]
    nj = seq // ts
    return pl.pallas_call(
        _fox_gate_kernel,
        grid=(batch, nj),
        in_specs=[pl.BlockSpec((ts, d), lambda b, j: (b * nj + j, 0)),
                  pl.BlockSpec((d, LANES), lambda b, j: (0, 0)),
                  pl.BlockSpec((1, LANES), lambda b, j: (0, 0))],
        out_specs=pl.BlockSpec((ts, LANES), lambda b, j: (b * nj + j, 0)),
        out_shape=jax.ShapeDtypeStruct((batch * seq, LANES), F32),
        scratch_shapes=[pltpu.VMEM((1, LANES), F32)],
        compiler_params=pltpu.CompilerParams(
            dimension_semantics=("parallel", "arbitrary"), vmem_limit_bytes=_vmem(40)),
        name="fox_gate",
    )(x2, wf, b_f)


FOX_HEADS_PER_STEP = 8


def _fox_attn_kernel(q_ref, k_ref, v_ref, c_ref, o_ref, vt_ref, crep_ref, *, tile, heads):
    hh = pl.program_id(1)
    i = pl.program_id(2)
    hd = FOX_HEAD_DIM
    nq = vt_ref.shape[1]

    @pl.when(i == 0)
    def _():
        lane = lax.broadcasted_iota(jnp.int32, (tile, LANES), 1)
        for j in range(nq):
            cj = c_ref[pl.ds(j * tile, tile), :]
            for g in range(heads):
                col = jnp.sum(jnp.where(lane == hh * heads + g, cj, 0.0), axis=1, keepdims=True)
                crep_ref[g, j] = jnp.broadcast_to(col, (tile, LANES))
                vj = v_ref[pl.ds(j * tile, tile), g * hd:(g + 1) * hd]
                vt_ref[g, j] = vj.astype(F32).T.astype(BF16)

    def scores(g, j):
        r0 = pl.multiple_of(j * tile, tile)
        kj = k_ref[pl.ds(r0, tile), g * hd:(g + 1) * hd]
        cj = jnp.concatenate([crep_ref[g, j]] * (tile // LANES), axis=1)
        return _dot_nt(kj, q_ref[:, g * hd:(g + 1) * hd]) - cj

    def update(g, j, carry, st):
        m, l, acc = carry
        m_new = jnp.maximum(m, jnp.max(st, axis=0, keepdims=True))
        a = jnp.exp2(m - m_new)
        p = jnp.exp2(st - m_new)
        l = a * l + jnp.sum(p, axis=0, keepdims=True)
        acc = a * acc + _dot(vt_ref[g, j], p.astype(BF16))
        return m_new, l, acc

    def body(j, carries):
        sts = [scores(g, j) for g in range(heads)]
        return tuple(update(g, j, carries[g], sts[g]) for g in range(heads))

    init = tuple((jnp.full((1, tile), NEG_BIG, F32), jnp.zeros((1, tile), F32),
                  jnp.zeros((hd, tile), F32)) for _ in range(heads))
    carries = lax.fori_loop(0, i, body, init)
    key = lax.broadcasted_iota(jnp.int32, (tile, tile), 0)
    qry = lax.broadcasted_iota(jnp.int32, (tile, tile), 1)
    sts = [jnp.where(key <= qry, scores(g, i), NEG_BIG) for g in range(heads)]
    for g in range(heads):
        _, l, acc = update(g, i, carries[g], sts[g])
        o_ref[:, g * hd:(g + 1) * hd] = (acc * (1.0 / l)).T.astype(o_ref.dtype)


def _fox_attention(h, c, batch, seq, tile):
    n = h.shape[0]
    hd = FOX_HEAD_DIM
    nh = FOX_HEADS
    gh = FOX_HEADS_PER_STEP
    ng = nh // gh
    nq = seq // tile
    kern = functools.partial(_fox_attn_kernel, tile=tile, heads=gh)
    return pl.pallas_call(
        kern,
        grid=(batch, ng, nq),
        in_specs=[pl.BlockSpec((tile, gh * hd), lambda b, hh, i: (b * nq + i, hh)),
                  pl.BlockSpec((seq, gh * hd), lambda b, hh, i: (b, ng + hh)),
                  pl.BlockSpec((seq, gh * hd), lambda b, hh, i: (b, 2 * ng + hh)),
                  pl.BlockSpec((seq, LANES), lambda b, hh, i: (b, 0))],
        out_specs=pl.BlockSpec((tile, gh * hd), lambda b, hh, i: (b * nq + i, hh)),
        out_shape=jax.ShapeDtypeStruct((n, nh * hd), BF16),
        scratch_shapes=[pltpu.VMEM((gh, nq, hd, tile), BF16),
                        pltpu.VMEM((gh, nq, tile, LANES), F32)],
        compiler_params=pltpu.CompilerParams(
            dimension_semantics=("parallel", "parallel", "arbitrary"), vmem_limit_bytes=_vmem(40)),
        name="fox_attention",
    )(h, h, h, c)


CONV_PAD = 32
CONV_ROWS = 256


def _conv_kernel(a_ref, g_ref, w_ref, cb_ref, ng_ref, nb_ref, o_ref, pad_ref):
    seq = a_ref.shape[0]
    pad_ref[pl.ds(0, CONV_PAD), :] = jnp.zeros((CONV_PAD, LANES), F32)
    pad_ref[pl.ds(CONV_PAD, seq), :] = a_ref[...].astype(F32) * _sigmoid(g_ref[...].astype(F32))
    off = CONV_PAD - (CONV_WIDTH - 1)
    for r in range(seq // CONV_ROWS):
        base = r * CONV_ROWS
        acc = jnp.zeros((CONV_ROWS, LANES), F32)
        for j in range(CONV_WIDTH):
            acc = acc + pad_ref[pl.ds(base + off + j, CONV_ROWS), :] * w_ref[pl.ds(j, 1), :]
        y = acc + cb_ref[...]
        mu = jnp.mean(y, axis=-1, keepdims=True)
        yc = y - mu
        var = jnp.mean(yc * yc, axis=-1, keepdims=True)
        yn = yc * lax.rsqrt(var + LN_EPS) * ng_ref[...] + nb_ref[...]
        o_ref[pl.ds(base, CONV_ROWS), :] = (yn * _sigmoid(yn)).astype(o_ref.dtype)


def _conv_module(h, a_col, g_col, conv_w, conv_b, cn_g, cn_b, batch, seq):
    n = h.shape[0]
    ch = conv_w.shape[1]
    ng = ch // CONV_GROUP
    assert seq % CONV_ROWS == 0
    vec = lambda: pl.BlockSpec((1, CONV_GROUP), lambda b, g: (0, g))
    return pl.pallas_call(
        _conv_kernel,
        grid=(batch, ng),
        in_specs=[pl.BlockSpec((seq, CONV_GROUP), lambda b, g: (b, a_col + g)),
                  pl.BlockSpec((seq, CONV_GROUP), lambda b, g: (b, g_col + g)),
                  pl.BlockSpec((CONV_WIDTH, CONV_GROUP), lambda b, g: (0, g)),
                  vec(), vec(), vec()],
        out_specs=pl.BlockSpec((seq, CONV_GROUP), lambda b, g: (b, g)),
        out_shape=jax.ShapeDtypeStruct((n, ch), BF16),
        scratch_shapes=[pltpu.VMEM((seq + CONV_PAD, LANES), F32)],
        compiler_params=pltpu.CompilerParams(
            dimension_semantics=("parallel", "parallel"), vmem_limit_bytes=_vmem(32)),
        name="conv_module",
    )(h, h, conv_w, conv_b, cn_g, cn_b)


def _layer_norm(y, g, b):
    mu = jnp.mean(y, axis=-1, keepdims=True)
    yc = y - mu
    var = jnp.mean(yc * yc, axis=-1, keepdims=True)
    return yc * lax.rsqrt(var + LN_EPS) * g + b


def _route_rows(xn, wr_hi, wr_lo, rbias, count_ref):
    x_hi, x_lo = _split_bf16(xn)
    logits = _dot_nt(wr_hi, x_hi) + _dot_nt(wr_hi, x_lo) + _dot_nt(wr_lo, x_hi)
    scores = _sigmoid(logits)
    sel = scores + rbias
    s = [sel[e:e + 1, :] for e in range(N_EXPERTS)]
    r = [scores[e:e + 1, :] for e in range(N_EXPERTS)]
    pg = EXPERTS_PER_GROUP

    def top2_sum(vals):
        best = None
        for a in range(len(vals)):
            for b in range(a + 1, len(vals)):
                t = vals[a] + vals[b]
                best = t if best is None else jnp.maximum(best, t)
        return best

    gs = [top2_sum(s[g * pg:(g + 1) * pg]) for g in range(N_GROUPS)]
    best, grp = gs[0], jnp.zeros_like(gs[0], dtype=jnp.int32)
    for g in range(1, N_GROUPS):
        upd = gs[g] > best
        best = jnp.where(upd, gs[g], best)
        grp = jnp.where(upd, g, grp)

    def pick_group(vals, k):
        out = vals[(N_GROUPS - 1) * pg + k]
        for g in range(N_GROUPS - 2, -1, -1):
            out = jnp.where(grp == g, vals[g * pg + k], out)
        return out

    v = [pick_group(s, k) for k in range(pg)]
    w = [pick_group(r, k) for k in range(pg)]
    b1, i1, w1 = v[0], jnp.zeros_like(grp), w[0]
    for k in range(1, pg):
        upd = v[k] > b1
        b1 = jnp.where(upd, v[k], b1)
        i1 = jnp.where(upd, k, i1)
        w1 = jnp.where(upd, w[k], w1)
    b2 = jnp.full_like(b1, -jnp.inf)
    i2, w2 = jnp.zeros_like(grp), jnp.zeros_like(w1)
    for k in range(pg):
        upd = jnp.logical_and(i1 != k, v[k] > b2)
        b2 = jnp.where(upd, v[k], b2)
        i2 = jnp.where(upd, k, i2)
        w2 = jnp.where(upd, w[k], w2)
    tot = w1 + w2
    e1 = grp * pg + i1
    e2 = grp * pg + i2
    eid = lax.broadcasted_iota(jnp.int32, scores.shape, 0)
    oh1 = jnp.where(eid == e1, 1.0, 0.0)
    oh2 = jnp.where(eid == e2, 1.0, 0.0)
    cnt = oh1 + oh2
    tm = cnt.shape[1]
    lane = lax.broadcasted_iota(jnp.int32, cnt.shape, 1)
    incl = cnt
    sh = 1
    while sh < tm:
        incl = incl + jnp.where(lane >= sh, pltpu.roll(incl, sh, axis=1), 0.0)
        sh *= 2
    before = incl - cnt + count_ref[...]
    count_ref[...] = count_ref[...] + incl[:, tm - 1:tm]
    rank1 = jnp.sum(oh1 * before, axis=0, keepdims=True)
    rank2 = jnp.sum(oh2 * before, axis=0, keepdims=True)
    rows = [e1.astype(F32), e2.astype(F32), w1 / tot, w2 / tot, rank1, rank2]
    rows += [jnp.zeros_like(w1)] * (8 - len(rows))
    return jnp.concatenate(rows, axis=0)


OUTPROJ_SUBTILES = 2


def _outproj_kernel(a1_ref, a2_ref, w_ref, x_ref, g_ref, b_ref, wrh_ref, wrl_ref, rb_ref,
                    xn_ref, rt_ref, rc_ref, cnt_ref, count_ref):
    @pl.when(pl.program_id(0) == 0)
    def _():
        count_ref[...] = jnp.zeros_like(count_ref)

    k1 = a1_ref.shape[1]
    k2 = a2_ref.shape[1]
    tm = x_ref.shape[0]
    sub = tm // OUTPROJ_SUBTILES
    rs = [pl.ds(t * sub, sub) for t in range(OUTPROJ_SUBTILES)]
    mix = [_dot(a1_ref[r, :], w_ref[pl.ds(0, k1), :]) + _dot(a2_ref[r, :], w_ref[pl.ds(k1, k2), :])
           for r in rs]
    for t, r in enumerate(rs):
        xn = _layer_norm(ALPHA * x_ref[r, :] + mix[t], g_ref[...], b_ref[...])
        xn_ref[r, :] = xn
        rows = _route_rows(xn, wrh_ref[...], wrl_ref[...], rb_ref[...], count_ref)
        rt_ref[:, r] = rows
        rc_ref[r, :] = rows.T
    cnt_ref[...] = jnp.broadcast_to(count_ref[...], cnt_ref.shape)


def _outproj_ln_route(a1, a1_col, a2, a2_col, kw, w, xres, ln_g, ln_b, wr_hi, wr_lo, rbias, tm):
    n, d = xres.shape
    full = lambda shape: pl.BlockSpec(shape, lambda i: (0, 0))
    return pl.pallas_call(
        _outproj_kernel,
        grid=(n // tm,),
        in_specs=[pl.BlockSpec((tm, kw), lambda i: (i, a1_col)),
                  pl.BlockSpec((tm, kw), lambda i: (i, a2_col)),
                  full(w.shape),
                  pl.BlockSpec((tm, d), lambda i: (i, 0)),
                  full((1, d)), full((1, d)),
                  full(wr_hi.shape), full(wr_lo.shape), full(rbias.shape)],
        out_specs=[pl.BlockSpec((tm, d), lambda i: (i, 0)),
                   pl.BlockSpec((8, tm), lambda i: (0, i)),
                   pl.BlockSpec((tm, 8), lambda i: (i, 0)),
                   pl.BlockSpec((N_EXPERTS, LANES), lambda i: (0, 0))],
        out_shape=[jax.ShapeDtypeStruct((n, d), F32),
                   jax.ShapeDtypeStruct((8, n), F32),
                   jax.ShapeDtypeStruct((n, 8), F32),
                   jax.ShapeDtypeStruct((N_EXPERTS, LANES), F32)],
        scratch_shapes=[pltpu.VMEM((N_EXPERTS, 1), F32)],
        compiler_params=pltpu.CompilerParams(
            dimension_semantics=("arbitrary",), vmem_limit_bytes=_vmem(52)),
        name="outproj_ln_route",
    )(a1, a2, w, xres, ln_g, ln_b, wr_hi, wr_lo, rbias)


EXPERT_CHUNK_ROWS = 128
EXPERT_STAGES = 4


TOK_WINDOW = 1024


def _expert_kernel(be_ref, na_ref, kb_ref, nk_ref, nx_ref, sl_ref, src_ref,
                   tok_hbm, x_hbm, wg_hbm, wu_hbm, wd_hbm, o_ref,
                   wgu_ref, wdn_ref, stg_ref, sem, xbuf_ref, tok_ref, gsem, tsem, *, layer):
    i = pl.program_id(0)
    tm, d = o_ref.shape
    de = wdn_ref.shape[1]
    na = na_ref[0]

    def tok_copy(j):
        base = pl.multiple_of((src_ref[j] // TOK_WINDOW) * TOK_WINDOW, TOK_WINDOW)
        dst = pl.multiple_of((j % 2) * (2 * TOK_WINDOW), 2 * TOK_WINDOW)
        return pltpu.make_async_copy(tok_hbm.at[pl.ds(base, 2 * TOK_WINDOW)],
                                     tok_ref.at[pl.ds(dst, 2 * TOK_WINDOW)], tsem.at[j % 2])

    def start_rows(j):
        offset = src_ref[j] % TOK_WINDOW
        for buf in range(2):
            @pl.when(j % 2 == buf)
            def _():
                first = buf * (2 * TOK_WINDOW) + offset
                for r in range(tm):
                    tok = tok_ref[first + r]
                    pltpu.make_async_copy(x_hbm.at[pl.ds(tok, 1), :],
                                          xbuf_ref.at[buf, pl.ds(r, 1), :], gsem.at[buf]).start()

    def wait_rows(j):
        pltpu.make_async_copy(x_hbm.at[pl.ds(0, tm), :], xbuf_ref.at[j % 2], gsem.at[j % 2]).wait()
    ch = EXPERT_CHUNK_ROWS
    n_in = d // ch
    n_dn = de // ch
    n_chunks = 2 * n_in + n_dn

    def chunk_copy(e, c):
        st = c % EXPERT_STAGES

        def gate_up(w_hbm, first, col0):
            r0 = pl.multiple_of((c - first) * ch, ch)
            cp = pltpu.make_async_copy(w_hbm.at[layer, e, pl.ds(r0, ch), :],
                                       stg_ref.at[st, :, pl.ds(0, de)], sem.at[st])

            def convert(slot):
                wgu_ref[slot, pl.ds(r0, ch), pl.ds(col0, de)] = stg_ref[st, :, pl.ds(0, de)].astype(BF16)
            return cp, convert

        def down():
            r0 = pl.multiple_of((c - 2 * n_in) * ch, ch)
            cp = pltpu.make_async_copy(wd_hbm.at[layer, e, pl.ds(r0, ch), :],
                                       stg_ref.at[st], sem.at[st])

            def convert(slot):
                wdn_ref[slot, pl.ds(r0, ch), :] = stg_ref[st].astype(BF16)
            return cp, convert

        return ((c < n_in, lambda: gate_up(wg_hbm, 0, 0)),
                (jnp.logical_and(c >= n_in, c < 2 * n_in), lambda: gate_up(wu_hbm, n_in, de)),
                (c >= 2 * n_in, down))

    def start_chunk(e, c):
        for cond, make in chunk_copy(e, c):
            @pl.when(cond)
            def _():
                make()[0].start()

    def finish_chunk(e, c, slot):
        for cond, make in chunk_copy(e, c):
            @pl.when(cond)
            def _():
                cp, convert = make()
                cp.wait()
                convert(slot)

    def prime(e):
        for c in range(EXPERT_STAGES):
            start_chunk(e, jnp.int32(c))

    def process(e, slot, c_lo, c_hi):
        def body(c, _):
            finish_chunk(e, c, slot)

            @pl.when(c + EXPERT_STAGES < n_chunks)
            def _():
                start_chunk(e, c + EXPERT_STAGES)
            return 0
        lax.fori_loop(c_lo, c_hi, body, 0)

    @pl.when(i >= na)
    def _():
        o_ref[...] = jnp.zeros_like(o_ref)

    @pl.when(i < na)
    def _():
        e = be_ref[i]
        kb = kb_ref[i]
        nk = nk_ref[i]
        nxt = nx_ref[i]
        slot = sl_ref[i]

        @pl.when(i == 0)
        def _():
            tok_copy(0).start()
            prime(e)
            tok_copy(0).wait()
            start_rows(0)
            tok_copy(1).start()
            process(e, slot, 0, n_chunks)

        @pl.when(jnp.logical_and(kb == 0, nxt >= 0))
        def _():
            prime(nxt)

        tok_copy(i + 1).wait()
        start_rows(i + 1)
        tok_copy(i + 2).start()
        wait_rows(i)
        gu = _dot(xbuf_ref[i % 2].astype(BF16), wgu_ref[slot])
        g = gu[:, :de]
        u = gu[:, de:]
        hdn = (g * _sigmoid(g)) * u
        o_ref[...] = _dot(hdn.astype(BF16), wdn_ref[slot])

        @pl.when(i == na - 1)
        def _():
            wait_rows(i + 1)
            tok_copy(i + 2).wait()

        @pl.when(nxt >= 0)
        def _():
            process(nxt, 1 - slot, (n_chunks * kb) // nk, (n_chunks * (kb + 1)) // nk)


def _expert_ffn(xn, sorted_tok, tables, w_gate, w_up, w_down, layer, tm, nb):
    d = xn.shape[1]
    de = w_down.shape[2]
    assert d % EXPERT_CHUNK_ROWS == 0 and de % EXPERT_CHUNK_ROWS == 0 and de % LANES == 0
    assert tm <= TOK_WINDOW
    hbm = lambda: pl.BlockSpec(memory_space=pl.ANY)
    return pl.pallas_call(
        functools.partial(_expert_kernel, layer=layer),
        grid_spec=pltpu.PrefetchScalarGridSpec(
            num_scalar_prefetch=len(tables),
            grid=(nb,),
            in_specs=[hbm(), hbm(), hbm(), hbm(), hbm()],
            out_specs=pl.BlockSpec((tm, d), lambda i, *_: (i, 0)),
            scratch_shapes=[pltpu.VMEM((2, d, 2 * de), BF16),
                            pltpu.VMEM((2, de, d), BF16),
                            pltpu.VMEM((EXPERT_STAGES, EXPERT_CHUNK_ROWS, d), F32),
                            pltpu.SemaphoreType.DMA((EXPERT_STAGES,)),
                            pltpu.VMEM((2, tm, d), F32),
                            pltpu.SMEM((2 * 2 * TOK_WINDOW,), jnp.int32),
                            pltpu.SemaphoreType.DMA((2,)),
                            pltpu.SemaphoreType.DMA((2,))]),
        out_shape=jax.ShapeDtypeStruct((nb * tm, d), F32),
        compiler_params=pltpu.CompilerParams(
            dimension_semantics=("arbitrary",), vmem_limit_bytes=_vmem(60)),
        name="expert_ffn",
    )(*tables, sorted_tok, xn, w_gate, w_up, w_down)


ROUTE_GATE_COL = 2


def _combine_kernel(pos_ref, posn_ref, x_ref, ys_hbm, rc_ref, g_ref, b_ref, xn_ref, *rest):
    xb_ref = rest[0] if len(rest) == 3 else None
    ybuf_ref, sem = rest[-2:]
    i = pl.program_id(0)
    nt = pl.num_programs(0)
    tm, d = x_ref.shape

    def start_rows(p_ref, buf):
        for r in range(tm):
            for k in range(TOP_K):
                pltpu.make_async_copy(ys_hbm.at[pl.ds(p_ref[0, 0, k * tm + r], 1), :],
                                      ybuf_ref.at[buf, k, pl.ds(r, 1), :], sem.at[buf]).start()

    @pl.when(i == 0)
    def _():
        start_rows(pos_ref, 0)

    for nxt_buf in range(2):
        @pl.when(jnp.logical_and(i + 1 < nt, (i + 1) % 2 == nxt_buf))
        def _():
            start_rows(posn_ref, nxt_buf)

    buf = i % 2
    for k in range(TOP_K):
        pltpu.make_async_copy(ys_hbm.at[pl.ds(0, tm), :], ybuf_ref.at[buf, k], sem.at[buf]).wait()
    g0 = rc_ref[:, ROUTE_GATE_COL:ROUTE_GATE_COL + 1]
    g1 = rc_ref[:, ROUTE_GATE_COL + 1:ROUTE_GATE_COL + 2]
    y = ALPHA * x_ref[...] + (g0 * ybuf_ref[buf, 0] + g1 * ybuf_ref[buf, 1])
    xn = _layer_norm(y, g_ref[...], b_ref[...])
    xn_ref[...] = xn
    if xb_ref is not None:
        xb_ref[...] = xn.astype(BF16)


POS_TILES_PER_STEP = 8


def _slot_pos_kernel(rt_ref, ps_ref, o_ref, *, tm):
    eid = lax.broadcasted_iota(jnp.int32, (N_EXPERTS, rt_ref.shape[1]), 0)
    for k in range(TOP_K):
        e = rt_ref[k:k + 1, :].astype(jnp.int32)
        base = jnp.sum(jnp.where(eid == e, ps_ref[...], 0.0), axis=0, keepdims=True)
        pos = (base + rt_ref[2 * TOP_K + k:2 * TOP_K + k + 1, :]).astype(jnp.int32)
        for t in range(o_ref.shape[0]):
            o_ref[t, :, pl.ds(k * tm, tm)] = pos[:, t * tm:(t + 1) * tm]


def _slot_positions(route, pstart, tm):
    n = route.shape[1]
    nt = n // tm
    step = POS_TILES_PER_STEP if nt % POS_TILES_PER_STEP == 0 else 1
    return pl.pallas_call(
        functools.partial(_slot_pos_kernel, tm=tm),
        grid=(nt // step,),
        in_specs=[pl.BlockSpec((8, step * tm), lambda i: (0, i)),
                  pl.BlockSpec((N_EXPERTS, 1), lambda i: (0, 0))],
        out_specs=pl.BlockSpec((step, 1, TOP_K * tm), lambda i: (i, 0, 0)),
        out_shape=jax.ShapeDtypeStruct((nt, 1, TOP_K * tm), jnp.int32),
        compiler_params=pltpu.CompilerParams(dimension_semantics=("parallel",)),
        name="slot_positions",
    )(route, pstart)


def _combine_ln(x, ys, pos_t, route_cols, ln_g, ln_b, tm, with_bf16):
    n, d = x.shape
    nt = n // tm
    n_out = 2 if with_bf16 else 1
    row = lambda: pl.BlockSpec((tm, d), lambda i: (i, 0))
    vec = lambda: pl.BlockSpec((1, d), lambda i: (0, 0))
    smem = lambda imap: pl.BlockSpec((1, 1, TOP_K * tm), imap, memory_space=pltpu.SMEM)
    return pl.pallas_call(
        _combine_kernel,
        grid=(nt,),
        in_specs=[smem(lambda i: (i, 0, 0)), smem(lambda i: (jnp.minimum(i + 1, nt - 1), 0, 0)),
                  row(), pl.BlockSpec(memory_space=pl.ANY),
                  pl.BlockSpec((tm, 8), lambda i: (i, 0)), vec(), vec()],
        out_specs=[row(), row()][:n_out],
        out_shape=[jax.ShapeDtypeStruct((n, d), F32), jax.ShapeDtypeStruct((n, d), BF16)][:n_out],
        scratch_shapes=[pltpu.VMEM((2, TOP_K, tm, d), F32), pltpu.SemaphoreType.DMA((2,))],
        compiler_params=pltpu.CompilerParams(
            dimension_semantics=("arbitrary",), vmem_limit_bytes=_vmem(48)),
        name="combine_ln",
    )(pos_t, pos_t, x, ys, route_cols, ln_g, ln_b)


def _gla_gate_kernel(x_ref, wl_ref, w2_ref, ba_ref, o_ref):
    low = _dot(x_ref[...], wl_ref[...]).astype(BF16)
    z = _dot(low, w2_ref[...]) + ba_ref[...]
    o_ref[...] = _log_sigmoid(z) * (1.0 / GLA_TAU)


def _gla_gate(xb, w_low, w_a2, b_a, tm):
    n, d = xb.shape
    kw = w_a2.shape[1]
    return pl.pallas_call(
        _gla_gate_kernel,
        grid=(n // tm,),
        in_specs=[pl.BlockSpec((tm, d), lambda i: (i, 0)),
                  pl.BlockSpec(w_low.shape, lambda i: (0, 0)),
                  pl.BlockSpec(w_a2.shape, lambda i: (0, 0)),
                  pl.BlockSpec((1, kw), lambda i: (0, 0))],
        out_specs=pl.BlockSpec((tm, kw), lambda i: (i, 0)),
        out_shape=jax.ShapeDtypeStruct((n, kw), F32),
        compiler_params=pltpu.CompilerParams(
            dimension_semantics=("parallel",), vmem_limit_bytes=_vmem(32)),
        name="gla_gate",
    )(xb, w_low, w_a2, b_a)


GLA_ROWS = 512


def _gla_kernel(q_ref, k_ref, v_ref, la_ref, g_ref, ng_ref, o_ref, state_ref, *, scale):
    rows, kw = q_ref.shape
    nh = state_ref.shape[0]
    hk, hv = state_ref.shape[1], state_ref.shape[2]
    ck = GLA_CHUNK

    @pl.when(pl.program_id(1) == 0)
    def _():
        state_ref[...] = jnp.zeros_like(state_ref)

    row = lax.broadcasted_iota(jnp.int32, (ck, ck), 0)
    col = lax.broadcasted_iota(jnp.int32, (ck, ck), 1)
    causal = row >= col
    tri = jnp.where(causal, 1.0, 0.0).astype(BF16)
    ones = jnp.ones((ck, LANES), BF16)

    def chunk(c, _):
        r0 = pl.multiple_of(c * ck, ck)
        heads = range(nh)
        ks = [slice(hh * hk, (hh + 1) * hk) for hh in heads]
        vs = [slice(hh * hv, (hh + 1) * hv) for hh in heads]
        la = [_split_bf16(la_ref[pl.ds(r0, ck), ks[hh]]) for hh in heads]
        b = [_dot(tri, la[hh][0]) + _dot(tri, la[hh][1]) for hh in heads]
        dsum = [_dot_tn(la[hh][0], ones) + _dot_tn(la[hh][1], ones) for hh in heads]
        qt, kt, kend = [], [], []
        for hh in heads:
            q = q_ref[pl.ds(r0, ck), ks[hh]].astype(F32) * scale
            k = k_ref[pl.ds(r0, ck), ks[hh]].astype(F32)
            b_last = b[hh][ck - 1:ck, :]
            qt.append((q * jnp.exp(b[hh])).astype(BF16))
            kt.append((k * jnp.exp(-b[hh])).astype(BF16))
            kend.append((k * jnp.exp(b_last - b[hh])).astype(BF16))
        v = [v_ref[pl.ds(r0, ck), vs[hh]] for hh in heads]
        state = [state_ref[hh] for hh in heads]
        attn = [_dot_nt(qt[hh], kt[hh]) for hh in heads]
        inter = [_dot(qt[hh], state[hh].astype(BF16)) for hh in heads]
        kv = [_dot_tn(kend[hh], v[hh]) for hh in heads]
        for hh in heads:
            decay = jnp.concatenate([jnp.exp(dsum[hh])] * (hv // LANES), axis=1)
            state_ref[hh] = state[hh] * decay + kv[hh]
        for hh in heads:
            o = _dot(jnp.where(causal, attn[hh], 0.0).astype(BF16), v[hh]) + inter[hh]
            o = o * lax.rsqrt(jnp.mean(o * o, axis=-1, keepdims=True) + LN_EPS)
            gate = g_ref[pl.ds(r0, ck), vs[hh]].astype(F32)
            o = o * ng_ref[:, vs[hh]] * (gate * _sigmoid(gate))
            o_ref[pl.ds(r0, ck), vs[hh]] = o.astype(o_ref.dtype)
        return 0

    lax.fori_loop(0, rows // ck, chunk, 0, unroll=2)


def _gla(h, la, norm_g, batch, seq):
    n = h.shape[0]
    kw = la.shape[1]
    vw = norm_g.shape[1]
    nh = GLA_HEADS
    hk, hv = kw // nh, vw // nh
    rows = GLA_ROWS
    nj = seq // rows
    kern = functools.partial(_gla_kernel, scale=hk ** -0.5)
    rowblk = lambda b, j: b * nj + j
    return pl.pallas_call(
        kern,
        grid=(batch, nj),
        in_specs=[pl.BlockSpec((rows, kw), lambda b, j: (rowblk(b, j), 0)),
                  pl.BlockSpec((rows, kw), lambda b, j: (rowblk(b, j), 1)),
                  pl.BlockSpec((rows, vw), lambda b, j: (rowblk(b, j), 2 * kw // vw)),
                  pl.BlockSpec((rows, kw), lambda b, j: (rowblk(b, j), 0)),
                  pl.BlockSpec((rows, vw), lambda b, j: (rowblk(b, j), 2 * kw // vw + 1)),
                  pl.BlockSpec((1, vw), lambda b, j: (0, 0))],
        out_specs=pl.BlockSpec((rows, vw), lambda b, j: (rowblk(b, j), 0)),
        out_shape=jax.ShapeDtypeStruct((n, vw), BF16),
        scratch_shapes=[pltpu.VMEM((nh, hk, hv), F32)],
        compiler_params=pltpu.CompilerParams(
            dimension_semantics=("parallel", "arbitrary"), vmem_limit_bytes=_vmem(48)),
        name="gla",
    )(h, h, h, la, h, norm_g)


EXPERT_TM = 256
COMBINE_TM = 256


def _tables_kernel(cnt_ref, be_ref, na_ref, kb_ref, nk_ref, nx_ref, sl_ref, src_ref, ps_ref, *, tm):
    n_tab = be_ref.shape[0]
    blocks_of = lambda e: (cnt_ref[e] + tm - 1) // tm
    run = jnp.int32(0)
    first = jnp.int32(0)
    order = jnp.int32(0)
    for e in range(N_EXPERTS):
        nbe = blocks_of(e)
        ps_ref[e] = run * tm

        def fill(k, _, e=e, run=run, first=first, order=order, nbe=nbe):
            i = run + k
            be_ref[i] = e
            kb_ref[i] = k
            nk_ref[i] = nbe
            sl_ref[i] = order % 2
            src_ref[i] = first + k * tm
            return 0
        lax.fori_loop(0, nbe, fill, 0)
        run = run + nbe
        first = first + cnt_ref[e]
        order = order + jnp.where(nbe > 0, 1, 0)
    na_ref[0] = run

    nxt = jnp.int32(-1)
    end = run
    for e in reversed(range(N_EXPERTS)):
        nbe = blocks_of(e)

        def fill_next(i, _, nxt=nxt):
            nx_ref[i] = nxt
            return 0
        lax.fori_loop(end - nbe, end, fill_next, 0)
        nxt = jnp.where(nbe > 0, e, nxt)
        end = end - nbe

    last = run - 1

    def pad(i, _):
        be_ref[i] = be_ref[last]
        kb_ref[i] = kb_ref[last]
        nk_ref[i] = nk_ref[last]
        nx_ref[i] = nx_ref[last]
        sl_ref[i] = sl_ref[last]
        src_ref[i] = 0
        return 0
    lax.fori_loop(run, n_tab, pad, 0)


def _block_tables(counts, tm, n_tab):
    i32 = jnp.int32
    smem = lambda: pl.BlockSpec(memory_space=pltpu.SMEM)
    tab = jax.ShapeDtypeStruct((n_tab,), i32)
    outs = pl.pallas_call(
        functools.partial(_tables_kernel, tm=tm),
        in_specs=[smem()],
        out_specs=[smem()] * 8,
        out_shape=[tab, jax.ShapeDtypeStruct((1,), i32), tab, tab, tab, tab, tab,
                   jax.ShapeDtypeStruct((N_EXPERTS,), i32)],
        name="block_tables",
    )(counts)
    return tuple(outs[:7]), outs[7]


def _moe(xn, route, route_cols, counts, w_gate, w_up, w_down, layer, ln_g, ln_b, with_bf16):
    n, d = xn.shape
    a = n * TOP_K
    tm = EXPERT_TM
    nb = a // tm + N_EXPERTS
    i32 = jnp.int32
    eidx = route[:TOP_K].astype(i32)
    assign = jnp.arange(n, dtype=i32)[None, :] * TOP_K + jnp.arange(TOP_K, dtype=i32)[:, None]
    sorted_tok = (lax.sort((eidx * a + assign).reshape(a)) % a) // TOP_K
    sorted_tok = jnp.concatenate([sorted_tok, jnp.zeros((2 * TOK_WINDOW,), i32)])
    tables, pstart = _block_tables(counts[:, 0].astype(i32), tm, nb + 2)
    ys = _expert_ffn(xn, sorted_tok, tables, w_gate, w_up, w_down, layer, tm, nb)
    pos = _slot_positions(route, pstart.reshape(N_EXPERTS, 1).astype(F32), COMBINE_TM)
    return _combine_ln(xn, ys, pos, route_cols, ln_g, ln_b, COMBINE_TM, with_bf16)


def kernel(x, even_w_in, even_b_f, even_conv_w, even_conv_b, even_conv_norm_g, even_conv_norm_b, even_w_out, odd_w_in, odd_w_a2, odd_b_a, odd_norm_g, odd_w_out, ln_mix_g, ln_mix_b, ln_ffn_g, ln_ffn_b, router_w, router_bias, expert_w_gate, expert_w_up, expert_w_down):
    batch, seq, d = x.shape
    n = batch * seq
    x2 = x.reshape(n, d)
    fw = FOX_HEADS * FOX_HEAD_DIM
    conv_ch = even_conv_w.shape[-1]
    kw = odd_w_a2.shape[-1]
    vw = odd_norm_g.shape[-1]
    row = lambda t: t.reshape(1, -1)

    wr_hi, wr_lo = _split_bf16(router_w.T)
    rbias = router_bias.reshape(N_EXPERTS, 1).astype(F32)
    experts_w = (expert_w_gate, expert_w_up, expert_w_down)

    w_t = jnp.swapaxes(even_w_in, 1, 2)
    q_scale = LOG2E * FOX_HEAD_DIM ** -0.5
    wf = jnp.zeros((d, LANES), F32).at[:, :FOX_HEADS].set(even_w_in[0, :, 3 * fw:3 * fw + FOX_HEADS])
    b_f = jnp.zeros((1, LANES), F32).at[0, :FOX_HEADS].set(even_b_f[0])
    h = _matmul(x2, w_t, 3 * fw, BF16, 512, fw, first_scale=q_scale)
    h_glu = _matmul(x2, w_t[:, 3 * fw + FOX_HEADS:], 2 * conv_ch, BF16, 512, conv_ch)
    c = _fox_gate(x2, wf.astype(BF16), b_f, batch, seq, 512)
    att = _fox_attention(h, c, batch, seq, 256)
    u = _conv_module(h_glu, 0, conv_ch // CONV_GROUP, even_conv_w[0, :, 0, :],
                     row(even_conv_b[0]), row(even_conv_norm_g[0]), row(even_conv_norm_b[0]),
                     batch, seq)
    xn, route, rcols, cnt = _outproj_ln_route(att, 0, u, 0, fw, even_w_out[0].astype(BF16), x2,
                                              row(ln_mix_g[0]), row(ln_mix_b[0]),
                                              wr_hi, wr_lo, rbias, 512)
    xn, xb = _moe(xn, route, rcols, cnt, *experts_w, 0, row(ln_ffn_g[0]), row(ln_ffn_b[0]), True)

    w_t = jnp.swapaxes(odd_w_in, 1, 2)
    h = _matmul(xb, w_t, 2 * kw + 2 * vw, BF16, 1024, 1024)
    w_low = jnp.zeros((d, LANES), F32).at[:, :GLA_LOW_RANK].set(odd_w_in[0, :, 2 * kw + 2 * vw:])
    w_low = w_low.astype(BF16)
    w_a2 = jnp.zeros((LANES, kw), F32).at[:GLA_LOW_RANK].set(odd_w_a2[0]).astype(BF16)
    la = _gla_gate(xb, w_low, w_a2, row(odd_b_a[0]), 1024)
    o = _gla(h, la, row(odd_norm_g[0]), batch, seq)
    half = vw // 2
    xn, route, rcols, cnt = _outproj_ln_route(o, 0, o, 1, half, odd_w_out[0].astype(BF16), xn,
                                              row(ln_mix_g[1]), row(ln_mix_b[1]),
                                              wr_hi, wr_lo, rbias, 512)
    (xn,) = _moe(xn, route, rcols, cnt, *experts_w, 1, row(ln_ffn_g[1]), row(ln_ffn_b[1]), False)
    return xn.reshape(batch, seq, d)
```

```python
import functools

import jax
import jax.numpy as jnp
from jax import lax
from jax.experimental import pallas as pl
from jax.experimental.pallas import tpu as pltpu

F32 = jnp.float32
BF16 = jnp.bfloat16

DEPTH = 2
ALPHA = (2 * DEPTH) ** 0.25
LN_EPS = 1e-5
FOX_HEADS = 8
FOX_HEAD_DIM = 128
CONV_WIDTH = 31
CONV_GROUP = 128
GLA_HEADS = 4
GLA_LOW_RANK = 16
GLA_TAU = 16.0
GLA_CHUNK = 64
N_EXPERTS = 16
N_GROUPS = 4
EXPERTS_PER_GROUP = N_EXPERTS // N_GROUPS
TOP_K = 2

LANES = 128
V7X_VMEM_BYTES = 64 * 1024 * 1024
NEG_BIG = -1e30
LOG2E = 1.4426950408889634


def _vmem(mib):
    assert mib * 1024 * 1024 < V7X_VMEM_BYTES
    return mib * 1024 * 1024


def _sigmoid(z):
    return 1.0 / (1.0 + jnp.exp(-z))


def _log_sigmoid(z):
    return jnp.minimum(z, 0.0) - jnp.log1p(jnp.exp(-jnp.abs(z)))


def _dot(a, b):
    return jnp.dot(a, b, preferred_element_type=F32)


def _dot_nt(a, b):
    return lax.dot_general(a, b, (((1,), (1,)), ((), ())), preferred_element_type=F32)


def _dot_tn(a, b):
    return lax.dot_general(a, b, (((0,), (0,)), ((), ())), preferred_element_type=F32)


def _split_bf16(v):
    hi = v.astype(BF16)
    lo = (v - hi.astype(F32)).astype(BF16)
    return hi, lo


def _mm_kernel(x_ref, w_ref, o_ref, wb_ref, *, first_scale):
    @pl.when(pl.program_id(1) == 0)
    def _():
        wb_ref[...] = w_ref[...].T.astype(BF16)

    acc = _dot(x_ref[...].astype(BF16), wb_ref[...])
    if first_scale is not None:
        acc = acc * jnp.where(pl.program_id(0) == 0, first_scale, 1.0)
    o_ref[...] = acc.astype(o_ref.dtype)


def _matmul(x, w_t, n_cols, out_dtype, tm, tn, first_scale=None):
    m, k = x.shape
    assert m % tm == 0 and n_cols % tn == 0 and n_cols <= w_t.shape[1] and k == w_t.shape[2]
    return pl.pallas_call(
        functools.partial(_mm_kernel, first_scale=first_scale),
        grid=(n_cols // tn, m // tm),
        in_specs=[pl.BlockSpec((tm, k), lambda j, i: (i, 0)),
                  pl.BlockSpec((None, tn, k), lambda j, i: (0, j, 0))],
        out_specs=pl.BlockSpec((tm, tn), lambda j, i: (i, j)),
        out_shape=jax.ShapeDtypeStruct((m, n_cols), out_dtype),
        scratch_shapes=[pltpu.VMEM((k, tn), BF16)],
        compiler_params=pltpu.CompilerParams(
            dimension_semantics=("parallel", "arbitrary"), vmem_limit_bytes=_vmem(52)),
        name="dense_proj",
    )(x, w_t)


def _fox_gate_kernel(x_ref, wf_ref, bf_ref, c_ref, xb_ref, carry_ref):
    j = pl.program_id(1)

    @pl.when(j == 0)
    def _():
        carry_ref[...] = jnp.zeros_like(carry_ref)

    xb = x_ref[...].astype(BF16)
    xb_ref[...] = xb
    z = _dot(xb, wf_ref[...]) + bf_ref[...]
    lf = _log_sigmoid(z)
    ts = lf.shape[0]
    row = lax.broadcasted_iota(jnp.int32, lf.shape, 0)
    sh = 1
    while sh < ts:
        lf = lf + jnp.where(row >= sh, pltpu.roll(lf, sh, axis=0), 0.0)
        sh *= 2
    c = lf + carry_ref[...]
    c_ref[...] = c * LOG2E
    carry_ref[...] = c[ts - 1:ts, :]


def _fox_gate(x2, wf, b_f, batch, seq, ts):
    d = x2.shape[1]
    nj = seq // ts
    return pl.pallas_call(
        _fox_gate_kernel,
        grid=(batch, nj),
        in_specs=[pl.BlockSpec((ts, d), lambda b, j: (b * nj + j, 0)),
                  pl.BlockSpec((d, LANES), lambda b, j: (0, 0)),
                  pl.BlockSpec((1, LANES), lambda b, j: (0, 0))],
        out_specs=[pl.BlockSpec((ts, LANES), lambda b, j: (b * nj + j, 0)),
                   pl.BlockSpec((ts, d), lambda b, j: (b * nj + j, 0))],
        out_shape=[jax.ShapeDtypeStruct((batch * seq, LANES), F32),
                   jax.ShapeDtypeStruct((batch * seq, d), BF16)],
        scratch_shapes=[pltpu.VMEM((1, LANES), F32)],
        compiler_params=pltpu.CompilerParams(
            dimension_semantics=("parallel", "arbitrary"), vmem_limit_bytes=_vmem(40)),
        name="fox_gate",
    )(x2, wf, b_f)


FOX_HEADS_PER_STEP = 8


def _fox_attn_kernel(q_ref, k_ref, v_ref, c_ref, o_ref, vt_ref, crep_ref, *, tile, heads):
    hh = pl.program_id(1)
    i = pl.program_id(2)
    hd = FOX_HEAD_DIM
    nq = vt_ref.shape[1]

    @pl.when(i == 0)
    def _():
        lane = lax.broadcasted_iota(jnp.int32, (tile, LANES), 1)
        for j in range(nq):
            cj = c_ref[pl.ds(j * tile, tile), :]
            for g in range(heads):
                col = jnp.sum(jnp.where(lane == hh * heads + g, cj, 0.0), axis=1, keepdims=True)
                crep_ref[g, j] = jnp.broadcast_to(col, (tile, LANES))
                vj = v_ref[pl.ds(j * tile, tile), g * hd:(g + 1) * hd]
                vt_ref[g, j] = vj.astype(F32).T.astype(BF16)

    def scores(g, j):
        r0 = pl.multiple_of(j * tile, tile)
        kj = k_ref[pl.ds(r0, tile), g * hd:(g + 1) * hd]
        cj = jnp.concatenate([crep_ref[g, j]] * (tile // LANES), axis=1)
        return _dot_nt(kj, q_ref[:, g * hd:(g + 1) * hd]) - cj

    def update(g, j, carry, st):
        m, l, acc = carry
        m_new = jnp.maximum(m, jnp.max(st, axis=0, keepdims=True))
        a = jnp.exp2(m - m_new)
        p = jnp.exp2(st - m_new)
        l = a * l + jnp.sum(p, axis=0, keepdims=True)
        acc = a * acc + _dot(vt_ref[g, j], p.astype(BF16))
        return m_new, l, acc

    def body(j, carries):
        sts = [scores(g, j) for g in range(heads)]
        return tuple(update(g, j, carries[g], sts[g]) for g in range(heads))

    init = tuple((jnp.full((1, tile), NEG_BIG, F32), jnp.zeros((1, tile), F32),
                  jnp.zeros((hd, tile), F32)) for _ in range(heads))
    carries = lax.fori_loop(0, i, body, init)
    key = lax.broadcasted_iota(jnp.int32, (tile, tile), 0)
    qry = lax.broadcasted_iota(jnp.int32, (tile, tile), 1)
    sts = [jnp.where(key <= qry, scores(g, i), NEG_BIG) for g in range(heads)]
    for g in range(heads):
        _, l, acc = update(g, i, carries[g], sts[g])
        o_ref[:, g * hd:(g + 1) * hd] = (acc * (1.0 / l)).T.astype(o_ref.dtype)


def _fox_attention(h, c, batch, seq, tile):
    n = h.shape[0]
    hd = FOX_HEAD_DIM
    nh = FOX_HEADS
    gh = FOX_HEADS_PER_STEP
    ng = nh // gh
    nq = seq // tile
    kern = functools.partial(_fox_attn_kernel, tile=tile, heads=gh)
    return pl.pallas_call(
        kern,
        grid=(batch, ng, nq),
        in_specs=[pl.BlockSpec((tile, gh * hd), lambda b, hh, i: (b * nq + i, hh)),
                  pl.BlockSpec((seq, gh * hd), lambda b, hh, i: (b, ng + hh)),
                  pl.BlockSpec((seq, gh * hd), lambda b, hh, i: (b, 2 * ng + hh)),
                  pl.BlockSpec((seq, LANES), lambda b, hh, i: (b, 0))],
        out_specs=pl.BlockSpec((tile, gh * hd), lambda b, hh, i: (b * nq + i, hh)),
        out_shape=jax.ShapeDtypeStruct((n, nh * hd), BF16),
        scratch_shapes=[pltpu.VMEM((gh, nq, hd, tile), BF16),
                        pltpu.VMEM((gh, nq, tile, LANES), F32)],
        compiler_params=pltpu.CompilerParams(
            dimension_semantics=("parallel", "parallel", "arbitrary"), vmem_limit_bytes=_vmem(40)),
        name="fox_attention",
    )(h, h, h, c)


CONV_PAD = 32
CONV_ROWS = 256


def _conv_kernel(a_ref, g_ref, w_ref, cb_ref, ng_ref, nb_ref, o_ref, pad_ref):
    seq = a_ref.shape[0]
    pad_ref[pl.ds(0, CONV_PAD), :] = jnp.zeros((CONV_PAD, LANES), F32)
    pad_ref[pl.ds(CONV_PAD, seq), :] = a_ref[...].astype(F32) * _sigmoid(g_ref[...].astype(F32))
    off = CONV_PAD - (CONV_WIDTH - 1)
    for r in range(seq // CONV_ROWS):
        base = r * CONV_ROWS
        acc = jnp.zeros((CONV_ROWS, LANES), F32)
        for j in range(CONV_WIDTH):
            acc = acc + pad_ref[pl.ds(base + off + j, CONV_ROWS), :] * w_ref[pl.ds(j, 1), :]
        y = acc + cb_ref[...]
        mu = jnp.mean(y, axis=-1, keepdims=True)
        yc = y - mu
        var = jnp.mean(yc * yc, axis=-1, keepdims=True)
        yn = yc * lax.rsqrt(var + LN_EPS) * ng_ref[...] + nb_ref[...]
        o_ref[pl.ds(base, CONV_ROWS), :] = (yn * _sigmoid(yn)).astype(o_ref.dtype)


def _conv_module(h, a_col, g_col, conv_w, conv_b, cn_g, cn_b, batch, seq):
    n = h.shape[0]
    ch = conv_w.shape[1]
    ng = ch // CONV_GROUP
    assert seq % CONV_ROWS == 0
    vec = lambda: pl.BlockSpec((1, CONV_GROUP), lambda b, g: (0, g))
    return pl.pallas_call(
        _conv_kernel,
        grid=(batch, ng),
        in_specs=[pl.BlockSpec((seq, CONV_GROUP), lambda b, g: (b, a_col + g)),
                  pl.BlockSpec((seq, CONV_GROUP), lambda b, g: (b, g_col + g)),
                  pl.BlockSpec((CONV_WIDTH, CONV_GROUP), lambda b, g: (0, g)),
                  vec(), vec(), vec()],
        out_specs=pl.BlockSpec((seq, CONV_GROUP), lambda b, g: (b, g)),
        out_shape=jax.ShapeDtypeStruct((n, ch), BF16),
        scratch_shapes=[pltpu.VMEM((seq + CONV_PAD, LANES), F32)],
        compiler_params=pltpu.CompilerParams(
            dimension_semantics=("parallel", "parallel"), vmem_limit_bytes=_vmem(32)),
        name="conv_module",
    )(h, h, conv_w, conv_b, cn_g, cn_b)


def _layer_norm(y, g, b):
    mu = jnp.mean(y, axis=-1, keepdims=True)
    yc = y - mu
    var = jnp.mean(yc * yc, axis=-1, keepdims=True)
    return yc * lax.rsqrt(var + LN_EPS) * g + b


def _route_rows(xn, wr_hi, wr_lo, rbias, count_ref):
    x_hi, x_lo = _split_bf16(xn)
    logits = _dot_nt(wr_hi, x_hi) + _dot_nt(wr_hi, x_lo) + _dot_nt(wr_lo, x_hi)
    scores = _sigmoid(logits)
    sel = scores + rbias
    s = [sel[e:e + 1, :] for e in range(N_EXPERTS)]
    r = [scores[e:e + 1, :] for e in range(N_EXPERTS)]
    pg = EXPERTS_PER_GROUP

    def top2_sum(vals):
        best = None
        for a in range(len(vals)):
            for b in range(a + 1, len(vals)):
                t = vals[a] + vals[b]
                best = t if best is None else jnp.maximum(best, t)
        return best

    gs = [top2_sum(s[g * pg:(g + 1) * pg]) for g in range(N_GROUPS)]
    best, grp = gs[0], jnp.zeros_like(gs[0], dtype=jnp.int32)
    for g in range(1, N_GROUPS):
        upd = gs[g] > best
        best = jnp.where(upd, gs[g], best)
        grp = jnp.where(upd, g, grp)

    def pick_group(vals, k):
        out = vals[(N_GROUPS - 1) * pg + k]
        for g in range(N_GROUPS - 2, -1, -1):
            out = jnp.where(grp == g, vals[g * pg + k], out)
        return out

    v = [pick_group(s, k) for k in range(pg)]
    w = [pick_group(r, k) for k in range(pg)]
    b1, i1, w1 = v[0], jnp.zeros_like(grp), w[0]
    for k in range(1, pg):
        upd = v[k] > b1
        b1 = jnp.where(upd, v[k], b1)
        i1 = jnp.where(upd, k, i1)
        w1 = jnp.where(upd, w[k], w1)
    b2 = jnp.full_like(b1, -jnp.inf)
    i2, w2 = jnp.zeros_like(grp), jnp.zeros_like(w1)
    for k in range(pg):
        upd = jnp.logical_and(i1 != k, v[k] > b2)
        b2 = jnp.where(upd, v[k], b2)
        i2 = jnp.where(upd, k, i2)
        w2 = jnp.where(upd, w[k], w2)
    tot = w1 + w2
    e1 = grp * pg + i1
    e2 = grp * pg + i2
    eid = lax.broadcasted_iota(jnp.int32, scores.shape, 0)
    oh1 = jnp.where(eid == e1, 1.0, 0.0)
    oh2 = jnp.where(eid == e2, 1.0, 0.0)
    cnt = oh1 + oh2
    tm = cnt.shape[1]
    lane = lax.broadcasted_iota(jnp.int32, cnt.shape, 1)
    incl = cnt
    sh = 1
    while sh < tm:
        incl = incl + jnp.where(lane >= sh, pltpu.roll(incl, sh, axis=1), 0.0)
        sh *= 2
    before = incl - cnt + count_ref[...]
    count_ref[...] = count_ref[...] + incl[:, tm - 1:tm]
    rank1 = jnp.sum(oh1 * before, axis=0, keepdims=True)
    rank2 = jnp.sum(oh2 * before, axis=0, keepdims=True)
    rows = [e1.astype(F32), e2.astype(F32), w1 / tot, w2 / tot, rank1, rank2]
    rows += [jnp.zeros_like(w1)] * (8 - len(rows))
    return jnp.concatenate(rows, axis=0)


OUTPROJ_SUBTILES = 2


def _outproj_kernel(a1_ref, a2_ref, w_ref, x_ref, g_ref, b_ref, wrh_ref, wrl_ref, rb_ref,
                    xn_ref, rt_ref, rc_ref, cnt_ref, count_ref):
    @pl.when(pl.program_id(0) == 0)
    def _():
        count_ref[...] = jnp.zeros_like(count_ref)

    k1 = a1_ref.shape[1]
    k2 = a2_ref.shape[1]
    tm = x_ref.shape[0]
    sub = tm // OUTPROJ_SUBTILES
    rs = [pl.ds(t * sub, sub) for t in range(OUTPROJ_SUBTILES)]
    mix = [_dot(a1_ref[r, :], w_ref[pl.ds(0, k1), :]) + _dot(a2_ref[r, :], w_ref[pl.ds(k1, k2), :])
           for r in rs]
    for t, r in enumerate(rs):
        xn = _layer_norm(ALPHA * x_ref[r, :] + mix[t], g_ref[...], b_ref[...])
        xn_ref[r, :] = xn
        rows = _route_rows(xn, wrh_ref[...], wrl_ref[...], rb_ref[...], count_ref)
        rt_ref[:, r] = rows
        rc_ref[r, :] = rows.T
    cnt_ref[...] = jnp.broadcast_to(count_ref[...], cnt_ref.shape)


def _outproj_ln_route(a1, a1_col, a2, a2_col, kw, w, xres, ln_g, ln_b, wr_hi, wr_lo, rbias, tm):
    n, d = xres.shape
    full = lambda shape: pl.BlockSpec(shape, lambda i: (0, 0))
    return pl.pallas_call(
        _outproj_kernel,
        grid=(n // tm,),
        in_specs=[pl.BlockSpec((tm, kw), lambda i: (i, a1_col)),
                  pl.BlockSpec((tm, kw), lambda i: (i, a2_col)),
                  full(w.shape),
                  pl.BlockSpec((tm, d), lambda i: (i, 0)),
                  full((1, d)), full((1, d)),
                  full(wr_hi.shape), full(wr_lo.shape), full(rbias.shape)],
        out_specs=[pl.BlockSpec((tm, d), lambda i: (i, 0)),
                   pl.BlockSpec((8, tm), lambda i: (0, i)),
                   pl.BlockSpec((tm, 8), lambda i: (i, 0)),
                   pl.BlockSpec((N_EXPERTS, LANES), lambda i: (0, 0))],
        out_shape=[jax.ShapeDtypeStruct((n, d), F32),
                   jax.ShapeDtypeStruct((8, n), F32),
                   jax.ShapeDtypeStruct((n, 8), F32),
                   jax.ShapeDtypeStruct((N_EXPERTS, LANES), F32)],
        scratch_shapes=[pltpu.VMEM((N_EXPERTS, 1), F32)],
        compiler_params=pltpu.CompilerParams(
            dimension_semantics=("arbitrary",), vmem_limit_bytes=_vmem(52)),
        name="outproj_ln_route",
    )(a1, a2, w, xres, ln_g, ln_b, wr_hi, wr_lo, rbias)


EXPERT_CHUNK_ROWS = 128
EXPERT_STAGES = 4


TOK_WINDOW = 1024


def _expert_kernel(be_ref, na_ref, kb_ref, nk_ref, nx_ref, sl_ref, src_ref,
                   tok_hbm, x_hbm, wg_hbm, wu_hbm, wd_hbm, o_ref,
                   wgu_ref, wdn_ref, stg_ref, sem, xbuf_ref, tok_ref, gsem, tsem, *, layer):
    i = pl.program_id(0)
    tm, d = o_ref.shape
    de = wdn_ref.shape[1]
    na = na_ref[0]

    def tok_copy(j):
        base = pl.multiple_of((src_ref[j] // TOK_WINDOW) * TOK_WINDOW, TOK_WINDOW)
        dst = pl.multiple_of((j % 2) * (2 * TOK_WINDOW), 2 * TOK_WINDOW)
        return pltpu.make_async_copy(tok_hbm.at[pl.ds(base, 2 * TOK_WINDOW)],
                                     tok_ref.at[pl.ds(dst, 2 * TOK_WINDOW)], tsem.at[j % 2])

    def start_rows(j):
        offset = src_ref[j] % TOK_WINDOW
        for buf in range(2):
            @pl.when(j % 2 == buf)
            def _():
                first = buf * (2 * TOK_WINDOW) + offset
                for r in range(tm):
                    tok = tok_ref[first + r]
                    pltpu.make_async_copy(x_hbm.at[pl.ds(tok, 1), :],
                                          xbuf_ref.at[buf, pl.ds(r, 1), :], gsem.at[buf]).start()

    def wait_rows(j):
        pltpu.make_async_copy(x_hbm.at[pl.ds(0, tm), :], xbuf_ref.at[j % 2], gsem.at[j % 2]).wait()
    ch = EXPERT_CHUNK_ROWS
    n_in = d // ch
    n_dn = de // ch
    n_chunks = 2 * n_in + n_dn

    def chunk_copy(e, c):
        st = c % EXPERT_STAGES

        def gate_up(w_hbm, first, col0):
            r0 = pl.multiple_of((c - first) * ch, ch)
            cp = pltpu.make_async_copy(w_hbm.at[layer, e, pl.ds(r0, ch), :],
                                       stg_ref.at[st, :, pl.ds(0, de)], sem.at[st])

            def convert(slot):
                wgu_ref[slot, pl.ds(r0, ch), pl.ds(col0, de)] = stg_ref[st, :, pl.ds(0, de)].astype(BF16)
            return cp, convert

        def down():
            r0 = pl.multiple_of((c - 2 * n_in) * ch, ch)
            cp = pltpu.make_async_copy(wd_hbm.at[layer, e, pl.ds(r0, ch), :],
                                       stg_ref.at[st], sem.at[st])

            def convert(slot):
                wdn_ref[slot, pl.ds(r0, ch), :] = stg_ref[st].astype(BF16)
            return cp, convert

        return ((c < n_in, lambda: gate_up(wg_hbm, 0, 0)),
                (jnp.logical_and(c >= n_in, c < 2 * n_in), lambda: gate_up(wu_hbm, n_in, de)),
                (c >= 2 * n_in, down))

    def start_chunk(e, c):
        for cond, make in chunk_copy(e, c):
            @pl.when(cond)
            def _():
                make()[0].start()

    def finish_chunk(e, c, slot):
        for cond, make in chunk_copy(e, c):
            @pl.when(cond)
            def _():
                cp, convert = make()
                cp.wait()
                convert(slot)

    def prime(e):
        for c in range(EXPERT_STAGES):
            start_chunk(e, jnp.int32(c))

    def process(e, slot, c_lo, c_hi):
        def body(c, _):
            finish_chunk(e, c, slot)

            @pl.when(c + EXPERT_STAGES < n_chunks)
            def _():
                start_chunk(e, c + EXPERT_STAGES)
            return 0
        lax.fori_loop(c_lo, c_hi, body, 0)

    @pl.when(i >= na)
    def _():
        o_ref[...] = jnp.zeros_like(o_ref)

    @pl.when(i < na)
    def _():
        e = be_ref[i]
        kb = kb_ref[i]
        nk = nk_ref[i]
        nxt = nx_ref[i]
        slot = sl_ref[i]

        @pl.when(i == 0)
        def _():
            tok_copy(0).start()
            prime(e)
            tok_copy(0).wait()
            start_rows(0)
            tok_copy(1).start()
            process(e, slot, 0, n_chunks)

        @pl.when(jnp.logical_and(kb == 0, nxt >= 0))
        def _():
            prime(nxt)

        tok_copy(i + 1).wait()
        start_rows(i + 1)
        tok_copy(i + 2).start()
        wait_rows(i)
        gu = _dot(xbuf_ref[i % 2].astype(BF16), wgu_ref[slot])
        g = gu[:, :de]
        u = gu[:, de:]
        hdn = (g * _sigmoid(g)) * u
        o_ref[...] = _dot(hdn.astype(BF16), wdn_ref[slot])

        @pl.when(i == na - 1)
        def _():
            wait_rows(i + 1)
            tok_copy(i + 2).wait()

        @pl.when(nxt >= 0)
        def _():
            process(nxt, 1 - slot, (n_chunks * kb) // nk, (n_chunks * (kb + 1)) // nk)


def _expert_ffn(xn, sorted_tok, tables, w_gate, w_up, w_down, layer, tm, nb):
    d = xn.shape[1]
    de = w_down.shape[2]
    assert d % EXPERT_CHUNK_ROWS == 0 and de % EXPERT_CHUNK_ROWS == 0 and de % LANES == 0
    assert tm <= TOK_WINDOW
    hbm = lambda: pl.BlockSpec(memory_space=pl.ANY)
    return pl.pallas_call(
        functools.partial(_expert_kernel, layer=layer),
        grid_spec=pltpu.PrefetchScalarGridSpec(
            num_scalar_prefetch=len(tables),
            grid=(nb,),
            in_specs=[hbm(), hbm(), hbm(), hbm(), hbm()],
            out_specs=pl.BlockSpec((tm, d), lambda i, *_: (i, 0)),
            scratch_shapes=[pltpu.VMEM((2, d, 2 * de), BF16),
                            pltpu.VMEM((2, de, d), BF16),
                            pltpu.VMEM((EXPERT_STAGES, EXPERT_CHUNK_ROWS, d), F32),
                            pltpu.SemaphoreType.DMA((EXPERT_STAGES,)),
                            pltpu.VMEM((2, tm, d), F32),
                            pltpu.SMEM((2 * 2 * TOK_WINDOW,), jnp.int32),
                            pltpu.SemaphoreType.DMA((2,)),
                            pltpu.SemaphoreType.DMA((2,))]),
        out_shape=jax.ShapeDtypeStruct((nb * tm, d), F32),
        compiler_params=pltpu.CompilerParams(
            dimension_semantics=("arbitrary",), vmem_limit_bytes=_vmem(60)),
        name="expert_ffn",
    )(*tables, sorted_tok, xn, w_gate, w_up, w_down)


ROUTE_GATE_COL = 2


def _combine_kernel(pos_ref, posn_ref, x_ref, ys_hbm, rc_ref, g_ref, b_ref, xn_ref, *rest):
    xb_ref = rest[0] if len(rest) == 3 else None
    ybuf_ref, sem = rest[-2:]
    i = pl.program_id(0)
    nt = pl.num_programs(0)
    tm, d = x_ref.shape

    def start_rows(p_ref, buf):
        for r in range(tm):
            for k in range(TOP_K):
                pltpu.make_async_copy(ys_hbm.at[pl.ds(p_ref[0, 0, k * tm + r], 1), :],
                                      ybuf_ref.at[buf, k, pl.ds(r, 1), :], sem.at[buf]).start()

    @pl.when(i == 0)
    def _():
        start_rows(pos_ref, 0)

    for nxt_buf in range(2):
        @pl.when(jnp.logical_and(i + 1 < nt, (i + 1) % 2 == nxt_buf))
        def _():
            start_rows(posn_ref, nxt_buf)

    buf = i % 2
    for k in range(TOP_K):
        pltpu.make_async_copy(ys_hbm.at[pl.ds(0, tm), :], ybuf_ref.at[buf, k], sem.at[buf]).wait()
    g0 = rc_ref[:, ROUTE_GATE_COL:ROUTE_GATE_COL + 1]
    g1 = rc_ref[:, ROUTE_GATE_COL + 1:ROUTE_GATE_COL + 2]
    y = ALPHA * x_ref[...] + (g0 * ybuf_ref[buf, 0] + g1 * ybuf_ref[buf, 1])
    xn = _layer_norm(y, g_ref[...], b_ref[...])
    xn_ref[...] = xn
    if xb_ref is not None:
        xb_ref[...] = xn.astype(BF16)


POS_TILES_PER_STEP = 8


def _slot_pos_kernel(rt_ref, ps_ref, o_ref, *, tm):
    eid = lax.broadcasted_iota(jnp.int32, (N_EXPERTS, rt_ref.shape[1]), 0)
    for k in range(TOP_K):
        e = rt_ref[k:k + 1, :].astype(jnp.int32)
        base = jnp.sum(jnp.where(eid == e, ps_ref[...], 0.0), axis=0, keepdims=True)
        pos = (base + rt_ref[2 * TOP_K + k:2 * TOP_K + k + 1, :]).astype(jnp.int32)
        for t in range(o_ref.shape[0]):
            o_ref[t, :, pl.ds(k * tm, tm)] = pos[:, t * tm:(t + 1) * tm]


def _slot_positions(route, pstart, tm):
    n = route.shape[1]
    nt = n // tm
    step = POS_TILES_PER_STEP if nt % POS_TILES_PER_STEP == 0 else 1
    return pl.pallas_call(
        functools.partial(_slot_pos_kernel, tm=tm),
        grid=(nt // step,),
        in_specs=[pl.BlockSpec((8, step * tm), lambda i: (0, i)),
                  pl.BlockSpec((N_EXPERTS, 1), lambda i: (0, 0))],
        out_specs=pl.BlockSpec((step, 1, TOP_K * tm), lambda i: (i, 0, 0)),
        out_shape=jax.ShapeDtypeStruct((nt, 1, TOP_K * tm), jnp.int32),
        compiler_params=pltpu.CompilerParams(dimension_semantics=("parallel",)),
        name="slot_positions",
    )(route, pstart)


def _combine_ln(x, ys, pos_t, route_cols, ln_g, ln_b, tm, with_bf16):
    n, d = x.shape
    nt = n // tm
    n_out = 2 if with_bf16 else 1
    row = lambda: pl.BlockSpec((tm, d), lambda i: (i, 0))
    vec = lambda: pl.BlockSpec((1, d), lambda i: (0, 0))
    smem = lambda imap: pl.BlockSpec((1, 1, TOP_K * tm), imap, memory_space=pltpu.SMEM)
    return pl.pallas_call(
        _combine_kernel,
        grid=(nt,),
        in_specs=[smem(lambda i: (i, 0, 0)), smem(lambda i: (jnp.minimum(i + 1, nt - 1), 0, 0)),
                  row(), pl.BlockSpec(memory_space=pl.ANY),
                  pl.BlockSpec((tm, 8), lambda i: (i, 0)), vec(), vec()],
        out_specs=[row(), row()][:n_out],
        out_shape=[jax.ShapeDtypeStruct((n, d), F32), jax.ShapeDtypeStruct((n, d), BF16)][:n_out],
        scratch_shapes=[pltpu.VMEM((2, TOP_K, tm, d), F32), pltpu.SemaphoreType.DMA((2,))],
        compiler_params=pltpu.CompilerParams(
            dimension_semantics=("arbitrary",), vmem_limit_bytes=_vmem(48)),
        name="combine_ln",
    )(pos_t, pos_t, x, ys, route_cols, ln_g, ln_b)


def _gla_gate_kernel(x_ref, wl_ref, w2_ref, ba_ref, o_ref):
    low = _dot(x_ref[...], wl_ref[...]).astype(BF16)
    z = _dot(low, w2_ref[...]) + ba_ref[...]
    o_ref[...] = _log_sigmoid(z) * (1.0 / GLA_TAU)


def _gla_gate(xb, w_low, w_a2, b_a, tm):
    n, d = xb.shape
    kw = w_a2.shape[1]
    return pl.pallas_call(
        _gla_gate_kernel,
        grid=(n // tm,),
        in_specs=[pl.BlockSpec((tm, d), lambda i: (i, 0)),
                  pl.BlockSpec(w_low.shape, lambda i: (0, 0)),
                  pl.BlockSpec(w_a2.shape, lambda i: (0, 0)),
                  pl.BlockSpec((1, kw), lambda i: (0, 0))],
        out_specs=pl.BlockSpec((tm, kw), lambda i: (i, 0)),
        out_shape=jax.ShapeDtypeStruct((n, kw), F32),
        compiler_params=pltpu.CompilerParams(
            dimension_semantics=("parallel",), vmem_limit_bytes=_vmem(32)),
        name="gla_gate",
    )(xb, w_low, w_a2, b_a)


GLA_ROWS = 512


def _gla_kernel(q_ref, k_ref, v_ref, la_ref, g_ref, ng_ref, o_ref, state_ref, *, scale):
    rows, kw = q_ref.shape
    nh = state_ref.shape[0]
    hk, hv = state_ref.shape[1], state_ref.shape[2]
    ck = 2 * GLA_CHUNK
    mid = GLA_CHUNK - 1

    @pl.when(pl.program_id(1) == 0)
    def _():
        state_ref[...] = jnp.zeros_like(state_ref)

    row = lax.broadcasted_iota(jnp.int32, (ck, ck), 0)
    col = lax.broadcasted_iota(jnp.int32, (ck, ck), 1)
    causal = row >= col
    tri = jnp.where(causal, 1.0, 0.0).astype(BF16)
    ones = jnp.ones((ck, LANES), BF16)

    def chunk(c, _):
        r0 = pl.multiple_of(c * ck, ck)
        heads = range(nh)
        ks = [slice(hh * hk, (hh + 1) * hk) for hh in heads]
        vs = [slice(hh * hv, (hh + 1) * hv) for hh in heads]
        la = [_split_bf16(la_ref[pl.ds(r0, ck), ks[hh]]) for hh in heads]
        b = [_dot(tri, la[hh][0]) + _dot(tri, la[hh][1]) for hh in heads]
        dsum = [_dot_tn(la[hh][0], ones) + _dot_tn(la[hh][1], ones) for hh in heads]
        qt, kt, kend, qin = [], [], [], []
        for hh in heads:
            q = q_ref[pl.ds(r0, ck), ks[hh]].astype(F32) * scale
            k = k_ref[pl.ds(r0, ck), ks[hh]].astype(F32)
            b_mid = b[hh][mid:mid + 1, :]
            b_last = b[hh][ck - 1:ck, :]
            qt.append((q * jnp.exp(b[hh] - b_mid)).astype(BF16))
            kt.append((k * jnp.exp(b_mid - b[hh])).astype(BF16))
            kend.append((k * jnp.exp(b_last - b[hh])).astype(BF16))
            qin.append((q * jnp.exp(b[hh])).astype(BF16))
        v = [v_ref[pl.ds(r0, ck), vs[hh]] for hh in heads]
        state = [state_ref[hh] for hh in heads]
        attn = [_dot_nt(qt[hh], kt[hh]) for hh in heads]
        inter = [_dot(qin[hh], state[hh].astype(BF16)) for hh in heads]
        kv = [_dot_tn(kend[hh], v[hh]) for hh in heads]
        for hh in heads:
            decay = jnp.concatenate([jnp.exp(dsum[hh])] * (hv // LANES), axis=1)
            state_ref[hh] = state[hh] * decay + kv[hh]
        for hh in heads:
            o = _dot(jnp.where(causal, attn[hh], 0.0).astype(BF16), v[hh]) + inter[hh]
            o = o * lax.rsqrt(jnp.mean(o * o, axis=-1, keepdims=True) + LN_EPS)
            gate = g_ref[pl.ds(r0, ck), vs[hh]].astype(F32)
            o = o * ng_ref[:, vs[hh]] * (gate * _sigmoid(gate))
            o_ref[pl.ds(r0, ck), vs[hh]] = o.astype(o_ref.dtype)
        return 0

    lax.fori_loop(0, rows // ck, chunk, 0, unroll=2)


def _gla(h, la, norm_g, batch, seq):
    n = h.shape[0]
    kw = la.shape[1]
    vw = norm_g.shape[1]
    nh = GLA_HEADS
    hk, hv = kw // nh, vw // nh
    rows = GLA_ROWS
    nj = seq // rows
    kern = functools.partial(_gla_kernel, scale=hk ** -0.5)
    rowblk = lambda b, j: b * nj + j
    return pl.pallas_call(
        kern,
        grid=(batch, nj),
        in_specs=[pl.BlockSpec((rows, kw), lambda b, j: (rowblk(b, j), 0)),
                  pl.BlockSpec((rows, kw), lambda b, j: (rowblk(b, j), 1)),
                  pl.BlockSpec((rows, vw), lambda b, j: (rowblk(b, j), 2 * kw // vw)),
                  pl.BlockSpec((rows, kw), lambda b, j: (rowblk(b, j), 0)),
                  pl.BlockSpec((rows, vw), lambda b, j: (rowblk(b, j), 2 * kw // vw + 1)),
                  pl.BlockSpec((1, vw), lambda b, j: (0, 0))],
        out_specs=pl.BlockSpec((rows, vw), lambda b, j: (rowblk(b, j), 0)),
        out_shape=jax.ShapeDtypeStruct((n, vw), BF16),
        scratch_shapes=[pltpu.VMEM((nh, hk, hv), F32)],
        compiler_params=pltpu.CompilerParams(
            dimension_semantics=("parallel", "arbitrary"), vmem_limit_bytes=_vmem(48)),
        name="gla",
    )(h, h, h, la, h, norm_g)


EXPERT_TM = 256
COMBINE_TM = 256


def _tables_kernel(cnt_ref, be_ref, na_ref, kb_ref, nk_ref, nx_ref, sl_ref, src_ref, ps_ref, *, tm):
    n_tab = be_ref.shape[0]
    blocks_of = lambda e: (cnt_ref[e] + tm - 1) // tm
    run = jnp.int32(0)
    first = jnp.int32(0)
    order = jnp.int32(0)
    for e in range(N_EXPERTS):
        nbe = blocks_of(e)
        ps_ref[e] = run * tm

        def fill(k, _, e=e, run=run, first=first, order=order, nbe=nbe):
            i = run + k
            be_ref[i] = e
            kb_ref[i] = k
            nk_ref[i] = nbe
            sl_ref[i] = order % 2
            src_ref[i] = first + k * tm
            return 0
        lax.fori_loop(0, nbe, fill, 0)
        run = run + nbe
        first = first + cnt_ref[e]
        order = order + jnp.where(nbe > 0, 1, 0)
    na_ref[0] = run

    nxt = jnp.int32(-1)
    end = run
    for e in reversed(range(N_EXPERTS)):
        nbe = blocks_of(e)

        def fill_next(i, _, nxt=nxt):
            nx_ref[i] = nxt
            return 0
        lax.fori_loop(end - nbe, end, fill_next, 0)
        nxt = jnp.where(nbe > 0, e, nxt)
        end = end - nbe

    last = run - 1

    def pad(i, _):
        be_ref[i] = be_ref[last]
        kb_ref[i] = kb_ref[last]
        nk_ref[i] = nk_ref[last]
        nx_ref[i] = nx_ref[last]
        sl_ref[i] = sl_ref[last]
        src_ref[i] = 0
        return 0
    lax.fori_loop(run, n_tab, pad, 0)


def _block_tables(counts, tm, n_tab):
    i32 = jnp.int32
    smem = lambda: pl.BlockSpec(memory_space=pltpu.SMEM)
    tab = jax.ShapeDtypeStruct((n_tab,), i32)
    outs = pl.pallas_call(
        functools.partial(_tables_kernel, tm=tm),
        in_specs=[smem()],
        out_specs=[smem()] * 8,
        out_shape=[tab, jax.ShapeDtypeStruct((1,), i32), tab, tab, tab, tab, tab,
                   jax.ShapeDtypeStruct((N_EXPERTS,), i32)],
        name="block_tables",
    )(counts)
    return tuple(outs[:7]), outs[7]


def _moe(xn, route, route_cols, counts, w_gate, w_up, w_down, layer, ln_g, ln_b, with_bf16):
    n, d = xn.shape
    a = n * TOP_K
    tm = EXPERT_TM
    nb = a // tm + N_EXPERTS
    i32 = jnp.int32
    eidx = route[:TOP_K].astype(i32)
    assign = jnp.arange(n, dtype=i32)[None, :] * TOP_K + jnp.arange(TOP_K, dtype=i32)[:, None]
    sorted_tok = (lax.sort((eidx * a + assign).reshape(a)) % a) // TOP_K
    sorted_tok = jnp.concatenate([sorted_tok, jnp.zeros((2 * TOK_WINDOW,), i32)])
    tables, pstart = _block_tables(counts[:, 0].astype(i32), tm, nb + 2)
    ys = _expert_ffn(xn, sorted_tok, tables, w_gate, w_up, w_down, layer, tm, nb)
    pos = _slot_positions(route, pstart.reshape(N_EXPERTS, 1).astype(F32), COMBINE_TM)
    return _combine_ln(xn, ys, pos, route_cols, ln_g, ln_b, COMBINE_TM, with_bf16)


def kernel(x, even_w_in, even_b_f, even_conv_w, even_conv_b, even_conv_norm_g, even_conv_norm_b, even_w_out, odd_w_in, odd_w_a2, odd_b_a, odd_norm_g, odd_w_out, ln_mix_g, ln_mix_b, ln_ffn_g, ln_ffn_b, router_w, router_bias, expert_w_gate, expert_w_up, expert_w_down):
    batch, seq, d = x.shape
    n = batch * seq
    x2 = x.reshape(n, d)
    fw = FOX_HEADS * FOX_HEAD_DIM
    conv_ch = even_conv_w.shape[-1]
    kw = odd_w_a2.shape[-1]
    vw = odd_norm_g.shape[-1]
    row = lambda t: t.reshape(1, -1)

    wr_hi, wr_lo = _split_bf16(router_w.T)
    rbias = router_bias.reshape(N_EXPERTS, 1).astype(F32)
    experts_w = (expert_w_gate, expert_w_up, expert_w_down)

    w_t = jnp.swapaxes(even_w_in, 1, 2)
    q_scale = LOG2E * FOX_HEAD_DIM ** -0.5
    wf = jnp.zeros((d, LANES), F32).at[:, :FOX_HEADS].set(even_w_in[0, :, 3 * fw:3 * fw + FOX_HEADS])
    b_f = jnp.zeros((1, LANES), F32).at[0, :FOX_HEADS].set(even_b_f[0])
    c, xb = _fox_gate(x2, wf.astype(BF16), b_f, batch, seq, 512)
    h = _matmul(xb, w_t, 3 * fw, BF16, 1024, fw, first_scale=q_scale)
    h_glu = _matmul(xb, w_t[:, 3 * fw + FOX_HEADS:], 2 * conv_ch, BF16, 1024, conv_ch)
    att = _fox_attention(h, c, batch, seq, 256)
    u = _conv_module(h_glu, 0, conv_ch // CONV_GROUP, even_conv_w[0, :, 0, :],
                     row(even_conv_b[0]), row(even_conv_norm_g[0]), row(even_conv_norm_b[0]),
                     batch, seq)
    xn, route, rcols, cnt = _outproj_ln_route(att, 0, u, 0, fw, even_w_out[0].astype(BF16), x2,
                                              row(ln_mix_g[0]), row(ln_mix_b[0]),
                                              wr_hi, wr_lo, rbias, 512)
    xn, xb = _moe(xn, route, rcols, cnt, *experts_w, 0, row(ln_ffn_g[0]), row(ln_ffn_b[0]), True)

    w_t = jnp.swapaxes(odd_w_in, 1, 2)
    h = _matmul(xb, w_t, 2 * kw + 2 * vw, BF16, 1024, 1024)
    w_low = jnp.zeros((d, LANES), F32).at[:, :GLA_LOW_RANK].set(odd_w_in[0, :, 2 * kw + 2 * vw:])
    w_low = w_low.astype(BF16)
    w_a2 = jnp.zeros((LANES, kw), F32).at[:GLA_LOW_RANK].set(odd_w_a2[0]).astype(BF16)
    la = _gla_gate(xb, w_low, w_a2, row(odd_b_a[0]), 1024)
    o = _gla(h, la, row(odd_norm_g[0]), batch, seq)
    half = vw // 2
    xn, route, rcols, cnt = _outproj_ln_route(o, 0, o, 1, half, odd_w_out[0].astype(BF16), xn,
                                              row(ln_mix_g[1]), row(ln_mix_b[1]),
                                              wr_hi, wr_lo, rbias, 512)
    (xn,) = _moe(xn, route, rcols, cnt, *experts_w, 1, row(ln_ffn_g[1]), row(ln_ffn_b[1]), False)
    return xn.reshape(batch, seq, d)
```

```python
import functools

import jax
import jax.numpy as jnp
from jax import lax
from jax.experimental import pallas as pl
from jax.experimental.pallas import tpu as pltpu

F32 = jnp.float32
BF16 = jnp.bfloat16

DEPTH = 2
ALPHA = (2 * DEPTH) ** 0.25
LN_EPS = 1e-5
FOX_HEADS = 8
FOX_HEAD_DIM = 128
CONV_WIDTH = 31
CONV_GROUP = 128
GLA_HEADS = 4
GLA_LOW_RANK = 16
GLA_TAU = 16.0
GLA_CHUNK = 64
N_EXPERTS = 16
N_GROUPS = 4
EXPERTS_PER_GROUP = N_EXPERTS // N_GROUPS
TOP_K = 2

LANES = 128
V7X_VMEM_BYTES = 64 * 1024 * 1024
NEG_BIG = -1e30
LOG2E = 1.4426950408889634


def _vmem(mib):
    assert mib * 1024 * 1024 < V7X_VMEM_BYTES
    return mib * 1024 * 1024


def _sigmoid(z):
    return 1.0 / (1.0 + jnp.exp(-z))


def _log_sigmoid(z):
    return jnp.minimum(z, 0.0) - jnp.log1p(jnp.exp(-jnp.abs(z)))


def _dot(a, b):
    return jnp.dot(a, b, preferred_element_type=F32)


def _dot_nt(a, b):
    return lax.dot_general(a, b, (((1,), (1,)), ((), ())), preferred_element_type=F32)


def _dot_tn(a, b):
    return lax.dot_general(a, b, (((0,), (0,)), ((), ())), preferred_element_type=F32)


def _split_bf16(v):
    hi = v.astype(BF16)
    lo = (v - hi.astype(F32)).astype(BF16)
    return hi, lo


def _mm_kernel(x_ref, w_ref, o_ref, wb_ref, *, first_scale):
    @pl.when(pl.program_id(1) == 0)
    def _():
        wb_ref[...] = w_ref[...].T.astype(BF16)

    acc = _dot(x_ref[...].astype(BF16), wb_ref[...])
    if first_scale is not None:
        acc = acc * jnp.where(pl.program_id(0) == 0, first_scale, 1.0)
    o_ref[...] = acc.astype(o_ref.dtype)


def _matmul(x, w_t, n_cols, out_dtype, tm, tn, first_scale=None):
    m, k = x.shape
    assert m % tm == 0 and n_cols % tn == 0 and n_cols <= w_t.shape[1] and k == w_t.shape[2]
    return pl.pallas_call(
        functools.partial(_mm_kernel, first_scale=first_scale),
        grid=(n_cols // tn, m // tm),
        in_specs=[pl.BlockSpec((tm, k), lambda j, i: (i, 0)),
                  pl.BlockSpec((None, tn, k), lambda j, i: (0, j, 0))],
        out_specs=pl.BlockSpec((tm, tn), lambda j, i: (i, j)),
        out_shape=jax.ShapeDtypeStruct((m, n_cols), out_dtype),
        scratch_shapes=[pltpu.VMEM((k, tn), BF16)],
        compiler_params=pltpu.CompilerParams(
            dimension_semantics=("parallel", "arbitrary"), vmem_limit_bytes=_vmem(52)),
        name="dense_proj",
    )(x, w_t)


def _fox_gate_kernel(x_ref, wf_ref, bf_ref, c_ref, xb_ref, carry_ref):
    j = pl.program_id(1)

    @pl.when(j == 0)
    def _():
        carry_ref[...] = jnp.zeros_like(carry_ref)

    xb = x_ref[...].astype(BF16)
    xb_ref[...] = xb
    z = _dot(xb, wf_ref[...]) + bf_ref[...]
    lf = _log_sigmoid(z)
    ts = lf.shape[0]
    row = lax.broadcasted_iota(jnp.int32, lf.shape, 0)
    sh = 1
    while sh < ts:
        lf = lf + jnp.where(row >= sh, pltpu.roll(lf, sh, axis=0), 0.0)
        sh *= 2
    c = lf + carry_ref[...]
    c_ref[...] = c * LOG2E
    carry_ref[...] = c[ts - 1:ts, :]


def _fox_gate(x2, wf, b_f, batch, seq, ts):
    d = x2.shape[1]
    nj = seq // ts
    return pl.pallas_call(
        _fox_gate_kernel,
        grid=(batch, nj),
        in_specs=[pl.BlockSpec((ts, d), lambda b, j: (b * nj + j, 0)),
                  pl.BlockSpec((d, LANES), lambda b, j: (0, 0)),
                  pl.BlockSpec((1, LANES), lambda b, j: (0, 0))],
        out_specs=[pl.BlockSpec((ts, LANES), lambda b, j: (b * nj + j, 0)),
                   pl.BlockSpec((ts, d), lambda b, j: (b * nj + j, 0))],
        out_shape=[jax.ShapeDtypeStruct((batch * seq, LANES), F32),
                   jax.ShapeDtypeStruct((batch * seq, d), BF16)],
        scratch_shapes=[pltpu.VMEM((1, LANES), F32)],
        compiler_params=pltpu.CompilerParams(
            dimension_semantics=("parallel", "arbitrary"), vmem_limit_bytes=_vmem(40)),
        name="fox_gate",
    )(x2, wf, b_f)


FOX_HEADS_PER_STEP = 8


def _fox_attn_kernel(q_ref, k_ref, v_ref, c_ref, o_ref, vt_ref, crep_ref, *, tile, heads):
    hh = pl.program_id(1)
    i = pl.program_id(2)
    hd = FOX_HEAD_DIM
    nq = vt_ref.shape[1]

    @pl.when(i == 0)
    def _():
        lane = lax.broadcasted_iota(jnp.int32, (tile, LANES), 1)
        for j in range(nq):
            cj = c_ref[pl.ds(j * tile, tile), :]
            for g in range(heads):
                col = jnp.sum(jnp.where(lane == hh * heads + g, cj, 0.0), axis=1, keepdims=True)
                crep_ref[g, j] = jnp.broadcast_to(col, (tile, LANES))
                vj = v_ref[pl.ds(j * tile, tile), g * hd:(g + 1) * hd]
                vt_ref[g, j] = vj.astype(F32).T.astype(BF16)

    def scores(g, j):
        r0 = pl.multiple_of(j * tile, tile)
        kj = k_ref[pl.ds(r0, tile), g * hd:(g + 1) * hd]
        cj = jnp.concatenate([crep_ref[g, j]] * (tile // LANES), axis=1)
        return _dot_nt(kj, q_ref[:, g * hd:(g + 1) * hd]) - cj

    def update(g, j, carry, st):
        m, l, acc = carry
        m_new = jnp.maximum(m, jnp.max(st, axis=0, keepdims=True))
        a = jnp.exp2(m - m_new)
        p = jnp.exp2(st - m_new)
        l = a * l + jnp.sum(p, axis=0, keepdims=True)
        acc = a * acc + _dot(vt_ref[g, j], p.astype(BF16))
        return m_new, l, acc

    def body(j, carries):
        sts = [scores(g, j) for g in range(heads)]
        return tuple(update(g, j, carries[g], sts[g]) for g in range(heads))

    init = tuple((jnp.full((1, tile), NEG_BIG, F32), jnp.zeros((1, tile), F32),
                  jnp.zeros((hd, tile), F32)) for _ in range(heads))
    carries = lax.fori_loop(0, i, body, init)
    key = lax.broadcasted_iota(jnp.int32, (tile, tile), 0)
    qry = lax.broadcasted_iota(jnp.int32, (tile, tile), 1)
    sts = [jnp.where(key <= qry, scores(g, i), NEG_BIG) for g in range(heads)]
    for g in range(heads):
        _, l, acc = update(g, i, carries[g], sts[g])
        o_ref[:, g * hd:(g + 1) * hd] = (acc * (1.0 / l)).T.astype(o_ref.dtype)


def _fox_attention(h, c, batch, seq, tile):
    n = h.shape[0]
    hd = FOX_HEAD_DIM
    nh = FOX_HEADS
    gh = FOX_HEADS_PER_STEP
    ng = nh // gh
    nq = seq // tile
    kern = functools.partial(_fox_attn_kernel, tile=tile, heads=gh)
    return pl.pallas_call(
        kern,
        grid=(batch, ng, nq),
        in_specs=[pl.BlockSpec((tile, gh * hd), lambda b, hh, i: (b * nq + i, hh)),
                  pl.BlockSpec((seq, gh * hd), lambda b, hh, i: (b, ng + hh)),
                  pl.BlockSpec((seq, gh * hd), lambda b, hh, i: (b, 2 * ng + hh)),
                  pl.BlockSpec((seq, LANES), lambda b, hh, i: (b, 0))],
        out_specs=pl.BlockSpec((tile, gh * hd), lambda b, hh, i: (b * nq + i, hh)),
        out_shape=jax.ShapeDtypeStruct((n, nh * hd), BF16),
        scratch_shapes=[pltpu.VMEM((gh, nq, hd, tile), BF16),
                        pltpu.VMEM((gh, nq, tile, LANES), F32)],
        compiler_params=pltpu.CompilerParams(
            dimension_semantics=("parallel", "parallel", "arbitrary"), vmem_limit_bytes=_vmem(40)),
        name="fox_attention",
    )(h, h, h, c)


CONV_PAD = 32
CONV_ROWS = 256


def _conv_kernel(a_ref, g_ref, w_ref, cb_ref, ng_ref, nb_ref, o_ref, pad_ref):
    seq = a_ref.shape[0]
    pad_ref[pl.ds(0, CONV_PAD), :] = jnp.zeros((CONV_PAD, LANES), F32)
    pad_ref[pl.ds(CONV_PAD, seq), :] = a_ref[...].astype(F32) * _sigmoid(g_ref[...].astype(F32))
    off = CONV_PAD - (CONV_WIDTH - 1)
    for r in range(seq // CONV_ROWS):
        base = r * CONV_ROWS
        acc = jnp.zeros((CONV_ROWS, LANES), F32)
        for j in range(CONV_WIDTH):
            acc = acc + pad_ref[pl.ds(base + off + j, CONV_ROWS), :] * w_ref[pl.ds(j, 1), :]
        y = acc + cb_ref[...]
        mu = jnp.mean(y, axis=-1, keepdims=True)
        yc = y - mu
        var = jnp.mean(yc * yc, axis=-1, keepdims=True)
        yn = yc * lax.rsqrt(var + LN_EPS) * ng_ref[...] + nb_ref[...]
        o_ref[pl.ds(base, CONV_ROWS), :] = (yn * _sigmoid(yn)).astype(o_ref.dtype)


def _conv_module(h, a_col, g_col, conv_w, conv_b, cn_g, cn_b, batch, seq):
    n = h.shape[0]
    ch = conv_w.shape[1]
    ng = ch // CONV_GROUP
    assert seq % CONV_ROWS == 0
    vec = lambda: pl.BlockSpec((1, CONV_GROUP), lambda b, g: (0, g))
    return pl.pallas_call(
        _conv_kernel,
        grid=(batch, ng),
        in_specs=[pl.BlockSpec((seq, CONV_GROUP), lambda b, g: (b, a_col + g)),
                  pl.BlockSpec((seq, CONV_GROUP), lambda b, g: (b, g_col + g)),
                  pl.BlockSpec((CONV_WIDTH, CONV_GROUP), lambda b, g: (0, g)),
                  vec(), vec(), vec()],
        out_specs=pl.BlockSpec((seq, CONV_GROUP), lambda b, g: (b, g)),
        out_shape=jax.ShapeDtypeStruct((n, ch), BF16),
        scratch_shapes=[pltpu.VMEM((seq + CONV_PAD, LANES), F32)],
        compiler_params=pltpu.CompilerParams(
            dimension_semantics=("parallel", "parallel"), vmem_limit_bytes=_vmem(32)),
        name="conv_module",
    )(h, h, conv_w, conv_b, cn_g, cn_b)


def _layer_norm(y, g, b):
    mu = jnp.mean(y, axis=-1, keepdims=True)
    yc = y - mu
    var = jnp.mean(yc * yc, axis=-1, keepdims=True)
    return yc * lax.rsqrt(var + LN_EPS) * g + b


def _route_rows(xn, wr_hi, wr_lo, rbias, count_ref):
    x_hi, x_lo = _split_bf16(xn)
    logits = _dot_nt(wr_hi, x_hi) + _dot_nt(wr_hi, x_lo) + _dot_nt(wr_lo, x_hi)
    scores = _sigmoid(logits)
    sel = scores + rbias
    s = [sel[e:e + 1, :] for e in range(N_EXPERTS)]
    r = [scores[e:e + 1, :] for e in range(N_EXPERTS)]
    pg = EXPERTS_PER_GROUP

    def top2_sum(vals):
        best = None
        for a in range(len(vals)):
            for b in range(a + 1, len(vals)):
                t = vals[a] + vals[b]
                best = t if best is None else jnp.maximum(best, t)
        return best

    gs = [top2_sum(s[g * pg:(g + 1) * pg]) for g in range(N_GROUPS)]
    best, grp = gs[0], jnp.zeros_like(gs[0], dtype=jnp.int32)
    for g in range(1, N_GROUPS):
        upd = gs[g] > best
        best = jnp.where(upd, gs[g], best)
        grp = jnp.where(upd, g, grp)

    def pick_group(vals, k):
        out = vals[(N_GROUPS - 1) * pg + k]
        for g in range(N_GROUPS - 2, -1, -1):
            out = jnp.where(grp == g, vals[g * pg + k], out)
        return out

    v = [pick_group(s, k) for k in range(pg)]
    w = [pick_group(r, k) for k in range(pg)]
    b1, i1, w1 = v[0], jnp.zeros_like(grp), w[0]
    for k in range(1, pg):
        upd = v[k] > b1
        b1 = jnp.where(upd, v[k], b1)
        i1 = jnp.where(upd, k, i1)
        w1 = jnp.where(upd, w[k], w1)
    b2 = jnp.full_like(b1, -jnp.inf)
    i2, w2 = jnp.zeros_like(grp), jnp.zeros_like(w1)
    for k in range(pg):
        upd = jnp.logical_and(i1 != k, v[k] > b2)
        b2 = jnp.where(upd, v[k], b2)
        i2 = jnp.where(upd, k, i2)
        w2 = jnp.where(upd, w[k], w2)
    tot = w1 + w2
    e1 = grp * pg + i1
    e2 = grp * pg + i2
    eid = lax.broadcasted_iota(jnp.int32, scores.shape, 0)
    oh1 = jnp.where(eid == e1, 1.0, 0.0)
    oh2 = jnp.where(eid == e2, 1.0, 0.0)
    cnt = oh1 + oh2
    tm = cnt.shape[1]
    lane = lax.broadcasted_iota(jnp.int32, cnt.shape, 1)
    incl = cnt
    sh = 1
    while sh < tm:
        incl = incl + jnp.where(lane >= sh, pltpu.roll(incl, sh, axis=1), 0.0)
        sh *= 2
    before = incl - cnt + count_ref[...]
    count_ref[...] = count_ref[...] + incl[:, tm - 1:tm]
    rank1 = jnp.sum(oh1 * before, axis=0, keepdims=True)
    rank2 = jnp.sum(oh2 * before, axis=0, keepdims=True)
    rows = [e1.astype(F32), e2.astype(F32), w1 / tot, w2 / tot, rank1, rank2]
    rows += [jnp.zeros_like(w1)] * (8 - len(rows))
    return jnp.concatenate(rows, axis=0)


OUTPROJ_SUBTILES = 2


def _outproj_kernel(a1_ref, a2_ref, w_ref, x_ref, g_ref, b_ref, wrh_ref, wrl_ref, rb_ref,
                    xn_ref, rt_ref, rc_ref, cnt_ref, count_ref):
    @pl.when(pl.program_id(0) == 0)
    def _():
        count_ref[...] = jnp.zeros_like(count_ref)

    k1 = a1_ref.shape[1]
    k2 = a2_ref.shape[1]
    tm = x_ref.shape[0]
    sub = tm // OUTPROJ_SUBTILES
    rs = [pl.ds(t * sub, sub) for t in range(OUTPROJ_SUBTILES)]
    mix = [_dot(a1_ref[r, :], w_ref[pl.ds(0, k1), :]) + _dot(a2_ref[r, :], w_ref[pl.ds(k1, k2), :])
           for r in rs]
    for t, r in enumerate(rs):
        xn = _layer_norm(ALPHA * x_ref[r, :] + mix[t], g_ref[...], b_ref[...])
        xn_ref[r, :] = xn
        rows = _route_rows(xn, wrh_ref[...], wrl_ref[...], rb_ref[...], count_ref)
        rt_ref[:, r] = rows
        rc_ref[r, :] = rows.T
    cnt_ref[...] = jnp.broadcast_to(count_ref[...], cnt_ref.shape)


def _outproj_ln_route(a1, a1_col, a2, a2_col, kw, w, xres, ln_g, ln_b, wr_hi, wr_lo, rbias, tm):
    n, d = xres.shape
    full = lambda shape: pl.BlockSpec(shape, lambda i: (0, 0))
    return pl.pallas_call(
        _outproj_kernel,
        grid=(n // tm,),
        in_specs=[pl.BlockSpec((tm, kw), lambda i: (i, a1_col)),
                  pl.BlockSpec((tm, kw), lambda i: (i, a2_col)),
                  full(w.shape),
                  pl.BlockSpec((tm, d), lambda i: (i, 0)),
                  full((1, d)), full((1, d)),
                  full(wr_hi.shape), full(wr_lo.shape), full(rbias.shape)],
        out_specs=[pl.BlockSpec((tm, d), lambda i: (i, 0)),
                   pl.BlockSpec((8, tm), lambda i: (0, i)),
                   pl.BlockSpec((tm, 8), lambda i: (i, 0)),
                   pl.BlockSpec((N_EXPERTS, LANES), lambda i: (0, 0))],
        out_shape=[jax.ShapeDtypeStruct((n, d), F32),
                   jax.ShapeDtypeStruct((8, n), F32),
                   jax.ShapeDtypeStruct((n, 8), F32),
                   jax.ShapeDtypeStruct((N_EXPERTS, LANES), F32)],
        scratch_shapes=[pltpu.VMEM((N_EXPERTS, 1), F32)],
        compiler_params=pltpu.CompilerParams(
            dimension_semantics=("arbitrary",), vmem_limit_bytes=_vmem(52)),
        name="outproj_ln_route",
    )(a1, a2, w, xres, ln_g, ln_b, wr_hi, wr_lo, rbias)


EXPERT_CHUNK_ROWS = 128
EXPERT_STAGES = 8


TOK_WINDOW = 1024


def _expert_kernel(be_ref, na_ref, kb_ref, nk_ref, nx_ref, sl_ref, src_ref,
                   tok_hbm, x_hbm, wg_hbm, wu_hbm, wd_hbm, o_ref,
                   wgu_ref, wdn_ref, stg_ref, sem, xbuf_ref, tok_ref, gsem, tsem, *, layer):
    i = pl.program_id(0)
    tm, d = o_ref.shape
    de = wdn_ref.shape[1]
    na = na_ref[0]

    def tok_copy(j):
        base = pl.multiple_of((src_ref[j] // TOK_WINDOW) * TOK_WINDOW, TOK_WINDOW)
        dst = pl.multiple_of((j % 2) * (2 * TOK_WINDOW), 2 * TOK_WINDOW)
        return pltpu.make_async_copy(tok_hbm.at[pl.ds(base, 2 * TOK_WINDOW)],
                                     tok_ref.at[pl.ds(dst, 2 * TOK_WINDOW)], tsem.at[j % 2])

    def start_rows(j):
        offset = src_ref[j] % TOK_WINDOW
        for buf in range(2):
            @pl.when(j % 2 == buf)
            def _():
                first = buf * (2 * TOK_WINDOW) + offset
                for r in range(tm):
                    tok = tok_ref[first + r]
                    pltpu.make_async_copy(x_hbm.at[pl.ds(tok, 1), :],
                                          xbuf_ref.at[buf, pl.ds(r, 1), :], gsem.at[buf]).start()

    def wait_rows(j):
        pltpu.make_async_copy(x_hbm.at[pl.ds(0, tm), :], xbuf_ref.at[j % 2], gsem.at[j % 2]).wait()
    ch = EXPERT_CHUNK_ROWS
    n_in = d // ch
    n_dn = de // ch
    n_chunks = 2 * n_in + n_dn

    def chunk_copy(e, c):
        st = c % EXPERT_STAGES

        def gate_up(w_hbm, first, col0):
            r0 = pl.multiple_of((c - first) * ch, ch)
            cp = pltpu.make_async_copy(w_hbm.at[layer, e, pl.ds(r0, ch), :],
                                       stg_ref.at[st, :, pl.ds(0, de)], sem.at[st])

            def convert(slot):
                wgu_ref[slot, pl.ds(r0, ch), pl.ds(col0, de)] = stg_ref[st, :, pl.ds(0, de)].astype(BF16)
            return cp, convert

        def down():
            r0 = pl.multiple_of((c - 2 * n_in) * ch, ch)
            cp = pltpu.make_async_copy(wd_hbm.at[layer, e, pl.ds(r0, ch), :],
                                       stg_ref.at[st], sem.at[st])

            def convert(slot):
                wdn_ref[slot, pl.ds(r0, ch), :] = stg_ref[st].astype(BF16)
            return cp, convert

        return ((c < n_in, lambda: gate_up(wg_hbm, 0, 0)),
                (jnp.logical_and(c >= n_in, c < 2 * n_in), lambda: gate_up(wu_hbm, n_in, de)),
                (c >= 2 * n_in, down))

    def start_chunk(e, c):
        for cond, make in chunk_copy(e, c):
            @pl.when(cond)
            def _():
                make()[0].start()

    def finish_chunk(e, c, slot):
        for cond, make in chunk_copy(e, c):
            @pl.when(cond)
            def _():
                cp, convert = make()
                cp.wait()
                convert(slot)

    def prime(e):
        for c in range(EXPERT_STAGES):
            start_chunk(e, jnp.int32(c))

    def process(e, slot, c_lo, c_hi):
        def body(c, _):
            finish_chunk(e, c, slot)

            @pl.when(c + EXPERT_STAGES < n_chunks)
            def _():
                start_chunk(e, c + EXPERT_STAGES)
            return 0
        lax.fori_loop(c_lo, c_hi, body, 0)

    @pl.when(i >= na)
    def _():
        o_ref[...] = jnp.zeros_like(o_ref)

    @pl.when(i < na)
    def _():
        e = be_ref[i]
        kb = kb_ref[i]
        nk = nk_ref[i]
        nxt = nx_ref[i]
        slot = sl_ref[i]

        @pl.when(i == 0)
        def _():
            tok_copy(0).start()
            prime(e)
            tok_copy(0).wait()
            start_rows(0)
            tok_copy(1).start()
            process(e, slot, 0, n_chunks)

        @pl.when(jnp.logical_and(kb == 0, nxt >= 0))
        def _():
            prime(nxt)

        tok_copy(i + 1).wait()
        start_rows(i + 1)
        tok_copy(i + 2).start()
        wait_rows(i)
        gu = _dot(xbuf_ref[i % 2].astype(BF16), wgu_ref[slot])
        g = gu[:, :de]
        u = gu[:, de:]
        hdn = (g * _sigmoid(g)) * u
        o_ref[...] = _dot(hdn.astype(BF16), wdn_ref[slot])

        @pl.when(i == na - 1)
        def _():
            wait_rows(i + 1)
            tok_copy(i + 2).wait()

        @pl.when(nxt >= 0)
        def _():
            process(nxt, 1 - slot, (n_chunks * kb) // nk, (n_chunks * (kb + 1)) // nk)


def _expert_ffn(xn, sorted_tok, tables, w_gate, w_up, w_down, layer, tm, nb):
    d = xn.shape[1]
    de = w_down.shape[2]
    assert d % EXPERT_CHUNK_ROWS == 0 and de % EXPERT_CHUNK_ROWS == 0 and de % LANES == 0
    assert tm <= TOK_WINDOW
    hbm = lambda: pl.BlockSpec(memory_space=pl.ANY)
    return pl.pallas_call(
        functools.partial(_expert_kernel, layer=layer),
        grid_spec=pltpu.PrefetchScalarGridSpec(
            num_scalar_prefetch=len(tables),
            grid=(nb,),
            in_specs=[hbm(), hbm(), hbm(), hbm(), hbm()],
            out_specs=pl.BlockSpec((tm, d), lambda i, *_: (i, 0)),
            scratch_shapes=[pltpu.VMEM((2, d, 2 * de), BF16),
                            pltpu.VMEM((2, de, d), BF16),
                            pltpu.VMEM((EXPERT_STAGES, EXPERT_CHUNK_ROWS, d), F32),
                            pltpu.SemaphoreType.DMA((EXPERT_STAGES,)),
                            pltpu.VMEM((2, tm, d), F32),
                            pltpu.SMEM((2 * 2 * TOK_WINDOW,), jnp.int32),
                            pltpu.SemaphoreType.DMA((2,)),
                            pltpu.SemaphoreType.DMA((2,))]),
        out_shape=jax.ShapeDtypeStruct((nb * tm, d), F32),
        compiler_params=pltpu.CompilerParams(
            dimension_semantics=("arbitrary",), vmem_limit_bytes=_vmem(60)),
        name="expert_ffn",
    )(*tables, sorted_tok, xn, w_gate, w_up, w_down)


ROUTE_GATE_COL = 2


def _combine_kernel(pos_ref, posn_ref, x_ref, ys_hbm, rc_ref, g_ref, b_ref, xn_ref, *rest):
    xb_ref = rest[0] if len(rest) == 3 else None
    ybuf_ref, sem = rest[-2:]
    i = pl.program_id(0)
    nt = pl.num_programs(0)
    tm, d = x_ref.shape

    def start_rows(p_ref, buf):
        for r in range(tm):
            for k in range(TOP_K):
                pltpu.make_async_copy(ys_hbm.at[pl.ds(p_ref[0, 0, k * tm + r], 1), :],
                                      ybuf_ref.at[buf, k, pl.ds(r, 1), :], sem.at[buf]).start()

    @pl.when(i == 0)
    def _():
        start_rows(pos_ref, 0)

    for nxt_buf in range(2):
        @pl.when(jnp.logical_and(i + 1 < nt, (i + 1) % 2 == nxt_buf))
        def _():
            start_rows(posn_ref, nxt_buf)

    buf = i % 2
    for k in range(TOP_K):
        pltpu.make_async_copy(ys_hbm.at[pl.ds(0, tm), :], ybuf_ref.at[buf, k], sem.at[buf]).wait()
    g0 = rc_ref[:, ROUTE_GATE_COL:ROUTE_GATE_COL + 1]
    g1 = rc_ref[:, ROUTE_GATE_COL + 1:ROUTE_GATE_COL + 2]
    y = ALPHA * x_ref[...] + (g0 * ybuf_ref[buf, 0] + g1 * ybuf_ref[buf, 1])
    xn = _layer_norm(y, g_ref[...], b_ref[...])
    xn_ref[...] = xn
    if xb_ref is not None:
        xb_ref[...] = xn.astype(BF16)


POS_TILES_PER_STEP = 8


def _slot_pos_kernel(rt_ref, ps_ref, o_ref, *, tm):
    eid = lax.broadcasted_iota(jnp.int32, (N_EXPERTS, rt_ref.shape[1]), 0)
    for k in range(TOP_K):
        e = rt_ref[k:k + 1, :].astype(jnp.int32)
        base = jnp.sum(jnp.where(eid == e, ps_ref[...], 0.0), axis=0, keepdims=True)
        pos = (base + rt_ref[2 * TOP_K + k:2 * TOP_K + k + 1, :]).astype(jnp.int32)
        for t in range(o_ref.shape[0]):
            o_ref[t, :, pl.ds(k * tm, tm)] = pos[:, t * tm:(t + 1) * tm]


def _slot_positions(route, pstart, tm):
    n = route.shape[1]
    nt = n // tm
    step = POS_TILES_PER_STEP if nt % POS_TILES_PER_STEP == 0 else 1
    return pl.pallas_call(
        functools.partial(_slot_pos_kernel, tm=tm),
        grid=(nt // step,),
        in_specs=[pl.BlockSpec((8, step * tm), lambda i: (0, i)),
                  pl.BlockSpec((N_EXPERTS, 1), lambda i: (0, 0))],
        out_specs=pl.BlockSpec((step, 1, TOP_K * tm), lambda i: (i, 0, 0)),
        out_shape=jax.ShapeDtypeStruct((nt, 1, TOP_K * tm), jnp.int32),
        compiler_params=pltpu.CompilerParams(dimension_semantics=("parallel",)),
        name="slot_positions",
    )(route, pstart)


def _combine_ln(x, ys, pos_t, route_cols, ln_g, ln_b, tm, with_bf16):
    n, d = x.shape
    nt = n // tm
    n_out = 2 if with_bf16 else 1
    row = lambda: pl.BlockSpec((tm, d), lambda i: (i, 0))
    vec = lambda: pl.BlockSpec((1, d), lambda i: (0, 0))
    smem = lambda imap: pl.BlockSpec((1, 1, TOP_K * tm), imap, memory_space=pltpu.SMEM)
    return pl.pallas_call(
        _combine_kernel,
        grid=(nt,),
        in_specs=[smem(lambda i: (i, 0, 0)), smem(lambda i: (jnp.minimum(i + 1, nt - 1), 0, 0)),
                  row(), pl.BlockSpec(memory_space=pl.ANY),
                  pl.BlockSpec((tm, 8), lambda i: (i, 0)), vec(), vec()],
        out_specs=[row(), row()][:n_out],
        out_shape=[jax.ShapeDtypeStruct((n, d), F32), jax.ShapeDtypeStruct((n, d), BF16)][:n_out],
        scratch_shapes=[pltpu.VMEM((2, TOP_K, tm, d), F32), pltpu.SemaphoreType.DMA((2,))],
        compiler_params=pltpu.CompilerParams(
            dimension_semantics=("arbitrary",), vmem_limit_bytes=_vmem(48)),
        name="combine_ln",
    )(pos_t, pos_t, x, ys, route_cols, ln_g, ln_b)


def _gla_gate_kernel(x_ref, wl_ref, w2_ref, ba_ref, o_ref):
    low = _dot(x_ref[...], wl_ref[...]).astype(BF16)
    z = _dot(low, w2_ref[...]) + ba_ref[...]
    o_ref[...] = _log_sigmoid(z) * (1.0 / GLA_TAU)


def _gla_gate(xb, w_low, w_a2, b_a, tm):
    n, d = xb.shape
    kw = w_a2.shape[1]
    return pl.pallas_call(
        _gla_gate_kernel,
        grid=(n // tm,),
        in_specs=[pl.BlockSpec((tm, d), lambda i: (i, 0)),
                  pl.BlockSpec(w_low.shape, lambda i: (0, 0)),
                  pl.BlockSpec(w_a2.shape, lambda i: (0, 0)),
                  pl.BlockSpec((1, kw), lambda i: (0, 0))],
        out_specs=pl.BlockSpec((tm, kw), lambda i: (i, 0)),
        out_shape=jax.ShapeDtypeStruct((n, kw), F32),
        compiler_params=pltpu.CompilerParams(
            dimension_semantics=("parallel",), vmem_limit_bytes=_vmem(32)),
        name="gla_gate",
    )(xb, w_low, w_a2, b_a)


GLA_ROWS = 512


def _gla_kernel(q_ref, k_ref, v_ref, la_ref, g_ref, ng_ref, o_ref, state_ref, *, scale):
    rows, kw = q_ref.shape
    nh = state_ref.shape[0]
    hk, hv = state_ref.shape[1], state_ref.shape[2]
    ck = 2 * GLA_CHUNK
    mid = GLA_CHUNK - 1

    @pl.when(pl.program_id(1) == 0)
    def _():
        state_ref[...] = jnp.zeros_like(state_ref)

    row = lax.broadcasted_iota(jnp.int32, (ck, ck), 0)
    col = lax.broadcasted_iota(jnp.int32, (ck, ck), 1)
    causal = row >= col
    tri = jnp.where(causal, 1.0, 0.0).astype(BF16)
    ones = jnp.ones((ck, LANES), BF16)

    def chunk(c, _):
        r0 = pl.multiple_of(c * ck, ck)
        heads = range(nh)
        ks = [slice(hh * hk, (hh + 1) * hk) for hh in heads]
        vs = [slice(hh * hv, (hh + 1) * hv) for hh in heads]
        la = [_split_bf16(la_ref[pl.ds(r0, ck), ks[hh]]) for hh in heads]
        b = [_dot(tri, la[hh][0]) + _dot(tri, la[hh][1]) for hh in heads]
        dsum = [_dot_tn(la[hh][0], ones) + _dot_tn(la[hh][1], ones) for hh in heads]
        qt, kt, kend, qin = [], [], [], []
        for hh in heads:
            q = q_ref[pl.ds(r0, ck), ks[hh]].astype(F32) * scale
            k = k_ref[pl.ds(r0, ck), ks[hh]].astype(F32)
            b_mid = b[hh][mid:mid + 1, :]
            b_last = b[hh][ck - 1:ck, :]
            qt.append((q * jnp.exp(b[hh] - b_mid)).astype(BF16))
            kt.append((k * jnp.exp(b_mid - b[hh])).astype(BF16))
            kend.append((k * jnp.exp(b_last - b[hh])).astype(BF16))
            qin.append((q * jnp.exp(b[hh])).astype(BF16))
        v = [v_ref[pl.ds(r0, ck), vs[hh]] for hh in heads]
        state = [state_ref[hh] for hh in heads]
        attn = [_dot_nt(qt[hh], kt[hh]) for hh in heads]
        inter = [_dot(qin[hh], state[hh].astype(BF16)) for hh in heads]
        kv = [_dot_tn(kend[hh], v[hh]) for hh in heads]
        for hh in heads:
            decay = jnp.concatenate([jnp.exp(dsum[hh])] * (hv // LANES), axis=1)
            state_ref[hh] = state[hh] * decay + kv[hh]
        for hh in heads:
            o = _dot(jnp.where(causal, attn[hh], 0.0).astype(BF16), v[hh]) + inter[hh]
            o = o * lax.rsqrt(jnp.mean(o * o, axis=-1, keepdims=True) + LN_EPS)
            gate = g_ref[pl.ds(r0, ck), vs[hh]].astype(F32)
            o = o * ng_ref[:, vs[hh]] * (gate * _sigmoid(gate))
            o_ref[pl.ds(r0, ck), vs[hh]] = o.astype(o_ref.dtype)
        return 0

    lax.fori_loop(0, rows // ck, chunk, 0, unroll=2)


def _gla(h, la, norm_g, batch, seq):
    n = h.shape[0]
    kw = la.shape[1]
    vw = norm_g.shape[1]
    nh = GLA_HEADS
    hk, hv = kw // nh, vw // nh
    rows = GLA_ROWS
    nj = seq // rows
    kern = functools.partial(_gla_kernel, scale=hk ** -0.5)
    rowblk = lambda b, j: b * nj + j
    return pl.pallas_call(
        kern,
        grid=(batch, nj),
        in_specs=[pl.BlockSpec((rows, kw), lambda b, j: (rowblk(b, j), 0)),
                  pl.BlockSpec((rows, kw), lambda b, j: (rowblk(b, j), 1)),
                  pl.BlockSpec((rows, vw), lambda b, j: (rowblk(b, j), 2 * kw // vw)),
                  pl.BlockSpec((rows, kw), lambda b, j: (rowblk(b, j), 0)),
                  pl.BlockSpec((rows, vw), lambda b, j: (rowblk(b, j), 2 * kw // vw + 1)),
                  pl.BlockSpec((1, vw), lambda b, j: (0, 0))],
        out_specs=pl.BlockSpec((rows, vw), lambda b, j: (rowblk(b, j), 0)),
        out_shape=jax.ShapeDtypeStruct((n, vw), BF16),
        scratch_shapes=[pltpu.VMEM((nh, hk, hv), F32)],
        compiler_params=pltpu.CompilerParams(
            dimension_semantics=("parallel", "arbitrary"), vmem_limit_bytes=_vmem(48)),
        name="gla",
    )(h, h, h, la, h, norm_g)


EXPERT_TM = 256
COMBINE_TM = 256


def _tables_kernel(cnt_ref, be_ref, na_ref, kb_ref, nk_ref, nx_ref, sl_ref, src_ref, ps_ref, *, tm):
    n_tab = be_ref.shape[0]
    blocks_of = lambda e: (cnt_ref[e] + tm - 1) // tm
    run = jnp.int32(0)
    first = jnp.int32(0)
    order = jnp.int32(0)
    for e in range(N_EXPERTS):
        nbe = blocks_of(e)
        ps_ref[e] = run * tm

        def fill(k, _, e=e, run=run, first=first, order=order, nbe=nbe):
            i = run + k
            be_ref[i] = e
            kb_ref[i] = k
            nk_ref[i] = nbe
            sl_ref[i] = order % 2
            src_ref[i] = first + k * tm
            return 0
        lax.fori_loop(0, nbe, fill, 0)
        run = run + nbe
        first = first + cnt_ref[e]
        order = order + jnp.where(nbe > 0, 1, 0)
    na_ref[0] = run

    nxt = jnp.int32(-1)
    end = run
    for e in reversed(range(N_EXPERTS)):
        nbe = blocks_of(e)

        def fill_next(i, _, nxt=nxt):
            nx_ref[i] = nxt
            return 0
        lax.fori_loop(end - nbe, end, fill_next, 0)
        nxt = jnp.where(nbe > 0, e, nxt)
        end = end - nbe

    last = run - 1

    def pad(i, _):
        be_ref[i] = be_ref[last]
        kb_ref[i] = kb_ref[last]
        nk_ref[i] = nk_ref[last]
        nx_ref[i] = nx_ref[last]
        sl_ref[i] = sl_ref[last]
        src_ref[i] = 0
        return 0
    lax.fori_loop(run, n_tab, pad, 0)


def _block_tables(counts, tm, n_tab):
    i32 = jnp.int32
    smem = lambda: pl.BlockSpec(memory_space=pltpu.SMEM)
    tab = jax.ShapeDtypeStruct((n_tab,), i32)
    outs = pl.pallas_call(
        functools.partial(_tables_kernel, tm=tm),
        in_specs=[smem()],
        out_specs=[smem()] * 8,
        out_shape=[tab, jax.ShapeDtypeStruct((1,), i32), tab, tab, tab, tab, tab,
                   jax.ShapeDtypeStruct((N_EXPERTS,), i32)],
        name="block_tables",
    )(counts)
    return tuple(outs[:7]), outs[7]


def _moe(xn, route, route_cols, counts, w_gate, w_up, w_down, layer, ln_g, ln_b, with_bf16):
    n, d = xn.shape
    a = n * TOP_K
    tm = EXPERT_TM
    nb = a // tm + N_EXPERTS
    i32 = jnp.int32
    eidx = route[:TOP_K].astype(i32)
    assign = jnp.arange(n, dtype=i32)[None, :] * TOP_K + jnp.arange(TOP_K, dtype=i32)[:, None]
    sorted_tok = (lax.sort((eidx * a + assign).reshape(a)) % a) // TOP_K
    sorted_tok = jnp.concatenate([sorted_tok, jnp.zeros((2 * TOK_WINDOW,), i32)])
    tables, pstart = _block_tables(counts[:, 0].astype(i32), tm, nb + 2)
    ys = _expert_ffn(xn, sorted_tok, tables, w_gate, w_up, w_down, layer, tm, nb)
    pos = _slot_positions(route, pstart.reshape(N_EXPERTS, 1).astype(F32), COMBINE_TM)
    return _combine_ln(xn, ys, pos, route_cols, ln_g, ln_b, COMBINE_TM, with_bf16)


def kernel(x, even_w_in, even_b_f, even_conv_w, even_conv_b, even_conv_norm_g, even_conv_norm_b, even_w_out, odd_w_in, odd_w_a2, odd_b_a, odd_norm_g, odd_w_out, ln_mix_g, ln_mix_b, ln_ffn_g, ln_ffn_b, router_w, router_bias, expert_w_gate, expert_w_up, expert_w_down):
    batch, seq, d = x.shape
    n = batch * seq
    x2 = x.reshape(n, d)
    fw = FOX_HEADS * FOX_HEAD_DIM
    conv_ch = even_conv_w.shape[-1]
    kw = odd_w_a2.shape[-1]
    vw = odd_norm_g.shape[-1]
    row = lambda t: t.reshape(1, -1)

    wr_hi, wr_lo = _split_bf16(router_w.T)
    rbias = router_bias.reshape(N_EXPERTS, 1).astype(F32)
    experts_w = (expert_w_gate, expert_w_up, expert_w_down)

    w_t = jnp.swapaxes(even_w_in, 1, 2)
    q_scale = LOG2E * FOX_HEAD_DIM ** -0.5
    wf = jnp.zeros((d, LANES), F32).at[:, :FOX_HEADS].set(even_w_in[0, :, 3 * fw:3 * fw + FOX_HEADS])
    b_f = jnp.zeros((1, LANES), F32).at[0, :FOX_HEADS].set(even_b_f[0])
    c, xb = _fox_gate(x2, wf.astype(BF16), b_f, batch, seq, 512)
    h = _matmul(xb, w_t, 3 * fw, BF16, 1024, fw, first_scale=q_scale)
    h_glu = _matmul(xb, w_t[:, 3 * fw + FOX_HEADS:], 2 * conv_ch, BF16, 1024, conv_ch)
    att = _fox_attention(h, c, batch, seq, 256)
    u = _conv_module(h_glu, 0, conv_ch // CONV_GROUP, even_conv_w[0, :, 0, :],
                     row(even_conv_b[0]), row(even_conv_norm_g[0]), row(even_conv_norm_b[0]),
                     batch, seq)
    xn, route, rcols, cnt = _outproj_ln_route(att, 0, u, 0, fw, even_w_out[0].astype(BF16), x2,
                                              row(ln_mix_g[0]), row(ln_mix_b[0]),
                                              wr_hi, wr_lo, rbias, 512)
    xn, xb = _moe(xn, route, rcols, cnt, *experts_w, 0, row(ln_ffn_g[0]), row(ln_ffn_b[0]), True)

    w_t = jnp.swapaxes(odd_w_in, 1, 2)
    h = _matmul(xb, w_t, 2 * kw + 2 * vw, BF16, 1024, 1024)
    w_low = jnp.zeros((d, LANES), F32).at[:, :GLA_LOW_RANK].set(odd_w_in[0, :, 2 * kw + 2 * vw:])
    w_low = w_low.astype(BF16)
    w_a2 = jnp.zeros((LANES, kw), F32).at[:GLA_LOW_RANK].set(odd_w_a2[0]).astype(BF16)
    la = _gla_gate(xb, w_low, w_a2, row(odd_b_a[0]), 1024)
    o = _gla(h, la, row(odd_norm_g[0]), batch, seq)
    half = vw // 2
    xn, route, rcols, cnt = _outproj_ln_route(o, 0, o, 1, half, odd_w_out[0].astype(BF16), xn,
                                              row(ln_mix_g[1]), row(ln_mix_b[1]),
                                              wr_hi, wr_lo, rbias, 512)
    (xn,) = _moe(xn, route, rcols, cnt, *experts_w, 1, row(ln_ffn_g[1]), row(ln_ffn_b[1]), False)
    return xn.reshape(batch, seq, d)
```

```python
import functools

import jax
import jax.numpy as jnp
from jax import lax
from jax.experimental import pallas as pl
from jax.experimental.pallas import tpu as pltpu

F32 = jnp.float32
BF16 = jnp.bfloat16

DEPTH = 2
ALPHA = (2 * DEPTH) ** 0.25
LN_EPS = 1e-5
FOX_HEADS = 8
FOX_HEAD_DIM = 128
CONV_WIDTH = 31
CONV_GROUP = 128
GLA_HEADS = 4
GLA_LOW_RANK = 16
GLA_TAU = 16.0
GLA_CHUNK = 64
N_EXPERTS = 16
N_GROUPS = 4
EXPERTS_PER_GROUP = N_EXPERTS // N_GROUPS
TOP_K = 2

LANES = 128
V7X_VMEM_BYTES = 64 * 1024 * 1024
NEG_BIG = -1e30
LOG2E = 1.4426950408889634


def _vmem(mib):
    assert mib * 1024 * 1024 < V7X_VMEM_BYTES
    return mib * 1024 * 1024


def _sigmoid(z):
    return 1.0 / (1.0 + jnp.exp(-z))


def _log_sigmoid(z):
    return jnp.minimum(z, 0.0) - jnp.log1p(jnp.exp(-jnp.abs(z)))


def _dot(a, b):
    return jnp.dot(a, b, preferred_element_type=F32)


def _dot_nt(a, b):
    return lax.dot_general(a, b, (((1,), (1,)), ((), ())), preferred_element_type=F32)


def _dot_tn(a, b):
    return lax.dot_general(a, b, (((0,), (0,)), ((), ())), preferred_element_type=F32)


def _split_bf16(v):
    hi = v.astype(BF16)
    lo = (v - hi.astype(F32)).astype(BF16)
    return hi, lo


def _mm_kernel(x_ref, w_ref, o_ref, wb_ref, *, first_scale):
    @pl.when(pl.program_id(1) == 0)
    def _():
        wb_ref[...] = w_ref[...].T.astype(BF16)

    acc = _dot(x_ref[...].astype(BF16), wb_ref[...])
    if first_scale is not None:
        acc = acc * jnp.where(pl.program_id(0) == 0, first_scale, 1.0)
    o_ref[...] = acc.astype(o_ref.dtype)


def _matmul(x, w_t, n_cols, out_dtype, tm, tn, first_scale=None):
    m, k = x.shape
    assert m % tm == 0 and n_cols % tn == 0 and n_cols <= w_t.shape[1] and k == w_t.shape[2]
    return pl.pallas_call(
        functools.partial(_mm_kernel, first_scale=first_scale),
        grid=(n_cols // tn, m // tm),
        in_specs=[pl.BlockSpec((tm, k), lambda j, i: (i, 0)),
                  pl.BlockSpec((None, tn, k), lambda j, i: (0, j, 0))],
        out_specs=pl.BlockSpec((tm, tn), lambda j, i: (i, j)),
        out_shape=jax.ShapeDtypeStruct((m, n_cols), out_dtype),
        scratch_shapes=[pltpu.VMEM((k, tn), BF16)],
        compiler_params=pltpu.CompilerParams(
            dimension_semantics=("parallel", "arbitrary"), vmem_limit_bytes=_vmem(52)),
        name="dense_proj",
    )(x, w_t)


def _fox_gate_kernel(x_ref, wf_ref, bf_ref, c_ref, xb_ref, carry_ref):
    j = pl.program_id(1)

    @pl.when(j == 0)
    def _():
        carry_ref[...] = jnp.zeros_like(carry_ref)

    xb = x_ref[...].astype(BF16)
    xb_ref[...] = xb
    z = _dot(xb, wf_ref[...]) + bf_ref[...]
    lf = _log_sigmoid(z)
    ts = lf.shape[0]
    row = lax.broadcasted_iota(jnp.int32, lf.shape, 0)
    sh = 1
    while sh < ts:
        lf = lf + jnp.where(row >= sh, pltpu.roll(lf, sh, axis=0), 0.0)
        sh *= 2
    c = lf + carry_ref[...]
    c_ref[...] = c * LOG2E
    carry_ref[...] = c[ts - 1:ts, :]


def _fox_gate(x2, wf, b_f, batch, seq, ts):
    d = x2.shape[1]
    nj = seq // ts
    return pl.pallas_call(
        _fox_gate_kernel,
        grid=(batch, nj),
        in_specs=[pl.BlockSpec((ts, d), lambda b, j: (b * nj + j, 0)),
                  pl.BlockSpec((d, LANES), lambda b, j: (0, 0)),
                  pl.BlockSpec((1, LANES), lambda b, j: (0, 0))],
        out_specs=[pl.BlockSpec((ts, LANES), lambda b, j: (b * nj + j, 0)),
                   pl.BlockSpec((ts, d), lambda b, j: (b * nj + j, 0))],
        out_shape=[jax.ShapeDtypeStruct((batch * seq, LANES), F32),
                   jax.ShapeDtypeStruct((batch * seq, d), BF16)],
        scratch_shapes=[pltpu.VMEM((1, LANES), F32)],
        compiler_params=pltpu.CompilerParams(
            dimension_semantics=("parallel", "arbitrary"), vmem_limit_bytes=_vmem(40)),
        name="fox_gate",
    )(x2, wf, b_f)


FOX_HEADS_PER_STEP = 8


def _fox_attn_kernel(q_ref, k_ref, v_ref, c_ref, o_ref, vt_ref, crep_ref, *, tile, heads):
    hh = pl.program_id(1)
    i = pl.program_id(2)
    hd = FOX_HEAD_DIM
    nq = vt_ref.shape[1]

    @pl.when(i == 0)
    def _():
        lane = lax.broadcasted_iota(jnp.int32, (tile, LANES), 1)
        for j in range(nq):
            cj = c_ref[pl.ds(j * tile, tile), :]
            for g in range(heads):
                col = jnp.sum(jnp.where(lane == hh * heads + g, cj, 0.0), axis=1, keepdims=True)
                crep_ref[g, j] = jnp.broadcast_to(col, (tile, LANES))
                vj = v_ref[pl.ds(j * tile, tile), g * hd:(g + 1) * hd]
                vt_ref[g, j] = vj.astype(F32).T.astype(BF16)

    def scores(g, j):
        r0 = pl.multiple_of(j * tile, tile)
        kj = k_ref[pl.ds(r0, tile), g * hd:(g + 1) * hd]
        cj = jnp.concatenate([crep_ref[g, j]] * (tile // LANES), axis=1)
        return _dot_nt(kj, q_ref[:, g * hd:(g + 1) * hd]) - cj

    def update(g, j, carry, st):
        m, l, acc = carry
        m_new = jnp.maximum(m, jnp.max(st, axis=0, keepdims=True))
        a = jnp.exp2(m - m_new)
        p = jnp.exp2(st - m_new)
        l = a * l + jnp.sum(p, axis=0, keepdims=True)
        acc = a * acc + _dot(vt_ref[g, j], p.astype(BF16))
        return m_new, l, acc

    def body(j, carries):
        sts = [scores(g, j) for g in range(heads)]
        return tuple(update(g, j, carries[g], sts[g]) for g in range(heads))

    init = tuple((jnp.full((1, tile), NEG_BIG, F32), jnp.zeros((1, tile), F32),
                  jnp.zeros((hd, tile), F32)) for _ in range(heads))
    carries = lax.fori_loop(0, i, body, init)
    key = lax.broadcasted_iota(jnp.int32, (tile, tile), 0)
    qry = lax.broadcasted_iota(jnp.int32, (tile, tile), 1)
    sts = [jnp.where(key <= qry, scores(g, i), NEG_BIG) for g in range(heads)]
    for g in range(heads):
        _, l, acc = update(g, i, carries[g], sts[g])
        o_ref[:, g * hd:(g + 1) * hd] = (acc * (1.0 / l)).T.astype(o_ref.dtype)


def _fox_attention(h, c, batch, seq, tile):
    n = h.shape[0]
    hd = FOX_HEAD_DIM
    nh = FOX_HEADS
    gh = FOX_HEADS_PER_STEP
    ng = nh // gh
    nq = seq // tile
    kern = functools.partial(_fox_attn_kernel, tile=tile, heads=gh)
    return pl.pallas_call(
        kern,
        grid=(batch, ng, nq),
        in_specs=[pl.BlockSpec((tile, gh * hd), lambda b, hh, i: (b * nq + i, hh)),
                  pl.BlockSpec((seq, gh * hd), lambda b, hh, i: (b, ng + hh)),
                  pl.BlockSpec((seq, gh * hd), lambda b, hh, i: (b, 2 * ng + hh)),
                  pl.BlockSpec((seq, LANES), lambda b, hh, i: (b, 0))],
        out_specs=pl.BlockSpec((tile, gh * hd), lambda b, hh, i: (b * nq + i, hh)),
        out_shape=jax.ShapeDtypeStruct((n, nh * hd), BF16),
        scratch_shapes=[pltpu.VMEM((gh, nq, hd, tile), BF16),
                        pltpu.VMEM((gh, nq, tile, LANES), F32)],
        compiler_params=pltpu.CompilerParams(
            dimension_semantics=("parallel", "parallel", "arbitrary"), vmem_limit_bytes=_vmem(40)),
        name="fox_attention",
    )(h, h, h, c)


CONV_PAD = 32
CONV_ROWS = 256


def _conv_kernel(a_ref, g_ref, w_ref, cb_ref, ng_ref, nb_ref, o_ref, pad_ref):
    seq = a_ref.shape[0]
    pad_ref[pl.ds(0, CONV_PAD), :] = jnp.zeros((CONV_PAD, LANES), F32)
    pad_ref[pl.ds(CONV_PAD, seq), :] = a_ref[...].astype(F32) * _sigmoid(g_ref[...].astype(F32))
    off = CONV_PAD - (CONV_WIDTH - 1)
    for r in range(seq // CONV_ROWS):
        base = r * CONV_ROWS
        acc = jnp.zeros((CONV_ROWS, LANES), F32)
        for j in range(CONV_WIDTH):
            acc = acc + pad_ref[pl.ds(base + off + j, CONV_ROWS), :] * w_ref[pl.ds(j, 1), :]
        y = acc + cb_ref[...]
        mu = jnp.mean(y, axis=-1, keepdims=True)
        yc = y - mu
        var = jnp.mean(yc * yc, axis=-1, keepdims=True)
        yn = yc * lax.rsqrt(var + LN_EPS) * ng_ref[...] + nb_ref[...]
        o_ref[pl.ds(base, CONV_ROWS), :] = (yn * _sigmoid(yn)).astype(o_ref.dtype)


def _conv_module(h, a_col, g_col, conv_w, conv_b, cn_g, cn_b, batch, seq):
    n = h.shape[0]
    ch = conv_w.shape[1]
    ng = ch // CONV_GROUP
    assert seq % CONV_ROWS == 0
    vec = lambda: pl.BlockSpec((1, CONV_GROUP), lambda b, g: (0, g))
    return pl.pallas_call(
        _conv_kernel,
        grid=(batch, ng),
        in_specs=[pl.BlockSpec((seq, CONV_GROUP), lambda b, g: (b, a_col + g)),
                  pl.BlockSpec((seq, CONV_GROUP), lambda b, g: (b, g_col + g)),
                  pl.BlockSpec((CONV_WIDTH, CONV_GROUP), lambda b, g: (0, g)),
                  vec(), vec(), vec()],
        out_specs=pl.BlockSpec((seq, CONV_GROUP), lambda b, g: (b, g)),
        out_shape=jax.ShapeDtypeStruct((n, ch), BF16),
        scratch_shapes=[pltpu.VMEM((seq + CONV_PAD, LANES), F32)],
        compiler_params=pltpu.CompilerParams(
            dimension_semantics=("parallel", "parallel"), vmem_limit_bytes=_vmem(32)),
        name="conv_module",
    )(h, h, conv_w, conv_b, cn_g, cn_b)


def _layer_norm(y, g, b):
    mu = jnp.mean(y, axis=-1, keepdims=True)
    yc = y - mu
    var = jnp.mean(yc * yc, axis=-1, keepdims=True)
    return yc * lax.rsqrt(var + LN_EPS) * g + b


def _route_rows(xn, wr_hi, wr_lo, rbias, count_ref):
    x_hi, x_lo = _split_bf16(xn)
    logits = _dot_nt(wr_hi, x_hi) + _dot_nt(wr_hi, x_lo) + _dot_nt(wr_lo, x_hi)
    scores = _sigmoid(logits)
    sel = scores + rbias
    s = [sel[e:e + 1, :] for e in range(N_EXPERTS)]
    r = [scores[e:e + 1, :] for e in range(N_EXPERTS)]
    pg = EXPERTS_PER_GROUP

    def top2_sum(vals):
        best = None
        for a in range(len(vals)):
            for b in range(a + 1, len(vals)):
                t = vals[a] + vals[b]
                best = t if best is None else jnp.maximum(best, t)
        return best

    gs = [top2_sum(s[g * pg:(g + 1) * pg]) for g in range(N_GROUPS)]
    best, grp = gs[0], jnp.zeros_like(gs[0], dtype=jnp.int32)
    for g in range(1, N_GROUPS):
        upd = gs[g] > best
        best = jnp.where(upd, gs[g], best)
        grp = jnp.where(upd, g, grp)

    def pick_group(vals, k):
        out = vals[(N_GROUPS - 1) * pg + k]
        for g in range(N_GROUPS - 2, -1, -1):
            out = jnp.where(grp == g, vals[g * pg + k], out)
        return out

    v = [pick_group(s, k) for k in range(pg)]
    w = [pick_group(r, k) for k in range(pg)]
    b1, i1, w1 = v[0], jnp.zeros_like(grp), w[0]
    for k in range(1, pg):
        upd = v[k] > b1
        b1 = jnp.where(upd, v[k], b1)
        i1 = jnp.where(upd, k, i1)
        w1 = jnp.where(upd, w[k], w1)
    b2 = jnp.full_like(b1, -jnp.inf)
    i2, w2 = jnp.zeros_like(grp), jnp.zeros_like(w1)
    for k in range(pg):
        upd = jnp.logical_and(i1 != k, v[k] > b2)
        b2 = jnp.where(upd, v[k], b2)
        i2 = jnp.where(upd, k, i2)
        w2 = jnp.where(upd, w[k], w2)
    tot = w1 + w2
    e1 = grp * pg + i1
    e2 = grp * pg + i2
    eid = lax.broadcasted_iota(jnp.int32, scores.shape, 0)
    oh1 = jnp.where(eid == e1, 1.0, 0.0)
    oh2 = jnp.where(eid == e2, 1.0, 0.0)
    cnt = oh1 + oh2
    tm = cnt.shape[1]
    lane = lax.broadcasted_iota(jnp.int32, cnt.shape, 1)
    incl = cnt
    sh = 1
    while sh < tm:
        incl = incl + jnp.where(lane >= sh, pltpu.roll(incl, sh, axis=1), 0.0)
        sh *= 2
    before = incl - cnt + count_ref[...]
    count_ref[...] = count_ref[...] + incl[:, tm - 1:tm]
    rank1 = jnp.sum(oh1 * before, axis=0, keepdims=True)
    rank2 = jnp.sum(oh2 * before, axis=0, keepdims=True)
    rows = [e1.astype(F32), e2.astype(F32), w1 / tot, w2 / tot, rank1, rank2]
    rows += [jnp.zeros_like(w1)] * (8 - len(rows))
    return jnp.concatenate(rows, axis=0)


OUTPROJ_SUBTILES = 2


def _outproj_kernel(a1_ref, a2_ref, w_ref, x_ref, g_ref, b_ref, wrh_ref, wrl_ref, rb_ref,
                    xn_ref, rt_ref, rc_ref, cnt_ref, count_ref):
    @pl.when(pl.program_id(0) == 0)
    def _():
        count_ref[...] = jnp.zeros_like(count_ref)

    k1 = a1_ref.shape[1]
    k2 = a2_ref.shape[1]
    tm = x_ref.shape[0]
    sub = tm // OUTPROJ_SUBTILES
    rs = [pl.ds(t * sub, sub) for t in range(OUTPROJ_SUBTILES)]
    mix = [_dot(a1_ref[r, :], w_ref[pl.ds(0, k1), :]) + _dot(a2_ref[r, :], w_ref[pl.ds(k1, k2), :])
           for r in rs]
    for t, r in enumerate(rs):
        xn = _layer_norm(ALPHA * x_ref[r, :] + mix[t], g_ref[...], b_ref[...])
        xn_ref[r, :] = xn
        rows = _route_rows(xn, wrh_ref[...], wrl_ref[...], rb_ref[...], count_ref)
        rt_ref[:, r] = rows
        rc_ref[r, :] = rows.T
    cnt_ref[...] = jnp.broadcast_to(count_ref[...], cnt_ref.shape)


def _outproj_ln_route(a1, a1_col, a2, a2_col, kw, w, xres, ln_g, ln_b, wr_hi, wr_lo, rbias, tm):
    n, d = xres.shape
    full = lambda shape: pl.BlockSpec(shape, lambda i: (0, 0))
    return pl.pallas_call(
        _outproj_kernel,
        grid=(n // tm,),
        in_specs=[pl.BlockSpec((tm, kw), lambda i: (i, a1_col)),
                  pl.BlockSpec((tm, kw), lambda i: (i, a2_col)),
                  full(w.shape),
                  pl.BlockSpec((tm, d), lambda i: (i, 0)),
                  full((1, d)), full((1, d)),
                  full(wr_hi.shape), full(wr_lo.shape), full(rbias.shape)],
        out_specs=[pl.BlockSpec((tm, d), lambda i: (i, 0)),
                   pl.BlockSpec((8, tm), lambda i: (0, i)),
                   pl.BlockSpec((tm, 8), lambda i: (i, 0)),
                   pl.BlockSpec((N_EXPERTS, LANES), lambda i: (0, 0))],
        out_shape=[jax.ShapeDtypeStruct((n, d), F32),
                   jax.ShapeDtypeStruct((8, n), F32),
                   jax.ShapeDtypeStruct((n, 8), F32),
                   jax.ShapeDtypeStruct((N_EXPERTS, LANES), F32)],
        scratch_shapes=[pltpu.VMEM((N_EXPERTS, 1), F32)],
        compiler_params=pltpu.CompilerParams(
            dimension_semantics=("arbitrary",), vmem_limit_bytes=_vmem(52)),
        name="outproj_ln_route",
    )(a1, a2, w, xres, ln_g, ln_b, wr_hi, wr_lo, rbias)


EXPERT_CHUNK_ROWS = 128
EXPERT_STAGES = 12


TOK_WINDOW = 1024


def _expert_kernel(be_ref, na_ref, kb_ref, nk_ref, nx_ref, sl_ref, src_ref,
                   tok_hbm, x_hbm, wg_hbm, wu_hbm, wd_hbm, o_ref,
                   wgu_ref, wdn_ref, stg_ref, sem, xbuf_ref, tok_ref, gsem, tsem, *, layer):
    i = pl.program_id(0)
    tm, d = o_ref.shape
    de = wdn_ref.shape[1]
    na = na_ref[0]

    def tok_copy(j):
        base = pl.multiple_of((src_ref[j] // TOK_WINDOW) * TOK_WINDOW, TOK_WINDOW)
        dst = pl.multiple_of((j % 2) * (2 * TOK_WINDOW), 2 * TOK_WINDOW)
        return pltpu.make_async_copy(tok_hbm.at[pl.ds(base, 2 * TOK_WINDOW)],
                                     tok_ref.at[pl.ds(dst, 2 * TOK_WINDOW)], tsem.at[j % 2])

    def start_rows(j):
        offset = src_ref[j] % TOK_WINDOW
        for buf in range(2):
            @pl.when(j % 2 == buf)
            def _():
                first = buf * (2 * TOK_WINDOW) + offset
                for r in range(tm):
                    tok = tok_ref[first + r]
                    pltpu.make_async_copy(x_hbm.at[pl.ds(tok, 1), :],
                                          xbuf_ref.at[buf, pl.ds(r, 1), :], gsem.at[buf]).start()

    def wait_rows(j):
        pltpu.make_async_copy(x_hbm.at[pl.ds(0, tm), :], xbuf_ref.at[j % 2], gsem.at[j % 2]).wait()
    ch = EXPERT_CHUNK_ROWS
    n_in = d // ch
    n_dn = de // ch
    n_chunks = 2 * n_in + n_dn

    def chunk_copy(e, c):
        st = c % EXPERT_STAGES

        def gate_up(w_hbm, first, col0):
            r0 = pl.multiple_of((c - first) * ch, ch)
            cp = pltpu.make_async_copy(w_hbm.at[layer, e, pl.ds(r0, ch), :],
                                       stg_ref.at[st, :, pl.ds(0, de)], sem.at[st])

            def convert(slot):
                wgu_ref[slot, pl.ds(r0, ch), pl.ds(col0, de)] = stg_ref[st, :, pl.ds(0, de)].astype(BF16)
            return cp, convert

        def down():
            r0 = pl.multiple_of((c - 2 * n_in) * ch, ch)
            cp = pltpu.make_async_copy(wd_hbm.at[layer, e, pl.ds(r0, ch), :],
                                       stg_ref.at[st], sem.at[st])

            def convert(slot):
                wdn_ref[slot, pl.ds(r0, ch), :] = stg_ref[st].astype(BF16)
            return cp, convert

        return ((c < n_in, lambda: gate_up(wg_hbm, 0, 0)),
                (jnp.logical_and(c >= n_in, c < 2 * n_in), lambda: gate_up(wu_hbm, n_in, de)),
                (c >= 2 * n_in, down))

    def start_chunk(e, c):
        for cond, make in chunk_copy(e, c):
            @pl.when(cond)
            def _():
                make()[0].start()

    def finish_chunk(e, c, slot):
        for cond, make in chunk_copy(e, c):
            @pl.when(cond)
            def _():
                cp, convert = make()
                cp.wait()
                convert(slot)

    def prime(e):
        for c in range(EXPERT_STAGES):
            start_chunk(e, jnp.int32(c))

    def process(e, slot, c_lo, c_hi):
        def body(c, _):
            finish_chunk(e, c, slot)

            @pl.when(c + EXPERT_STAGES < n_chunks)
            def _():
                start_chunk(e, c + EXPERT_STAGES)
            return 0
        lax.fori_loop(c_lo, c_hi, body, 0)

    @pl.when(i >= na)
    def _():
        o_ref[...] = jnp.zeros_like(o_ref)

    @pl.when(i < na)
    def _():
        e = be_ref[i]
        kb = kb_ref[i]
        nk = nk_ref[i]
        nxt = nx_ref[i]
        slot = sl_ref[i]

        @pl.when(i == 0)
        def _():
            tok_copy(0).start()
            prime(e)
            tok_copy(0).wait()
            start_rows(0)
            tok_copy(1).start()
            process(e, slot, 0, n_chunks)

        @pl.when(jnp.logical_and(kb == 0, nxt >= 0))
        def _():
            prime(nxt)

        tok_copy(i + 1).wait()
        start_rows(i + 1)
        tok_copy(i + 2).start()
        wait_rows(i)
        gu = _dot(xbuf_ref[i % 2].astype(BF16), wgu_ref[slot])
        g = gu[:, :de]
        u = gu[:, de:]
        hdn = (g * _sigmoid(g)) * u
        o_ref[...] = _dot(hdn.astype(BF16), wdn_ref[slot])

        @pl.when(i == na - 1)
        def _():
            wait_rows(i + 1)
            tok_copy(i + 2).wait()

        @pl.when(nxt >= 0)
        def _():
            process(nxt, 1 - slot, (n_chunks * kb) // nk, (n_chunks * (kb + 1)) // nk)


def _expert_ffn(xn, sorted_tok, tables, w_gate, w_up, w_down, layer, tm, nb):
    d = xn.shape[1]
    de = w_down.shape[2]
    assert d % EXPERT_CHUNK_ROWS == 0 and de % EXPERT_CHUNK_ROWS == 0 and de % LANES == 0
    assert tm <= TOK_WINDOW
    hbm = lambda: pl.BlockSpec(memory_space=pl.ANY)
    return pl.pallas_call(
        functools.partial(_expert_kernel, layer=layer),
        grid_spec=pltpu.PrefetchScalarGridSpec(
            num_scalar_prefetch=len(tables),
            grid=(nb,),
            in_specs=[hbm(), hbm(), hbm(), hbm(), hbm()],
            out_specs=pl.BlockSpec((tm, d), lambda i, *_: (i, 0)),
            scratch_shapes=[pltpu.VMEM((2, d, 2 * de), BF16),
                            pltpu.VMEM((2, de, d), BF16),
                            pltpu.VMEM((EXPERT_STAGES, EXPERT_CHUNK_ROWS, d), F32),
                            pltpu.SemaphoreType.DMA((EXPERT_STAGES,)),
                            pltpu.VMEM((2, tm, d), F32),
                            pltpu.SMEM((2 * 2 * TOK_WINDOW,), jnp.int32),
                            pltpu.SemaphoreType.DMA((2,)),
                            pltpu.SemaphoreType.DMA((2,))]),
        out_shape=jax.ShapeDtypeStruct((nb * tm, d), F32),
        compiler_params=pltpu.CompilerParams(
            dimension_semantics=("arbitrary",), vmem_limit_bytes=_vmem(60)),
        name="expert_ffn",
    )(*tables, sorted_tok, xn, w_gate, w_up, w_down)


ROUTE_GATE_COL = 2


def _combine_kernel(pos_ref, posn_ref, x_ref, ys_hbm, rc_ref, g_ref, b_ref, xn_ref, *rest):
    xb_ref = rest[0] if len(rest) == 3 else None
    ybuf_ref, sem = rest[-2:]
    i = pl.program_id(0)
    nt = pl.num_programs(0)
    tm, d = x_ref.shape

    def start_rows(p_ref, buf):
        for r in range(tm):
            for k in range(TOP_K):
                pltpu.make_async_copy(ys_hbm.at[pl.ds(p_ref[0, 0, k * tm + r], 1), :],
                                      ybuf_ref.at[buf, k, pl.ds(r, 1), :], sem.at[buf]).start()

    @pl.when(i == 0)
    def _():
        start_rows(pos_ref, 0)

    for nxt_buf in range(2):
        @pl.when(jnp.logical_and(i + 1 < nt, (i + 1) % 2 == nxt_buf))
        def _():
            start_rows(posn_ref, nxt_buf)

    buf = i % 2
    for k in range(TOP_K):
        pltpu.make_async_copy(ys_hbm.at[pl.ds(0, tm), :], ybuf_ref.at[buf, k], sem.at[buf]).wait()
    g0 = rc_ref[:, ROUTE_GATE_COL:ROUTE_GATE_COL + 1]
    g1 = rc_ref[:, ROUTE_GATE_COL + 1:ROUTE_GATE_COL + 2]
    y = ALPHA * x_ref[...] + (g0 * ybuf_ref[buf, 0] + g1 * ybuf_ref[buf, 1])
    xn = _layer_norm(y, g_ref[...], b_ref[...])
    xn_ref[...] = xn
    if xb_ref is not None:
        xb_ref[...] = xn.astype(BF16)


POS_TILES_PER_STEP = 8


def _slot_pos_kernel(rt_ref, ps_ref, o_ref, *, tm):
    eid = lax.broadcasted_iota(jnp.int32, (N_EXPERTS, rt_ref.shape[1]), 0)
    for k in range(TOP_K):
        e = rt_ref[k:k + 1, :].astype(jnp.int32)
        base = jnp.sum(jnp.where(eid == e, ps_ref[...], 0.0), axis=0, keepdims=True)
        pos = (base + rt_ref[2 * TOP_K + k:2 * TOP_K + k + 1, :]).astype(jnp.int32)
        for t in range(o_ref.shape[0]):
            o_ref[t, :, pl.ds(k * tm, tm)] = pos[:, t * tm:(t + 1) * tm]


def _slot_positions(route, pstart, tm):
    n = route.shape[1]
    nt = n // tm
    step = POS_TILES_PER_STEP if nt % POS_TILES_PER_STEP == 0 else 1
    return pl.pallas_call(
        functools.partial(_slot_pos_kernel, tm=tm),
        grid=(nt // step,),
        in_specs=[pl.BlockSpec((8, step * tm), lambda i: (0, i)),
                  pl.BlockSpec((N_EXPERTS, 1), lambda i: (0, 0))],
        out_specs=pl.BlockSpec((step, 1, TOP_K * tm), lambda i: (i, 0, 0)),
        out_shape=jax.ShapeDtypeStruct((nt, 1, TOP_K * tm), jnp.int32),
        compiler_params=pltpu.CompilerParams(dimension_semantics=("parallel",)),
        name="slot_positions",
    )(route, pstart)


def _combine_ln(x, ys, pos_t, route_cols, ln_g, ln_b, tm, with_bf16):
    n, d = x.shape
    nt = n // tm
    n_out = 2 if with_bf16 else 1
    row = lambda: pl.BlockSpec((tm, d), lambda i: (i, 0))
    vec = lambda: pl.BlockSpec((1, d), lambda i: (0, 0))
    smem = lambda imap: pl.BlockSpec((1, 1, TOP_K * tm), imap, memory_space=pltpu.SMEM)
    return pl.pallas_call(
        _combine_kernel,
        grid=(nt,),
        in_specs=[smem(lambda i: (i, 0, 0)), smem(lambda i: (jnp.minimum(i + 1, nt - 1), 0, 0)),
                  row(), pl.BlockSpec(memory_space=pl.ANY),
                  pl.BlockSpec((tm, 8), lambda i: (i, 0)), vec(), vec()],
        out_specs=[row(), row()][:n_out],
        out_shape=[jax.ShapeDtypeStruct((n, d), F32), jax.ShapeDtypeStruct((n, d), BF16)][:n_out],
        scratch_shapes=[pltpu.VMEM((2, TOP_K, tm, d), F32), pltpu.SemaphoreType.DMA((2,))],
        compiler_params=pltpu.CompilerParams(
            dimension_semantics=("arbitrary",), vmem_limit_bytes=_vmem(48)),
        name="combine_ln",
    )(pos_t, pos_t, x, ys, route_cols, ln_g, ln_b)


def _gla_gate_kernel(x_ref, wl_ref, w2_ref, ba_ref, o_ref):
    low = _dot(x_ref[...], wl_ref[...]).astype(BF16)
    z = _dot(low, w2_ref[...]) + ba_ref[...]
    o_ref[...] = _log_sigmoid(z) * (1.0 / GLA_TAU)


def _gla_gate(xb, w_low, w_a2, b_a, tm):
    n, d = xb.shape
    kw = w_a2.shape[1]
    return pl.pallas_call(
        _gla_gate_kernel,
        grid=(n // tm,),
        in_specs=[pl.BlockSpec((tm, d), lambda i: (i, 0)),
                  pl.BlockSpec(w_low.shape, lambda i: (0, 0)),
                  pl.BlockSpec(w_a2.shape, lambda i: (0, 0)),
                  pl.BlockSpec((1, kw), lambda i: (0, 0))],
        out_specs=pl.BlockSpec((tm, kw), lambda i: (i, 0)),
        out_shape=jax.ShapeDtypeStruct((n, kw), F32),
        compiler_params=pltpu.CompilerParams(
            dimension_semantics=("parallel",), vmem_limit_bytes=_vmem(32)),
        name="gla_gate",
    )(xb, w_low, w_a2, b_a)


GLA_ROWS = 512


def _gla_kernel(q_ref, k_ref, v_ref, la_ref, g_ref, ng_ref, o_ref, state_ref, *, scale):
    rows, kw = q_ref.shape
    nh = state_ref.shape[0]
    hk, hv = state_ref.shape[1], state_ref.shape[2]
    ck = 2 * GLA_CHUNK
    mid = GLA_CHUNK - 1

    @pl.when(pl.program_id(1) == 0)
    def _():
        state_ref[...] = jnp.zeros_like(state_ref)

    row = lax.broadcasted_iota(jnp.int32, (ck, ck), 0)
    col = lax.broadcasted_iota(jnp.int32, (ck, ck), 1)
    causal = row >= col
    tri = jnp.where(causal, 1.0, 0.0).astype(BF16)
    ones = jnp.ones((ck, LANES), BF16)

    def chunk(c, _):
        r0 = pl.multiple_of(c * ck, ck)
        heads = range(nh)
        ks = [slice(hh * hk, (hh + 1) * hk) for hh in heads]
        vs = [slice(hh * hv, (hh + 1) * hv) for hh in heads]
        la = [_split_bf16(la_ref[pl.ds(r0, ck), ks[hh]]) for hh in heads]
        b = [_dot(tri, la[hh][0]) + _dot(tri, la[hh][1]) for hh in heads]
        dsum = [_dot_tn(la[hh][0], ones) + _dot_tn(la[hh][1], ones) for hh in heads]
        qt, kt, kend, qin = [], [], [], []
        for hh in heads:
            q = q_ref[pl.ds(r0, ck), ks[hh]].astype(F32) * scale
            k = k_ref[pl.ds(r0, ck), ks[hh]].astype(F32)
            b_mid = b[hh][mid:mid + 1, :]
            b_last = b[hh][ck - 1:ck, :]
            qt.append((q * jnp.exp(b[hh] - b_mid)).astype(BF16))
            kt.append((k * jnp.exp(b_mid - b[hh])).astype(BF16))
            kend.append((k * jnp.exp(b_last - b[hh])).astype(BF16))
            qin.append((q * jnp.exp(b[hh])).astype(BF16))
        v = [v_ref[pl.ds(r0, ck), vs[hh]] for hh in heads]
        state = [state_ref[hh] for hh in heads]
        attn = [_dot_nt(qt[hh], kt[hh]) for hh in heads]
        inter = [_dot(qin[hh], state[hh].astype(BF16)) for hh in heads]
        kv = [_dot_tn(kend[hh], v[hh]) for hh in heads]
        for hh in heads:
            decay = jnp.concatenate([jnp.exp(dsum[hh])] * (hv // LANES), axis=1)
            state_ref[hh] = state[hh] * decay + kv[hh]
        for hh in heads:
            o = _dot(jnp.where(causal, attn[hh], 0.0).astype(BF16), v[hh]) + inter[hh]
            o = o * lax.rsqrt(jnp.mean(o * o, axis=-1, keepdims=True) + LN_EPS)
            gate = g_ref[pl.ds(r0, ck), vs[hh]].astype(F32)
            o = o * ng_ref[:, vs[hh]] * (gate * _sigmoid(gate))
            o_ref[pl.ds(r0, ck), vs[hh]] = o.astype(o_ref.dtype)
        return 0

    lax.fori_loop(0, rows // ck, chunk, 0, unroll=2)


def _gla(h, la, norm_g, batch, seq):
    n = h.shape[0]
    kw = la.shape[1]
    vw = norm_g.shape[1]
    nh = GLA_HEADS
    hk, hv = kw // nh, vw // nh
    rows = GLA_ROWS
    nj = seq // rows
    kern = functools.partial(_gla_kernel, scale=hk ** -0.5)
    rowblk = lambda b, j: b * nj + j
    return pl.pallas_call(
        kern,
        grid=(batch, nj),
        in_specs=[pl.BlockSpec((rows, kw), lambda b, j: (rowblk(b, j), 0)),
                  pl.BlockSpec((rows, kw), lambda b, j: (rowblk(b, j), 1)),
                  pl.BlockSpec((rows, vw), lambda b, j: (rowblk(b, j), 2 * kw // vw)),
                  pl.BlockSpec((rows, kw), lambda b, j: (rowblk(b, j), 0)),
                  pl.BlockSpec((rows, vw), lambda b, j: (rowblk(b, j), 2 * kw // vw + 1)),
                  pl.BlockSpec((1, vw), lambda b, j: (0, 0))],
        out_specs=pl.BlockSpec((rows, vw), lambda b, j: (rowblk(b, j), 0)),
        out_shape=jax.ShapeDtypeStruct((n, vw), BF16),
        scratch_shapes=[pltpu.VMEM((nh, hk, hv), F32)],
        compiler_params=pltpu.CompilerParams(
            dimension_semantics=("parallel", "arbitrary"), vmem_limit_bytes=_vmem(48)),
        name="gla",
    )(h, h, h, la, h, norm_g)


EXPERT_TM = 256
COMBINE_TM = 256


def _tables_kernel(cnt_ref, be_ref, na_ref, kb_ref, nk_ref, nx_ref, sl_ref, src_ref, ps_ref, *, tm):
    n_tab = be_ref.shape[0]
    blocks_of = lambda e: (cnt_ref[e] + tm - 1) // tm
    run = jnp.int32(0)
    first = jnp.int32(0)
    order = jnp.int32(0)
    for e in range(N_EXPERTS):
        nbe = blocks_of(e)
        ps_ref[e] = run * tm

        def fill(k, _, e=e, run=run, first=first, order=order, nbe=nbe):
            i = run + k
            be_ref[i] = e
            kb_ref[i] = k
            nk_ref[i] = nbe
            sl_ref[i] = order % 2
            src_ref[i] = first + k * tm
            return 0
        lax.fori_loop(0, nbe, fill, 0)
        run = run + nbe
        first = first + cnt_ref[e]
        order = order + jnp.where(nbe > 0, 1, 0)
    na_ref[0] = run

    nxt = jnp.int32(-1)
    end = run
    for e in reversed(range(N_EXPERTS)):
        nbe = blocks_of(e)

        def fill_next(i, _, nxt=nxt):
            nx_ref[i] = nxt
            return 0
        lax.fori_loop(end - nbe, end, fill_next, 0)
        nxt = jnp.where(nbe > 0, e, nxt)
        end = end - nbe

    last = run - 1

    def pad(i, _):
        be_ref[i] = be_ref[last]
        kb_ref[i] = kb_ref[last]
        nk_ref[i] = nk_ref[last]
        nx_ref[i] = nx_ref[last]
        sl_ref[i] = sl_ref[last]
        src_ref[i] = 0
        return 0
    lax.fori_loop(run, n_tab, pad, 0)


def _block_tables(counts, tm, n_tab):
    i32 = jnp.int32
    smem = lambda: pl.BlockSpec(memory_space=pltpu.SMEM)
    tab = jax.ShapeDtypeStruct((n_tab,), i32)
    outs = pl.pallas_call(
        functools.partial(_tables_kernel, tm=tm),
        in_specs=[smem()],
        out_specs=[smem()] * 8,
        out_shape=[tab, jax.ShapeDtypeStruct((1,), i32), tab, tab, tab, tab, tab,
                   jax.ShapeDtypeStruct((N_EXPERTS,), i32)],
        name="block_tables",
    )(counts)
    return tuple(outs[:7]), outs[7]


def _moe(xn, route, route_cols, counts, w_gate, w_up, w_down, layer, ln_g, ln_b, with_bf16):
    n, d = xn.shape
    a = n * TOP_K
    tm = EXPERT_TM
    nb = a // tm + N_EXPERTS
    i32 = jnp.int32
    eidx = route[:TOP_K].astype(i32)
    assign = jnp.arange(n, dtype=i32)[None, :] * TOP_K + jnp.arange(TOP_K, dtype=i32)[:, None]
    sorted_tok = (lax.sort((eidx * a + assign).reshape(a)) % a) // TOP_K
    sorted_tok = jnp.concatenate([sorted_tok, jnp.zeros((2 * TOK_WINDOW,), i32)])
    tables, pstart = _block_tables(counts[:, 0].astype(i32), tm, nb + 2)
    ys = _expert_ffn(xn, sorted_tok, tables, w_gate, w_up, w_down, layer, tm, nb)
    pos = _slot_positions(route, pstart.reshape(N_EXPERTS, 1).astype(F32), COMBINE_TM)
    return _combine_ln(xn, ys, pos, route_cols, ln_g, ln_b, COMBINE_TM, with_bf16)


def kernel(x, even_w_in, even_b_f, even_conv_w, even_conv_b, even_conv_norm_g, even_conv_norm_b, even_w_out, odd_w_in, odd_w_a2, odd_b_a, odd_norm_g, odd_w_out, ln_mix_g, ln_mix_b, ln_ffn_g, ln_ffn_b, router_w, router_bias, expert_w_gate, expert_w_up, expert_w_down):
    batch, seq, d = x.shape
    n = batch * seq
    x2 = x.reshape(n, d)
    fw = FOX_HEADS * FOX_HEAD_DIM
    conv_ch = even_conv_w.shape[-1]
    kw = odd_w_a2.shape[-1]
    vw = odd_norm_g.shape[-1]
    row = lambda t: t.reshape(1, -1)

    wr_hi, wr_lo = _split_bf16(router_w.T)
    rbias = router_bias.reshape(N_EXPERTS, 1).astype(F32)
    experts_w = (expert_w_gate, expert_w_up, expert_w_down)

    w_t = jnp.swapaxes(even_w_in, 1, 2)
    q_scale = LOG2E * FOX_HEAD_DIM ** -0.5
    wf = jnp.zeros((d, LANES), F32).at[:, :FOX_HEADS].set(even_w_in[0, :, 3 * fw:3 * fw + FOX_HEADS])
    b_f = jnp.zeros((1, LANES), F32).at[0, :FOX_HEADS].set(even_b_f[0])
    c, xb = _fox_gate(x2, wf.astype(BF16), b_f, batch, seq, 512)
    h = _matmul(xb, w_t, 3 * fw, BF16, 1024, fw, first_scale=q_scale)
    h_glu = _matmul(xb, w_t[:, 3 * fw + FOX_HEADS:], 2 * conv_ch, BF16, 1024, conv_ch)
    att = _fox_attention(h, c, batch, seq, 256)
    u = _conv_module(h_glu, 0, conv_ch // CONV_GROUP, even_conv_w[0, :, 0, :],
                     row(even_conv_b[0]), row(even_conv_norm_g[0]), row(even_conv_norm_b[0]),
                     batch, seq)
    xn, route, rcols, cnt = _outproj_ln_route(att, 0, u, 0, fw, even_w_out[0].astype(BF16), x2,
                                              row(ln_mix_g[0]), row(ln_mix_b[0]),
                                              wr_hi, wr_lo, rbias, 512)
    xn, xb = _moe(xn, route, rcols, cnt, *experts_w, 0, row(ln_ffn_g[0]), row(ln_ffn_b[0]), True)

    w_t = jnp.swapaxes(odd_w_in, 1, 2)
    h = _matmul(xb, w_t, 2 * kw + 2 * vw, BF16, 1024, 1024)
    w_low = jnp.zeros((d, LANES), F32).at[:, :GLA_LOW_RANK].set(odd_w_in[0, :, 2 * kw + 2 * vw:])
    w_low = w_low.astype(BF16)
    w_a2 = jnp.zeros((LANES, kw), F32).at[:GLA_LOW_RANK].set(odd_w_a2[0]).astype(BF16)
    la = _gla_gate(xb, w_low, w_a2, row(odd_b_a[0]), 1024)
    o = _gla(h, la, row(odd_norm_g[0]), batch, seq)
    half = vw // 2
    xn, route, rcols, cnt = _outproj_ln_route(o, 0, o, 1, half, odd_w_out[0].astype(BF16), xn,
                                              row(ln_mix_g[1]), row(ln_mix_b[1]),
                                              wr_hi, wr_lo, rbias, 512)
    (xn,) = _moe(xn, route, rcols, cnt, *experts_w, 1, row(ln_ffn_g[1]), row(ln_ffn_b[1]), False)
    return xn.reshape(batch, seq, d)
```

```python
import functools

import jax
import jax.numpy as jnp
from jax import lax
from jax.experimental import pallas as pl
from jax.experimental.pallas import tpu as pltpu

F32 = jnp.float32
BF16 = jnp.bfloat16

DEPTH = 2
ALPHA = (2 * DEPTH) ** 0.25
LN_EPS = 1e-5
FOX_HEADS = 8
FOX_HEAD_DIM = 128
CONV_WIDTH = 31
CONV_GROUP = 128
GLA_HEADS = 4
GLA_LOW_RANK = 16
GLA_TAU = 16.0
GLA_CHUNK = 64
N_EXPERTS = 16
N_GROUPS = 4
EXPERTS_PER_GROUP = N_EXPERTS // N_GROUPS
TOP_K = 2

LANES = 128
V7X_VMEM_BYTES = 64 * 1024 * 1024
NEG_BIG = -1e30
LOG2E = 1.4426950408889634


def _vmem(mib):
    assert mib * 1024 * 1024 < V7X_VMEM_BYTES
    return mib * 1024 * 1024


def _sigmoid(z):
    return 1.0 / (1.0 + jnp.exp(-z))


def _log_sigmoid(z):
    return jnp.minimum(z, 0.0) - jnp.log1p(jnp.exp(-jnp.abs(z)))


def _dot(a, b):
    return jnp.dot(a, b, preferred_element_type=F32)


def _dot_nt(a, b):
    return lax.dot_general(a, b, (((1,), (1,)), ((), ())), preferred_element_type=F32)


def _dot_tn(a, b):
    return lax.dot_general(a, b, (((0,), (0,)), ((), ())), preferred_element_type=F32)


def _split_bf16(v):
    hi = v.astype(BF16)
    lo = (v - hi.astype(F32)).astype(BF16)
    return hi, lo


def _mm_kernel(x_ref, w_ref, o_ref, wb_ref, *, first_scale):
    @pl.when(pl.program_id(1) == 0)
    def _():
        wb_ref[...] = w_ref[...].T.astype(BF16)

    acc = _dot(x_ref[...].astype(BF16), wb_ref[...])
    if first_scale is not None:
        acc = acc * jnp.where(pl.program_id(0) == 0, first_scale, 1.0)
    o_ref[...] = acc.astype(o_ref.dtype)


def _matmul(x, w_t, n_cols, out_dtype, tm, tn, first_scale=None):
    m, k = x.shape
    assert m % tm == 0 and n_cols % tn == 0 and n_cols <= w_t.shape[1] and k == w_t.shape[2]
    return pl.pallas_call(
        functools.partial(_mm_kernel, first_scale=first_scale),
        grid=(n_cols // tn, m // tm),
        in_specs=[pl.BlockSpec((tm, k), lambda j, i: (i, 0)),
                  pl.BlockSpec((None, tn, k), lambda j, i: (0, j, 0))],
        out_specs=pl.BlockSpec((tm, tn), lambda j, i: (i, j)),
        out_shape=jax.ShapeDtypeStruct((m, n_cols), out_dtype),
        scratch_shapes=[pltpu.VMEM((k, tn), BF16)],
        compiler_params=pltpu.CompilerParams(
            dimension_semantics=("parallel", "arbitrary"), vmem_limit_bytes=_vmem(52)),
        name="dense_proj",
    )(x, w_t)


def _fox_gate_kernel(x_ref, wf_ref, bf_ref, c_ref, xb_ref, carry_ref):
    j = pl.program_id(1)

    @pl.when(j == 0)
    def _():
        carry_ref[...] = jnp.zeros_like(carry_ref)

    xb = x_ref[...].astype(BF16)
    xb_ref[...] = xb
    z = _dot(xb, wf_ref[...]) + bf_ref[...]
    lf = _log_sigmoid(z)
    ts = lf.shape[0]
    row = lax.broadcasted_iota(jnp.int32, lf.shape, 0)
    sh = 1
    while sh < ts:
        lf = lf + jnp.where(row >= sh, pltpu.roll(lf, sh, axis=0), 0.0)
        sh *= 2
    c = lf + carry_ref[...]
    c_ref[...] = c * LOG2E
    carry_ref[...] = c[ts - 1:ts, :]


def _fox_gate(x2, wf, b_f, batch, seq, ts):
    d = x2.shape[1]
    nj = seq // ts
    return pl.pallas_call(
        _fox_gate_kernel,
        grid=(batch, nj),
        in_specs=[pl.BlockSpec((ts, d), lambda b, j: (b * nj + j, 0)),
                  pl.BlockSpec((d, LANES), lambda b, j: (0, 0)),
                  pl.BlockSpec((1, LANES), lambda b, j: (0, 0))],
        out_specs=[pl.BlockSpec((ts, LANES), lambda b, j: (b * nj + j, 0)),
                   pl.BlockSpec((ts, d), lambda b, j: (b * nj + j, 0))],
        out_shape=[jax.ShapeDtypeStruct((batch * seq, LANES), F32),
                   jax.ShapeDtypeStruct((batch * seq, d), BF16)],
        scratch_shapes=[pltpu.VMEM((1, LANES), F32)],
        compiler_params=pltpu.CompilerParams(
            dimension_semantics=("parallel", "arbitrary"), vmem_limit_bytes=_vmem(40)),
        name="fox_gate",
    )(x2, wf, b_f)


FOX_HEADS_PER_STEP = 8


def _fox_attn_kernel(q_ref, k_ref, v_ref, c_ref, o_ref, vt_ref, crep_ref, *, tile, heads):
    hh = pl.program_id(1)
    i = pl.program_id(2)
    hd = FOX_HEAD_DIM
    nq = vt_ref.shape[1]

    @pl.when(i == 0)
    def _():
        lane = lax.broadcasted_iota(jnp.int32, (tile, LANES), 1)
        for j in range(nq):
            cj = c_ref[pl.ds(j * tile, tile), :]
            for g in range(heads):
                col = jnp.sum(jnp.where(lane == hh * heads + g, cj, 0.0), axis=1, keepdims=True)
                crep_ref[g, j] = jnp.broadcast_to(col, (tile, LANES))
                vj = v_ref[pl.ds(j * tile, tile), g * hd:(g + 1) * hd]
                vt_ref[g, j] = vj.astype(F32).T.astype(BF16)

    def scores(g, j):
        r0 = pl.multiple_of(j * tile, tile)
        kj = k_ref[pl.ds(r0, tile), g * hd:(g + 1) * hd]
        cj = jnp.concatenate([crep_ref[g, j]] * (tile // LANES), axis=1)
        return _dot_nt(kj, q_ref[:, g * hd:(g + 1) * hd]) - cj

    def update(g, j, carry, st):
        m, l, acc = carry
        m_new = jnp.maximum(m, jnp.max(st, axis=0, keepdims=True))
        a = jnp.exp2(m - m_new)
        p = jnp.exp2(st - m_new)
        l = a * l + jnp.sum(p, axis=0, keepdims=True)
        acc = a * acc + _dot(vt_ref[g, j], p.astype(BF16))
        return m_new, l, acc

    def body(j, carries):
        sts = [scores(g, j) for g in range(heads)]
        return tuple(update(g, j, carries[g], sts[g]) for g in range(heads))

    init = tuple((jnp.full((1, tile), NEG_BIG, F32), jnp.zeros((1, tile), F32),
                  jnp.zeros((hd, tile), F32)) for _ in range(heads))
    carries = lax.fori_loop(0, i, body, init)
    key = lax.broadcasted_iota(jnp.int32, (tile, tile), 0)
    qry = lax.broadcasted_iota(jnp.int32, (tile, tile), 1)
    sts = [jnp.where(key <= qry, scores(g, i), NEG_BIG) for g in range(heads)]
    for g in range(heads):
        _, l, acc = update(g, i, carries[g], sts[g])
        o_ref[:, g * hd:(g + 1) * hd] = (acc * (1.0 / l)).T.astype(o_ref.dtype)


def _fox_attention(h, c, batch, seq, tile):
    n = h.shape[0]
    hd = FOX_HEAD_DIM
    nh = FOX_HEADS
    gh = FOX_HEADS_PER_STEP
    ng = nh // gh
    nq = seq // tile
    kern = functools.partial(_fox_attn_kernel, tile=tile, heads=gh)
    return pl.pallas_call(
        kern,
        grid=(batch, ng, nq),
        in_specs=[pl.BlockSpec((tile, gh * hd), lambda b, hh, i: (b * nq + i, hh)),
                  pl.BlockSpec((seq, gh * hd), lambda b, hh, i: (b, ng + hh)),
                  pl.BlockSpec((seq, gh * hd), lambda b, hh, i: (b, 2 * ng + hh)),
                  pl.BlockSpec((seq, LANES), lambda b, hh, i: (b, 0))],
        out_specs=pl.BlockSpec((tile, gh * hd), lambda b, hh, i: (b * nq + i, hh)),
        out_shape=jax.ShapeDtypeStruct((n, nh * hd), BF16),
        scratch_shapes=[pltpu.VMEM((gh, nq, hd, tile), BF16),
                        pltpu.VMEM((gh, nq, tile, LANES), F32)],
        compiler_params=pltpu.CompilerParams(
            dimension_semantics=("parallel", "parallel", "arbitrary"), vmem_limit_bytes=_vmem(40)),
        name="fox_attention",
    )(h, h, h, c)


CONV_PAD = 32
CONV_ROWS = 256


def _conv_kernel(a_ref, g_ref, w_ref, cb_ref, ng_ref, nb_ref, o_ref, pad_ref):
    seq = a_ref.shape[0]
    pad_ref[pl.ds(0, CONV_PAD), :] = jnp.zeros((CONV_PAD, LANES), F32)
    pad_ref[pl.ds(CONV_PAD, seq), :] = a_ref[...].astype(F32) * _sigmoid(g_ref[...].astype(F32))
    off = CONV_PAD - (CONV_WIDTH - 1)
    for r in range(seq // CONV_ROWS):
        base = r * CONV_ROWS
        acc = jnp.zeros((CONV_ROWS, LANES), F32)
        for j in range(CONV_WIDTH):
            acc = acc + pad_ref[pl.ds(base + off + j, CONV_ROWS), :] * w_ref[pl.ds(j, 1), :]
        y = acc + cb_ref[...]
        mu = jnp.mean(y, axis=-1, keepdims=True)
        yc = y - mu
        var = jnp.mean(yc * yc, axis=-1, keepdims=True)
        yn = yc * lax.rsqrt(var + LN_EPS) * ng_ref[...] + nb_ref[...]
        o_ref[pl.ds(base, CONV_ROWS), :] = (yn * _sigmoid(yn)).astype(o_ref.dtype)


def _conv_module(h, a_col, g_col, conv_w, conv_b, cn_g, cn_b, batch, seq):
    n = h.shape[0]
    ch = conv_w.shape[1]
    ng = ch // CONV_GROUP
    assert seq % CONV_ROWS == 0
    vec = lambda: pl.BlockSpec((1, CONV_GROUP), lambda b, g: (0, g))
    return pl.pallas_call(
        _conv_kernel,
        grid=(batch, ng),
        in_specs=[pl.BlockSpec((seq, CONV_GROUP), lambda b, g: (b, a_col + g)),
                  pl.BlockSpec((seq, CONV_GROUP), lambda b, g: (b, g_col + g)),
                  pl.BlockSpec((CONV_WIDTH, CONV_GROUP), lambda b, g: (0, g)),
                  vec(), vec(), vec()],
        out_specs=pl.BlockSpec((seq, CONV_GROUP), lambda b, g: (b, g)),
        out_shape=jax.ShapeDtypeStruct((n, ch), BF16),
        scratch_shapes=[pltpu.VMEM((seq + CONV_PAD, LANES), F32)],
        compiler_params=pltpu.CompilerParams(
            dimension_semantics=("parallel", "parallel"), vmem_limit_bytes=_vmem(32)),
        name="conv_module",
    )(h, h, conv_w, conv_b, cn_g, cn_b)


def _layer_norm(y, g, b):
    mu = jnp.mean(y, axis=-1, keepdims=True)
    yc = y - mu
    var = jnp.mean(yc * yc, axis=-1, keepdims=True)
    return yc * lax.rsqrt(var + LN_EPS) * g + b


def _route_rows(xn, wr_hi, wr_lo, rbias, count_ref):
    x_hi, x_lo = _split_bf16(xn)
    logits = _dot_nt(wr_hi, x_hi) + _dot_nt(wr_hi, x_lo) + _dot_nt(wr_lo, x_hi)
    scores = _sigmoid(logits)
    sel = scores + rbias
    s = [sel[e:e + 1, :] for e in range(N_EXPERTS)]
    r = [scores[e:e + 1, :] for e in range(N_EXPERTS)]
    pg = EXPERTS_PER_GROUP

    def top2_sum(vals):
        best = None
        for a in range(len(vals)):
            for b in range(a + 1, len(vals)):
                t = vals[a] + vals[b]
                best = t if best is None else jnp.maximum(best, t)
        return best

    gs = [top2_sum(s[g * pg:(g + 1) * pg]) for g in range(N_GROUPS)]
    best, grp = gs[0], jnp.zeros_like(gs[0], dtype=jnp.int32)
    for g in range(1, N_GROUPS):
        upd = gs[g] > best
        best = jnp.where(upd, gs[g], best)
        grp = jnp.where(upd, g, grp)

    def pick_group(vals, k):
        out = vals[(N_GROUPS - 1) * pg + k]
        for g in range(N_GROUPS - 2, -1, -1):
            out = jnp.where(grp == g, vals[g * pg + k], out)
        return out

    v = [pick_group(s, k) for k in range(pg)]
    w = [pick_group(r, k) for k in range(pg)]
    b1, i1, w1 = v[0], jnp.zeros_like(grp), w[0]
    for k in range(1, pg):
        upd = v[k] > b1
        b1 = jnp.where(upd, v[k], b1)
        i1 = jnp.where(upd, k, i1)
        w1 = jnp.where(upd, w[k], w1)
    b2 = jnp.full_like(b1, -jnp.inf)
    i2, w2 = jnp.zeros_like(grp), jnp.zeros_like(w1)
    for k in range(pg):
        upd = jnp.logical_and(i1 != k, v[k] > b2)
        b2 = jnp.where(upd, v[k], b2)
        i2 = jnp.where(upd, k, i2)
        w2 = jnp.where(upd, w[k], w2)
    tot = w1 + w2
    e1 = grp * pg + i1
    e2 = grp * pg + i2
    eid = lax.broadcasted_iota(jnp.int32, scores.shape, 0)
    oh1 = jnp.where(eid == e1, 1.0, 0.0)
    oh2 = jnp.where(eid == e2, 1.0, 0.0)
    cnt = oh1 + oh2
    tm = cnt.shape[1]
    lane = lax.broadcasted_iota(jnp.int32, cnt.shape, 1)
    incl = cnt
    sh = 1
    while sh < tm:
        incl = incl + jnp.where(lane >= sh, pltpu.roll(incl, sh, axis=1), 0.0)
        sh *= 2
    before = incl - cnt + count_ref[...]
    count_ref[...] = count_ref[...] + incl[:, tm - 1:tm]
    rank1 = jnp.sum(oh1 * before, axis=0, keepdims=True)
    rank2 = jnp.sum(oh2 * before, axis=0, keepdims=True)
    rows = [e1.astype(F32), e2.astype(F32), w1 / tot, w2 / tot, rank1, rank2]
    rows += [jnp.zeros_like(w1)] * (8 - len(rows))
    return jnp.concatenate(rows, axis=0)


OUTPROJ_SUBTILES = 2


def _outproj_kernel(a1_ref, a2_ref, w_ref, x_ref, g_ref, b_ref, wrh_ref, wrl_ref, rb_ref,
                    xn_ref, rt_ref, rc_ref, cnt_ref, count_ref):
    @pl.when(pl.program_id(0) == 0)
    def _():
        count_ref[...] = jnp.zeros_like(count_ref)

    k1 = a1_ref.shape[1]
    k2 = a2_ref.shape[1]
    tm = x_ref.shape[0]
    sub = tm // OUTPROJ_SUBTILES
    rs = [pl.ds(t * sub, sub) for t in range(OUTPROJ_SUBTILES)]
    mix = [_dot(a1_ref[r, :], w_ref[pl.ds(0, k1), :]) + _dot(a2_ref[r, :], w_ref[pl.ds(k1, k2), :])
           for r in rs]
    for t, r in enumerate(rs):
        xn = _layer_norm(ALPHA * x_ref[r, :] + mix[t], g_ref[...], b_ref[...])
        xn_ref[r, :] = xn
        rows = _route_rows(xn, wrh_ref[...], wrl_ref[...], rb_ref[...], count_ref)
        rt_ref[:, r] = rows
        rc_ref[r, :] = rows.T
    cnt_ref[...] = jnp.broadcast_to(count_ref[...], cnt_ref.shape)


def _outproj_ln_route(a1, a1_col, a2, a2_col, kw, w, xres, ln_g, ln_b, wr_hi, wr_lo, rbias, tm):
    n, d = xres.shape
    full = lambda shape: pl.BlockSpec(shape, lambda i: (0, 0))
    return pl.pallas_call(
        _outproj_kernel,
        grid=(n // tm,),
        in_specs=[pl.BlockSpec((tm, kw), lambda i: (i, a1_col)),
                  pl.BlockSpec((tm, kw), lambda i: (i, a2_col)),
                  full(w.shape),
                  pl.BlockSpec((tm, d), lambda i: (i, 0)),
                  full((1, d)), full((1, d)),
                  full(wr_hi.shape), full(wr_lo.shape), full(rbias.shape)],
        out_specs=[pl.BlockSpec((tm, d), lambda i: (i, 0)),
                   pl.BlockSpec((8, tm), lambda i: (0, i)),
                   pl.BlockSpec((tm, 8), lambda i: (i, 0)),
                   pl.BlockSpec((N_EXPERTS, LANES), lambda i: (0, 0))],
        out_shape=[jax.ShapeDtypeStruct((n, d), F32),
                   jax.ShapeDtypeStruct((8, n), F32),
                   jax.ShapeDtypeStruct((n, 8), F32),
                   jax.ShapeDtypeStruct((N_EXPERTS, LANES), F32)],
        scratch_shapes=[pltpu.VMEM((N_EXPERTS, 1), F32)],
        compiler_params=pltpu.CompilerParams(
            dimension_semantics=("arbitrary",), vmem_limit_bytes=_vmem(52)),
        name="outproj_ln_route",
    )(a1, a2, w, xres, ln_g, ln_b, wr_hi, wr_lo, rbias)


EXPERT_CHUNK_ROWS = 128
EXPERT_STAGES = 8


TOK_WINDOW = 1024


def _expert_kernel(be_ref, na_ref, kb_ref, nk_ref, nx_ref, sl_ref, src_ref,
                   tok_hbm, x_hbm, wg_hbm, wu_hbm, wd_hbm, o_ref,
                   wgu_ref, wdn_ref, stg_ref, sem, xbuf_ref, tok_ref, gsem, tsem, *, layer):
    i = pl.program_id(0)
    tm, d = o_ref.shape
    de = wdn_ref.shape[1]
    na = na_ref[0]

    def tok_copy(j):
        base = pl.multiple_of((src_ref[j] // TOK_WINDOW) * TOK_WINDOW, TOK_WINDOW)
        dst = pl.multiple_of((j % 2) * (2 * TOK_WINDOW), 2 * TOK_WINDOW)
        return pltpu.make_async_copy(tok_hbm.at[pl.ds(base, 2 * TOK_WINDOW)],
                                     tok_ref.at[pl.ds(dst, 2 * TOK_WINDOW)], tsem.at[j % 2])

    def start_rows(j):
        offset = src_ref[j] % TOK_WINDOW
        for buf in range(2):
            @pl.when(j % 2 == buf)
            def _():
                first = buf * (2 * TOK_WINDOW) + offset
                for r in range(tm):
                    tok = tok_ref[first + r]
                    pltpu.make_async_copy(x_hbm.at[pl.ds(tok, 1), :],
                                          xbuf_ref.at[buf, pl.ds(r, 1), :], gsem.at[buf]).start()

    def wait_rows(j):
        pltpu.make_async_copy(x_hbm.at[pl.ds(0, tm), :], xbuf_ref.at[j % 2], gsem.at[j % 2]).wait()
    ch = EXPERT_CHUNK_ROWS
    n_in = d // ch
    n_dn = de // ch
    n_chunks = 2 * n_in + n_dn

    def chunk_copy(e, c):
        st = c % EXPERT_STAGES

        def gate_up(w_hbm, first, col0):
            r0 = pl.multiple_of((c - first) * ch, ch)
            cp = pltpu.make_async_copy(w_hbm.at[layer, e, pl.ds(r0, ch), :],
                                       stg_ref.at[st, :, pl.ds(0, de)], sem.at[st])

            def convert(slot):
                wgu_ref[slot, pl.ds(r0, ch), pl.ds(col0, de)] = stg_ref[st, :, pl.ds(0, de)].astype(BF16)
            return cp, convert

        def down():
            r0 = pl.multiple_of((c - 2 * n_in) * ch, ch)
            cp = pltpu.make_async_copy(wd_hbm.at[layer, e, pl.ds(r0, ch), :],
                                       stg_ref.at[st], sem.at[st])

            def convert(slot):
                wdn_ref[slot, pl.ds(r0, ch), :] = stg_ref[st].astype(BF16)
            return cp, convert

        return ((c < n_in, lambda: gate_up(wg_hbm, 0, 0)),
                (jnp.logical_and(c >= n_in, c < 2 * n_in), lambda: gate_up(wu_hbm, n_in, de)),
                (c >= 2 * n_in, down))

    def start_chunk(e, c):
        for cond, make in chunk_copy(e, c):
            @pl.when(cond)
            def _():
                make()[0].start(priority=1)

    def finish_chunk(e, c, slot):
        for cond, make in chunk_copy(e, c):
            @pl.when(cond)
            def _():
                cp, convert = make()
                cp.wait()
                convert(slot)

    def prime(e):
        for c in range(EXPERT_STAGES):
            start_chunk(e, jnp.int32(c))

    def process(e, slot, c_lo, c_hi):
        def body(c, _):
            finish_chunk(e, c, slot)

            @pl.when(c + EXPERT_STAGES < n_chunks)
            def _():
                start_chunk(e, c + EXPERT_STAGES)
            return 0
        lax.fori_loop(c_lo, c_hi, body, 0)

    @pl.when(i >= na)
    def _():
        o_ref[...] = jnp.zeros_like(o_ref)

    @pl.when(i < na)
    def _():
        e = be_ref[i]
        kb = kb_ref[i]
        nk = nk_ref[i]
        nxt = nx_ref[i]
        slot = sl_ref[i]

        @pl.when(i == 0)
        def _():
            tok_copy(0).start()
            prime(e)
            tok_copy(0).wait()
            start_rows(0)
            tok_copy(1).start()
            process(e, slot, 0, n_chunks)

        @pl.when(jnp.logical_and(kb == 0, nxt >= 0))
        def _():
            prime(nxt)

        tok_copy(i + 1).wait()
        start_rows(i + 1)
        tok_copy(i + 2).start()
        wait_rows(i)
        gu = _dot(xbuf_ref[i % 2].astype(BF16), wgu_ref[slot])
        g = gu[:, :de]
        u = gu[:, de:]
        hdn = (g * _sigmoid(g)) * u
        o_ref[...] = _dot(hdn.astype(BF16), wdn_ref[slot])

        @pl.when(i == na - 1)
        def _():
            wait_rows(i + 1)
            tok_copy(i + 2).wait()

        @pl.when(nxt >= 0)
        def _():
            process(nxt, 1 - slot, (n_chunks * kb) // nk, (n_chunks * (kb + 1)) // nk)


def _expert_ffn(xn, sorted_tok, tables, w_gate, w_up, w_down, layer, tm, nb):
    d = xn.shape[1]
    de = w_down.shape[2]
    assert d % EXPERT_CHUNK_ROWS == 0 and de % EXPERT_CHUNK_ROWS == 0 and de % LANES == 0
    assert tm <= TOK_WINDOW
    hbm = lambda: pl.BlockSpec(memory_space=pl.ANY)
    return pl.pallas_call(
        functools.partial(_expert_kernel, layer=layer),
        grid_spec=pltpu.PrefetchScalarGridSpec(
            num_scalar_prefetch=len(tables),
            grid=(nb,),
            in_specs=[hbm(), hbm(), hbm(), hbm(), hbm()],
            out_specs=pl.BlockSpec((tm, d), lambda i, *_: (i, 0)),
            scratch_shapes=[pltpu.VMEM((2, d, 2 * de), BF16),
                            pltpu.VMEM((2, de, d), BF16),
                            pltpu.VMEM((EXPERT_STAGES, EXPERT_CHUNK_ROWS, d), F32),
                            pltpu.SemaphoreType.DMA((EXPERT_STAGES,)),
                            pltpu.VMEM((2, tm, d), F32),
                            pltpu.SMEM((2 * 2 * TOK_WINDOW,), jnp.int32),
                            pltpu.SemaphoreType.DMA((2,)),
                            pltpu.SemaphoreType.DMA((2,))]),
        out_shape=jax.ShapeDtypeStruct((nb * tm, d), F32),
        compiler_params=pltpu.CompilerParams(
            dimension_semantics=("arbitrary",), vmem_limit_bytes=_vmem(60)),
        name="expert_ffn",
    )(*tables, sorted_tok, xn, w_gate, w_up, w_down)


ROUTE_GATE_COL = 2


def _combine_kernel(pos_ref, posn_ref, x_ref, ys_hbm, rc_ref, g_ref, b_ref, xn_ref, *rest):
    xb_ref = rest[0] if len(rest) == 3 else None
    ybuf_ref, sem = rest[-2:]
    i = pl.program_id(0)
    nt = pl.num_programs(0)
    tm, d = x_ref.shape

    def start_rows(p_ref, buf):
        for r in range(tm):
            for k in range(TOP_K):
                pltpu.make_async_copy(ys_hbm.at[pl.ds(p_ref[0, 0, k * tm + r], 1), :],
                                      ybuf_ref.at[buf, k, pl.ds(r, 1), :],
                                      sem.at[buf]).start(priority=k % 2)

    @pl.when(i == 0)
    def _():
        start_rows(pos_ref, 0)

    for nxt_buf in range(2):
        @pl.when(jnp.logical_and(i + 1 < nt, (i + 1) % 2 == nxt_buf))
        def _():
            start_rows(posn_ref, nxt_buf)

    buf = i % 2
    for k in range(TOP_K):
        pltpu.make_async_copy(ys_hbm.at[pl.ds(0, tm), :], ybuf_ref.at[buf, k], sem.at[buf]).wait()
    g0 = rc_ref[:, ROUTE_GATE_COL:ROUTE_GATE_COL + 1]
    g1 = rc_ref[:, ROUTE_GATE_COL + 1:ROUTE_GATE_COL + 2]
    y = ALPHA * x_ref[...] + (g0 * ybuf_ref[buf, 0] + g1 * ybuf_ref[buf, 1])
    xn = _layer_norm(y, g_ref[...], b_ref[...])
    xn_ref[...] = xn
    if xb_ref is not None:
        xb_ref[...] = xn.astype(BF16)


POS_TILES_PER_STEP = 8


def _slot_pos_kernel(rt_ref, ps_ref, o_ref, *, tm):
    eid = lax.broadcasted_iota(jnp.int32, (N_EXPERTS, rt_ref.shape[1]), 0)
    for k in range(TOP_K):
        e = rt_ref[k:k + 1, :].astype(jnp.int32)
        base = jnp.sum(jnp.where(eid == e, ps_ref[...], 0.0), axis=0, keepdims=True)
        pos = (base + rt_ref[2 * TOP_K + k:2 * TOP_K + k + 1, :]).astype(jnp.int32)
        for t in range(o_ref.shape[0]):
            o_ref[t, :, pl.ds(k * tm, tm)] = pos[:, t * tm:(t + 1) * tm]


def _slot_positions(route, pstart, tm):
    n = route.shape[1]
    nt = n // tm
    step = POS_TILES_PER_STEP if nt % POS_TILES_PER_STEP == 0 else 1
    return pl.pallas_call(
        functools.partial(_slot_pos_kernel, tm=tm),
        grid=(nt // step,),
        in_specs=[pl.BlockSpec((8, step * tm), lambda i: (0, i)),
                  pl.BlockSpec((N_EXPERTS, 1), lambda i: (0, 0))],
        out_specs=pl.BlockSpec((step, 1, TOP_K * tm), lambda i: (i, 0, 0)),
        out_shape=jax.ShapeDtypeStruct((nt, 1, TOP_K * tm), jnp.int32),
        compiler_params=pltpu.CompilerParams(dimension_semantics=("parallel",)),
        name="slot_positions",
    )(route, pstart)


def _combine_ln(x, ys, pos_t, route_cols, ln_g, ln_b, tm, with_bf16):
    n, d = x.shape
    nt = n // tm
    n_out = 2 if with_bf16 else 1
    row = lambda: pl.BlockSpec((tm, d), lambda i: (i, 0))
    vec = lambda: pl.BlockSpec((1, d), lambda i: (0, 0))
    smem = lambda imap: pl.BlockSpec((1, 1, TOP_K * tm), imap, memory_space=pltpu.SMEM)
    return pl.pallas_call(
        _combine_kernel,
        grid=(nt,),
        in_specs=[smem(lambda i: (i, 0, 0)), smem(lambda i: (jnp.minimum(i + 1, nt - 1), 0, 0)),
                  row(), pl.BlockSpec(memory_space=pl.ANY),
                  pl.BlockSpec((tm, 8), lambda i: (i, 0)), vec(), vec()],
        out_specs=[row(), row()][:n_out],
        out_shape=[jax.ShapeDtypeStruct((n, d), F32), jax.ShapeDtypeStruct((n, d), BF16)][:n_out],
        scratch_shapes=[pltpu.VMEM((2, TOP_K, tm, d), F32), pltpu.SemaphoreType.DMA((2,))],
        compiler_params=pltpu.CompilerParams(
            dimension_semantics=("arbitrary",), vmem_limit_bytes=_vmem(48)),
        name="combine_ln",
    )(pos_t, pos_t, x, ys, route_cols, ln_g, ln_b)


def _gla_gate_kernel(x_ref, wl_ref, w2_ref, ba_ref, o_ref):
    low = _dot(x_ref[...], wl_ref[...]).astype(BF16)
    z = _dot(low, w2_ref[...]) + ba_ref[...]
    o_ref[...] = _log_sigmoid(z) * (1.0 / GLA_TAU)


def _gla_gate(xb, w_low, w_a2, b_a, tm):
    n, d = xb.shape
    kw = w_a2.shape[1]
    return pl.pallas_call(
        _gla_gate_kernel,
        grid=(n // tm,),
        in_specs=[pl.BlockSpec((tm, d), lambda i: (i, 0)),
                  pl.BlockSpec(w_low.shape, lambda i: (0, 0)),
                  pl.BlockSpec(w_a2.shape, lambda i: (0, 0)),
                  pl.BlockSpec((1, kw), lambda i: (0, 0))],
        out_specs=pl.BlockSpec((tm, kw), lambda i: (i, 0)),
        out_shape=jax.ShapeDtypeStruct((n, kw), F32),
        compiler_params=pltpu.CompilerParams(
            dimension_semantics=("parallel",), vmem_limit_bytes=_vmem(32)),
        name="gla_gate",
    )(xb, w_low, w_a2, b_a)


GLA_ROWS = 512


def _gla_kernel(q_ref, k_ref, v_ref, la_ref, g_ref, ng_ref, o_ref, state_ref, *, scale):
    rows, kw = q_ref.shape
    nh = state_ref.shape[0]
    hk, hv = state_ref.shape[1], state_ref.shape[2]
    ck = 2 * GLA_CHUNK
    mid = GLA_CHUNK - 1

    @pl.when(pl.program_id(1) == 0)
    def _():
        state_ref[...] = jnp.zeros_like(state_ref)

    row = lax.broadcasted_iota(jnp.int32, (ck, ck), 0)
    col = lax.broadcasted_iota(jnp.int32, (ck, ck), 1)
    causal = row >= col
    tri = jnp.where(causal, 1.0, 0.0).astype(BF16)
    ones = jnp.ones((ck, LANES), BF16)

    def chunk(c, _):
        r0 = pl.multiple_of(c * ck, ck)
        heads = range(nh)
        ks = [slice(hh * hk, (hh + 1) * hk) for hh in heads]
        vs = [slice(hh * hv, (hh + 1) * hv) for hh in heads]
        la = [_split_bf16(la_ref[pl.ds(r0, ck), ks[hh]]) for hh in heads]
        b = [_dot(tri, la[hh][0]) + _dot(tri, la[hh][1]) for hh in heads]
        dsum = [_dot_tn(la[hh][0], ones) + _dot_tn(la[hh][1], ones) for hh in heads]
        qt, kt, kend, qin = [], [], [], []
        for hh in heads:
            q = q_ref[pl.ds(r0, ck), ks[hh]].astype(F32) * scale
            k = k_ref[pl.ds(r0, ck), ks[hh]].astype(F32)
            b_mid = b[hh][mid:mid + 1, :]
            b_last = b[hh][ck - 1:ck, :]
            qt.append((q * jnp.exp(b[hh] - b_mid)).astype(BF16))
            kt.append((k * jnp.exp(b_mid - b[hh])).astype(BF16))
            kend.append((k * jnp.exp(b_last - b[hh])).astype(BF16))
            qin.append((q * jnp.exp(b[hh])).astype(BF16))
        v = [v_ref[pl.ds(r0, ck), vs[hh]] for hh in heads]
        state = [state_ref[hh] for hh in heads]
        attn = [_dot_nt(qt[hh], kt[hh]) for hh in heads]
        inter = [_dot(qin[hh], state[hh].astype(BF16)) for hh in heads]
        kv = [_dot_tn(kend[hh], v[hh]) for hh in heads]
        for hh in heads:
            decay = jnp.concatenate([jnp.exp(dsum[hh])] * (hv // LANES), axis=1)
            state_ref[hh] = state[hh] * decay + kv[hh]
        for hh in heads:
            o = _dot(jnp.where(causal, attn[hh], 0.0).astype(BF16), v[hh]) + inter[hh]
            o = o * lax.rsqrt(jnp.mean(o * o, axis=-1, keepdims=True) + LN_EPS)
            gate = g_ref[pl.ds(r0, ck), vs[hh]].astype(F32)
            o = o * ng_ref[:, vs[hh]] * (gate * _sigmoid(gate))
            o_ref[pl.ds(r0, ck), vs[hh]] = o.astype(o_ref.dtype)
        return 0

    lax.fori_loop(0, rows // ck, chunk, 0, unroll=2)


def _gla(h, la, norm_g, batch, seq):
    n = h.shape[0]
    kw = la.shape[1]
    vw = norm_g.shape[1]
    nh = GLA_HEADS
    hk, hv = kw // nh, vw // nh
    rows = GLA_ROWS
    nj = seq // rows
    kern = functools.partial(_gla_kernel, scale=hk ** -0.5)
    rowblk = lambda b, j: b * nj + j
    return pl.pallas_call(
        kern,
        grid=(batch, nj),
        in_specs=[pl.BlockSpec((rows, kw), lambda b, j: (rowblk(b, j), 0)),
                  pl.BlockSpec((rows, kw), lambda b, j: (rowblk(b, j), 1)),
                  pl.BlockSpec((rows, vw), lambda b, j: (rowblk(b, j), 2 * kw // vw)),
                  pl.BlockSpec((rows, kw), lambda b, j: (rowblk(b, j), 0)),
                  pl.BlockSpec((rows, vw), lambda b, j: (rowblk(b, j), 2 * kw // vw + 1)),
                  pl.BlockSpec((1, vw), lambda b, j: (0, 0))],
        out_specs=pl.BlockSpec((rows, vw), lambda b, j: (rowblk(b, j), 0)),
        out_shape=jax.ShapeDtypeStruct((n, vw), BF16),
        scratch_shapes=[pltpu.VMEM((nh, hk, hv), F32)],
        compiler_params=pltpu.CompilerParams(
            dimension_semantics=("parallel", "arbitrary"), vmem_limit_bytes=_vmem(48)),
        name="gla",
    )(h, h, h, la, h, norm_g)


EXPERT_TM = 256
COMBINE_TM = 256


def _tables_kernel(cnt_ref, be_ref, na_ref, kb_ref, nk_ref, nx_ref, sl_ref, src_ref, ps_ref, *, tm):
    n_tab = be_ref.shape[0]
    blocks_of = lambda e: (cnt_ref[e] + tm - 1) // tm
    run = jnp.int32(0)
    first = jnp.int32(0)
    order = jnp.int32(0)
    for e in range(N_EXPERTS):
        nbe = blocks_of(e)
        ps_ref[e] = run * tm

        def fill(k, _, e=e, run=run, first=first, order=order, nbe=nbe):
            i = run + k
            be_ref[i] = e
            kb_ref[i] = k
            nk_ref[i] = nbe
            sl_ref[i] = order % 2
            src_ref[i] = first + k * tm
            return 0
        lax.fori_loop(0, nbe, fill, 0)
        run = run + nbe
        first = first + cnt_ref[e]
        order = order + jnp.where(nbe > 0, 1, 0)
    na_ref[0] = run

    nxt = jnp.int32(-1)
    end = run
    for e in reversed(range(N_EXPERTS)):
        nbe = blocks_of(e)

        def fill_next(i, _, nxt=nxt):
            nx_ref[i] = nxt
            return 0
        lax.fori_loop(end - nbe, end, fill_next, 0)
        nxt = jnp.where(nbe > 0, e, nxt)
        end = end - nbe

    last = run - 1

    def pad(i, _):
        be_ref[i] = be_ref[last]
        kb_ref[i] = kb_ref[last]
        nk_ref[i] = nk_ref[last]
        nx_ref[i] = nx_ref[last]
        sl_ref[i] = sl_ref[last]
        src_ref[i] = 0
        return 0
    lax.fori_loop(run, n_tab, pad, 0)


def _block_tables(counts, tm, n_tab):
    i32 = jnp.int32
    smem = lambda: pl.BlockSpec(memory_space=pltpu.SMEM)
    tab = jax.ShapeDtypeStruct((n_tab,), i32)
    outs = pl.pallas_call(
        functools.partial(_tables_kernel, tm=tm),
        in_specs=[smem()],
        out_specs=[smem()] * 8,
        out_shape=[tab, jax.ShapeDtypeStruct((1,), i32), tab, tab, tab, tab, tab,
                   jax.ShapeDtypeStruct((N_EXPERTS,), i32)],
        name="block_tables",
    )(counts)
    return tuple(outs[:7]), outs[7]


def _moe(xn, route, route_cols, counts, w_gate, w_up, w_down, layer, ln_g, ln_b, with_bf16):
    n, d = xn.shape
    a = n * TOP_K
    tm = EXPERT_TM
    nb = a // tm + N_EXPERTS
    i32 = jnp.int32
    eidx = route[:TOP_K].astype(i32)
    assign = jnp.arange(n, dtype=i32)[None, :] * TOP_K + jnp.arange(TOP_K, dtype=i32)[:, None]
    sorted_tok = (lax.sort((eidx * a + assign).reshape(a)) % a) // TOP_K
    sorted_tok = jnp.concatenate([sorted_tok, jnp.zeros((2 * TOK_WINDOW,), i32)])
    tables, pstart = _block_tables(counts[:, 0].astype(i32), tm, nb + 2)
    ys = _expert_ffn(xn, sorted_tok, tables, w_gate, w_up, w_down, layer, tm, nb)
    pos = _slot_positions(route, pstart.reshape(N_EXPERTS, 1).astype(F32), COMBINE_TM)
    return _combine_ln(xn, ys, pos, route_cols, ln_g, ln_b, COMBINE_TM, with_bf16)


def kernel(x, even_w_in, even_b_f, even_conv_w, even_conv_b, even_conv_norm_g, even_conv_norm_b, even_w_out, odd_w_in, odd_w_a2, odd_b_a, odd_norm_g, odd_w_out, ln_mix_g, ln_mix_b, ln_ffn_g, ln_ffn_b, router_w, router_bias, expert_w_gate, expert_w_up, expert_w_down):
    batch, seq, d = x.shape
    n = batch * seq
    x2 = x.reshape(n, d)
    fw = FOX_HEADS * FOX_HEAD_DIM
    conv_ch = even_conv_w.shape[-1]
    kw = odd_w_a2.shape[-1]
    vw = odd_norm_g.shape[-1]
    row = lambda t: t.reshape(1, -1)

    wr_hi, wr_lo = _split_bf16(router_w.T)
    rbias = router_bias.reshape(N_EXPERTS, 1).astype(F32)
    experts_w = (expert_w_gate, expert_w_up, expert_w_down)

    w_t = jnp.swapaxes(even_w_in, 1, 2)
    q_scale = LOG2E * FOX_HEAD_DIM ** -0.5
    wf = jnp.zeros((d, LANES), F32).at[:, :FOX_HEADS].set(even_w_in[0, :, 3 * fw:3 * fw + FOX_HEADS])
    b_f = jnp.zeros((1, LANES), F32).at[0, :FOX_HEADS].set(even_b_f[0])
    c, xb = _fox_gate(x2, wf.astype(BF16), b_f, batch, seq, 512)
    h = _matmul(xb, w_t, 3 * fw, BF16, 1024, fw, first_scale=q_scale)
    h_glu = _matmul(xb, w_t[:, 3 * fw + FOX_HEADS:], 2 * conv_ch, BF16, 1024, conv_ch)
    att = _fox_attention(h, c, batch, seq, 256)
    u = _conv_module(h_glu, 0, conv_ch // CONV_GROUP, even_conv_w[0, :, 0, :],
                     row(even_conv_b[0]), row(even_conv_norm_g[0]), row(even_conv_norm_b[0]),
                     batch, seq)
    xn, route, rcols, cnt = _outproj_ln_route(att, 0, u, 0, fw, even_w_out[0].astype(BF16), x2,
                                              row(ln_mix_g[0]), row(ln_mix_b[0]),
                                              wr_hi, wr_lo, rbias, 512)
    xn, xb = _moe(xn, route, rcols, cnt, *experts_w, 0, row(ln_ffn_g[0]), row(ln_ffn_b[0]), True)

    w_t = jnp.swapaxes(odd_w_in, 1, 2)
    h = _matmul(xb, w_t, 2 * kw + 2 * vw, BF16, 1024, 1024)
    w_low = jnp.zeros((d, LANES), F32).at[:, :GLA_LOW_RANK].set(odd_w_in[0, :, 2 * kw + 2 * vw:])
    w_low = w_low.astype(BF16)
    w_a2 = jnp.zeros((LANES, kw), F32).at[:GLA_LOW_RANK].set(odd_w_a2[0]).astype(BF16)
    la = _gla_gate(xb, w_low, w_a2, row(odd_b_a[0]), 1024)
    o = _gla(h, la, row(odd_norm_g[0]), batch, seq)
    half = vw // 2
    xn, route, rcols, cnt = _outproj_ln_route(o, 0, o, 1, half, odd_w_out[0].astype(BF16), xn,
                                              row(ln_mix_g[1]), row(ln_mix_b[1]),
                                              wr_hi, wr_lo, rbias, 512)
    (xn,) = _moe(xn, route, rcols, cnt, *experts_w, 1, row(ln_ffn_g[1]), row(ln_ffn_b[1]), False)
    return xn.reshape(batch, seq, d)
```

```python
import functools

import jax
import jax.numpy as jnp
from jax import lax
from jax.experimental import pallas as pl
from jax.experimental.pallas import tpu as pltpu

F32 = jnp.float32
BF16 = jnp.bfloat16

DEPTH = 2
ALPHA = (2 * DEPTH) ** 0.25
LN_EPS = 1e-5
FOX_HEADS = 8
FOX_HEAD_DIM = 128
CONV_WIDTH = 31
CONV_GROUP = 128
GLA_HEADS = 4
GLA_LOW_RANK = 16
GLA_TAU = 16.0
GLA_CHUNK = 64
N_EXPERTS = 16
N_GROUPS = 4
EXPERTS_PER_GROUP = N_EXPERTS // N_GROUPS
TOP_K = 2

LANES = 128
V7X_VMEM_BYTES = 64 * 1024 * 1024
NEG_BIG = -1e30
LOG2E = 1.4426950408889634


def _vmem(mib):
    assert mib * 1024 * 1024 < V7X_VMEM_BYTES
    return mib * 1024 * 1024


def _sigmoid(z):
    return 1.0 / (1.0 + jnp.exp(-z))


def _log_sigmoid(z):
    return jnp.minimum(z, 0.0) - jnp.log1p(jnp.exp(-jnp.abs(z)))


def _dot(a, b):
    return jnp.dot(a, b, preferred_element_type=F32)


def _dot_nt(a, b):
    return lax.dot_general(a, b, (((1,), (1,)), ((), ())), preferred_element_type=F32)


def _dot_tn(a, b):
    return lax.dot_general(a, b, (((0,), (0,)), ((), ())), preferred_element_type=F32)


def _split_bf16(v):
    hi = v.astype(BF16)
    lo = (v - hi.astype(F32)).astype(BF16)
    return hi, lo


def _mm_kernel(x_ref, w_ref, o_ref, wb_ref, *, first_scale):
    @pl.when(pl.program_id(1) == 0)
    def _():
        wb_ref[...] = w_ref[...].T.astype(BF16)

    acc = _dot(x_ref[...].astype(BF16), wb_ref[...])
    if first_scale is not None:
        acc = acc * jnp.where(pl.program_id(0) == 0, first_scale, 1.0)
    o_ref[...] = acc.astype(o_ref.dtype)


def _matmul(x, w_t, n_cols, out_dtype, tm, tn, first_scale=None):
    m, k = x.shape
    assert m % tm == 0 and n_cols % tn == 0 and n_cols <= w_t.shape[1] and k == w_t.shape[2]
    return pl.pallas_call(
        functools.partial(_mm_kernel, first_scale=first_scale),
        grid=(n_cols // tn, m // tm),
        in_specs=[pl.BlockSpec((tm, k), lambda j, i: (i, 0)),
                  pl.BlockSpec((None, tn, k), lambda j, i: (0, j, 0))],
        out_specs=pl.BlockSpec((tm, tn), lambda j, i: (i, j)),
        out_shape=jax.ShapeDtypeStruct((m, n_cols), out_dtype),
        scratch_shapes=[pltpu.VMEM((k, tn), BF16)],
        compiler_params=pltpu.CompilerParams(
            dimension_semantics=("parallel", "arbitrary"), vmem_limit_bytes=_vmem(52)),
        name="dense_proj",
    )(x, w_t)


def _fox_gate_kernel(x_ref, wf_ref, bf_ref, c_ref, xb_ref, carry_ref):
    j = pl.program_id(1)

    @pl.when(j == 0)
    def _():
        carry_ref[...] = jnp.zeros_like(carry_ref)

    xb = x_ref[...].astype(BF16)
    xb_ref[...] = xb
    z = _dot(xb, wf_ref[...]) + bf_ref[...]
    lf = _log_sigmoid(z)
    ts = lf.shape[0]
    row = lax.broadcasted_iota(jnp.int32, lf.shape, 0)
    sh = 1
    while sh < ts:
        lf = lf + jnp.where(row >= sh, pltpu.roll(lf, sh, axis=0), 0.0)
        sh *= 2
    c = lf + carry_ref[...]
    c_ref[...] = c * LOG2E
    carry_ref[...] = c[ts - 1:ts, :]


def _fox_gate(x2, wf, b_f, batch, seq, ts):
    d = x2.shape[1]
    nj = seq // ts
    return pl.pallas_call(
        _fox_gate_kernel,
        grid=(batch, nj),
        in_specs=[pl.BlockSpec((ts, d), lambda b, j: (b * nj + j, 0)),
                  pl.BlockSpec((d, LANES), lambda b, j: (0, 0)),
                  pl.BlockSpec((1, LANES), lambda b, j: (0, 0))],
        out_specs=[pl.BlockSpec((ts, LANES), lambda b, j: (b * nj + j, 0)),
                   pl.BlockSpec((ts, d), lambda b, j: (b * nj + j, 0))],
        out_shape=[jax.ShapeDtypeStruct((batch * seq, LANES), F32),
                   jax.ShapeDtypeStruct((batch * seq, d), BF16)],
        scratch_shapes=[pltpu.VMEM((1, LANES), F32)],
        compiler_params=pltpu.CompilerParams(
            dimension_semantics=("parallel", "arbitrary"), vmem_limit_bytes=_vmem(40)),
        name="fox_gate",
    )(x2, wf, b_f)


FOX_HEADS_PER_STEP = 8


def _fox_attn_kernel(q_ref, k_ref, v_ref, c_ref, o_ref, vt_ref, crep_ref, *, tile, heads):
    hh = pl.program_id(1)
    i = pl.program_id(2)
    hd = FOX_HEAD_DIM
    nq = vt_ref.shape[1]

    @pl.when(i == 0)
    def _():
        lane = lax.broadcasted_iota(jnp.int32, (tile, LANES), 1)
        for j in range(nq):
            cj = c_ref[pl.ds(j * tile, tile), :]
            for g in range(heads):
                col = jnp.sum(jnp.where(lane == hh * heads + g, cj, 0.0), axis=1, keepdims=True)
                crep_ref[g, j] = jnp.broadcast_to(col, (tile, LANES))
                vj = v_ref[pl.ds(j * tile, tile), g * hd:(g + 1) * hd]
                vt_ref[g, j] = vj.astype(F32).T.astype(BF16)

    def scores(g, j):
        r0 = pl.multiple_of(j * tile, tile)
        kj = k_ref[pl.ds(r0, tile), g * hd:(g + 1) * hd]
        cj = jnp.concatenate([crep_ref[g, j]] * (tile // LANES), axis=1)
        return _dot_nt(kj, q_ref[:, g * hd:(g + 1) * hd]) - cj

    def update(g, j, carry, st):
        m, l, acc = carry
        m_new = jnp.maximum(m, jnp.max(st, axis=0, keepdims=True))
        a = jnp.exp2(m - m_new)
        p = jnp.exp2(st - m_new)
        l = a * l + jnp.sum(p, axis=0, keepdims=True)
        acc = a * acc + _dot(vt_ref[g, j], p.astype(BF16))
        return m_new, l, acc

    def body(j, carries):
        sts = [scores(g, j) for g in range(heads)]
        return tuple(update(g, j, carries[g], sts[g]) for g in range(heads))

    init = tuple((jnp.full((1, tile), NEG_BIG, F32), jnp.zeros((1, tile), F32),
                  jnp.zeros((hd, tile), F32)) for _ in range(heads))
    carries = lax.fori_loop(0, i, body, init)
    key = lax.broadcasted_iota(jnp.int32, (tile, tile), 0)
    qry = lax.broadcasted_iota(jnp.int32, (tile, tile), 1)
    sts = [jnp.where(key <= qry, scores(g, i), NEG_BIG) for g in range(heads)]
    for g in range(heads):
        _, l, acc = update(g, i, carries[g], sts[g])
        o_ref[:, g * hd:(g + 1) * hd] = (acc * (1.0 / l)).T.astype(o_ref.dtype)


def _fox_attention(h, c, batch, seq, tile):
    n = h.shape[0]
    hd = FOX_HEAD_DIM
    nh = FOX_HEADS
    gh = FOX_HEADS_PER_STEP
    ng = nh // gh
    nq = seq // tile
    kern = functools.partial(_fox_attn_kernel, tile=tile, heads=gh)
    return pl.pallas_call(
        kern,
        grid=(batch, ng, nq),
        in_specs=[pl.BlockSpec((tile, gh * hd), lambda b, hh, i: (b * nq + i, hh)),
                  pl.BlockSpec((seq, gh * hd), lambda b, hh, i: (b, ng + hh)),
                  pl.BlockSpec((seq, gh * hd), lambda b, hh, i: (b, 2 * ng + hh)),
                  pl.BlockSpec((seq, LANES), lambda b, hh, i: (b, 0))],
        out_specs=pl.BlockSpec((tile, gh * hd), lambda b, hh, i: (b * nq + i, hh)),
        out_shape=jax.ShapeDtypeStruct((n, nh * hd), BF16),
        scratch_shapes=[pltpu.VMEM((gh, nq, hd, tile), BF16),
                        pltpu.VMEM((gh, nq, tile, LANES), F32)],
        compiler_params=pltpu.CompilerParams(
            dimension_semantics=("parallel", "parallel", "arbitrary"), vmem_limit_bytes=_vmem(40)),
        name="fox_attention",
    )(h, h, h, c)


CONV_PAD = 32
CONV_ROWS = 256


def _conv_kernel(a_ref, g_ref, w_ref, cb_ref, ng_ref, nb_ref, o_ref, pad_ref):
    seq = a_ref.shape[0]
    pad_ref[pl.ds(0, CONV_PAD), :] = jnp.zeros((CONV_PAD, LANES), F32)
    pad_ref[pl.ds(CONV_PAD, seq), :] = a_ref[...].astype(F32) * _sigmoid(g_ref[...].astype(F32))
    off = CONV_PAD - (CONV_WIDTH - 1)
    for r in range(seq // CONV_ROWS):
        base = r * CONV_ROWS
        acc = jnp.zeros((CONV_ROWS, LANES), F32)
        for j in range(CONV_WIDTH):
            acc = acc + pad_ref[pl.ds(base + off + j, CONV_ROWS), :] * w_ref[pl.ds(j, 1), :]
        y = acc + cb_ref[...]
        mu = jnp.mean(y, axis=-1, keepdims=True)
        yc = y - mu
        var = jnp.mean(yc * yc, axis=-1, keepdims=True)
        yn = yc * lax.rsqrt(var + LN_EPS) * ng_ref[...] + nb_ref[...]
        o_ref[pl.ds(base, CONV_ROWS), :] = (yn * _sigmoid(yn)).astype(o_ref.dtype)


def _conv_module(h, a_col, g_col, conv_w, conv_b, cn_g, cn_b, batch, seq):
    n = h.shape[0]
    ch = conv_w.shape[1]
    ng = ch // CONV_GROUP
    assert seq % CONV_ROWS == 0
    vec = lambda: pl.BlockSpec((1, CONV_GROUP), lambda b, g: (0, g))
    return pl.pallas_call(
        _conv_kernel,
        grid=(batch, ng),
        in_specs=[pl.BlockSpec((seq, CONV_GROUP), lambda b, g: (b, a_col + g)),
                  pl.BlockSpec((seq, CONV_GROUP), lambda b, g: (b, g_col + g)),
                  pl.BlockSpec((CONV_WIDTH, CONV_GROUP), lambda b, g: (0, g)),
                  vec(), vec(), vec()],
        out_specs=pl.BlockSpec((seq, CONV_GROUP), lambda b, g: (b, g)),
        out_shape=jax.ShapeDtypeStruct((n, ch), BF16),
        scratch_shapes=[pltpu.VMEM((seq + CONV_PAD, LANES), F32)],
        compiler_params=pltpu.CompilerParams(
            dimension_semantics=("parallel", "parallel"), vmem_limit_bytes=_vmem(32)),
        name="conv_module",
    )(h, h, conv_w, conv_b, cn_g, cn_b)


def _layer_norm(y, g, b):
    mu = jnp.mean(y, axis=-1, keepdims=True)
    yc = y - mu
    var = jnp.mean(yc * yc, axis=-1, keepdims=True)
    return yc * lax.rsqrt(var + LN_EPS) * g + b


def _route_rows(xn, wr_hi, wr_lo, rbias, count_ref):
    x_hi, x_lo = _split_bf16(xn)
    logits = _dot_nt(wr_hi, x_hi) + _dot_nt(wr_hi, x_lo) + _dot_nt(wr_lo, x_hi)
    scores = _sigmoid(logits)
    sel = scores + rbias
    s = [sel[e:e + 1, :] for e in range(N_EXPERTS)]
    r = [scores[e:e + 1, :] for e in range(N_EXPERTS)]
    pg = EXPERTS_PER_GROUP

    def top2_sum(vals):
        best = None
        for a in range(len(vals)):
            for b in range(a + 1, len(vals)):
                t = vals[a] + vals[b]
                best = t if best is None else jnp.maximum(best, t)
        return best

    gs = [top2_sum(s[g * pg:(g + 1) * pg]) for g in range(N_GROUPS)]
    best, grp = gs[0], jnp.zeros_like(gs[0], dtype=jnp.int32)
    for g in range(1, N_GROUPS):
        upd = gs[g] > best
        best = jnp.where(upd, gs[g], best)
        grp = jnp.where(upd, g, grp)

    def pick_group(vals, k):
        out = vals[(N_GROUPS - 1) * pg + k]
        for g in range(N_GROUPS - 2, -1, -1):
            out = jnp.where(grp == g, vals[g * pg + k], out)
        return out

    v = [pick_group(s, k) for k in range(pg)]
    w = [pick_group(r, k) for k in range(pg)]
    b1, i1, w1 = v[0], jnp.zeros_like(grp), w[0]
    for k in range(1, pg):
        upd = v[k] > b1
        b1 = jnp.where(upd, v[k], b1)
        i1 = jnp.where(upd, k, i1)
        w1 = jnp.where(upd, w[k], w1)
    b2 = jnp.full_like(b1, -jnp.inf)
    i2, w2 = jnp.zeros_like(grp), jnp.zeros_like(w1)
    for k in range(pg):
        upd = jnp.logical_and(i1 != k, v[k] > b2)
        b2 = jnp.where(upd, v[k], b2)
        i2 = jnp.where(upd, k, i2)
        w2 = jnp.where(upd, w[k], w2)
    tot = w1 + w2
    e1 = grp * pg + i1
    e2 = grp * pg + i2
    eid = lax.broadcasted_iota(jnp.int32, scores.shape, 0)
    oh1 = jnp.where(eid == e1, 1.0, 0.0)
    oh2 = jnp.where(eid == e2, 1.0, 0.0)
    cnt = oh1 + oh2
    tm = cnt.shape[1]
    lane = lax.broadcasted_iota(jnp.int32, cnt.shape, 1)
    incl = cnt
    sh = 1
    while sh < tm:
        incl = incl + jnp.where(lane >= sh, pltpu.roll(incl, sh, axis=1), 0.0)
        sh *= 2
    before = incl - cnt + count_ref[...]
    count_ref[...] = count_ref[...] + incl[:, tm - 1:tm]
    rank1 = jnp.sum(oh1 * before, axis=0, keepdims=True)
    rank2 = jnp.sum(oh2 * before, axis=0, keepdims=True)
    rows = [e1.astype(F32), e2.astype(F32), w1 / tot, w2 / tot, rank1, rank2]
    rows += [jnp.zeros_like(w1)] * (8 - len(rows))
    return jnp.concatenate(rows, axis=0)


OUTPROJ_SUBTILES = 2


def _outproj_kernel(a1_ref, a2_ref, w_ref, x_ref, g_ref, b_ref, wrh_ref, wrl_ref, rb_ref,
                    xn_ref, rt_ref, rc_ref, cnt_ref, count_ref):
    @pl.when(pl.program_id(0) == 0)
    def _():
        count_ref[...] = jnp.zeros_like(count_ref)

    k1 = a1_ref.shape[1]
    k2 = a2_ref.shape[1]
    tm = x_ref.shape[0]
    sub = tm // OUTPROJ_SUBTILES
    rs = [pl.ds(t * sub, sub) for t in range(OUTPROJ_SUBTILES)]
    mix = [_dot(a1_ref[r, :], w_ref[pl.ds(0, k1), :]) + _dot(a2_ref[r, :], w_ref[pl.ds(k1, k2), :])
           for r in rs]
    for t, r in enumerate(rs):
        xn = _layer_norm(ALPHA * x_ref[r, :] + mix[t], g_ref[...], b_ref[...])
        xn_ref[r, :] = xn
        rows = _route_rows(xn, wrh_ref[...], wrl_ref[...], rb_ref[...], count_ref)
        rt_ref[:, r] = rows
        rc_ref[r, :] = rows.T
    cnt_ref[...] = jnp.broadcast_to(count_ref[...], cnt_ref.shape)


def _outproj_ln_route(a1, a1_col, a2, a2_col, kw, w, xres, ln_g, ln_b, wr_hi, wr_lo, rbias, tm):
    n, d = xres.shape
    full = lambda shape: pl.BlockSpec(shape, lambda i: (0, 0))
    return pl.pallas_call(
        _outproj_kernel,
        grid=(n // tm,),
        in_specs=[pl.BlockSpec((tm, kw), lambda i: (i, a1_col)),
                  pl.BlockSpec((tm, kw), lambda i: (i, a2_col)),
                  full(w.shape),
                  pl.BlockSpec((tm, d), lambda i: (i, 0)),
                  full((1, d)), full((1, d)),
                  full(wr_hi.shape), full(wr_lo.shape), full(rbias.shape)],
        out_specs=[pl.BlockSpec((tm, d), lambda i: (i, 0)),
                   pl.BlockSpec((8, tm), lambda i: (0, i)),
                   pl.BlockSpec((tm, 8), lambda i: (i, 0)),
                   pl.BlockSpec((N_EXPERTS, LANES), lambda i: (0, 0))],
        out_shape=[jax.ShapeDtypeStruct((n, d), F32),
                   jax.ShapeDtypeStruct((8, n), F32),
                   jax.ShapeDtypeStruct((n, 8), F32),
                   jax.ShapeDtypeStruct((N_EXPERTS, LANES), F32)],
        scratch_shapes=[pltpu.VMEM((N_EXPERTS, 1), F32)],
        compiler_params=pltpu.CompilerParams(
            dimension_semantics=("arbitrary",), vmem_limit_bytes=_vmem(52)),
        name="outproj_ln_route",
    )(a1, a2, w, xres, ln_g, ln_b, wr_hi, wr_lo, rbias)


EXPERT_CHUNK_ROWS = 128
EXPERT_STAGES = 8


TOK_WINDOW = 1024


def _expert_kernel(be_ref, na_ref, kb_ref, nk_ref, nx_ref, sl_ref, src_ref,
                   tok_hbm, x_hbm, wg_hbm, wu_hbm, wd_hbm, o_ref,
                   wgu_ref, wdn_ref, stg_ref, sem, xbuf_ref, tok_ref, gsem, tsem, *, layer):
    i = pl.program_id(0)
    tm, d = o_ref.shape
    de = wdn_ref.shape[1]
    na = na_ref[0]

    def tok_copy(j):
        base = pl.multiple_of((src_ref[j] // TOK_WINDOW) * TOK_WINDOW, TOK_WINDOW)
        dst = pl.multiple_of((j % 2) * (2 * TOK_WINDOW), 2 * TOK_WINDOW)
        return pltpu.make_async_copy(tok_hbm.at[pl.ds(base, 2 * TOK_WINDOW)],
                                     tok_ref.at[pl.ds(dst, 2 * TOK_WINDOW)], tsem.at[j % 2])

    def start_rows(j):
        offset = src_ref[j] % TOK_WINDOW
        for buf in range(2):
            @pl.when(j % 2 == buf)
            def _():
                first = buf * (2 * TOK_WINDOW) + offset
                for r in range(tm):
                    tok = tok_ref[first + r]
                    pltpu.make_async_copy(x_hbm.at[pl.ds(tok, 1), :],
                                          xbuf_ref.at[buf, pl.ds(r, 1), :], gsem.at[buf]).start()

    def wait_rows(j):
        pltpu.make_async_copy(x_hbm.at[pl.ds(0, tm), :], xbuf_ref.at[j % 2], gsem.at[j % 2]).wait()
    ch = EXPERT_CHUNK_ROWS
    n_in = d // ch
    n_dn = de // ch
    n_chunks = 2 * n_in + n_dn

    def chunk_copy(e, c):
        st = c % EXPERT_STAGES

        def gate_up(w_hbm, first, col0):
            r0 = pl.multiple_of((c - first) * ch, ch)
            cp = pltpu.make_async_copy(w_hbm.at[layer, e, pl.ds(r0, ch), :],
                                       stg_ref.at[st, :, pl.ds(0, de)], sem.at[st])

            def convert(slot):
                wgu_ref[slot, pl.ds(r0, ch), pl.ds(col0, de)] = stg_ref[st, :, pl.ds(0, de)].astype(BF16)
            return cp, convert

        def down():
            r0 = pl.multiple_of((c - 2 * n_in) * ch, ch)
            cp = pltpu.make_async_copy(wd_hbm.at[layer, e, pl.ds(r0, ch), :],
                                       stg_ref.at[st], sem.at[st])

            def convert(slot):
                wdn_ref[slot, pl.ds(r0, ch), :] = stg_ref[st].astype(BF16)
            return cp, convert

        return ((c < n_in, lambda: gate_up(wg_hbm, 0, 0)),
                (jnp.logical_and(c >= n_in, c < 2 * n_in), lambda: gate_up(wu_hbm, n_in, de)),
                (c >= 2 * n_in, down))

    def start_chunk(e, c):
        for cond, make in chunk_copy(e, c):
            @pl.when(cond)
            def _():
                make()[0].start()

    def finish_chunk(e, c, slot):
        for cond, make in chunk_copy(e, c):
            @pl.when(cond)
            def _():
                cp, convert = make()
                cp.wait()
                convert(slot)

    def prime(e):
        for c in range(EXPERT_STAGES):
            start_chunk(e, jnp.int32(c))

    def process(e, slot, c_lo, c_hi):
        def body(c, _):
            finish_chunk(e, c, slot)

            @pl.when(c + EXPERT_STAGES < n_chunks)
            def _():
                start_chunk(e, c + EXPERT_STAGES)
            return 0
        lax.fori_loop(c_lo, c_hi, body, 0)

    @pl.when(i >= na)
    def _():
        o_ref[...] = jnp.zeros_like(o_ref)

    @pl.when(i < na)
    def _():
        e = be_ref[i]
        kb = kb_ref[i]
        nk = nk_ref[i]
        nxt = nx_ref[i]
        slot = sl_ref[i]

        @pl.when(i == 0)
        def _():
            tok_copy(0).start()
            prime(e)
            tok_copy(0).wait()
            start_rows(0)
            tok_copy(1).start()
            process(e, slot, 0, n_chunks)

        @pl.when(jnp.logical_and(kb == 0, nxt >= 0))
        def _():
            prime(nxt)

        tok_copy(i + 1).wait()
        start_rows(i + 1)
        tok_copy(i + 2).start()
        wait_rows(i)
        gu = _dot(xbuf_ref[i % 2].astype(BF16), wgu_ref[slot])
        g = gu[:, :de]
        u = gu[:, de:]
        hdn = (g * _sigmoid(g)) * u
        o_ref[...] = _dot(hdn.astype(BF16), wdn_ref[slot])

        @pl.when(i == na - 1)
        def _():
            wait_rows(i + 1)
            tok_copy(i + 2).wait()

        @pl.when(nxt >= 0)
        def _():
            process(nxt, 1 - slot, (n_chunks * kb) // nk, (n_chunks * (kb + 1)) // nk)


def _expert_ffn(xn, sorted_tok, tables, w_gate, w_up, w_down, layer, tm, nb):
    d = xn.shape[1]
    de = w_down.shape[2]
    assert d % EXPERT_CHUNK_ROWS == 0 and de % EXPERT_CHUNK_ROWS == 0 and de % LANES == 0
    assert tm <= TOK_WINDOW
    hbm = lambda: pl.BlockSpec(memory_space=pl.ANY)
    return pl.pallas_call(
        functools.partial(_expert_kernel, layer=layer),
        grid_spec=pltpu.PrefetchScalarGridSpec(
            num_scalar_prefetch=len(tables),
            grid=(nb,),
            in_specs=[hbm(), hbm(), hbm(), hbm(), hbm()],
            out_specs=pl.BlockSpec((tm, d), lambda i, *_: (i, 0)),
            scratch_shapes=[pltpu.VMEM((2, d, 2 * de), BF16),
                            pltpu.VMEM((2, de, d), BF16),
                            pltpu.VMEM((EXPERT_STAGES, EXPERT_CHUNK_ROWS, d), F32),
                            pltpu.SemaphoreType.DMA((EXPERT_STAGES,)),
                            pltpu.VMEM((2, tm, d), F32),
                            pltpu.SMEM((2 * 2 * TOK_WINDOW,), jnp.int32),
                            pltpu.SemaphoreType.DMA((2,)),
                            pltpu.SemaphoreType.DMA((2,))]),
        out_shape=jax.ShapeDtypeStruct((nb * tm, d), F32),
        compiler_params=pltpu.CompilerParams(
            dimension_semantics=("arbitrary",), vmem_limit_bytes=_vmem(60)),
        name="expert_ffn",
    )(*tables, sorted_tok, xn, w_gate, w_up, w_down)


ROUTE_GATE_COL = 2


def _combine_kernel(pos_ref, posn_ref, x_ref, ys_hbm, rc_ref, g_ref, b_ref, xn_ref, *rest):
    xb_ref = rest[0] if len(rest) == 3 else None
    ybuf_ref, sem = rest[-2:]
    i = pl.program_id(0)
    nt = pl.num_programs(0)
    tm, d = x_ref.shape

    def start_rows(p_ref, buf):
        for r in range(tm):
            for k in range(TOP_K):
                pltpu.make_async_copy(ys_hbm.at[pl.ds(p_ref[0, 0, k * tm + r], 1), :],
                                      ybuf_ref.at[buf, k, pl.ds(r, 1), :], sem.at[buf]).start()

    @pl.when(i == 0)
    def _():
        start_rows(pos_ref, 0)

    for nxt_buf in range(2):
        @pl.when(jnp.logical_and(i + 1 < nt, (i + 1) % 2 == nxt_buf))
        def _():
            start_rows(posn_ref, nxt_buf)

    buf = i % 2
    for k in range(TOP_K):
        pltpu.make_async_copy(ys_hbm.at[pl.ds(0, tm), :], ybuf_ref.at[buf, k], sem.at[buf]).wait()
    g0 = rc_ref[:, ROUTE_GATE_COL:ROUTE_GATE_COL + 1]
    g1 = rc_ref[:, ROUTE_GATE_COL + 1:ROUTE_GATE_COL + 2]
    y = ALPHA * x_ref[...] + (g0 * ybuf_ref[buf, 0] + g1 * ybuf_ref[buf, 1])
    xn = _layer_norm(y, g_ref[...], b_ref[...])
    xn_ref[...] = xn
    if xb_ref is not None:
        xb_ref[...] = xn.astype(BF16)


POS_TILES_PER_STEP = 8


def _slot_pos_kernel(rt_ref, ps_ref, o_ref, *, tm):
    eid = lax.broadcasted_iota(jnp.int32, (N_EXPERTS, rt_ref.shape[1]), 0)
    for k in range(TOP_K):
        e = rt_ref[k:k + 1, :].astype(jnp.int32)
        base = jnp.sum(jnp.where(eid == e, ps_ref[...], 0.0), axis=0, keepdims=True)
        pos = (base + rt_ref[2 * TOP_K + k:2 * TOP_K + k + 1, :]).astype(jnp.int32)
        for t in range(o_ref.shape[0]):
            o_ref[t, :, pl.ds(k * tm, tm)] = pos[:, t * tm:(t + 1) * tm]


def _slot_positions(route, pstart, tm):
    n = route.shape[1]
    nt = n // tm
    step = POS_TILES_PER_STEP if nt % POS_TILES_PER_STEP == 0 else 1
    return pl.pallas_call(
        functools.partial(_slot_pos_kernel, tm=tm),
        grid=(nt // step,),
        in_specs=[pl.BlockSpec((8, step * tm), lambda i: (0, i)),
                  pl.BlockSpec((N_EXPERTS, 1), lambda i: (0, 0))],
        out_specs=pl.BlockSpec((step, 1, TOP_K * tm), lambda i: (i, 0, 0)),
        out_shape=jax.ShapeDtypeStruct((nt, 1, TOP_K * tm), jnp.int32),
        compiler_params=pltpu.CompilerParams(dimension_semantics=("parallel",)),
        name="slot_positions",
    )(route, pstart)


def _combine_ln(x, ys, pos_t, route_cols, ln_g, ln_b, tm, with_bf16):
    n, d = x.shape
    nt = n // tm
    n_out = 2 if with_bf16 else 1
    row = lambda: pl.BlockSpec((tm, d), lambda i: (i, 0))
    vec = lambda: pl.BlockSpec((1, d), lambda i: (0, 0))
    smem = lambda imap: pl.BlockSpec((1, 1, TOP_K * tm), imap, memory_space=pltpu.SMEM)
    return pl.pallas_call(
        _combine_kernel,
        grid=(nt,),
        in_specs=[smem(lambda i: (i, 0, 0)), smem(lambda i: (jnp.minimum(i + 1, nt - 1), 0, 0)),
                  row(), pl.BlockSpec(memory_space=pl.ANY),
                  pl.BlockSpec((tm, 8), lambda i: (i, 0)), vec(), vec()],
        out_specs=[row(), row()][:n_out],
        out_shape=[jax.ShapeDtypeStruct((n, d), F32), jax.ShapeDtypeStruct((n, d), BF16)][:n_out],
        scratch_shapes=[pltpu.VMEM((2, TOP_K, tm, d), F32), pltpu.SemaphoreType.DMA((2,))],
        compiler_params=pltpu.CompilerParams(
            dimension_semantics=("arbitrary",), vmem_limit_bytes=_vmem(48)),
        name="combine_ln",
    )(pos_t, pos_t, x, ys, route_cols, ln_g, ln_b)


def _gla_gate_kernel(x_ref, wl_ref, w2_ref, ba_ref, o_ref):
    low = _dot(x_ref[...], wl_ref[...]).astype(BF16)
    z = _dot(low, w2_ref[...]) + ba_ref[...]
    o_ref[...] = _log_sigmoid(z) * (1.0 / GLA_TAU)


def _gla_gate(xb, w_low, w_a2, b_a, tm):
    n, d = xb.shape
    kw = w_a2.shape[1]
    return pl.pallas_call(
        _gla_gate_kernel,
        grid=(n // tm,),
        in_specs=[pl.BlockSpec((tm, d), lambda i: (i, 0)),
                  pl.BlockSpec(w_low.shape, lambda i: (0, 0)),
                  pl.BlockSpec(w_a2.shape, lambda i: (0, 0)),
                  pl.BlockSpec((1, kw), lambda i: (0, 0))],
        out_specs=pl.BlockSpec((tm, kw), lambda i: (i, 0)),
        out_shape=jax.ShapeDtypeStruct((n, kw), F32),
        compiler_params=pltpu.CompilerParams(
            dimension_semantics=("parallel",), vmem_limit_bytes=_vmem(32)),
        name="gla_gate",
    )(xb, w_low, w_a2, b_a)


GLA_ROWS = 512


def _gla_kernel(q_ref, k_ref, v_ref, la_ref, g_ref, ng_ref, o_ref, state_ref, *, scale):
    rows, kw = q_ref.shape
    nh = state_ref.shape[0]
    hk, hv = state_ref.shape[1], state_ref.shape[2]
    ck = 2 * GLA_CHUNK
    mid = GLA_CHUNK - 1

    @pl.when(pl.program_id(1) == 0)
    def _():
        state_ref[...] = jnp.zeros_like(state_ref)

    row = lax.broadcasted_iota(jnp.int32, (ck, ck), 0)
    col = lax.broadcasted_iota(jnp.int32, (ck, ck), 1)
    causal = row >= col
    tri = jnp.where(causal, 1.0, 0.0).astype(BF16)
    ones = jnp.ones((ck, LANES), BF16)

    def chunk(c, _):
        r0 = pl.multiple_of(c * ck, ck)
        heads = range(nh)
        ks = [slice(hh * hk, (hh + 1) * hk) for hh in heads]
        vs = [slice(hh * hv, (hh + 1) * hv) for hh in heads]
        la = [_split_bf16(la_ref[pl.ds(r0, ck), ks[hh]]) for hh in heads]
        b = [_dot(tri, la[hh][0]) + _dot(tri, la[hh][1]) for hh in heads]
        dsum = [_dot_tn(la[hh][0], ones) + _dot_tn(la[hh][1], ones) for hh in heads]
        qt, kt, kend, qin = [], [], [], []
        for hh in heads:
            q = q_ref[pl.ds(r0, ck), ks[hh]].astype(F32) * scale
            k = k_ref[pl.ds(r0, ck), ks[hh]].astype(F32)
            b_mid = b[hh][mid:mid + 1, :]
            b_last = b[hh][ck - 1:ck, :]
            qt.append((q * jnp.exp(b[hh] - b_mid)).astype(BF16))
            kt.append((k * jnp.exp(b_mid - b[hh])).astype(BF16))
            kend.append((k * jnp.exp(b_last - b[hh])).astype(BF16))
            qin.append((q * jnp.exp(b[hh])).astype(BF16))
        v = [v_ref[pl.ds(r0, ck), vs[hh]] for hh in heads]
        state = [state_ref[hh] for hh in heads]
        attn = [_dot_nt(qt[hh], kt[hh]) for hh in heads]
        inter = [_dot(qin[hh], state[hh].astype(BF16)) for hh in heads]
        kv = [_dot_tn(kend[hh], v[hh]) for hh in heads]
        for hh in heads:
            decay = jnp.concatenate([jnp.exp(dsum[hh])] * (hv // LANES), axis=1)
            state_ref[hh] = state[hh] * decay + kv[hh]
        for hh in heads:
            o = _dot(jnp.where(causal, attn[hh], 0.0).astype(BF16), v[hh]) + inter[hh]
            o = o * lax.rsqrt(jnp.mean(o * o, axis=-1, keepdims=True) + LN_EPS)
            gate = g_ref[pl.ds(r0, ck), vs[hh]].astype(F32)
            o = o * ng_ref[:, vs[hh]] * (gate * _sigmoid(gate))
            o_ref[pl.ds(r0, ck), vs[hh]] = o.astype(o_ref.dtype)
        return 0

    lax.fori_loop(0, rows // ck, chunk, 0, unroll=4)


def _gla(h, la, norm_g, batch, seq):
    n = h.shape[0]
    kw = la.shape[1]
    vw = norm_g.shape[1]
    nh = GLA_HEADS
    hk, hv = kw // nh, vw // nh
    rows = GLA_ROWS
    nj = seq // rows
    kern = functools.partial(_gla_kernel, scale=hk ** -0.5)
    rowblk = lambda b, j: b * nj + j
    return pl.pallas_call(
        kern,
        grid=(batch, nj),
        in_specs=[pl.BlockSpec((rows, kw), lambda b, j: (rowblk(b, j), 0)),
                  pl.BlockSpec((rows, kw), lambda b, j: (rowblk(b, j), 1)),
                  pl.BlockSpec((rows, vw), lambda b, j: (rowblk(b, j), 2 * kw // vw)),
                  pl.BlockSpec((rows, kw), lambda b, j: (rowblk(b, j), 0)),
                  pl.BlockSpec((rows, vw), lambda b, j: (rowblk(b, j), 2 * kw // vw + 1)),
                  pl.BlockSpec((1, vw), lambda b, j: (0, 0))],
        out_specs=pl.BlockSpec((rows, vw), lambda b, j: (rowblk(b, j), 0)),
        out_shape=jax.ShapeDtypeStruct((n, vw), BF16),
        scratch_shapes=[pltpu.VMEM((nh, hk, hv), F32)],
        compiler_params=pltpu.CompilerParams(
            dimension_semantics=("parallel", "arbitrary"), vmem_limit_bytes=_vmem(48)),
        name="gla",
    )(h, h, h, la, h, norm_g)


EXPERT_TM = 256
COMBINE_TM = 256


def _tables_kernel(cnt_ref, be_ref, na_ref, kb_ref, nk_ref, nx_ref, sl_ref, src_ref, ps_ref, *, tm):
    n_tab = be_ref.shape[0]
    blocks_of = lambda e: (cnt_ref[e] + tm - 1) // tm
    run = jnp.int32(0)
    first = jnp.int32(0)
    order = jnp.int32(0)
    for e in range(N_EXPERTS):
        nbe = blocks_of(e)
        ps_ref[e] = run * tm

        def fill(k, _, e=e, run=run, first=first, order=order, nbe=nbe):
            i = run + k
            be_ref[i] = e
            kb_ref[i] = k
            nk_ref[i] = nbe
            sl_ref[i] = order % 2
            src_ref[i] = first + k * tm
            return 0
        lax.fori_loop(0, nbe, fill, 0)
        run = run + nbe
        first = first + cnt_ref[e]
        order = order + jnp.where(nbe > 0, 1, 0)
    na_ref[0] = run

    nxt = jnp.int32(-1)
    end = run
    for e in reversed(range(N_EXPERTS)):
        nbe = blocks_of(e)

        def fill_next(i, _, nxt=nxt):
            nx_ref[i] = nxt
            return 0
        lax.fori_loop(end - nbe, end, fill_next, 0)
        nxt = jnp.where(nbe > 0, e, nxt)
        end = end - nbe

    last = run - 1

    def pad(i, _):
        be_ref[i] = be_ref[last]
        kb_ref[i] = kb_ref[last]
        nk_ref[i] = nk_ref[last]
        nx_ref[i] = nx_ref[last]
        sl_ref[i] = sl_ref[last]
        src_ref[i] = 0
        return 0
    lax.fori_loop(run, n_tab, pad, 0)


def _block_tables(counts, tm, n_tab):
    i32 = jnp.int32
    smem = lambda: pl.BlockSpec(memory_space=pltpu.SMEM)
    tab = jax.ShapeDtypeStruct((n_tab,), i32)
    outs = pl.pallas_call(
        functools.partial(_tables_kernel, tm=tm),
        in_specs=[smem()],
        out_specs=[smem()] * 8,
        out_shape=[tab, jax.ShapeDtypeStruct((1,), i32), tab, tab, tab, tab, tab,
                   jax.ShapeDtypeStruct((N_EXPERTS,), i32)],
        name="block_tables",
    )(counts)
    return tuple(outs[:7]), outs[7]


def _moe(xn, route, route_cols, counts, w_gate, w_up, w_down, layer, ln_g, ln_b, with_bf16):
    n, d = xn.shape
    a = n * TOP_K
    tm = EXPERT_TM
    nb = a // tm + N_EXPERTS
    i32 = jnp.int32
    eidx = route[:TOP_K].astype(i32)
    assign = jnp.arange(n, dtype=i32)[None, :] * TOP_K + jnp.arange(TOP_K, dtype=i32)[:, None]
    sorted_tok = (lax.sort((eidx * a + assign).reshape(a)) % a) // TOP_K
    sorted_tok = jnp.concatenate([sorted_tok, jnp.zeros((2 * TOK_WINDOW,), i32)])
    tables, pstart = _block_tables(counts[:, 0].astype(i32), tm, nb + 2)
    ys = _expert_ffn(xn, sorted_tok, tables, w_gate, w_up, w_down, layer, tm, nb)
    pos = _slot_positions(route, pstart.reshape(N_EXPERTS, 1).astype(F32), COMBINE_TM)
    return _combine_ln(xn, ys, pos, route_cols, ln_g, ln_b, COMBINE_TM, with_bf16)


def kernel(x, even_w_in, even_b_f, even_conv_w, even_conv_b, even_conv_norm_g, even_conv_norm_b, even_w_out, odd_w_in, odd_w_a2, odd_b_a, odd_norm_g, odd_w_out, ln_mix_g, ln_mix_b, ln_ffn_g, ln_ffn_b, router_w, router_bias, expert_w_gate, expert_w_up, expert_w_down):
    batch, seq, d = x.shape
    n = batch * seq
    x2 = x.reshape(n, d)
    fw = FOX_HEADS * FOX_HEAD_DIM
    conv_ch = even_conv_w.shape[-1]
    kw = odd_w_a2.shape[-1]
    vw = odd_norm_g.shape[-1]
    row = lambda t: t.reshape(1, -1)

    wr_hi, wr_lo = _split_bf16(router_w.T)
    rbias = router_bias.reshape(N_EXPERTS, 1).astype(F32)
    experts_w = (expert_w_gate, expert_w_up, expert_w_down)

    w_t = jnp.swapaxes(even_w_in, 1, 2)
    q_scale = LOG2E * FOX_HEAD_DIM ** -0.5
    wf = jnp.zeros((d, LANES), F32).at[:, :FOX_HEADS].set(even_w_in[0, :, 3 * fw:3 * fw + FOX_HEADS])
    b_f = jnp.zeros((1, LANES), F32).at[0, :FOX_HEADS].set(even_b_f[0])
    c, xb = _fox_gate(x2, wf.astype(BF16), b_f, batch, seq, 1024)
    h = _matmul(xb, w_t, 3 * fw, BF16, 1024, fw, first_scale=q_scale)
    h_glu = _matmul(xb, w_t[:, 3 * fw + FOX_HEADS:], 2 * conv_ch, BF16, 1024, conv_ch)
    att = _fox_attention(h, c, batch, seq, 256)
    u = _conv_module(h_glu, 0, conv_ch // CONV_GROUP, even_conv_w[0, :, 0, :],
                     row(even_conv_b[0]), row(even_conv_norm_g[0]), row(even_conv_norm_b[0]),
                     batch, seq)
    xn, route, rcols, cnt = _outproj_ln_route(att, 0, u, 0, fw, even_w_out[0].astype(BF16), x2,
                                              row(ln_mix_g[0]), row(ln_mix_b[0]),
                                              wr_hi, wr_lo, rbias, 512)
    xn, xb = _moe(xn, route, rcols, cnt, *experts_w, 0, row(ln_ffn_g[0]), row(ln_ffn_b[0]), True)

    w_t = jnp.swapaxes(odd_w_in, 1, 2)
    h = _matmul(xb, w_t, 2 * kw + 2 * vw, BF16, 1024, 1024)
    w_low = jnp.zeros((d, LANES), F32).at[:, :GLA_LOW_RANK].set(odd_w_in[0, :, 2 * kw + 2 * vw:])
    w_low = w_low.astype(BF16)
    w_a2 = jnp.zeros((LANES, kw), F32).at[:GLA_LOW_RANK].set(odd_w_a2[0]).astype(BF16)
    la = _gla_gate(xb, w_low, w_a2, row(odd_b_a[0]), 1024)
    o = _gla(h, la, row(odd_norm_g[0]), batch, seq)
    half = vw // 2
    xn, route, rcols, cnt = _outproj_ln_route(o, 0, o, 1, half, odd_w_out[0].astype(BF16), xn,
                                              row(ln_mix_g[1]), row(ln_mix_b[1]),
                                              wr_hi, wr_lo, rbias, 512)
    (xn,) = _moe(xn, route, rcols, cnt, *experts_w, 1, row(ln_ffn_g[1]), row(ln_ffn_b[1]), False)
    return xn.reshape(batch, seq, d)
```
